```python
import jax, jax.numpy as jnp
from jax import lax
import numpy as np

D_MODEL = 1024
BATCH = 2
SEQ = 8192
DEPTH = 2

NORM_EPS = 1e-6
NEG_INF = -1e30
A_HEADS = 8
A_HEAD_DIM = 64
A_WIDTH = A_HEADS * A_HEAD_DIM
A_BRANCHES = ((128, 1), (512, 4), (2048, 16))
ATTN_BLOCK = 128
ROPE_THETA = 500000.0
ROPE_DIM = A_HEAD_DIM // 4
B_WIDTH = D_MODEL - A_WIDTH
CONV_WIDTH = 31
C_HEADS = 4
C_QK_DIM = 256
C_V_DIM = 512
C_QK_WIDTH = C_HEADS * C_QK_DIM
C_V_WIDTH = C_HEADS * C_V_DIM
RET_CHUNK = 128
RET_ROT_THETA = 10000.0
MOE_GROUPS = 4
MOE_EXPERTS_PER_GROUP = 8
MOE_EXPERTS = MOE_GROUPS * MOE_EXPERTS_PER_GROUP
MOE_TOP_K = 2
EXPERT_HIDDEN = 512
MOE_BLOCK = 128

EVEN_IN_WIDTH = 3 * A_WIDTH + 2 * B_WIDTH
ODD_IN_WIDTH = 2 * C_QK_WIDTH + 2 * C_V_WIDTH
N_EVEN = (DEPTH + 1) // 2
N_ODD = DEPTH // 2

kernel_name = 'hybrid_dilated_conv_retention_hmoe'

F32 = jnp.float32


def rms_norm(x, gain):
    xf = x.astype(F32)
    y = xf * lax.rsqrt(jnp.mean(xf * xf, axis=-1, keepdims=True) + NORM_EPS)
    return (y * gain.astype(F32)).astype(x.dtype)


def rotary_tables(seq, inv_freq):
    ang = jnp.arange(seq, dtype=F32)[:, None] * inv_freq[None, :]
    return jnp.cos(ang), jnp.sin(ang)


def rotate(x, cos, sin):
    x1, x2 = jnp.split(x, 2, axis=-1)
    return jnp.concatenate([x1 * cos - x2 * sin, x2 * cos + x1 * sin], axis=-1)


def partial_rope(x, cos, sin):
    xr = rotate(x[..., :ROPE_DIM].astype(F32), cos, sin).astype(x.dtype)
    return jnp.concatenate([xr, x[..., ROPE_DIM:]], axis=-1)


def banded_causal_attention(q, k, v, n_steps):
    bsz, grp, length, dh = q.shape
    nb = -(-length // ATTN_BLOCK)
    padw = ((0, 0), (0, 0), (0, nb * ATTN_BLOCK - length), (0, 0))
    qb, kb, vb = (jnp.pad(t, padw).reshape(bsz, grp, nb, ATTN_BLOCK, dh) for t in (q, k, v))

    def with_prev(t):
        prev = jnp.pad(t, ((0, 0), (0, 0), (1, 0), (0, 0), (0, 0)))[:, :, :-1]
        return jnp.concatenate([prev, t], axis=3)

    kk, vv = with_prev(kb), with_prev(vb)
    s = jnp.einsum('bgnqd,bgnkd->bgnqk', qb, kk, preferred_element_type=F32) * (dh ** -0.5)
    qi = jnp.arange(ATTN_BLOCK)[:, None]
    kj = jnp.arange(2 * ATTN_BLOCK)[None, :]
    dist = ATTN_BLOCK + qi - kj
    band = (dist >= 0) & (dist <= n_steps)
    blk = jnp.arange(nb)[:, None, None]
    valid = band[None] & ((blk > 0) | (kj >= ATTN_BLOCK)[None])
    s = jnp.where(valid, s, NEG_INF)
    m = jnp.max(s, axis=-1, keepdims=True)
    e = jnp.exp(s - m)
    den = jnp.sum(e, axis=-1, keepdims=True)
    out = jnp.einsum('bgnqk,bgnkd->bgnqd', (e / den).astype(v.dtype), vv, preferred_element_type=F32)
    lse = (m + jnp.log(den))[..., 0]
    out = out.reshape(bsz, grp, nb * ATTN_BLOCK, dh)[:, :, :length]
    lse = lse.reshape(bsz, grp, nb * ATTN_BLOCK)[:, :, :length]
    return out, lse


def to_residue_classes(t, dil):
    bsz, h, seq, d = t.shape
    return t.reshape(bsz, h, seq // dil, dil, d).transpose(0, 1, 3, 2, 4).reshape(bsz, h * dil, seq // dil, d)


def from_residue_classes(t, h, dil):
    bsz, _, length, d = t.shape
    return t.reshape(bsz, h, dil, length, d).transpose(0, 1, 3, 2, 4).reshape(bsz, h, dil * length, d)


def dilated_window_attention(q, k, v):
    h = q.shape[1]
    outs, lses = [], []
    for window, dil in A_BRANCHES:
        o, l = banded_causal_attention(to_residue_classes(q, dil), to_residue_classes(k, dil),
                                       to_residue_classes(v, dil), window // dil)
        outs.append(from_residue_classes(o, h, dil))
        lses.append(from_residue_classes(l[..., None], h, dil)[..., 0])
    w = jax.nn.softmax(jnp.stack(lses), axis=0)
    return jnp.einsum('rbhs,rbhsd->bhsd', w, jnp.stack(outs))


def conformer_conv(val, gate, conv_w, conv_b, ln_g, ln_b):
    a = val * jax.nn.sigmoid(gate)
    c = a.shape[-1]
    y = lax.conv_general_dilated(a, conv_w[:, None, :].astype(a.dtype), window_strides=(1,),
                                 padding=[(CONV_WIDTH - 1, 0)],
                                 dimension_numbers=('NWC', 'WIO', 'NWC'),
                                 feature_group_count=c) + conv_b.astype(a.dtype)
    yf = y.astype(F32)
    mu = jnp.mean(yf, axis=-1, keepdims=True)
    var = jnp.mean(jnp.square(yf - mu), axis=-1, keepdims=True)
    yn = (yf - mu) * lax.rsqrt(var + NORM_EPS) * ln_g.astype(F32) + ln_b.astype(F32)
    return jax.nn.silu(yn).astype(val.dtype)


def even_mixer(u, w_in, w_out, conv_w, conv_b, conv_g, conv_beta, cos, sin):
    bsz, seq, _ = u.shape
    proj = u @ w_in
    q, k, v, bval, bgate = jnp.split(
        proj, [A_WIDTH, 2 * A_WIDTH, 3 * A_WIDTH, 3 * A_WIDTH + B_WIDTH], axis=-1)

    def heads(t):
        return t.reshape(bsz, seq, A_HEADS, A_HEAD_DIM).transpose(0, 2, 1, 3)

    qh = partial_rope(heads(q), cos, sin)
    kh = partial_rope(heads(k), cos, sin)
    a_out = dilated_window_attention(qh, kh, heads(v))
    a_out = a_out.transpose(0, 2, 1, 3).reshape(bsz, seq, A_WIDTH).astype(u.dtype)
    b_out = conformer_conv(bval, bgate, conv_w, conv_b, conv_g, conv_beta)
    return jnp.concatenate([a_out, b_out], axis=-1) @ w_out


def retention(q, k, v, log_decay):
    bsz, h, seq, dk = q.shape
    dv = v.shape[-1]
    nc = seq // RET_CHUNK
    qc = q.reshape(bsz, h, nc, RET_CHUNK, dk)
    kc = k.reshape(bsz, h, nc, RET_CHUNK, dk)
    vc = v.reshape(bsz, h, nc, RET_CHUNK, dv)
    pos = jnp.arange(RET_CHUNK, dtype=F32)
    diff = pos[:, None] - pos[None, :]
    intra = jnp.where(diff >= 0, jnp.exp(log_decay[:, None, None] * jnp.maximum(diff, 0.0)), 0.0)
    scores = jnp.einsum('bhnid,bhnjd->bhnij', qc, kc) * intra[:, None]
    inner = jnp.einsum('bhnij,bhnje->bhnie', scores, vc)
    q_decay = jnp.exp(log_decay[:, None] * (pos + 1.0))[None, :, :, None]
    k_decay = jnp.exp(log_decay[:, None] * (RET_CHUNK - 1.0 - pos))[None, :, :, None]
    chunk_decay = jnp.exp(log_decay * RET_CHUNK)[None, :, None, None]

    def step(state, xs):
        qn, kn, vn = xs
        cross = jnp.einsum('bhid,bhde->bhie', qn, state) * q_decay
        state = state * chunk_decay + jnp.einsum('bhjd,bhje->bhde', kn * k_decay, vn)
        return state, cross

    init = jnp.zeros((bsz, h, dk, dv), F32)
    xs = (qc.transpose(2, 0, 1, 3, 4), kc.transpose(2, 0, 1, 3, 4), vc.transpose(2, 0, 1, 3, 4))
    _, cross = lax.scan(step, init, xs)
    out = inner + cross.transpose(1, 2, 0, 3, 4)
    return out.reshape(bsz, h, seq, dv)


def odd_mixer(u, w_in, w_out, cos, sin):
    bsz, seq, _ = u.shape
    proj = u @ w_in
    q, k, v, g = jnp.split(proj, [C_QK_WIDTH, 2 * C_QK_WIDTH, 2 * C_QK_WIDTH + C_V_WIDTH], axis=-1)

    def heads(t, d):
        return t.reshape(bsz, seq, C_HEADS, d).transpose(0, 2, 1, 3).astype(F32)

    qh = rotate(heads(q, C_QK_DIM), cos, sin)
    kh = rotate(heads(k, C_QK_DIM), cos, sin) * (C_QK_DIM ** -0.5)
    log_decay = jnp.log(1.0 - jnp.exp2(-5.0 - jnp.arange(C_HEADS, dtype=F32)))
    o = retention(qh, kh, heads(v, C_V_DIM), log_decay)
    mu = jnp.mean(o, axis=-1, keepdims=True)
    var = jnp.mean(jnp.square(o - mu), axis=-1, keepdims=True)
    o = (o - mu) * lax.rsqrt(var + NORM_EPS)
    o = o.transpose(0, 2, 1, 3).reshape(bsz, seq, C_V_WIDTH)
    y = jax.nn.silu(g.astype(F32)) * o
    return y.astype(u.dtype) @ w_out


def grouped_experts(x2, experts, gates, w_gate, w_up, w_down):
    n_tok, d = x2.shape
    n_assign = n_tok * MOE_TOP_K
    flat_e = experts.reshape(-1)
    flat_tok = jnp.repeat(jnp.arange(n_tok, dtype=jnp.int32), MOE_TOP_K)
    flat_g = gates.reshape(-1)
    order = jnp.argsort(flat_e)
    sorted_e = flat_e[order]
    counts = jnp.bincount(flat_e, length=MOE_EXPERTS)
    starts = jnp.cumsum(counts) - counts
    padded = (counts + MOE_BLOCK - 1) // MOE_BLOCK * MOE_BLOCK
    pad_end = jnp.cumsum(padded)
    pad_start = pad_end - padded
    dest = pad_start[sorted_e] + jnp.arange(n_assign, dtype=jnp.int32) - starts[sorted_e]
    n_rows = n_assign + MOE_EXPERTS * MOE_BLOCK
    n_blocks = n_rows // MOE_BLOCK
    row_tok = jnp.zeros((n_rows,), jnp.int32).at[dest].set(flat_tok[order])
    row_gate = jnp.zeros((n_rows,), F32).at[dest].set(flat_g[order])
    block_expert = jnp.minimum(
        jnp.searchsorted(pad_end, jnp.arange(n_blocks, dtype=jnp.int32) * MOE_BLOCK, side='right'),
        MOE_EXPERTS - 1)
    xb = x2[row_tok].reshape(n_blocks, MOE_BLOCK, d)

    def expert_block(args):
        xblk, e = args
        hid = jax.nn.silu(xblk @ w_gate[e]) * (xblk @ w_up[e])
        return hid @ w_down[e]

    yb = lax.map(expert_block, (xb, block_expert))
    yr = yb.reshape(n_rows, d).astype(F32) * row_gate[:, None]
    return jax.ops.segment_sum(yr, row_tok, num_segments=n_tok).astype(x2.dtype)


def hierarchical_moe(h, w_r1, b_r1, w_r2, b_r2, w_gate, w_up, w_down):
    bsz, seq, d = h.shape
    x2 = h.reshape(-1, d)
    n_tok = x2.shape[0]
    group_prob = jax.nn.softmax((x2 @ w_r1).astype(F32) + b_r1.astype(F32), axis=-1)
    gp, gi = lax.top_k(group_prob, 1)
    fine = ((x2 @ w_r2).astype(F32) + b_r2.astype(F32)).reshape(n_tok, MOE_GROUPS, MOE_EXPERTS_PER_GROUP)
    sel = jnp.einsum('tg,tge->te', jax.nn.one_hot(gi[:, 0], MOE_GROUPS, dtype=F32), fine)
    ev, ei = lax.top_k(sel, MOE_TOP_K)
    gates = gp * jax.nn.softmax(ev, axis=-1)
    experts = gi * MOE_EXPERTS_PER_GROUP + ei
    y = grouped_experts(x2, experts, gates, w_gate, w_up, w_down)
    return y.reshape(bsz, seq, d)


def setup_inputs(seed: int = 0) -> dict:
    key = jax.random.key(seed)
    ks = jax.random.split(key, 20)

    def nrm(k, shape, scale):
        return jax.random.normal(k, shape, F32) * scale

    return {
        'x': nrm(ks[0], (BATCH, SEQ, D_MODEL), 1.0),
        'mix_norm': 1.0 + nrm(ks[1], (DEPTH, D_MODEL), 0.02),
        'ffn_norm': 1.0 + nrm(ks[2], (DEPTH, D_MODEL), 0.02),
        'final_norm': 1.0 + nrm(ks[3], (D_MODEL,), 0.02),
        'even_w_in': nrm(ks[4], (N_EVEN, D_MODEL, EVEN_IN_WIDTH), D_MODEL ** -0.5),
        'even_w_out': nrm(ks[5], (N_EVEN, A_WIDTH + B_WIDTH, D_MODEL), (A_WIDTH + B_WIDTH) ** -0.5),
        'conv_w': nrm(ks[6], (N_EVEN, CONV_WIDTH, B_WIDTH), CONV_WIDTH ** -0.5),
        'conv_b': nrm(ks[7], (N_EVEN, B_WIDTH), 0.02),
        'conv_norm_g': 1.0 + nrm(ks[8], (N_EVEN, B_WIDTH), 0.02),
        'conv_norm_b': nrm(ks[9], (N_EVEN, B_WIDTH), 0.02),
        'odd_w_in': nrm(ks[10], (N_ODD, D_MODEL, ODD_IN_WIDTH), D_MODEL ** -0.5),
        'odd_w_out': nrm(ks[11], (N_ODD, C_V_WIDTH, D_MODEL), C_V_WIDTH ** -0.5),
        'router_w1': nrm(ks[12], (DEPTH, D_MODEL, MOE_GROUPS), D_MODEL ** -0.5),
        'router_b1': nrm(ks[13], (DEPTH, MOE_GROUPS), 0.01),
        'router_w2': nrm(ks[14], (DEPTH, D_MODEL, MOE_EXPERTS), D_MODEL ** -0.5),
        'router_b2': nrm(ks[15], (DEPTH, MOE_EXPERTS), 0.01),
        'expert_w_gate': nrm(ks[16], (DEPTH, MOE_EXPERTS, D_MODEL, EXPERT_HIDDEN), D_MODEL ** -0.5),
        'expert_w_up': nrm(ks[17], (DEPTH, MOE_EXPERTS, D_MODEL, EXPERT_HIDDEN), D_MODEL ** -0.5),
        'expert_w_down': nrm(ks[18], (DEPTH, MOE_EXPERTS, EXPERT_HIDDEN, D_MODEL), EXPERT_HIDDEN ** -0.5),
    }


def reference(x, mix_norm, ffn_norm, final_norm, even_w_in, even_w_out, conv_w, conv_b,
              conv_norm_g, conv_norm_b, odd_w_in, odd_w_out, router_w1, router_b1, router_w2,
              router_b2, expert_w_gate, expert_w_up, expert_w_down):
    seq = x.shape[1]
    cos_a, sin_a = rotary_tables(seq, ROPE_THETA ** (-jnp.arange(0, ROPE_DIM, 2, dtype=F32) / ROPE_DIM))
    cos_r, sin_r = rotary_tables(seq, RET_ROT_THETA ** (-jnp.linspace(0.0, 1.0, C_QK_DIM // 2, dtype=F32)))
    h = x
    for layer in range(DEPTH):
        u = rms_norm(h, mix_norm[layer])
        i = layer // 2
        if layer % 2 == 0:
            h = h + even_mixer(u, even_w_in[i], even_w_out[i], conv_w[i], conv_b[i],
                               conv_norm_g[i], conv_norm_b[i], cos_a, sin_a)
        else:
            h = h + odd_mixer(u, odd_w_in[i], odd_w_out[i], cos_r, sin_r)
        h = h + hierarchical_moe(rms_norm(h, ffn_norm[layer]), router_w1[layer], router_b1[layer],
                                 router_w2[layer], router_b2[layer], expert_w_gate[layer],
                                 expert_w_up[layer], expert_w_down[layer])
    return rms_norm(h, final_norm)
```

```python
import functools

import jax
import jax.numpy as jnp
from jax import lax
from jax.experimental import pallas as pl
from jax.experimental.pallas import tpu as pltpu

F32 = jnp.float32
BF16 = jnp.bfloat16
I32 = jnp.int32

NORM_EPS = 1e-6
NEG_INF = -1e30

D_MODEL = 1024
A_HEADS = 8
A_HEAD_DIM = 64
A_WIDTH = A_HEADS * A_HEAD_DIM
A_BRANCHES = ((128, 1), (512, 4), (2048, 16))
ATTN_BLOCK = 128
ROPE_THETA = 500000.0
ROPE_DIM = A_HEAD_DIM // 4
B_WIDTH = D_MODEL - A_WIDTH
CONV_WIDTH = 31
C_HEADS = 4
C_QK_DIM = 256
C_V_DIM = 512
C_QK_WIDTH = C_HEADS * C_QK_DIM
C_V_WIDTH = C_HEADS * C_V_DIM
RET_CHUNK = 128
RET_ROT_THETA = 10000.0
MOE_GROUPS = 4
MOE_EXPERTS_PER_GROUP = 8
MOE_EXPERTS = MOE_GROUPS * MOE_EXPERTS_PER_GROUP
EXPERT_HIDDEN = 512
MOE_BLOCK = 128
EVEN_IN_WIDTH = 3 * A_WIDTH + 2 * B_WIDTH
ODD_IN_WIDTH = 2 * C_QK_WIDTH + 2 * C_V_WIDTH

LANES = 128
ROW_TILE = 512
CONV_TILE = 512
CONV_HALO = 32
CONV_CHUNK = 64
ROUTER_LANE0 = MOE_GROUPS
DMA_TILE = 256
VMEM_LIMIT = 56 * 1024 * 1024


def _cparams(sem):
    return pltpu.CompilerParams(dimension_semantics=sem, vmem_limit_bytes=VMEM_LIMIT)


def _rms(x, gain):
    ms = jnp.mean(x * x, axis=-1, keepdims=True)
    return x * lax.rsqrt(ms + NORM_EPS) * gain


def _even_in_kernel(h_ref, g_ref, w_ref, c_ref, s1_ref, s2_ref, qkv_ref, conv_ref):
    u = _rms(h_ref[...], g_ref[...]).astype(BF16)
    acc = jnp.dot(u, w_ref[...], preferred_element_type=F32)
    c, s1, s2 = c_ref[...], s1_ref[...], s2_ref[...]
    for j in range(2 * A_WIDTH // LANES):
        xg = acc[:, j * LANES:(j + 1) * LANES]
        if j < A_WIDTH // LANES:
            xg = xg * (A_HEAD_DIM ** -0.5)
        r = xg * c + pltpu.roll(xg, LANES - ROPE_DIM // 2, 1) * s1 + pltpu.roll(xg, ROPE_DIM // 2, 1) * s2
        qkv_ref[:, j * LANES:(j + 1) * LANES] = r.astype(BF16)
    qkv_ref[:, 2 * A_WIDTH:3 * A_WIDTH] = acc[:, 2 * A_WIDTH:3 * A_WIDTH].astype(BF16)
    conv_ref[...] = acc[:, 3 * A_WIDTH:].astype(BF16)


def _even_in(h, gain, w_bf16, tabs, seq):
    t = h.shape[0]
    tm = ROW_TILE
    nseq = seq // tm
    tab_spec = pl.BlockSpec((tm, LANES), lambda i: (i % nseq, 0))
    return pl.pallas_call(
        _even_in_kernel,
        grid=(t // tm,),
        in_specs=[
            pl.BlockSpec((tm, D_MODEL), lambda i: (i, 0)),
            pl.BlockSpec((1, D_MODEL), lambda i: (0, 0)),
            pl.BlockSpec((D_MODEL, EVEN_IN_WIDTH), lambda i: (0, 0)),
            tab_spec, tab_spec, tab_spec,
        ],
        out_specs=[
            pl.BlockSpec((tm, 3 * A_WIDTH), lambda i: (i, 0)),
            pl.BlockSpec((tm, 2 * B_WIDTH), lambda i: (i, 0)),
        ],
        out_shape=[
            jax.ShapeDtypeStruct((t, 3 * A_WIDTH), BF16),
            jax.ShapeDtypeStruct((t, 2 * B_WIDTH), BF16),
        ],
        compiler_params=_cparams(("parallel",)),
        name="even_in_proj",
    )(h, gain.reshape(1, D_MODEL), w_bf16, *tabs)


def _attn_rope_tables(seq):
    half = ROPE_DIM // 2
    inv_freq = ROPE_THETA ** (-jnp.arange(0, ROPE_DIM, 2, dtype=F32) / ROPE_DIM)
    ang = jnp.arange(seq, dtype=F32)[:, None] * inv_freq[None, :]
    cos, sin = jnp.cos(ang), jnp.sin(ang)
    rest = A_HEAD_DIM - ROPE_DIM
    ones = jnp.ones((seq, rest), F32)
    z_rest = jnp.zeros((seq, rest), F32)
    z_half = jnp.zeros((seq, half), F32)
    c = jnp.concatenate([cos, cos, ones], axis=1)
    s1 = jnp.concatenate([-sin, z_half, z_rest], axis=1)
    s2 = jnp.concatenate([z_half, sin, z_rest], axis=1)
    rep = LANES // A_HEAD_DIM
    return tuple(jnp.tile(a, (1, rep)) for a in (c, s1, s2))


def _attn_kernel(q_ref, kp_ref, kc_ref, vp_ref, vc_ref, o_ref, lse_ref):
    n = pl.program_id(2)
    blk = ATTN_BLOCK
    qi = lax.broadcasted_iota(I32, (blk, 2 * blk), 0)
    kj = lax.broadcasted_iota(I32, (blk, 2 * blk), 1)
    dist = blk + qi - kj
    valid = (dist >= 0) & (dist <= blk) & ((kj >= blk) | (n > 0))
    valid2 = jnp.concatenate([valid, valid], axis=0)
    lane = lax.broadcasted_iota(I32, (blk, LANES), 1)
    lo = lane < A_HEAD_DIM
    lse_tile = jnp.zeros((blk, LANES), F32)
    for g in range(A_WIDTH // LANES):
        sl = slice(g * LANES, (g + 1) * LANES)
        q = q_ref[:, sl].astype(F32)
        q2 = jnp.concatenate([jnp.where(lo, q, 0.0), jnp.where(lo, 0.0, q)], axis=0).astype(BF16)
        kk = jnp.concatenate([kp_ref[:, sl], kc_ref[:, sl]], axis=0)
        vv = jnp.concatenate([vp_ref[:, sl], vc_ref[:, sl]], axis=0)
        s = lax.dot_general(q2, kk, (((1,), (1,)), ((), ())), preferred_element_type=F32)
        s = jnp.where(valid2, s, NEG_INF)
        m = jnp.max(s, axis=-1, keepdims=True)
        e = jnp.exp(s - m)
        den = jnp.sum(e, axis=-1, keepdims=True)
        pv = jnp.dot(e.astype(BF16), vv, preferred_element_type=F32) / den
        o_ref[:, sl] = jnp.where(lo, pv[:blk], pv[blk:]).astype(BF16)
        lse = m + jnp.log(den)
        lse_tile = jnp.where(lane == 2 * g, lse[:blk], lse_tile)
        lse_tile = jnp.where(lane == 2 * g + 1, lse[blk:], lse_tile)
    lse_ref[...] = lse_tile


def _attn_branch(qkv_d, batch, seq, dil):
    blk = ATTN_BLOCK
    nb = seq // dil // blk
    rows = qkv_d.shape[0]

    def cur(col):
        return pl.BlockSpec((blk, A_WIDTH), lambda b, r, n: (b * nb + n, 3 * r + col))

    def prev(col):
        return pl.BlockSpec((blk, A_WIDTH), lambda b, r, n: (b * nb + jnp.maximum(n - 1, 0), 3 * r + col))

    return pl.pallas_call(
        _attn_kernel,
        grid=(batch, dil, nb),
        in_specs=[cur(0), prev(1), cur(1), prev(2), cur(2)],
        out_specs=[
            pl.BlockSpec((blk, A_WIDTH), lambda b, r, n: (b * nb + n, r)),
            pl.BlockSpec((blk, LANES), lambda b, r, n: (b * nb + n, r)),
        ],
        out_shape=[
            jax.ShapeDtypeStruct((rows, dil * A_WIDTH), BF16),
            jax.ShapeDtypeStruct((rows, dil * LANES), F32),
        ],
        compiler_params=_cparams(("parallel", "parallel", "arbitrary")),
        name=f"dilated_attn_d{dil}",
    )(qkv_d, qkv_d, qkv_d, qkv_d, qkv_d)


def _conv_kernel(val_ref, gate_ref, w_ref, b_ref, g_ref, beta_ref, o_ref, abuf):
    n = pl.program_id(1)
    tc = CONV_TILE

    @pl.when(n == 0)
    def _():
        abuf[0:CONV_HALO, :] = jnp.zeros((CONV_HALO, B_WIDTH), F32)

    @pl.when(n > 0)
    def _():
        abuf[0:CONV_HALO, :] = abuf[tc:tc + CONV_HALO, :]

    val = val_ref[...].astype(F32)
    gate = gate_ref[...].astype(F32)
    abuf[CONV_HALO:CONV_HALO + tc, :] = val * jax.nn.sigmoid(gate)
    off = CONV_HALO - (CONV_WIDTH - 1)
    for c in range(tc // CONV_CHUNK):
        acc = jnp.broadcast_to(b_ref[...], (CONV_CHUNK, B_WIDTH))
        for j in range(CONV_WIDTH):
            r0 = c * CONV_CHUNK + off + j
            acc = acc + w_ref[j:j + 1, :] * abuf[r0:r0 + CONV_CHUNK, :]
        mu = jnp.mean(acc, axis=-1, keepdims=True)
        cen = acc - mu
        var = jnp.mean(cen * cen, axis=-1, keepdims=True)
        yn = cen * lax.rsqrt(var + NORM_EPS) * g_ref[...] + beta_ref[...]
        o_ref[c * CONV_CHUNK:(c + 1) * CONV_CHUNK, :] = (yn * jax.nn.sigmoid(yn)).astype(BF16)


def _conv(conv_in, w, b, g, beta, batch, seq):
    t = conv_in.shape[0]
    tc = CONV_TILE
    nt = seq // tc
    w_pad = jnp.concatenate([w, jnp.zeros((CONV_HALO - CONV_WIDTH, B_WIDTH), F32)], axis=0)
    vec = pl.BlockSpec((1, B_WIDTH), lambda bb, n: (0, 0))
    return pl.pallas_call(
        _conv_kernel,
        grid=(batch, nt),
        in_specs=[
            pl.BlockSpec((tc, B_WIDTH), lambda bb, n: (bb * nt + n, 0)),
            pl.BlockSpec((tc, B_WIDTH), lambda bb, n: (bb * nt + n, 1)),
            pl.BlockSpec((CONV_HALO, B_WIDTH), lambda bb, n: (0, 0)),
            vec, vec, vec,
        ],
        out_specs=pl.BlockSpec((tc, B_WIDTH), lambda bb, n: (bb * nt + n, 0)),
        out_shape=jax.ShapeDtypeStruct((t, B_WIDTH), BF16),
        scratch_shapes=[pltpu.VMEM((tc + CONV_HALO, B_WIDTH), F32)],
        compiler_params=_cparams(("parallel", "arbitrary")),
        name="conformer_conv",
    )(conv_in, conv_in, w_pad, b.reshape(1, -1), g.reshape(1, -1), beta.reshape(1, -1))


def _even_out_kernel(h_ref, o1_ref, o2_ref, o3_ref, l1_ref, l2_ref, l3_ref, bc_ref, w_ref, out_ref):
    l1, l2, l3 = l1_ref[...], l2_ref[...], l3_ref[...]
    m = jnp.maximum(jnp.maximum(l1, l2), l3)
    e1, e2, e3 = jnp.exp(l1 - m), jnp.exp(l2 - m), jnp.exp(l3 - m)
    inv = 1.0 / (e1 + e2 + e3)
    row = lax.broadcasted_iota(I32, (LANES, A_WIDTH), 0)
    col = lax.broadcasted_iota(I32, (LANES, A_WIDTH), 1)
    expand = (col // A_HEAD_DIM == row).astype(F32)
    a = jnp.zeros(o1_ref.shape, F32)
    for e, o_ref in ((e1, o1_ref), (e2, o2_ref), (e3, o3_ref)):
        wfull = jnp.dot(e * inv, expand, precision=lax.Precision.HIGHEST, preferred_element_type=F32)
        a = a + wfull * o_ref[...].astype(F32)
    acc = jnp.dot(a.astype(BF16), w_ref[0:A_WIDTH, :], preferred_element_type=F32)
    acc = acc + jnp.dot(bc_ref[...], w_ref[A_WIDTH:, :], preferred_element_type=F32)
    out_ref[...] = h_ref[...] + acc


def _even_out(h, outs, lses, bconv, w_bf16):
    t = h.shape[0]
    tm = ROW_TILE
    row = lambda width: pl.BlockSpec((tm, width), lambda i: (i, 0))
    return pl.pallas_call(
        _even_out_kernel,
        grid=(t // tm,),
        in_specs=[row(D_MODEL), row(A_WIDTH), row(A_WIDTH), row(A_WIDTH), row(LANES), row(LANES), row(LANES),
                  row(B_WIDTH), pl.BlockSpec((D_MODEL, D_MODEL), lambda i: (0, 0))],
        out_specs=row(D_MODEL),
        out_shape=jax.ShapeDtypeStruct((t, D_MODEL), F32),
        compiler_params=_cparams(("parallel",)),
        name="even_out_proj",
    )(h, *outs, *lses, bconv, w_bf16)


def _odd_in_kernel(h_ref, g_ref, w_ref, cos_ref, sin_ref, o_ref):
    j = pl.program_id(0)
    u = _rms(h_ref[...], g_ref[...]).astype(BF16)
    acc = jnp.dot(u, w_ref[...], preferred_element_type=F32)

    @pl.when(j == 0)
    def _():
        cos, sin = cos_ref[...], sin_ref[...]
        half = C_QK_DIM // 2
        for hd in range(2 * C_HEADS):
            x1 = acc[:, hd * C_QK_DIM:hd * C_QK_DIM + half]
            x2 = acc[:, hd * C_QK_DIM + half:(hd + 1) * C_QK_DIM]
            r1 = x1 * cos - x2 * sin
            r2 = x2 * cos + x1 * sin
            if hd >= C_HEADS:
                r1 = r1 * (C_QK_DIM ** -0.5)
                r2 = r2 * (C_QK_DIM ** -0.5)
            o_ref[:, hd * C_QK_DIM:hd * C_QK_DIM + half] = r1.astype(BF16)
            o_ref[:, hd * C_QK_DIM + half:(hd + 1) * C_QK_DIM] = r2.astype(BF16)

    @pl.when(j > 0)
    def _():
        o_ref[...] = acc.astype(BF16)


def _odd_in(h, gain, w_bf16, cos, sin, seq):
    t = h.shape[0]
    tm = ROW_TILE
    tn = 2 * C_QK_WIDTH
    nseq = seq // tm
    tab = pl.BlockSpec((tm, C_QK_DIM // 2), lambda j, i: (i % nseq, 0))
    return pl.pallas_call(
        _odd_in_kernel,
        grid=(ODD_IN_WIDTH // tn, t // tm),
        in_specs=[
            pl.BlockSpec((tm, D_MODEL), lambda j, i: (i, 0)),
            pl.BlockSpec((1, D_MODEL), lambda j, i: (0, 0)),
            pl.BlockSpec((D_MODEL, tn), lambda j, i: (0, j)),
            tab, tab,
        ],
        out_specs=pl.BlockSpec((tm, tn), lambda j, i: (i, j)),
        out_shape=jax.ShapeDtypeStruct((t, ODD_IN_WIDTH), BF16),
        compiler_params=_cparams(("parallel", "parallel")),
        name="odd_in_proj",
    )(h, gain.reshape(1, D_MODEL), w_bf16, cos, sin)


def _ret_kernel(ld_ref, q_ref, k_ref, v_ref, g_ref, o_ref, state):
    hd = pl.program_id(1)
    c = pl.program_id(2)
    ch = RET_CHUNK

    @pl.when(c == 0)
    def _():
        state[...] = jnp.zeros(state.shape, F32)

    ld = ld_ref[hd]
    q, k, v = q_ref[...], k_ref[...], v_ref[...]
    ii = lax.broadcasted_iota(I32, (ch, ch), 0)
    jj = lax.broadcasted_iota(I32, (ch, ch), 1)
    diff = (ii - jj).astype(F32)
    intra = jnp.where(diff >= 0, jnp.exp(ld * jnp.maximum(diff, 0.0)), 0.0)
    s = lax.dot_general(q, k, (((1,), (1,)), ((), ())), preferred_element_type=F32) * intra
    inner = jnp.dot(s.astype(BF16), v, preferred_element_type=F32)
    pos = lax.broadcasted_iota(I32, (ch, 1), 0).astype(F32)
    q_decay = jnp.exp(ld * (pos + 1.0))
    k_decay = jnp.exp(ld * (ch - 1.0 - pos))
    chunk_decay = jnp.exp(ld * jnp.full((1, 1), float(ch), F32))
    st = state[...]
    cross = jnp.dot(q, st.astype(BF16), preferred_element_type=F32) * q_decay
    kd_t = jnp.transpose(k.astype(F32) * k_decay).astype(BF16)
    state[...] = st * chunk_decay + jnp.dot(kd_t, v, preferred_element_type=F32)
    out = inner + cross
    mu = jnp.mean(out, axis=-1, keepdims=True)
    cen = out - mu
    var = jnp.mean(cen * cen, axis=-1, keepdims=True)
    o = cen * lax.rsqrt(var + NORM_EPS)
    gf = g_ref[...].astype(F32)
    o_ref[...] = (gf * jax.nn.sigmoid(gf) * o).astype(BF16)


def _retention(proj, log_decay, batch, seq):
    t = proj.shape[0]
    ch = RET_CHUNK
    nc = seq // ch
    qk_blocks = C_QK_WIDTH // C_QK_DIM
    v_blocks = C_V_WIDTH // C_V_DIM
    v0 = 2 * C_QK_WIDTH // C_V_DIM
    grid_spec = pltpu.PrefetchScalarGridSpec(
        num_scalar_prefetch=1,
        grid=(batch, C_HEADS, nc),
        in_specs=[
            pl.BlockSpec((ch, C_QK_DIM), lambda b, hd, c, ld: (b * nc + c, hd)),
            pl.BlockSpec((ch, C_QK_DIM), lambda b, hd, c, ld: (b * nc + c, qk_blocks + hd)),
            pl.BlockSpec((ch, C_V_DIM), lambda b, hd, c, ld: (b * nc + c, v0 + hd)),
            pl.BlockSpec((ch, C_V_DIM), lambda b, hd, c, ld: (b * nc + c, v0 + v_blocks + hd)),
        ],
        out_specs=pl.BlockSpec((ch, C_V_DIM), lambda b, hd, c, ld: (b * nc + c, hd)),
        scratch_shapes=[pltpu.VMEM((C_QK_DIM, C_V_DIM), F32)],
    )
    return pl.pallas_call(
        _ret_kernel,
        grid_spec=grid_spec,
        out_shape=jax.ShapeDtypeStruct((t, C_V_WIDTH), BF16),
        compiler_params=_cparams(("parallel", "parallel", "arbitrary")),
        name="retention",
    )(log_decay, proj, proj, proj, proj)


def _odd_out_kernel(h_ref, y_ref, w_ref, out_ref):
    out_ref[...] = h_ref[...] + jnp.dot(y_ref[...], w_ref[...], preferred_element_type=F32)


def _odd_out(h, y, w_bf16):
    t = h.shape[0]
    tm = ROW_TILE
    return pl.pallas_call(
        _odd_out_kernel,
        grid=(t // tm,),
        in_specs=[pl.BlockSpec((tm, D_MODEL), lambda i: (i, 0)),
                  pl.BlockSpec((tm, C_V_WIDTH), lambda i: (i, 0)),
                  pl.BlockSpec((C_V_WIDTH, D_MODEL), lambda i: (0, 0))],
        out_specs=pl.BlockSpec((tm, D_MODEL), lambda i: (i, 0)),
        out_shape=jax.ShapeDtypeStruct((t, D_MODEL), F32),
        compiler_params=_cparams(("parallel",)),
        name="odd_out_proj",
    )(h, y, w_bf16)


def _router_kernel(h_ref, g_ref, wr_ref, br_ref, x2_ref, meta_ref, gates_ref, cnt_ref, base_ref):
    i = pl.program_id(0)
    tm = h_ref.shape[0]

    @pl.when(i == 0)
    def _():
        base_ref[...] = jnp.zeros(base_ref.shape, F32)

    x2 = _rms(h_ref[...], g_ref[...])
    x2_ref[...] = x2
    logits = jnp.dot(x2, wr_ref[...], precision=lax.Precision.HIGHEST, preferred_element_type=F32) + br_ref[...]
    lane = lax.broadcasted_iota(I32, (tm, LANES), 1)
    is_grp = lane < MOE_GROUPS
    lg = jnp.where(is_grp, logits, -jnp.inf)
    gmax = jnp.max(lg, axis=-1, keepdims=True)
    gsum = jnp.sum(jnp.where(is_grp, jnp.exp(logits - gmax), 0.0), axis=-1, keepdims=True)
    gp = 1.0 / gsum
    gi = jnp.min(jnp.where(lg == gmax, lane, LANES), axis=-1, keepdims=True)
    ex = lane - ROUTER_LANE0
    in_grp = (ex >= 0) & (ex < MOE_EXPERTS) & ((ex // MOE_EXPERTS_PER_GROUP) == gi)
    sel = jnp.where(in_grp, logits, -jnp.inf)
    v1 = jnp.max(sel, axis=-1, keepdims=True)
    i1 = jnp.min(jnp.where(sel == v1, lane, LANES), axis=-1, keepdims=True)
    sel2 = jnp.where(lane == i1, -jnp.inf, sel)
    v2 = jnp.max(sel2, axis=-1, keepdims=True)
    i2 = jnp.min(jnp.where(sel2 == v2, lane, LANES), axis=-1, keepdims=True)
    tt = jnp.exp(v2 - v1)
    g0 = gp / (1.0 + tt)
    g1 = gp * tt / (1.0 + tt)
    oh0 = (lane == i1).astype(F32)
    oh1 = (lane == i2).astype(F32)
    oh = oh0 + oh1
    rr = lax.broadcasted_iota(I32, (tm, tm), 0)
    cc = lax.broadcasted_iota(I32, (tm, tm), 1)
    tri = (cc < rr).astype(BF16)
    prefix = jnp.dot(tri, oh.astype(BF16), preferred_element_type=F32)
    tot = base_ref[...] + prefix
    rank0 = jnp.sum(oh0 * tot, axis=-1, keepdims=True).astype(I32)
    rank1 = jnp.sum(oh1 * tot, axis=-1, keepdims=True).astype(I32)
    base_ref[...] = base_ref[...] + jnp.sum(oh, axis=0, keepdims=True)
    meta = jnp.where(lane == 0, i1 - ROUTER_LANE0, 0)
    meta = jnp.where(lane == 1, i2 - ROUTER_LANE0, meta)
    meta = jnp.where(lane == 2, rank0, meta)
    meta = jnp.where(lane == 3, rank1, meta)
    meta_ref[...] = meta
    gates_ref[...] = jnp.where(lane == 0, g0, jnp.where(lane == 1, g1, 0.0))
    cnt_ref[...] = base_ref[...]


def _router(h, gain, wr, br):
    t = h.shape[0]
    tm = ROW_TILE
    row = lambda width: pl.BlockSpec((tm, width), lambda i: (i, 0))
    fixed = lambda shape: pl.BlockSpec(shape, lambda i: (0, 0))
    return pl.pallas_call(
        _router_kernel,
        grid=(t // tm,),
        in_specs=[row(D_MODEL), fixed((1, D_MODEL)), fixed((D_MODEL, LANES)), fixed((1, LANES))],
        out_specs=[row(D_MODEL), row(LANES), row(LANES), fixed((1, LANES))],
        out_shape=[
            jax.ShapeDtypeStruct((t, D_MODEL), F32),
            jax.ShapeDtypeStruct((t, LANES), I32),
            jax.ShapeDtypeStruct((t, LANES), F32),
            jax.ShapeDtypeStruct((1, LANES), F32),
        ],
        scratch_shapes=[pltpu.VMEM((1, LANES), F32)],
        compiler_params=_cparams(("arbitrary",)),
        name="moe_router",
    )(h, gain.reshape(1, D_MODEL), wr, br)


def _scatter_kernel(dest_ref, x_ref, xs_in_ref, xs_ref, sem):
    del xs_in_ref
    i = pl.program_id(0)
    ts = x_ref.shape[0]

    def issue(tok, carry):
        for k in range(2):
            d = dest_ref[(i * ts + tok) * 2 + k]
            pltpu.make_async_copy(x_ref.at[pl.ds(tok, 1), :], xs_ref.at[pl.ds(d, 1), :], sem).start()
        return carry

    lax.fori_loop(0, ts, issue, 0, unroll=8)
    for _ in range(2):
        pltpu.make_async_copy(x_ref, xs_ref.at[pl.ds(0, ts), :], sem).wait()


def _scatter_rows(dest_flat, x2, n_rows):
    t = x2.shape[0]
    ts = DMA_TILE
    xs0 = jnp.zeros((n_rows, D_MODEL), F32)
    grid_spec = pltpu.PrefetchScalarGridSpec(
        num_scalar_prefetch=1,
        grid=(t // ts,),
        in_specs=[pl.BlockSpec((ts, D_MODEL), lambda i, d: (i, 0)),
                  pl.BlockSpec(memory_space=pl.ANY)],
        out_specs=pl.BlockSpec(memory_space=pl.ANY),
        scratch_shapes=[pltpu.SemaphoreType.DMA(())],
    )
    return pl.pallas_call(
        _scatter_kernel,
        grid_spec=grid_spec,
        out_shape=jax.ShapeDtypeStruct((n_rows, D_MODEL), F32),
        input_output_aliases={2: 0},
        compiler_params=_cparams(("arbitrary",)),
        name="moe_scatter_rows",
    )(dest_flat, x2, xs0)


def _expert_kernel(be_ref, nu_ref, x_ref, wg_ref, wu_ref, wd_ref, y_ref):
    i = pl.program_id(0)

    @pl.when(i < nu_ref[0])
    def _():
        x = x_ref[...].astype(BF16)
        a = jnp.dot(x, wg_ref[...], preferred_element_type=F32)
        b = jnp.dot(x, wu_ref[...], preferred_element_type=F32)
        hid = (a * jax.nn.sigmoid(a) * b).astype(BF16)
        y_ref[...] = jnp.dot(hid, wd_ref[...], preferred_element_type=F32)

    @pl.when(i >= nu_ref[0])
    def _():
        y_ref[...] = jnp.zeros(y_ref.shape, F32)


def _experts(block_expert, n_used, xs, wg, wu, wd):
    n_rows = xs.shape[0]
    blk = MOE_BLOCK
    grid_spec = pltpu.PrefetchScalarGridSpec(
        num_scalar_prefetch=2,
        grid=(n_rows // blk,),
        in_specs=[
            pl.BlockSpec((blk, D_MODEL), lambda i, be, nu: (i, 0)),
            pl.BlockSpec((None, D_MODEL, EXPERT_HIDDEN), lambda i, be, nu: (be[i], 0, 0)),
            pl.BlockSpec((None, D_MODEL, EXPERT_HIDDEN), lambda i, be, nu: (be[i], 0, 0)),
            pl.BlockSpec((None, EXPERT_HIDDEN, D_MODEL), lambda i, be, nu: (be[i], 0, 0)),
        ],
        out_specs=pl.BlockSpec((blk, D_MODEL), lambda i, be, nu: (i, 0)),
    )
    return pl.pallas_call(
        _expert_kernel,
        grid_spec=grid_spec,
        out_shape=jax.ShapeDtypeStruct((n_rows, D_MODEL), F32),
        compiler_params=_cparams(("arbitrary",)),
        name="moe_experts",
    )(block_expert, n_used, xs, wg, wu, wd)


def _combine_kernel(dest_ref, h_ref, gates_ref, ys_ref, *rest, final):
    if final:
        fg_ref, out_ref, buf, sem = rest
    else:
        out_ref, buf, sem = rest
    i = pl.program_id(0)
    ts = h_ref.shape[0]

    def issue(tok, carry):
        for k in range(2):
            d = dest_ref[(i * ts + tok) * 2 + k]
            pltpu.make_async_copy(ys_ref.at[pl.ds(d, 1), :], buf.at[k, pl.ds(tok, 1), :], sem).start()
        return carry

    lax.fori_loop(0, ts, issue, 0, unroll=8)
    for k in range(2):
        pltpu.make_async_copy(ys_ref.at[pl.ds(0, ts), :], buf.at[k], sem).wait()
    gates = gates_ref[...]
    out = h_ref[...] + (gates[:, 0:1] * buf[0] + gates[:, 1:2] * buf[1])
    if final:
        out = _rms(out, fg_ref[...])
    out_ref[...] = out


def _combine(dest_flat, h, gates, ys, final_gain=None):
    t = h.shape[0]
    ts = DMA_TILE
    final = final_gain is not None
    in_specs = [pl.BlockSpec((ts, D_MODEL), lambda i, d: (i, 0)),
                pl.BlockSpec((ts, LANES), lambda i, d: (i, 0)),
                pl.BlockSpec(memory_space=pl.ANY)]
    args = [dest_flat, h, gates, ys]
    if final:
        in_specs.append(pl.BlockSpec((1, D_MODEL), lambda i, d: (0, 0)))
        args.append(final_gain.reshape(1, D_MODEL))
    grid_spec = pltpu.PrefetchScalarGridSpec(
        num_scalar_prefetch=1,
        grid=(t // ts,),
        in_specs=in_specs,
        out_specs=pl.BlockSpec((ts, D_MODEL), lambda i, d: (i, 0)),
        scratch_shapes=[pltpu.VMEM((2, ts, D_MODEL), F32), pltpu.SemaphoreType.DMA(())],
    )
    return pl.pallas_call(
        functools.partial(_combine_kernel, final=final),
        grid_spec=grid_spec,
        out_shape=jax.ShapeDtypeStruct((t, D_MODEL), F32),
        compiler_params=_cparams(("arbitrary",)),
        name="moe_combine_final" if final else "moe_combine",
    )(*args)


def _moe(h, gain, w_r1, b_r1, w_r2, b_r2, wg, wu, wd, final_gain=None):
    t = h.shape[0]
    pad_w = jnp.zeros((D_MODEL, LANES - MOE_GROUPS - MOE_EXPERTS), F32)
    wr = jnp.concatenate([w_r1, w_r2, pad_w], axis=1)
    br = jnp.concatenate([b_r1, b_r2, jnp.zeros((LANES - MOE_GROUPS - MOE_EXPERTS,), F32)]).reshape(1, LANES)
    x2, meta, gates, counts = _router(h, gain, wr, br)
    cnt = counts[0, ROUTER_LANE0:ROUTER_LANE0 + MOE_EXPERTS].astype(I32)
    padded = (cnt + MOE_BLOCK - 1) // MOE_BLOCK * MOE_BLOCK
    pad_end = jnp.cumsum(padded)
    pad_start = pad_end - padded
    n_rows = t * 2 + MOE_EXPERTS * MOE_BLOCK
    n_blocks = n_rows // MOE_BLOCK
    dest = pad_start[meta[:, 0:2]] + meta[:, 2:4]
    dest_flat = dest.reshape(-1).astype(I32)
    block_expert = jnp.minimum(
        jnp.searchsorted(pad_end, jnp.arange(n_blocks, dtype=I32) * MOE_BLOCK, side="right"),
        MOE_EXPERTS - 1).astype(I32)
    n_used = (pad_end[-1:] // MOE_BLOCK).astype(I32)
    xs = _scatter_rows(dest_flat, x2, n_rows)
    ys = _experts(block_expert, n_used, xs, wg, wu, wd)
    return _combine(dest_flat, h, gates, ys, final_gain)


def kernel(x, mix_norm, ffn_norm, final_norm, even_w_in, even_w_out, conv_w, conv_b, conv_norm_g, conv_norm_b,
           odd_w_in, odd_w_out, router_w1, router_b1, router_w2, router_b2, expert_w_gate, expert_w_up,
           expert_w_down):
    batch, seq, d = x.shape
    assert d == D_MODEL and seq % (ATTN_BLOCK * max(dl for _, dl in A_BRANCHES)) == 0
    assert all(w // dl == ATTN_BLOCK for w, dl in A_BRANCHES)
    t = batch * seq
    h = x.reshape(t, d)

    qkv, conv_in = _even_in(h, mix_norm[0], even_w_in[0].astype(BF16), _attn_rope_tables(seq), seq)
    outs, lses = [], []
    for _, dil in A_BRANCHES:
        o_d, lse_d = _attn_branch(qkv.reshape(t // dil, dil * 3 * A_WIDTH), batch, seq, dil)
        outs.append(o_d.reshape(t, A_WIDTH))
        lses.append(lse_d.reshape(t, LANES))
    bconv = _conv(conv_in, conv_w[0], conv_b[0], conv_norm_g[0], conv_norm_b[0], batch, seq)
    h = _even_out(h, outs, lses, bconv, even_w_out[0].astype(BF16))
    h = _moe(h, ffn_norm[0], router_w1[0], router_b1[0], router_w2[0], router_b2[0],
             expert_w_gate[0].astype(BF16), expert_w_up[0].astype(BF16), expert_w_down[0].astype(BF16))

    inv_freq = RET_ROT_THETA ** (-jnp.linspace(0.0, 1.0, C_QK_DIM // 2, dtype=F32))
    ang = jnp.arange(seq, dtype=F32)[:, None] * inv_freq[None, :]
    proj = _odd_in(h, mix_norm[1], odd_w_in[0].astype(BF16), jnp.cos(ang), jnp.sin(ang), seq)
    log_decay = jnp.log(1.0 - jnp.exp2(-5.0 - jnp.arange(C_HEADS, dtype=F32)))
    y = _retention(proj, log_decay, batch, seq)
    h = _odd_out(h, y, odd_w_out[0].astype(BF16))
    out = _moe(h, ffn_norm[1], router_w1[1], router_b1[1], router_w2[1], router_b2[1],
               expert_w_gate[1].astype(BF16), expert_w_up[1].astype(BF16), expert_w_down[1].astype(BF16),
               final_gain=final_norm)
    return out.reshape(batch, seq, d)
```

```python
import functools

import jax
import jax.numpy as jnp
from jax import lax
from jax.experimental import pallas as pl
from jax.experimental.pallas import tpu as pltpu

F32 = jnp.float32
BF16 = jnp.bfloat16
I32 = jnp.int32

NORM_EPS = 1e-6
NEG_INF = -1e30

D_MODEL = 1024
A_HEADS = 8
A_HEAD_DIM = 64
A_WIDTH = A_HEADS * A_HEAD_DIM
A_BRANCHES = ((128, 1), (512, 4), (2048, 16))
ATTN_BLOCK = 128
ROPE_THETA = 500000.0
ROPE_DIM = A_HEAD_DIM // 4
B_WIDTH = D_MODEL - A_WIDTH
CONV_WIDTH = 31
C_HEADS = 4
C_QK_DIM = 256
C_V_DIM = 512
C_QK_WIDTH = C_HEADS * C_QK_DIM
C_V_WIDTH = C_HEADS * C_V_DIM
RET_CHUNK = 128
RET_ROT_THETA = 10000.0
MOE_GROUPS = 4
MOE_EXPERTS_PER_GROUP = 8
MOE_EXPERTS = MOE_GROUPS * MOE_EXPERTS_PER_GROUP
EXPERT_HIDDEN = 512
MOE_BLOCK = 128
EVEN_IN_WIDTH = 3 * A_WIDTH + 2 * B_WIDTH
ODD_IN_WIDTH = 2 * C_QK_WIDTH + 2 * C_V_WIDTH

LANES = 128
ROW_TILE = 512
CONV_TILE = 512
CONV_HALO = 32
CONV_CHUNK = 64
ROUTER_LANE0 = MOE_GROUPS
DMA_TILE = 256
VMEM_LIMIT = 56 * 1024 * 1024


def _cparams(sem):
    return pltpu.CompilerParams(dimension_semantics=sem, vmem_limit_bytes=VMEM_LIMIT)


def _cast_once(w_ref, wb_ref, first):
    @pl.when(first)
    def _():
        wb_ref[...] = w_ref[...].astype(BF16)


def _rms(x, gain):
    ms = jnp.mean(x * x, axis=-1, keepdims=True)
    return x * lax.rsqrt(ms + NORM_EPS) * gain


def _even_in_kernel(h_ref, g_ref, w_ref, c_ref, s1_ref, s2_ref, qkv_ref, conv_ref, wb_ref):
    _cast_once(w_ref, wb_ref, pl.program_id(0) == 0)
    u = _rms(h_ref[...], g_ref[...]).astype(BF16)
    acc = jnp.dot(u, wb_ref[...], preferred_element_type=F32)
    c, s1, s2 = c_ref[...], s1_ref[...], s2_ref[...]
    for j in range(2 * A_WIDTH // LANES):
        xg = acc[:, j * LANES:(j + 1) * LANES]
        if j < A_WIDTH // LANES:
            xg = xg * (A_HEAD_DIM ** -0.5)
        r = xg * c + pltpu.roll(xg, LANES - ROPE_DIM // 2, 1) * s1 + pltpu.roll(xg, ROPE_DIM // 2, 1) * s2
        qkv_ref[:, j * LANES:(j + 1) * LANES] = r.astype(BF16)
    qkv_ref[:, 2 * A_WIDTH:3 * A_WIDTH] = acc[:, 2 * A_WIDTH:3 * A_WIDTH].astype(BF16)
    conv_ref[...] = acc[:, 3 * A_WIDTH:].astype(BF16)


def _even_in(h, gain, w, tabs, seq):
    t = h.shape[0]
    tm = ROW_TILE
    nseq = seq // tm
    tab_spec = pl.BlockSpec((tm, LANES), lambda i: (i % nseq, 0))
    return pl.pallas_call(
        _even_in_kernel,
        grid=(t // tm,),
        in_specs=[
            pl.BlockSpec((tm, D_MODEL), lambda i: (i, 0)),
            pl.BlockSpec((1, D_MODEL), lambda i: (0, 0)),
            pl.BlockSpec((None, D_MODEL, EVEN_IN_WIDTH), lambda i: (0, 0, 0)),
            tab_spec, tab_spec, tab_spec,
        ],
        out_specs=[
            pl.BlockSpec((tm, 3 * A_WIDTH), lambda i: (i, 0)),
            pl.BlockSpec((tm, 2 * B_WIDTH), lambda i: (i, 0)),
        ],
        out_shape=[
            jax.ShapeDtypeStruct((t, 3 * A_WIDTH), BF16),
            jax.ShapeDtypeStruct((t, 2 * B_WIDTH), BF16),
        ],
        scratch_shapes=[pltpu.VMEM((D_MODEL, EVEN_IN_WIDTH), BF16)],
        compiler_params=_cparams(("arbitrary",)),
        name="even_in_proj",
    )(h, gain.reshape(1, D_MODEL), w, *tabs)


def _attn_rope_tables(seq):
    half = ROPE_DIM // 2
    inv_freq = ROPE_THETA ** (-jnp.arange(0, ROPE_DIM, 2, dtype=F32) / ROPE_DIM)
    ang = jnp.arange(seq, dtype=F32)[:, None] * inv_freq[None, :]
    cos, sin = jnp.cos(ang), jnp.sin(ang)
    rest = A_HEAD_DIM - ROPE_DIM
    ones = jnp.ones((seq, rest), F32)
    z_rest = jnp.zeros((seq, rest), F32)
    z_half = jnp.zeros((seq, half), F32)
    c = jnp.concatenate([cos, cos, ones], axis=1)
    s1 = jnp.concatenate([-sin, z_half, z_rest], axis=1)
    s2 = jnp.concatenate([z_half, sin, z_rest], axis=1)
    rep = LANES // A_HEAD_DIM
    return tuple(jnp.tile(a, (1, rep)) for a in (c, s1, s2))


def _attn_kernel(q_ref, kp_ref, kc_ref, vp_ref, vc_ref, o_ref, lse_ref):
    n = pl.program_id(2)
    blk = ATTN_BLOCK
    qi = lax.broadcasted_iota(I32, (blk, 2 * blk), 0)
    kj = lax.broadcasted_iota(I32, (blk, 2 * blk), 1)
    dist = blk + qi - kj
    valid = (dist >= 0) & (dist <= blk) & ((kj >= blk) | (n > 0))
    valid2 = jnp.concatenate([valid, valid], axis=0)
    lane = lax.broadcasted_iota(I32, (blk, LANES), 1)
    lo = lane < A_HEAD_DIM
    lse_tile = jnp.zeros((blk, LANES), F32)
    for g in range(A_WIDTH // LANES):
        sl = slice(g * LANES, (g + 1) * LANES)
        q = q_ref[:, sl].astype(F32)
        q2 = jnp.concatenate([jnp.where(lo, q, 0.0), jnp.where(lo, 0.0, q)], axis=0).astype(BF16)
        kk = jnp.concatenate([kp_ref[:, sl], kc_ref[:, sl]], axis=0)
        vv = jnp.concatenate([vp_ref[:, sl], vc_ref[:, sl]], axis=0)
        s = lax.dot_general(q2, kk, (((1,), (1,)), ((), ())), preferred_element_type=F32)
        s = jnp.where(valid2, s, NEG_INF)
        m = jnp.max(s, axis=-1, keepdims=True)
        e = jnp.exp(s - m)
        den = jnp.sum(e, axis=-1, keepdims=True)
        pv = jnp.dot(e.astype(BF16), vv, preferred_element_type=F32) / den
        o_ref[:, sl] = jnp.where(lo, pv[:blk], pv[blk:]).astype(BF16)
        lse = m + jnp.log(den)
        lse_tile = jnp.where(lane == 2 * g, lse[:blk], lse_tile)
        lse_tile = jnp.where(lane == 2 * g + 1, lse[blk:], lse_tile)
    lse_ref[...] = lse_tile


def _attn_branch(qkv_d, batch, seq, dil):
    blk = ATTN_BLOCK
    nb = seq // dil // blk
    rows = qkv_d.shape[0]

    def cur(col):
        return pl.BlockSpec((blk, A_WIDTH), lambda b, r, n: (b * nb + n, 3 * r + col))

    def prev(col):
        return pl.BlockSpec((blk, A_WIDTH), lambda b, r, n: (b * nb + jnp.maximum(n - 1, 0), 3 * r + col))

    return pl.pallas_call(
        _attn_kernel,
        grid=(batch, dil, nb),
        in_specs=[cur(0), prev(1), cur(1), prev(2), cur(2)],
        out_specs=[
            pl.BlockSpec((blk, A_WIDTH), lambda b, r, n: (b * nb + n, r)),
            pl.BlockSpec((blk, LANES), lambda b, r, n: (b * nb + n, r)),
        ],
        out_shape=[
            jax.ShapeDtypeStruct((rows, dil * A_WIDTH), BF16),
            jax.ShapeDtypeStruct((rows, dil * LANES), F32),
        ],
        compiler_params=_cparams(("parallel", "parallel", "arbitrary")),
        name=f"dilated_attn_d{dil}",
    )(qkv_d, qkv_d, qkv_d, qkv_d, qkv_d)


def _conv_kernel(val_ref, gate_ref, w_ref, b_ref, g_ref, beta_ref, o_ref, abuf):
    n = pl.program_id(1)
    tc = CONV_TILE

    @pl.when(n == 0)
    def _():
        abuf[0:CONV_HALO, :] = jnp.zeros((CONV_HALO, B_WIDTH), F32)

    @pl.when(n > 0)
    def _():
        abuf[0:CONV_HALO, :] = abuf[tc:tc + CONV_HALO, :]

    val = val_ref[...].astype(F32)
    gate = gate_ref[...].astype(F32)
    abuf[CONV_HALO:CONV_HALO + tc, :] = val * jax.nn.sigmoid(gate)
    off = CONV_HALO - (CONV_WIDTH - 1)
    for c in range(tc // CONV_CHUNK):
        acc = jnp.broadcast_to(b_ref[...], (CONV_CHUNK, B_WIDTH))
        for j in range(CONV_WIDTH):
            r0 = c * CONV_CHUNK + off + j
            acc = acc + w_ref[j:j + 1, :] * abuf[r0:r0 + CONV_CHUNK, :]
        mu = jnp.mean(acc, axis=-1, keepdims=True)
        cen = acc - mu
        var = jnp.mean(cen * cen, axis=-1, keepdims=True)
        yn = cen * lax.rsqrt(var + NORM_EPS) * g_ref[...] + beta_ref[...]
        o_ref[c * CONV_CHUNK:(c + 1) * CONV_CHUNK, :] = (yn * jax.nn.sigmoid(yn)).astype(BF16)


def _conv(conv_in, w, b, g, beta, batch, seq):
    t = conv_in.shape[0]
    tc = CONV_TILE
    nt = seq // tc
    w_pad = jnp.concatenate([w, jnp.zeros((CONV_HALO - CONV_WIDTH, B_WIDTH), F32)], axis=0)
    vec = pl.BlockSpec((1, B_WIDTH), lambda bb, n: (0, 0))
    return pl.pallas_call(
        _conv_kernel,
        grid=(batch, nt),
        in_specs=[
            pl.BlockSpec((tc, B_WIDTH), lambda bb, n: (bb * nt + n, 0)),
            pl.BlockSpec((tc, B_WIDTH), lambda bb, n: (bb * nt + n, 1)),
            pl.BlockSpec((CONV_HALO, B_WIDTH), lambda bb, n: (0, 0)),
            vec, vec, vec,
        ],
        out_specs=pl.BlockSpec((tc, B_WIDTH), lambda bb, n: (bb * nt + n, 0)),
        out_shape=jax.ShapeDtypeStruct((t, B_WIDTH), BF16),
        scratch_shapes=[pltpu.VMEM((tc + CONV_HALO, B_WIDTH), F32)],
        compiler_params=_cparams(("parallel", "arbitrary")),
        name="conformer_conv",
    )(conv_in, conv_in, w_pad, b.reshape(1, -1), g.reshape(1, -1), beta.reshape(1, -1))


def _even_out_kernel(h_ref, o1_ref, o2_ref, o3_ref, l1_ref, l2_ref, l3_ref, bc_ref, w_ref, out_ref, wb_ref):
    _cast_once(w_ref, wb_ref, pl.program_id(0) == 0)
    l1, l2, l3 = l1_ref[...], l2_ref[...], l3_ref[...]
    m = jnp.maximum(jnp.maximum(l1, l2), l3)
    e1, e2, e3 = jnp.exp(l1 - m), jnp.exp(l2 - m), jnp.exp(l3 - m)
    inv = 1.0 / (e1 + e2 + e3)
    row = lax.broadcasted_iota(I32, (LANES, A_WIDTH), 0)
    col = lax.broadcasted_iota(I32, (LANES, A_WIDTH), 1)
    expand = (col // A_HEAD_DIM == row).astype(F32)
    a = jnp.zeros(o1_ref.shape, F32)
    for e, o_ref in ((e1, o1_ref), (e2, o2_ref), (e3, o3_ref)):
        wfull = jnp.dot(e * inv, expand, precision=lax.Precision.HIGHEST, preferred_element_type=F32)
        a = a + wfull * o_ref[...].astype(F32)
    acc = jnp.dot(a.astype(BF16), wb_ref[0:A_WIDTH, :], preferred_element_type=F32)
    acc = acc + jnp.dot(bc_ref[...], wb_ref[A_WIDTH:, :], preferred_element_type=F32)
    out_ref[...] = h_ref[...] + acc


def _even_out(h, outs, lses, bconv, w):
    t = h.shape[0]
    tm = ROW_TILE
    row = lambda width: pl.BlockSpec((tm, width), lambda i: (i, 0))
    return pl.pallas_call(
        _even_out_kernel,
        grid=(t // tm,),
        in_specs=[row(D_MODEL), row(A_WIDTH), row(A_WIDTH), row(A_WIDTH), row(LANES), row(LANES), row(LANES),
                  row(B_WIDTH), pl.BlockSpec((None, D_MODEL, D_MODEL), lambda i: (0, 0, 0))],
        out_specs=row(D_MODEL),
        out_shape=jax.ShapeDtypeStruct((t, D_MODEL), F32),
        scratch_shapes=[pltpu.VMEM((D_MODEL, D_MODEL), BF16)],
        compiler_params=_cparams(("arbitrary",)),
        name="even_out_proj",
    )(h, *outs, *lses, bconv, w)


def _odd_in_kernel(h_ref, g_ref, w_ref, cos_ref, sin_ref, o_ref, wb_ref):
    j = pl.program_id(0)
    _cast_once(w_ref, wb_ref, pl.program_id(1) == 0)
    u = _rms(h_ref[...], g_ref[...]).astype(BF16)
    acc = jnp.dot(u, wb_ref[...], preferred_element_type=F32)

    @pl.when(j == 0)
    def _():
        cos, sin = cos_ref[...], sin_ref[...]
        half = C_QK_DIM // 2
        for hd in range(2 * C_HEADS):
            x1 = acc[:, hd * C_QK_DIM:hd * C_QK_DIM + half]
            x2 = acc[:, hd * C_QK_DIM + half:(hd + 1) * C_QK_DIM]
            r1 = x1 * cos - x2 * sin
            r2 = x2 * cos + x1 * sin
            if hd >= C_HEADS:
                r1 = r1 * (C_QK_DIM ** -0.5)
                r2 = r2 * (C_QK_DIM ** -0.5)
            o_ref[:, hd * C_QK_DIM:hd * C_QK_DIM + half] = r1.astype(BF16)
            o_ref[:, hd * C_QK_DIM + half:(hd + 1) * C_QK_DIM] = r2.astype(BF16)

    @pl.when(j > 0)
    def _():
        o_ref[...] = acc.astype(BF16)


def _odd_in(h, gain, w, cos, sin, seq):
    t = h.shape[0]
    tm = ROW_TILE
    tn = 2 * C_QK_WIDTH
    nseq = seq // tm
    tab = pl.BlockSpec((tm, C_QK_DIM // 2), lambda j, i: (i % nseq, 0))
    return pl.pallas_call(
        _odd_in_kernel,
        grid=(ODD_IN_WIDTH // tn, t // tm),
        in_specs=[
            pl.BlockSpec((tm, D_MODEL), lambda j, i: (i, 0)),
            pl.BlockSpec((1, D_MODEL), lambda j, i: (0, 0)),
            pl.BlockSpec((None, D_MODEL, tn), lambda j, i: (0, 0, j)),
            tab, tab,
        ],
        out_specs=pl.BlockSpec((tm, tn), lambda j, i: (i, j)),
        out_shape=jax.ShapeDtypeStruct((t, ODD_IN_WIDTH), BF16),
        scratch_shapes=[pltpu.VMEM((D_MODEL, tn), BF16)],
        compiler_params=_cparams(("arbitrary", "arbitrary")),
        name="odd_in_proj",
    )(h, gain.reshape(1, D_MODEL), w, cos, sin)


def _ret_kernel(ld_ref, q_ref, k_ref, v_ref, g_ref, o_ref, state):
    hd = pl.program_id(1)
    c = pl.program_id(2)
    ch = RET_CHUNK

    @pl.when(c == 0)
    def _():
        state[...] = jnp.zeros(state.shape, F32)

    ld = ld_ref[hd]
    q, k, v = q_ref[...], k_ref[...], v_ref[...]
    ii = lax.broadcasted_iota(I32, (ch, ch), 0)
    jj = lax.broadcasted_iota(I32, (ch, ch), 1)
    diff = (ii - jj).astype(F32)
    intra = jnp.where(diff >= 0, jnp.exp(ld * jnp.maximum(diff, 0.0)), 0.0)
    s = lax.dot_general(q, k, (((1,), (1,)), ((), ())), preferred_element_type=F32) * intra
    inner = jnp.dot(s.astype(BF16), v, preferred_element_type=F32)
    pos = lax.broadcasted_iota(I32, (ch, 1), 0).astype(F32)
    q_decay = jnp.exp(ld * (pos + 1.0))
    k_decay = jnp.exp(ld * (ch - 1.0 - pos))
    chunk_decay = jnp.exp(ld * jnp.full((1, 1), float(ch), F32))
    st = state[...]
    cross = jnp.dot(q, st.astype(BF16), preferred_element_type=F32) * q_decay
    kd_t = jnp.transpose(k.astype(F32) * k_decay).astype(BF16)
    state[...] = st * chunk_decay + jnp.dot(kd_t, v, preferred_element_type=F32)
    out = inner + cross
    mu = jnp.mean(out, axis=-1, keepdims=True)
    cen = out - mu
    var = jnp.mean(cen * cen, axis=-1, keepdims=True)
    o = cen * lax.rsqrt(var + NORM_EPS)
    gf = g_ref[...].astype(F32)
    o_ref[...] = (gf * jax.nn.sigmoid(gf) * o).astype(BF16)


def _retention(proj, log_decay, batch, seq):
    t = proj.shape[0]
    ch = RET_CHUNK
    nc = seq // ch
    qk_blocks = C_QK_WIDTH // C_QK_DIM
    v_blocks = C_V_WIDTH // C_V_DIM
    v0 = 2 * C_QK_WIDTH // C_V_DIM
    grid_spec = pltpu.PrefetchScalarGridSpec(
        num_scalar_prefetch=1,
        grid=(batch, C_HEADS, nc),
        in_specs=[
            pl.BlockSpec((ch, C_QK_DIM), lambda b, hd, c, ld: (b * nc + c, hd)),
            pl.BlockSpec((ch, C_QK_DIM), lambda b, hd, c, ld: (b * nc + c, qk_blocks + hd)),
            pl.BlockSpec((ch, C_V_DIM), lambda b, hd, c, ld: (b * nc + c, v0 + hd)),
            pl.BlockSpec((ch, C_V_DIM), lambda b, hd, c, ld: (b * nc + c, v0 + v_blocks + hd)),
        ],
        out_specs=pl.BlockSpec((ch, C_V_DIM), lambda b, hd, c, ld: (b * nc + c, hd)),
        scratch_shapes=[pltpu.VMEM((C_QK_DIM, C_V_DIM), F32)],
    )
    return pl.pallas_call(
        _ret_kernel,
        grid_spec=grid_spec,
        out_shape=jax.ShapeDtypeStruct((t, C_V_WIDTH), BF16),
        compiler_params=_cparams(("parallel", "parallel", "arbitrary")),
        name="retention",
    )(log_decay, proj, proj, proj, proj)


def _odd_out_kernel(h_ref, y_ref, w_ref, out_ref, wb_ref):
    _cast_once(w_ref, wb_ref, pl.program_id(0) == 0)
    out_ref[...] = h_ref[...] + jnp.dot(y_ref[...], wb_ref[...], preferred_element_type=F32)


def _odd_out(h, y, w):
    t = h.shape[0]
    tm = ROW_TILE
    return pl.pallas_call(
        _odd_out_kernel,
        grid=(t // tm,),
        in_specs=[pl.BlockSpec((tm, D_MODEL), lambda i: (i, 0)),
                  pl.BlockSpec((tm, C_V_WIDTH), lambda i: (i, 0)),
                  pl.BlockSpec((None, C_V_WIDTH, D_MODEL), lambda i: (0, 0, 0))],
        out_specs=pl.BlockSpec((tm, D_MODEL), lambda i: (i, 0)),
        out_shape=jax.ShapeDtypeStruct((t, D_MODEL), F32),
        scratch_shapes=[pltpu.VMEM((C_V_WIDTH, D_MODEL), BF16)],
        compiler_params=_cparams(("arbitrary",)),
        name="odd_out_proj",
    )(h, y, w)


def _router_kernel(h_ref, g_ref, wr_ref, br_ref, x2_ref, meta_ref, gates_ref, cnt_ref, base_ref):
    i = pl.program_id(0)
    tm = h_ref.shape[0]

    @pl.when(i == 0)
    def _():
        base_ref[...] = jnp.zeros(base_ref.shape, F32)

    x2 = _rms(h_ref[...], g_ref[...])
    x2_ref[...] = x2
    logits = jnp.dot(x2, wr_ref[...], precision=lax.Precision.HIGHEST, preferred_element_type=F32) + br_ref[...]
    lane = lax.broadcasted_iota(I32, (tm, LANES), 1)
    is_grp = lane < MOE_GROUPS
    lg = jnp.where(is_grp, logits, -jnp.inf)
    gmax = jnp.max(lg, axis=-1, keepdims=True)
    gsum = jnp.sum(jnp.where(is_grp, jnp.exp(logits - gmax), 0.0), axis=-1, keepdims=True)
    gp = 1.0 / gsum
    gi = jnp.min(jnp.where(lg == gmax, lane, LANES), axis=-1, keepdims=True)
    ex = lane - ROUTER_LANE0
    in_grp = (ex >= 0) & (ex < MOE_EXPERTS) & ((ex // MOE_EXPERTS_PER_GROUP) == gi)
    sel = jnp.where(in_grp, logits, -jnp.inf)
    v1 = jnp.max(sel, axis=-1, keepdims=True)
    i1 = jnp.min(jnp.where(sel == v1, lane, LANES), axis=-1, keepdims=True)
    sel2 = jnp.where(lane == i1, -jnp.inf, sel)
    v2 = jnp.max(sel2, axis=-1, keepdims=True)
    i2 = jnp.min(jnp.where(sel2 == v2, lane, LANES), axis=-1, keepdims=True)
    tt = jnp.exp(v2 - v1)
    g0 = gp / (1.0 + tt)
    g1 = gp * tt / (1.0 + tt)
    oh0 = (lane == i1).astype(F32)
    oh1 = (lane == i2).astype(F32)
    oh = oh0 + oh1
    rr = lax.broadcasted_iota(I32, (tm, tm), 0)
    cc = lax.broadcasted_iota(I32, (tm, tm), 1)
    tri = (cc < rr).astype(BF16)
    prefix = jnp.dot(tri, oh.astype(BF16), preferred_element_type=F32)
    tot = base_ref[...] + prefix
    rank0 = jnp.sum(oh0 * tot, axis=-1, keepdims=True).astype(I32)
    rank1 = jnp.sum(oh1 * tot, axis=-1, keepdims=True).astype(I32)
    base_ref[...] = base_ref[...] + jnp.sum(oh, axis=0, keepdims=True)
    meta = jnp.where(lane == 0, i1 - ROUTER_LANE0, 0)
    meta = jnp.where(lane == 1, i2 - ROUTER_LANE0, meta)
    meta = jnp.where(lane == 2, rank0, meta)
    meta = jnp.where(lane == 3, rank1, meta)
    meta_ref[...] = meta
    gates_ref[...] = jnp.where(lane == 0, g0, jnp.where(lane == 1, g1, 0.0))
    cnt_ref[...] = base_ref[...]


def _router(h, gain, wr, br):
    t = h.shape[0]
    tm = ROW_TILE
    row = lambda width: pl.BlockSpec((tm, width), lambda i: (i, 0))
    fixed = lambda shape: pl.BlockSpec(shape, lambda i: (0, 0))
    return pl.pallas_call(
        _router_kernel,
        grid=(t // tm,),
        in_specs=[row(D_MODEL), fixed((1, D_MODEL)), fixed((D_MODEL, LANES)), fixed((1, LANES))],
        out_specs=[row(D_MODEL), row(LANES), row(LANES), fixed((1, LANES))],
        out_shape=[
            jax.ShapeDtypeStruct((t, D_MODEL), F32),
            jax.ShapeDtypeStruct((t, LANES), I32),
            jax.ShapeDtypeStruct((t, LANES), F32),
            jax.ShapeDtypeStruct((1, LANES), F32),
        ],
        scratch_shapes=[pltpu.VMEM((1, LANES), F32)],
        compiler_params=_cparams(("arbitrary",)),
        name="moe_router",
    )(h, gain.reshape(1, D_MODEL), wr, br)


def _scatter_kernel(dest_ref, x_ref, xs_in_ref, xs_ref, sem):
    del xs_in_ref
    i = pl.program_id(0)
    ts = x_ref.shape[0]

    def issue(tok, carry):
        for k in range(2):
            d = dest_ref[(i * ts + tok) * 2 + k]
            pltpu.make_async_copy(x_ref.at[pl.ds(tok, 1), :], xs_ref.at[pl.ds(d, 1), :], sem).start()
        return carry

    lax.fori_loop(0, ts, issue, 0, unroll=8)
    for _ in range(2):
        pltpu.make_async_copy(x_ref, xs_ref.at[pl.ds(0, ts), :], sem).wait()


def _scatter_rows(dest_flat, x2, n_rows):
    t = x2.shape[0]
    ts = DMA_TILE
    xs0 = jnp.zeros((n_rows, D_MODEL), F32)
    grid_spec = pltpu.PrefetchScalarGridSpec(
        num_scalar_prefetch=1,
        grid=(t // ts,),
        in_specs=[pl.BlockSpec((ts, D_MODEL), lambda i, d: (i, 0)),
                  pl.BlockSpec(memory_space=pl.ANY)],
        out_specs=pl.BlockSpec(memory_space=pl.ANY),
        scratch_shapes=[pltpu.SemaphoreType.DMA(())],
    )
    return pl.pallas_call(
        _scatter_kernel,
        grid_spec=grid_spec,
        out_shape=jax.ShapeDtypeStruct((n_rows, D_MODEL), F32),
        input_output_aliases={2: 0},
        compiler_params=_cparams(("arbitrary",)),
        name="moe_scatter_rows",
    )(dest_flat, x2, xs0)


def _expert_kernel(be_ref, nu_ref, x_ref, wg_ref, wu_ref, wd_ref, y_ref, wgb_ref, wub_ref, wdb_ref):
    i = pl.program_id(0)
    new_expert = (i == 0) | (be_ref[i] != be_ref[jnp.maximum(i - 1, 0)])
    _cast_once(wg_ref, wgb_ref, new_expert)
    _cast_once(wu_ref, wub_ref, new_expert)
    _cast_once(wd_ref, wdb_ref, new_expert)

    @pl.when(i < nu_ref[0])
    def _():
        x = x_ref[...].astype(BF16)
        a = jnp.dot(x, wgb_ref[...], preferred_element_type=F32)
        b = jnp.dot(x, wub_ref[...], preferred_element_type=F32)
        hid = (a * jax.nn.sigmoid(a) * b).astype(BF16)
        y_ref[...] = jnp.dot(hid, wdb_ref[...], preferred_element_type=F32)

    @pl.when(i >= nu_ref[0])
    def _():
        y_ref[...] = jnp.zeros(y_ref.shape, F32)


def _experts(block_expert, n_used, xs, wg, wu, wd, layer):
    n_rows = xs.shape[0]
    blk = MOE_BLOCK
    grid_spec = pltpu.PrefetchScalarGridSpec(
        num_scalar_prefetch=2,
        grid=(n_rows // blk,),
        in_specs=[
            pl.BlockSpec((blk, D_MODEL), lambda i, be, nu: (i, 0)),
            pl.BlockSpec((None, None, D_MODEL, EXPERT_HIDDEN), lambda i, be, nu: (layer, be[i], 0, 0)),
            pl.BlockSpec((None, None, D_MODEL, EXPERT_HIDDEN), lambda i, be, nu: (layer, be[i], 0, 0)),
            pl.BlockSpec((None, None, EXPERT_HIDDEN, D_MODEL), lambda i, be, nu: (layer, be[i], 0, 0)),
        ],
        out_specs=pl.BlockSpec((blk, D_MODEL), lambda i, be, nu: (i, 0)),
        scratch_shapes=[pltpu.VMEM((D_MODEL, EXPERT_HIDDEN), BF16), pltpu.VMEM((D_MODEL, EXPERT_HIDDEN), BF16),
                        pltpu.VMEM((EXPERT_HIDDEN, D_MODEL), BF16)],
    )
    return pl.pallas_call(
        _expert_kernel,
        grid_spec=grid_spec,
        out_shape=jax.ShapeDtypeStruct((n_rows, D_MODEL), F32),
        compiler_params=_cparams(("arbitrary",)),
        name="moe_experts",
    )(block_expert, n_used, xs, wg, wu, wd)


def _combine_kernel(dest_ref, h_ref, gates_ref, ys_ref, *rest, final):
    if final:
        fg_ref, out_ref, buf, sem = rest
    else:
        out_ref, buf, sem = rest
    i = pl.program_id(0)
    ts = h_ref.shape[0]

    def issue(tok, carry):
        for k in range(2):
            d = dest_ref[(i * ts + tok) * 2 + k]
            pltpu.make_async_copy(ys_ref.at[pl.ds(d, 1), :], buf.at[k, pl.ds(tok, 1), :], sem).start()
        return carry

    lax.fori_loop(0, ts, issue, 0, unroll=8)
    for k in range(2):
        pltpu.make_async_copy(ys_ref.at[pl.ds(0, ts), :], buf.at[k], sem).wait()
    gates = gates_ref[...]
    out = h_ref[...] + (gates[:, 0:1] * buf[0] + gates[:, 1:2] * buf[1])
    if final:
        out = _rms(out, fg_ref[...])
    out_ref[...] = out


def _combine(dest_flat, h, gates, ys, final_gain=None):
    t = h.shape[0]
    ts = DMA_TILE
    final = final_gain is not None
    in_specs = [pl.BlockSpec((ts, D_MODEL), lambda i, d: (i, 0)),
                pl.BlockSpec((ts, LANES), lambda i, d: (i, 0)),
                pl.BlockSpec(memory_space=pl.ANY)]
    args = [dest_flat, h, gates, ys]
    if final:
        in_specs.append(pl.BlockSpec((1, D_MODEL), lambda i, d: (0, 0)))
        args.append(final_gain.reshape(1, D_MODEL))
    grid_spec = pltpu.PrefetchScalarGridSpec(
        num_scalar_prefetch=1,
        grid=(t // ts,),
        in_specs=in_specs,
        out_specs=pl.BlockSpec((ts, D_MODEL), lambda i, d: (i, 0)),
        scratch_shapes=[pltpu.VMEM((2, ts, D_MODEL), F32), pltpu.SemaphoreType.DMA(())],
    )
    return pl.pallas_call(
        functools.partial(_combine_kernel, final=final),
        grid_spec=grid_spec,
        out_shape=jax.ShapeDtypeStruct((t, D_MODEL), F32),
        compiler_params=_cparams(("arbitrary",)),
        name="moe_combine_final" if final else "moe_combine",
    )(*args)


def _moe(h, gain, w_r1, b_r1, w_r2, b_r2, wg, wu, wd, layer, final_gain=None):
    t = h.shape[0]
    pad_w = jnp.zeros((D_MODEL, LANES - MOE_GROUPS - MOE_EXPERTS), F32)
    wr = jnp.concatenate([w_r1, w_r2, pad_w], axis=1)
    br = jnp.concatenate([b_r1, b_r2, jnp.zeros((LANES - MOE_GROUPS - MOE_EXPERTS,), F32)]).reshape(1, LANES)
    x2, meta, gates, counts = _router(h, gain, wr, br)
    cnt = counts[0, ROUTER_LANE0:ROUTER_LANE0 + MOE_EXPERTS].astype(I32)
    padded = (cnt + MOE_BLOCK - 1) // MOE_BLOCK * MOE_BLOCK
    pad_end = jnp.cumsum(padded)
    pad_start = pad_end - padded
    n_rows = t * 2 + MOE_EXPERTS * MOE_BLOCK
    n_blocks = n_rows // MOE_BLOCK
    dest = pad_start[meta[:, 0:2]] + meta[:, 2:4]
    dest_flat = dest.reshape(-1).astype(I32)
    block_start = jnp.arange(n_blocks, dtype=I32) * MOE_BLOCK
    block_expert = jnp.minimum(
        jnp.sum((pad_end[None, :] <= block_start[:, None]).astype(I32), axis=1), MOE_EXPERTS - 1)
    n_used = (pad_end[-1:] // MOE_BLOCK).astype(I32)
    xs = _scatter_rows(dest_flat, x2, n_rows)
    ys = _experts(block_expert, n_used, xs, wg, wu, wd, layer)
    return _combine(dest_flat, h, gates, ys, final_gain)


def kernel(x, mix_norm, ffn_norm, final_norm, even_w_in, even_w_out, conv_w, conv_b, conv_norm_g, conv_norm_b,
           odd_w_in, odd_w_out, router_w1, router_b1, router_w2, router_b2, expert_w_gate, expert_w_up,
           expert_w_down):
    batch, seq, d = x.shape
    assert d == D_MODEL and seq % (ATTN_BLOCK * max(dl for _, dl in A_BRANCHES)) == 0
    assert all(w // dl == ATTN_BLOCK for w, dl in A_BRANCHES)
    t = batch * seq
    h = x.reshape(t, d)

    qkv, conv_in = _even_in(h, mix_norm[0], even_w_in, _attn_rope_tables(seq), seq)
    outs, lses = [], []
    for _, dil in A_BRANCHES:
        o_d, lse_d = _attn_branch(qkv.reshape(t // dil, dil * 3 * A_WIDTH), batch, seq, dil)
        outs.append(o_d.reshape(t, A_WIDTH))
        lses.append(lse_d.reshape(t, LANES))
    bconv = _conv(conv_in, conv_w[0], conv_b[0], conv_norm_g[0], conv_norm_b[0], batch, seq)
    h = _even_out(h, outs, lses, bconv, even_w_out)
    h = _moe(h, ffn_norm[0], router_w1[0], router_b1[0], router_w2[0], router_b2[0],
             expert_w_gate, expert_w_up, expert_w_down, 0)

    inv_freq = RET_ROT_THETA ** (-jnp.linspace(0.0, 1.0, C_QK_DIM // 2, dtype=F32))
    ang = jnp.arange(seq, dtype=F32)[:, None] * inv_freq[None, :]
    proj = _odd_in(h, mix_norm[1], odd_w_in, jnp.cos(ang), jnp.sin(ang), seq)
    log_decay = jnp.log(1.0 - jnp.exp2(-5.0 - jnp.arange(C_HEADS, dtype=F32)))
    y = _retention(proj, log_decay, batch, seq)
    h = _odd_out(h, y, odd_w_out)
    out = _moe(h, ffn_norm[1], router_w1[1], router_b1[1], router_w2[1], router_b2[1],
               expert_w_gate, expert_w_up, expert_w_down, 1, final_gain=final_norm)
    return out.reshape(batch, seq, d)
```

```python
import functools

import jax
import jax.numpy as jnp
from jax import lax
from jax.experimental import pallas as pl
from jax.experimental.pallas import tpu as pltpu

F32 = jnp.float32
BF16 = jnp.bfloat16
I32 = jnp.int32

NORM_EPS = 1e-6
NEG_INF = -1e30

D_MODEL = 1024
A_HEADS = 8
A_HEAD_DIM = 64
A_WIDTH = A_HEADS * A_HEAD_DIM
A_BRANCHES = ((128, 1), (512, 4), (2048, 16))
ATTN_BLOCK = 128
ROPE_THETA = 500000.0
ROPE_DIM = A_HEAD_DIM // 4
B_WIDTH = D_MODEL - A_WIDTH
CONV_WIDTH = 31
C_HEADS = 4
C_QK_DIM = 256
C_V_DIM = 512
C_QK_WIDTH = C_HEADS * C_QK_DIM
C_V_WIDTH = C_HEADS * C_V_DIM
RET_CHUNK = 128
RET_ROT_THETA = 10000.0
MOE_GROUPS = 4
MOE_EXPERTS_PER_GROUP = 8
MOE_EXPERTS = MOE_GROUPS * MOE_EXPERTS_PER_GROUP
EXPERT_HIDDEN = 512
MOE_BLOCK = 128
EVEN_IN_WIDTH = 3 * A_WIDTH + 2 * B_WIDTH
ODD_IN_WIDTH = 2 * C_QK_WIDTH + 2 * C_V_WIDTH

LANES = 128
ROW_TILE = 512
CONV_TILE = 512
CONV_HALO = 32
CONV_CHUNK = 64
ROUTER_LANE0 = MOE_GROUPS
VMEM_LIMIT = 56 * 1024 * 1024
VMEM_LIMIT_EXPERTS = 60 * 1024 * 1024
SLABS = D_MODEL // LANES
RANK_BITS = 16
GATHER_PITCH = MOE_BLOCK + 8
DUMMY_SLOTS = ROW_TILE


def _cparams(sem, vmem=VMEM_LIMIT):
    return pltpu.CompilerParams(dimension_semantics=sem, vmem_limit_bytes=vmem)


def _cast_once(w_ref, wb_ref, first):
    @pl.when(first)
    def _():
        wb_ref[...] = w_ref[...].astype(BF16)


def _rms(x, gain):
    ms = jnp.mean(x * x, axis=-1, keepdims=True)
    return x * lax.rsqrt(ms + NORM_EPS) * gain


def _even_in_kernel(h_ref, g_ref, w_ref, c_ref, s1_ref, s2_ref, qkv_ref, conv_ref, wb_ref):
    _cast_once(w_ref, wb_ref, pl.program_id(0) == 0)
    u = _rms(h_ref[...], g_ref[...]).astype(BF16)
    acc = jnp.dot(u, wb_ref[...], preferred_element_type=F32)
    c, s1, s2 = c_ref[...], s1_ref[...], s2_ref[...]
    for j in range(2 * A_WIDTH // LANES):
        xg = acc[:, j * LANES:(j + 1) * LANES]
        if j < A_WIDTH // LANES:
            xg = xg * (A_HEAD_DIM ** -0.5)
        r = xg * c + pltpu.roll(xg, LANES - ROPE_DIM // 2, 1) * s1 + pltpu.roll(xg, ROPE_DIM // 2, 1) * s2
        qkv_ref[:, j * LANES:(j + 1) * LANES] = r.astype(BF16)
    qkv_ref[:, 2 * A_WIDTH:3 * A_WIDTH] = acc[:, 2 * A_WIDTH:3 * A_WIDTH].astype(BF16)
    conv_ref[...] = acc[:, 3 * A_WIDTH:].astype(BF16)


def _even_in(h, gain, w, tabs, seq):
    t = h.shape[0]
    tm = ROW_TILE
    nseq = seq // tm
    tab_spec = pl.BlockSpec((tm, LANES), lambda i: (i % nseq, 0))
    return pl.pallas_call(
        _even_in_kernel,
        grid=(t // tm,),
        in_specs=[
            pl.BlockSpec((tm, D_MODEL), lambda i: (i, 0)),
            pl.BlockSpec((1, D_MODEL), lambda i: (0, 0)),
            pl.BlockSpec((None, D_MODEL, EVEN_IN_WIDTH), lambda i: (0, 0, 0)),
            tab_spec, tab_spec, tab_spec,
        ],
        out_specs=[
            pl.BlockSpec((tm, 3 * A_WIDTH), lambda i: (i, 0)),
            pl.BlockSpec((tm, 2 * B_WIDTH), lambda i: (i, 0)),
        ],
        out_shape=[
            jax.ShapeDtypeStruct((t, 3 * A_WIDTH), BF16),
            jax.ShapeDtypeStruct((t, 2 * B_WIDTH), BF16),
        ],
        scratch_shapes=[pltpu.VMEM((D_MODEL, EVEN_IN_WIDTH), BF16)],
        compiler_params=_cparams(("arbitrary",)),
        name="even_in_proj",
    )(h, gain.reshape(1, D_MODEL), w, *tabs)


def _attn_rope_tables(seq):
    half = ROPE_DIM // 2
    inv_freq = ROPE_THETA ** (-jnp.arange(0, ROPE_DIM, 2, dtype=F32) / ROPE_DIM)
    ang = jnp.arange(seq, dtype=F32)[:, None] * inv_freq[None, :]
    cos, sin = jnp.cos(ang), jnp.sin(ang)
    rest = A_HEAD_DIM - ROPE_DIM
    ones = jnp.ones((seq, rest), F32)
    z_rest = jnp.zeros((seq, rest), F32)
    z_half = jnp.zeros((seq, half), F32)
    c = jnp.concatenate([cos, cos, ones], axis=1)
    s1 = jnp.concatenate([-sin, z_half, z_rest], axis=1)
    s2 = jnp.concatenate([z_half, sin, z_rest], axis=1)
    rep = LANES // A_HEAD_DIM
    return tuple(jnp.tile(a, (1, rep)) for a in (c, s1, s2))


def _attn_kernel(q_ref, kp_ref, kc_ref, vp_ref, vc_ref, o_ref, lse_ref):
    n = pl.program_id(2)
    blk = ATTN_BLOCK
    qi = lax.broadcasted_iota(I32, (blk, 2 * blk), 0)
    kj = lax.broadcasted_iota(I32, (blk, 2 * blk), 1)
    dist = blk + qi - kj
    valid = (dist >= 0) & (dist <= blk) & ((kj >= blk) | (n > 0))
    valid2 = jnp.concatenate([valid, valid], axis=0)
    lane = lax.broadcasted_iota(I32, (blk, LANES), 1)
    lo = lane < A_HEAD_DIM
    lse_tile = jnp.zeros((blk, LANES), F32)
    for g in range(A_WIDTH // LANES):
        sl = slice(g * LANES, (g + 1) * LANES)
        q = q_ref[:, sl].astype(F32)
        q2 = jnp.concatenate([jnp.where(lo, q, 0.0), jnp.where(lo, 0.0, q)], axis=0).astype(BF16)
        kk = jnp.concatenate([kp_ref[:, sl], kc_ref[:, sl]], axis=0)
        vv = jnp.concatenate([vp_ref[:, sl], vc_ref[:, sl]], axis=0)
        s = lax.dot_general(q2, kk, (((1,), (1,)), ((), ())), preferred_element_type=F32)
        s = jnp.where(valid2, s, NEG_INF)
        m = jnp.max(s, axis=-1, keepdims=True)
        e = jnp.exp(s - m)
        den = jnp.sum(e, axis=-1, keepdims=True)
        pv = jnp.dot(e.astype(BF16), vv, preferred_element_type=F32) / den
        o_ref[:, sl] = jnp.where(lo, pv[:blk], pv[blk:]).astype(BF16)
        lse = m + jnp.log(den)
        lse_tile = jnp.where(lane == 2 * g, lse[:blk], lse_tile)
        lse_tile = jnp.where(lane == 2 * g + 1, lse[blk:], lse_tile)
    lse_ref[...] = lse_tile


def _attn_branch(qkv_d, batch, seq, dil):
    blk = ATTN_BLOCK
    nb = seq // dil // blk
    rows = qkv_d.shape[0]

    def cur(col):
        return pl.BlockSpec((blk, A_WIDTH), lambda b, r, n: (b * nb + n, 3 * r + col))

    def prev(col):
        return pl.BlockSpec((blk, A_WIDTH), lambda b, r, n: (b * nb + jnp.maximum(n - 1, 0), 3 * r + col))

    return pl.pallas_call(
        _attn_kernel,
        grid=(batch, dil, nb),
        in_specs=[cur(0), prev(1), cur(1), prev(2), cur(2)],
        out_specs=[
            pl.BlockSpec((blk, A_WIDTH), lambda b, r, n: (b * nb + n, r)),
            pl.BlockSpec((blk, LANES), lambda b, r, n: (b * nb + n, r)),
        ],
        out_shape=[
            jax.ShapeDtypeStruct((rows, dil * A_WIDTH), BF16),
            jax.ShapeDtypeStruct((rows, dil * LANES), F32),
        ],
        compiler_params=_cparams(("parallel", "parallel", "arbitrary")),
        name=f"dilated_attn_d{dil}",
    )(qkv_d, qkv_d, qkv_d, qkv_d, qkv_d)


def _conv_kernel(val_ref, gate_ref, w_ref, b_ref, g_ref, beta_ref, o_ref, abuf):
    n = pl.program_id(1)
    tc = CONV_TILE

    @pl.when(n == 0)
    def _():
        abuf[0:CONV_HALO, :] = jnp.zeros((CONV_HALO, B_WIDTH), F32)

    @pl.when(n > 0)
    def _():
        abuf[0:CONV_HALO, :] = abuf[tc:tc + CONV_HALO, :]

    val = val_ref[...].astype(F32)
    gate = gate_ref[...].astype(F32)
    abuf[CONV_HALO:CONV_HALO + tc, :] = val * jax.nn.sigmoid(gate)
    off = CONV_HALO - (CONV_WIDTH - 1)
    for c in range(tc // CONV_CHUNK):
        acc = jnp.broadcast_to(b_ref[...], (CONV_CHUNK, B_WIDTH))
        for j in range(CONV_WIDTH):
            r0 = c * CONV_CHUNK + off + j
            acc = acc + w_ref[j:j + 1, :] * abuf[r0:r0 + CONV_CHUNK, :]
        mu = jnp.mean(acc, axis=-1, keepdims=True)
        cen = acc - mu
        var = jnp.mean(cen * cen, axis=-1, keepdims=True)
        yn = cen * lax.rsqrt(var + NORM_EPS) * g_ref[...] + beta_ref[...]
        o_ref[c * CONV_CHUNK:(c + 1) * CONV_CHUNK, :] = (yn * jax.nn.sigmoid(yn)).astype(BF16)


def _conv(conv_in, w, b, g, beta, batch, seq):
    t = conv_in.shape[0]
    tc = CONV_TILE
    nt = seq // tc
    w_pad = jnp.concatenate([w, jnp.zeros((CONV_HALO - CONV_WIDTH, B_WIDTH), F32)], axis=0)
    vec = pl.BlockSpec((1, B_WIDTH), lambda bb, n: (0, 0))
    return pl.pallas_call(
        _conv_kernel,
        grid=(batch, nt),
        in_specs=[
            pl.BlockSpec((tc, B_WIDTH), lambda bb, n: (bb * nt + n, 0)),
            pl.BlockSpec((tc, B_WIDTH), lambda bb, n: (bb * nt + n, 1)),
            pl.BlockSpec((CONV_HALO, B_WIDTH), lambda bb, n: (0, 0)),
            vec, vec, vec,
        ],
        out_specs=pl.BlockSpec((tc, B_WIDTH), lambda bb, n: (bb * nt + n, 0)),
        out_shape=jax.ShapeDtypeStruct((t, B_WIDTH), BF16),
        scratch_shapes=[pltpu.VMEM((tc + CONV_HALO, B_WIDTH), F32)],
        compiler_params=_cparams(("parallel", "arbitrary")),
        name="conformer_conv",
    )(conv_in, conv_in, w_pad, b.reshape(1, -1), g.reshape(1, -1), beta.reshape(1, -1))


def _even_out_kernel(h_ref, o1_ref, o2_ref, o3_ref, l1_ref, l2_ref, l3_ref, bc_ref, w_ref, out_ref, wb_ref):
    _cast_once(w_ref, wb_ref, pl.program_id(0) == 0)
    l1, l2, l3 = l1_ref[...], l2_ref[...], l3_ref[...]
    m = jnp.maximum(jnp.maximum(l1, l2), l3)
    e1, e2, e3 = jnp.exp(l1 - m), jnp.exp(l2 - m), jnp.exp(l3 - m)
    inv = 1.0 / (e1 + e2 + e3)
    row = lax.broadcasted_iota(I32, (LANES, A_WIDTH), 0)
    col = lax.broadcasted_iota(I32, (LANES, A_WIDTH), 1)
    expand = (col // A_HEAD_DIM == row).astype(F32)
    a = jnp.zeros(o1_ref.shape, F32)
    for e, o_ref in ((e1, o1_ref), (e2, o2_ref), (e3, o3_ref)):
        wfull = jnp.dot(e * inv, expand, precision=lax.Precision.HIGHEST, preferred_element_type=F32)
        a = a + wfull * o_ref[...].astype(F32)
    acc = jnp.dot(a.astype(BF16), wb_ref[0:A_WIDTH, :], preferred_element_type=F32)
    acc = acc + jnp.dot(bc_ref[...], wb_ref[A_WIDTH:, :], preferred_element_type=F32)
    out_ref[...] = h_ref[...] + acc


def _even_out(h, outs, lses, bconv, w):
    t = h.shape[0]
    tm = ROW_TILE
    row = lambda width: pl.BlockSpec((tm, width), lambda i: (i, 0))
    return pl.pallas_call(
        _even_out_kernel,
        grid=(t // tm,),
        in_specs=[row(D_MODEL), row(A_WIDTH), row(A_WIDTH), row(A_WIDTH), row(LANES), row(LANES), row(LANES),
                  row(B_WIDTH), pl.BlockSpec((None, D_MODEL, D_MODEL), lambda i: (0, 0, 0))],
        out_specs=row(D_MODEL),
        out_shape=jax.ShapeDtypeStruct((t, D_MODEL), F32),
        scratch_shapes=[pltpu.VMEM((D_MODEL, D_MODEL), BF16)],
        compiler_params=_cparams(("arbitrary",)),
        name="even_out_proj",
    )(h, *outs, *lses, bconv, w)


def _odd_in_kernel(h_ref, g_ref, w_ref, cos_ref, sin_ref, o_ref, wb_ref):
    j = pl.program_id(0)
    _cast_once(w_ref, wb_ref, pl.program_id(1) == 0)
    u = _rms(h_ref[...], g_ref[...]).astype(BF16)
    acc = jnp.dot(u, wb_ref[...], preferred_element_type=F32)

    @pl.when(j == 0)
    def _():
        cos, sin = cos_ref[...], sin_ref[...]
        half = C_QK_DIM // 2
        for hd in range(2 * C_HEADS):
            x1 = acc[:, hd * C_QK_DIM:hd * C_QK_DIM + half]
            x2 = acc[:, hd * C_QK_DIM + half:(hd + 1) * C_QK_DIM]
            r1 = x1 * cos - x2 * sin
            r2 = x2 * cos + x1 * sin
            if hd >= C_HEADS:
                r1 = r1 * (C_QK_DIM ** -0.5)
                r2 = r2 * (C_QK_DIM ** -0.5)
            o_ref[:, hd * C_QK_DIM:hd * C_QK_DIM + half] = r1.astype(BF16)
            o_ref[:, hd * C_QK_DIM + half:(hd + 1) * C_QK_DIM] = r2.astype(BF16)

    @pl.when(j > 0)
    def _():
        o_ref[...] = acc.astype(BF16)


def _odd_in(h, gain, w, cos, sin, seq):
    t = h.shape[0]
    tm = ROW_TILE
    tn = 2 * C_QK_WIDTH
    nseq = seq // tm
    tab = pl.BlockSpec((tm, C_QK_DIM // 2), lambda j, i: (i % nseq, 0))
    return pl.pallas_call(
        _odd_in_kernel,
        grid=(ODD_IN_WIDTH // tn, t // tm),
        in_specs=[
            pl.BlockSpec((tm, D_MODEL), lambda j, i: (i, 0)),
            pl.BlockSpec((1, D_MODEL), lambda j, i: (0, 0)),
            pl.BlockSpec((None, D_MODEL, tn), lambda j, i: (0, 0, j)),
            tab, tab,
        ],
        out_specs=pl.BlockSpec((tm, tn), lambda j, i: (i, j)),
        out_shape=jax.ShapeDtypeStruct((t, ODD_IN_WIDTH), BF16),
        scratch_shapes=[pltpu.VMEM((D_MODEL, tn), BF16)],
        compiler_params=_cparams(("arbitrary", "arbitrary")),
        name="odd_in_proj",
    )(h, gain.reshape(1, D_MODEL), w, cos, sin)


def _ret_kernel(ld_ref, q_ref, k_ref, v_ref, g_ref, o_ref, state):
    hd = pl.program_id(1)
    c = pl.program_id(2)
    ch = RET_CHUNK

    @pl.when(c == 0)
    def _():
        state[...] = jnp.zeros(state.shape, F32)

    ld = ld_ref[hd]
    q, k, v = q_ref[...], k_ref[...], v_ref[...]
    ii = lax.broadcasted_iota(I32, (ch, ch), 0)
    jj = lax.broadcasted_iota(I32, (ch, ch), 1)
    diff = (ii - jj).astype(F32)
    intra = jnp.where(diff >= 0, jnp.exp(ld * jnp.maximum(diff, 0.0)), 0.0)
    s = lax.dot_general(q, k, (((1,), (1,)), ((), ())), preferred_element_type=F32) * intra
    inner = jnp.dot(s.astype(BF16), v, preferred_element_type=F32)
    pos = lax.broadcasted_iota(I32, (ch, 1), 0).astype(F32)
    q_decay = jnp.exp(ld * (pos + 1.0))
    k_decay = jnp.exp(ld * (ch - 1.0 - pos))
    chunk_decay = jnp.exp(ld * jnp.full((1, 1), float(ch), F32))
    st = state[...]
    cross = jnp.dot(q, st.astype(BF16), preferred_element_type=F32) * q_decay
    kd_t = jnp.transpose(k.astype(F32) * k_decay).astype(BF16)
    state[...] = st * chunk_decay + jnp.dot(kd_t, v, preferred_element_type=F32)
    out = inner + cross
    mu = jnp.mean(out, axis=-1, keepdims=True)
    cen = out - mu
    var = jnp.mean(cen * cen, axis=-1, keepdims=True)
    o = cen * lax.rsqrt(var + NORM_EPS)
    gf = g_ref[...].astype(F32)
    o_ref[...] = (gf * jax.nn.sigmoid(gf) * o).astype(BF16)


def _retention(proj, log_decay, batch, seq):
    t = proj.shape[0]
    ch = RET_CHUNK
    nc = seq // ch
    qk_blocks = C_QK_WIDTH // C_QK_DIM
    v_blocks = C_V_WIDTH // C_V_DIM
    v0 = 2 * C_QK_WIDTH // C_V_DIM
    grid_spec = pltpu.PrefetchScalarGridSpec(
        num_scalar_prefetch=1,
        grid=(batch, C_HEADS, nc),
        in_specs=[
            pl.BlockSpec((ch, C_QK_DIM), lambda b, hd, c, ld: (b * nc + c, hd)),
            pl.BlockSpec((ch, C_QK_DIM), lambda b, hd, c, ld: (b * nc + c, qk_blocks + hd)),
            pl.BlockSpec((ch, C_V_DIM), lambda b, hd, c, ld: (b * nc + c, v0 + hd)),
            pl.BlockSpec((ch, C_V_DIM), lambda b, hd, c, ld: (b * nc + c, v0 + v_blocks + hd)),
        ],
        out_specs=pl.BlockSpec((ch, C_V_DIM), lambda b, hd, c, ld: (b * nc + c, hd)),
        scratch_shapes=[pltpu.VMEM((C_QK_DIM, C_V_DIM), F32)],
    )
    return pl.pallas_call(
        _ret_kernel,
        grid_spec=grid_spec,
        out_shape=jax.ShapeDtypeStruct((t, C_V_WIDTH), BF16),
        compiler_params=_cparams(("parallel", "parallel", "arbitrary")),
        name="retention",
    )(log_decay, proj, proj, proj, proj)


def _odd_out_kernel(h_ref, y_ref, w_ref, out_ref, wb_ref):
    _cast_once(w_ref, wb_ref, pl.program_id(0) == 0)
    out_ref[...] = h_ref[...] + jnp.dot(y_ref[...], wb_ref[...], preferred_element_type=F32)


def _odd_out(h, y, w):
    t = h.shape[0]
    tm = ROW_TILE
    return pl.pallas_call(
        _odd_out_kernel,
        grid=(t // tm,),
        in_specs=[pl.BlockSpec((tm, D_MODEL), lambda i: (i, 0)),
                  pl.BlockSpec((tm, C_V_WIDTH), lambda i: (i, 0)),
                  pl.BlockSpec((None, C_V_WIDTH, D_MODEL), lambda i: (0, 0, 0))],
        out_specs=pl.BlockSpec((tm, D_MODEL), lambda i: (i, 0)),
        out_shape=jax.ShapeDtypeStruct((t, D_MODEL), F32),
        scratch_shapes=[pltpu.VMEM((C_V_WIDTH, D_MODEL), BF16)],
        compiler_params=_cparams(("arbitrary",)),
        name="odd_out_proj",
    )(h, y, w)


def _router_kernel(h_ref, g_ref, wr_ref, br_ref, x2s_ref, codes_ref, gates_ref, cnt_ref, base_ref, *,
                   tiles_per_batch):
    i = pl.program_id(0)
    tm = h_ref.shape[0]
    batch = i // tiles_per_batch

    @pl.when(i % tiles_per_batch == 0)
    def _():
        base_ref[...] = jnp.zeros(base_ref.shape, F32)

    x2 = _rms(h_ref[...], g_ref[...])
    for j in range(SLABS):
        x2s_ref[pl.ds(j, tm, stride=SLABS), :] = x2[:, j * LANES:(j + 1) * LANES]
    logits = jnp.dot(x2, wr_ref[...], precision=lax.Precision.HIGHEST, preferred_element_type=F32) + br_ref[...]
    lane = lax.broadcasted_iota(I32, (tm, LANES), 1)
    is_grp = lane < MOE_GROUPS
    lg = jnp.where(is_grp, logits, -jnp.inf)
    gmax = jnp.max(lg, axis=-1, keepdims=True)
    gsum = jnp.sum(jnp.where(is_grp, jnp.exp(logits - gmax), 0.0), axis=-1, keepdims=True)
    gp = 1.0 / gsum
    gi = jnp.min(jnp.where(lg == gmax, lane, LANES), axis=-1, keepdims=True)
    ex = lane - ROUTER_LANE0
    in_grp = (ex >= 0) & (ex < MOE_EXPERTS) & ((ex // MOE_EXPERTS_PER_GROUP) == gi)
    sel = jnp.where(in_grp, logits, -jnp.inf)
    v1 = jnp.max(sel, axis=-1, keepdims=True)
    i1 = jnp.min(jnp.where(sel == v1, lane, LANES), axis=-1, keepdims=True)
    sel2 = jnp.where(lane == i1, -jnp.inf, sel)
    v2 = jnp.max(sel2, axis=-1, keepdims=True)
    i2 = jnp.min(jnp.where(sel2 == v2, lane, LANES), axis=-1, keepdims=True)
    tt = jnp.exp(v2 - v1)
    g0 = gp / (1.0 + tt)
    g1 = gp * tt / (1.0 + tt)
    oh0 = (lane == i1).astype(F32)
    oh1 = (lane == i2).astype(F32)
    oh = oh0 + oh1
    rr = lax.broadcasted_iota(I32, (tm, tm), 0)
    cc = lax.broadcasted_iota(I32, (tm, tm), 1)
    tri = (cc < rr).astype(BF16)
    prefix = jnp.dot(tri, oh.astype(BF16), preferred_element_type=F32)
    tot = base_ref[...] + prefix
    rank0 = jnp.sum(oh0 * tot, axis=-1, keepdims=True).astype(I32)
    rank1 = jnp.sum(oh1 * tot, axis=-1, keepdims=True).astype(I32)
    base_ref[...] = base_ref[...] + jnp.sum(oh, axis=0, keepdims=True)
    eoff = batch * MOE_EXPERTS - ROUTER_LANE0
    code0 = (i1 + eoff) * (1 << RANK_BITS) + rank0
    code1 = (i2 + eoff) * (1 << RANK_BITS) + rank1
    meta = jnp.where(lane == 0, code0, jnp.where(lane == 1, code1, 0))
    codes_ref[...] = jnp.transpose(meta)[0:8, :]
    gates_ref[...] = jnp.where(lane == 0, g0, jnp.where(lane == 1, g1, 0.0))
    cnt_ref[...] = jnp.broadcast_to(base_ref[...], cnt_ref.shape)


def _router(h, gain, wr, br, batch):
    t = h.shape[0]
    tm = ROW_TILE
    tpb = t // batch // tm
    row = lambda width: pl.BlockSpec((tm, width), lambda i: (i, 0))
    fixed = lambda shape: pl.BlockSpec(shape, lambda i: (0, 0))
    return pl.pallas_call(
        functools.partial(_router_kernel, tiles_per_batch=tpb),
        grid=(t // tm,),
        in_specs=[row(D_MODEL), fixed((1, D_MODEL)), fixed((D_MODEL, LANES)), fixed((1, LANES))],
        out_specs=[pl.BlockSpec((tm * SLABS, LANES), lambda i: (i, 0)),
                   pl.BlockSpec((8, tm), lambda i: (0, i)), row(LANES),
                   pl.BlockSpec((8, LANES), lambda i: (i // tpb, 0))],
        out_shape=[
            jax.ShapeDtypeStruct((t * SLABS, LANES), F32),
            jax.ShapeDtypeStruct((8, t), I32),
            jax.ShapeDtypeStruct((t, LANES), F32),
            jax.ShapeDtypeStruct((batch * 8, LANES), F32),
        ],
        scratch_shapes=[pltpu.VMEM((1, LANES), F32)],
        compiler_params=_cparams(("arbitrary",)),
        name="moe_router",
    )(h, gain.reshape(1, D_MODEL), wr, br)


INVERT_UNROLL = 8


def _invert_kernel(trips_ref, codes_ref, pstart_ref, fill_lo_ref, fill_hi_ref, slot_ref, *, batch):
    per_batch = codes_ref.shape[0] // batch
    blk_shift = MOE_BLOCK.bit_length() - 1
    assert MOE_BLOCK == 1 << blk_shift and 2 * MOE_BLOCK <= DUMMY_SLOTS

    def fill_segment(sgm, carry):
        def fill_row(r, c):
            slot_ref[r] = per_batch + (lax.shift_right_logical(r, blk_shift) & 1) * MOE_BLOCK + (r & (MOE_BLOCK - 1))
            return c

        return lax.fori_loop(fill_lo_ref[sgm], fill_hi_ref[sgm], fill_row, carry)

    lax.fori_loop(0, fill_lo_ref.shape[0], fill_segment, 0)
    for b in range(batch):
        def place(i, carry, b=b):
            for v in range(INVERT_UNROLL):
                a = i * INVERT_UNROLL + v
                code = codes_ref[b * per_batch + a]
                row = pstart_ref[lax.shift_right_logical(code, RANK_BITS)] + (code & ((1 << RANK_BITS) - 1))
                slot_ref[row] = a
            return carry

        lax.fori_loop(0, trips_ref[0], place, 0)


def _invert(codes_flat, pad_start, fill_lo, fill_hi, n_rows, batch):
    smem = pl.BlockSpec(memory_space=pltpu.SMEM)
    per_batch = codes_flat.shape[0] // batch
    assert per_batch % INVERT_UNROLL == 0
    trips = jnp.full((1,), per_batch // INVERT_UNROLL, I32)
    return pl.pallas_call(
        functools.partial(_invert_kernel, batch=batch),
        in_specs=[smem] * 5,
        out_specs=smem,
        out_shape=jax.ShapeDtypeStruct((n_rows,), I32),
        name="moe_invert_rows",
    )(trips, codes_flat, pad_start, fill_lo, fill_hi)


def _expert_kernel(be_ref, slot_ref, x2s_ref, wg_ref, wu_ref, wd_ref, ys_ref,
                   xres, wgb_ref, wub_ref, wdb_ref, tile, ybuf, sems, *, batch):
    b = pl.program_id(0)
    n = pl.program_id(1)
    nb = pl.num_programs(1)
    g = b * nb + n
    last = pl.num_programs(0) * nb - 1
    blk = MOE_BLOCK
    seq = xres.shape[0] // SLABS
    p = g % 2

    @pl.when(n == 0)
    def _():
        pltpu.sync_copy(x2s_ref.at[pl.ds(pl.multiple_of(b * (seq * SLABS), SLABS), seq * SLABS), :], xres)

    new_expert = (g == 0) | (be_ref[g] != be_ref[jnp.maximum(g - 1, 0)])
    _cast_once(wg_ref, wgb_ref, new_expert)
    _cast_once(wu_ref, wub_ref, new_expert)
    _cast_once(wd_ref, wdb_ref, new_expert)

    def wait_block(q):
        pltpu.make_async_copy(ybuf.at[q], ys_ref.at[pl.ds(0, blk * SLABS), :], sems.at[q]).wait()

    @pl.when(g == 0)
    def _():
        ybuf[0] = jnp.zeros(ybuf.shape[1:], F32)
        for bb in range(batch):
            for c in range(DUMMY_SLOTS // blk):
                start = (bb * (2 * seq + DUMMY_SLOTS) + 2 * seq + c * blk) * SLABS
                zero = pltpu.make_async_copy(ybuf.at[0], ys_ref.at[pl.ds(start, blk * SLABS), :], sems.at[0])
                zero.start()
                zero.wait()

    slots = [slot_ref[g * blk + mi] for mi in range(blk)]
    for mi in range(blk):
        s = slots[mi]
        tok = jnp.minimum(jnp.where(s >= seq, s - seq, s), seq - 1)
        slab = xres[pl.ds(pl.multiple_of(tok * SLABS, SLABS), SLABS), :]
        tile[pl.ds(mi, SLABS, stride=GATHER_PITCH), :] = slab
    x = jnp.concatenate([tile[j * GATHER_PITCH:j * GATHER_PITCH + blk, :].astype(BF16) for j in range(SLABS)],
                        axis=1)
    a = jnp.dot(x, wgb_ref[...], preferred_element_type=F32)
    u = jnp.dot(x, wub_ref[...], preferred_element_type=F32)
    hid = (a * jax.nn.sigmoid(a) * u).astype(BF16)
    y = jnp.dot(hid, wdb_ref[...], preferred_element_type=F32)

    @pl.when(g >= 2)
    def _():
        wait_block(p)

    for j in range(SLABS):
        ybuf[p, pl.ds(j, blk, stride=SLABS), :] = y[:, j * LANES:(j + 1) * LANES]
    row0 = b * (2 * seq + DUMMY_SLOTS)
    for mi in range(blk):
        dst = pl.multiple_of((row0 + slots[mi]) * SLABS, SLABS)
        pltpu.make_async_copy(ybuf.at[p, pl.ds(mi * SLABS, SLABS), :], ys_ref.at[pl.ds(dst, SLABS), :],
                              sems.at[p]).start()

    @pl.when(g == last)
    def _():
        wait_block(p)
        wait_block(1 - p)


def _experts(block_expert, row_slot, x2s, wg, wu, wd, layer, batch, seq):
    blk = MOE_BLOCK
    nb = row_slot.shape[0] // blk // batch
    assert batch * nb >= 2
    wspec = lambda shape: pl.BlockSpec((None, None) + shape, lambda b, n, be, rs: (layer, be[b * nb + n], 0, 0))
    grid_spec = pltpu.PrefetchScalarGridSpec(
        num_scalar_prefetch=2,
        grid=(batch, nb),
        in_specs=[pl.BlockSpec(memory_space=pl.ANY), wspec((D_MODEL, EXPERT_HIDDEN)),
                  wspec((D_MODEL, EXPERT_HIDDEN)), wspec((EXPERT_HIDDEN, D_MODEL))],
        out_specs=pl.BlockSpec(memory_space=pl.ANY),
        scratch_shapes=[
            pltpu.VMEM((seq * SLABS, LANES), F32),
            pltpu.VMEM((D_MODEL, EXPERT_HIDDEN), BF16), pltpu.VMEM((D_MODEL, EXPERT_HIDDEN), BF16),
            pltpu.VMEM((EXPERT_HIDDEN, D_MODEL), BF16),
            pltpu.VMEM((SLABS * GATHER_PITCH, LANES), F32),
            pltpu.VMEM((2, blk * SLABS, LANES), F32),
            pltpu.SemaphoreType.DMA((2,)),
        ],
    )
    return pl.pallas_call(
        functools.partial(_expert_kernel, batch=batch),
        grid_spec=grid_spec,
        out_shape=jax.ShapeDtypeStruct((batch * (2 * seq + DUMMY_SLOTS) * SLABS, LANES), F32),
        compiler_params=_cparams(("arbitrary", "arbitrary"), VMEM_LIMIT_EXPERTS),
        name="moe_experts",
    )(block_expert, row_slot, x2s, wg, wu, wd)


def _combine_kernel(h_ref, gates_ref, y0_ref, y1_ref, *rest, final):
    if final:
        fg_ref, out_ref = rest
    else:
        (out_ref,) = rest
    tm = h_ref.shape[0]
    gates = gates_ref[...]
    g0, g1 = gates[:, 0:1], gates[:, 1:2]
    parts = []
    for j in range(SLABS):
        y0 = y0_ref[pl.ds(j, tm, stride=SLABS), :]
        y1 = y1_ref[pl.ds(j, tm, stride=SLABS), :]
        parts.append(g0 * y0 + g1 * y1)
    out = h_ref[...] + jnp.concatenate(parts, axis=1)
    if final:
        out = _rms(out, fg_ref[...])
    out_ref[...] = out


def _combine(h, gates, ys, batch, seq, final_gain=None):
    t = h.shape[0]
    tm = ROW_TILE
    tpb = seq // tm
    bstride = 2 * tpb + DUMMY_SLOTS // tm
    final = final_gain is not None
    in_specs = [pl.BlockSpec((tm, D_MODEL), lambda i: (i, 0)),
                pl.BlockSpec((tm, LANES), lambda i: (i, 0)),
                pl.BlockSpec((tm * SLABS, LANES), lambda i: (i // tpb * bstride + i % tpb, 0)),
                pl.BlockSpec((tm * SLABS, LANES), lambda i: (i // tpb * bstride + tpb + i % tpb, 0))]
    args = [h, gates, ys, ys]
    if final:
        in_specs.append(pl.BlockSpec((1, D_MODEL), lambda i: (0, 0)))
        args.append(final_gain.reshape(1, D_MODEL))
    return pl.pallas_call(
        functools.partial(_combine_kernel, final=final),
        grid=(t // tm,),
        in_specs=in_specs,
        out_specs=pl.BlockSpec((tm, D_MODEL), lambda i: (i, 0)),
        out_shape=jax.ShapeDtypeStruct((t, D_MODEL), F32),
        compiler_params=_cparams(("parallel",)),
        name="moe_combine_final" if final else "moe_combine",
    )(*args)


def _moe(h, gain, w_r1, b_r1, w_r2, b_r2, wg, wu, wd, layer, batch, final_gain=None):
    t = h.shape[0]
    seq = t // batch
    assert seq <= (1 << RANK_BITS) and seq % ROW_TILE == 0 and DUMMY_SLOTS % ROW_TILE == 0
    pad_w = jnp.zeros((D_MODEL, LANES - MOE_GROUPS - MOE_EXPERTS), F32)
    wr = jnp.concatenate([w_r1, w_r2, pad_w], axis=1)
    br = jnp.concatenate([b_r1, b_r2, jnp.zeros((LANES - MOE_GROUPS - MOE_EXPERTS,), F32)]).reshape(1, LANES)
    x2s, codes, gates, counts = _router(h, gain, wr, br, batch)
    cnt = counts.reshape(batch, 8, LANES)[:, 0, ROUTER_LANE0:ROUTER_LANE0 + MOE_EXPERTS].astype(I32)
    padded = (cnt + MOE_BLOCK - 1) // MOE_BLOCK * MOE_BLOCK
    pad_end = jnp.cumsum(padded, axis=1)
    rows_pb = seq * 2 + MOE_EXPERTS * MOE_BLOCK
    nb = rows_pb // MOE_BLOCK
    pad_start = pad_end - padded + (jnp.arange(batch, dtype=I32) * rows_pb)[:, None]
    block_start = jnp.arange(nb, dtype=I32) * MOE_BLOCK
    block_expert = jnp.minimum(
        jnp.sum((pad_end[:, None, :] <= block_start[None, :, None]).astype(I32), axis=2), MOE_EXPERTS - 1)
    codes_flat = codes[0:2].reshape(2, batch, seq).transpose(1, 0, 2).reshape(-1)
    batch_end = ((jnp.arange(batch, dtype=I32) + 1) * rows_pb)[:, None]
    fill_lo = jnp.concatenate([pad_start + cnt, pad_start[:, -1:] + padded[:, -1:]], axis=1)
    fill_hi = jnp.concatenate([pad_start + padded, batch_end], axis=1)
    row_slot = _invert(codes_flat, pad_start.reshape(-1), fill_lo.reshape(-1), fill_hi.reshape(-1),
                       batch * rows_pb, batch)
    ys = _experts(block_expert.reshape(-1), row_slot, x2s, wg, wu, wd, layer, batch, seq)
    return _combine(h, gates, ys, batch, seq, final_gain)


def kernel(x, mix_norm, ffn_norm, final_norm, even_w_in, even_w_out, conv_w, conv_b, conv_norm_g, conv_norm_b,
           odd_w_in, odd_w_out, router_w1, router_b1, router_w2, router_b2, expert_w_gate, expert_w_up,
           expert_w_down):
    batch, seq, d = x.shape
    assert d == D_MODEL and seq % (ATTN_BLOCK * max(dl for _, dl in A_BRANCHES)) == 0
    assert all(w // dl == ATTN_BLOCK for w, dl in A_BRANCHES)
    t = batch * seq
    h = x.reshape(t, d)

    qkv, conv_in = _even_in(h, mix_norm[0], even_w_in, _attn_rope_tables(seq), seq)
    outs, lses = [], []
    for _, dil in A_BRANCHES:
        o_d, lse_d = _attn_branch(qkv.reshape(t // dil, dil * 3 * A_WIDTH), batch, seq, dil)
        outs.append(o_d.reshape(t, A_WIDTH))
        lses.append(lse_d.reshape(t, LANES))
    bconv = _conv(conv_in, conv_w[0], conv_b[0], conv_norm_g[0], conv_norm_b[0], batch, seq)
    h = _even_out(h, outs, lses, bconv, even_w_out)
    h = _moe(h, ffn_norm[0], router_w1[0], router_b1[0], router_w2[0], router_b2[0],
             expert_w_gate, expert_w_up, expert_w_down, 0, batch)

    inv_freq = RET_ROT_THETA ** (-jnp.linspace(0.0, 1.0, C_QK_DIM // 2, dtype=F32))
    ang = jnp.arange(seq, dtype=F32)[:, None] * inv_freq[None, :]
    proj = _odd_in(h, mix_norm[1], odd_w_in, jnp.cos(ang), jnp.sin(ang), seq)
    log_decay = jnp.log(1.0 - jnp.exp2(-5.0 - jnp.arange(C_HEADS, dtype=F32)))
    y = _retention(proj, log_decay, batch, seq)
    h = _odd_out(h, y, odd_w_out)
    out = _moe(h, ffn_norm[1], router_w1[1], router_b1[1], router_w2[1], router_b2[1],
               expert_w_gate, expert_w_up, expert_w_down, 1, batch, final_gain=final_norm)
    return out.reshape(batch, seq, d)
```

```python
import functools

import jax
import jax.numpy as jnp
from jax import lax
from jax.experimental import pallas as pl
from jax.experimental.pallas import tpu as pltpu

F32 = jnp.float32
BF16 = jnp.bfloat16
I32 = jnp.int32

NORM_EPS = 1e-6
NEG_INF = -1e30

D_MODEL = 1024
A_HEADS = 8
A_HEAD_DIM = 64
A_WIDTH = A_HEADS * A_HEAD_DIM
A_BRANCHES = ((128, 1), (512, 4), (2048, 16))
ATTN_BLOCK = 128
ROPE_THETA = 500000.0
ROPE_DIM = A_HEAD_DIM // 4
B_WIDTH = D_MODEL - A_WIDTH
CONV_WIDTH = 31
C_HEADS = 4
C_QK_DIM = 256
C_V_DIM = 512
C_QK_WIDTH = C_HEADS * C_QK_DIM
C_V_WIDTH = C_HEADS * C_V_DIM
RET_CHUNK = 128
RET_ROT_THETA = 10000.0
MOE_GROUPS = 4
MOE_EXPERTS_PER_GROUP = 8
MOE_EXPERTS = MOE_GROUPS * MOE_EXPERTS_PER_GROUP
EXPERT_HIDDEN = 512
MOE_BLOCK = 128
EVEN_IN_WIDTH = 3 * A_WIDTH + 2 * B_WIDTH
ODD_IN_WIDTH = 2 * C_QK_WIDTH + 2 * C_V_WIDTH

LANES = 128
ROW_TILE = 512
CONV_TILE = 512
CONV_HALO = 32
CONV_CHUNK = 64
ROUTER_LANE0 = MOE_GROUPS
VMEM_LIMIT = 56 * 1024 * 1024
VMEM_LIMIT_EXPERTS = 60 * 1024 * 1024
SLABS = D_MODEL // LANES
RANK_BITS = 16
GATHER_PITCH = MOE_BLOCK + 8
DUMMY_SLOTS = ROW_TILE


def _cparams(sem, vmem=VMEM_LIMIT):
    return pltpu.CompilerParams(dimension_semantics=sem, vmem_limit_bytes=vmem)


def _cast_once(w_ref, wb_ref, first):
    @pl.when(first)
    def _():
        wb_ref[...] = w_ref[...].astype(BF16)


def _rms(x, gain):
    ms = jnp.mean(x * x, axis=-1, keepdims=True)
    return x * lax.rsqrt(ms + NORM_EPS) * gain


def _even_in_kernel(h_ref, g_ref, w_ref, c_ref, s1_ref, s2_ref, qkv_ref, conv_ref, wb_ref):
    _cast_once(w_ref, wb_ref, pl.program_id(0) == 0)
    u = _rms(h_ref[...], g_ref[...]).astype(BF16)
    acc = jnp.dot(u, wb_ref[...], preferred_element_type=F32)
    c, s1, s2 = c_ref[...], s1_ref[...], s2_ref[...]
    for j in range(2 * A_WIDTH // LANES):
        xg = acc[:, j * LANES:(j + 1) * LANES]
        if j < A_WIDTH // LANES:
            xg = xg * (A_HEAD_DIM ** -0.5)
        r = xg * c + pltpu.roll(xg, LANES - ROPE_DIM // 2, 1) * s1 + pltpu.roll(xg, ROPE_DIM // 2, 1) * s2
        qkv_ref[:, j * LANES:(j + 1) * LANES] = r.astype(BF16)
    qkv_ref[:, 2 * A_WIDTH:3 * A_WIDTH] = acc[:, 2 * A_WIDTH:3 * A_WIDTH].astype(BF16)
    conv_ref[...] = acc[:, 3 * A_WIDTH:].astype(BF16)


def _even_in(h, gain, w, tabs, seq):
    t = h.shape[0]
    tm = ROW_TILE
    nseq = seq // tm
    tab_spec = pl.BlockSpec((tm, LANES), lambda i: (i % nseq, 0))
    return pl.pallas_call(
        _even_in_kernel,
        grid=(t // tm,),
        in_specs=[
            pl.BlockSpec((tm, D_MODEL), lambda i: (i, 0)),
            pl.BlockSpec((1, D_MODEL), lambda i: (0, 0)),
            pl.BlockSpec((None, D_MODEL, EVEN_IN_WIDTH), lambda i: (0, 0, 0)),
            tab_spec, tab_spec, tab_spec,
        ],
        out_specs=[
            pl.BlockSpec((tm, 3 * A_WIDTH), lambda i: (i, 0)),
            pl.BlockSpec((tm, 2 * B_WIDTH), lambda i: (i, 0)),
        ],
        out_shape=[
            jax.ShapeDtypeStruct((t, 3 * A_WIDTH), BF16),
            jax.ShapeDtypeStruct((t, 2 * B_WIDTH), BF16),
        ],
        scratch_shapes=[pltpu.VMEM((D_MODEL, EVEN_IN_WIDTH), BF16)],
        compiler_params=_cparams(("arbitrary",)),
        name="even_in_proj",
    )(h, gain.reshape(1, D_MODEL), w, *tabs)


def _attn_rope_tables(seq):
    half = ROPE_DIM // 2
    inv_freq = ROPE_THETA ** (-jnp.arange(0, ROPE_DIM, 2, dtype=F32) / ROPE_DIM)
    ang = jnp.arange(seq, dtype=F32)[:, None] * inv_freq[None, :]
    cos, sin = jnp.cos(ang), jnp.sin(ang)
    rest = A_HEAD_DIM - ROPE_DIM
    ones = jnp.ones((seq, rest), F32)
    z_rest = jnp.zeros((seq, rest), F32)
    z_half = jnp.zeros((seq, half), F32)
    c = jnp.concatenate([cos, cos, ones], axis=1)
    s1 = jnp.concatenate([-sin, z_half, z_rest], axis=1)
    s2 = jnp.concatenate([z_half, sin, z_rest], axis=1)
    rep = LANES // A_HEAD_DIM
    return tuple(jnp.tile(a, (1, rep)) for a in (c, s1, s2))


def _attn_kernel(q_ref, kp_ref, kc_ref, vp_ref, vc_ref, o_ref, lse_ref):
    n = pl.program_id(2)
    blk = ATTN_BLOCK
    qi = lax.broadcasted_iota(I32, (blk, 2 * blk), 0)
    kj = lax.broadcasted_iota(I32, (blk, 2 * blk), 1)
    dist = blk + qi - kj
    valid = (dist >= 0) & (dist <= blk) & ((kj >= blk) | (n > 0))
    valid2 = jnp.concatenate([valid, valid], axis=0)
    lane = lax.broadcasted_iota(I32, (blk, LANES), 1)
    lo = lane < A_HEAD_DIM
    lse_tile = jnp.zeros((blk, LANES), F32)
    for g in range(A_WIDTH // LANES):
        sl = slice(g * LANES, (g + 1) * LANES)
        q = q_ref[:, sl].astype(F32)
        q2 = jnp.concatenate([jnp.where(lo, q, 0.0), jnp.where(lo, 0.0, q)], axis=0).astype(BF16)
        kk = jnp.concatenate([kp_ref[:, sl], kc_ref[:, sl]], axis=0)
        vv = jnp.concatenate([vp_ref[:, sl], vc_ref[:, sl]], axis=0)
        s = lax.dot_general(q2, kk, (((1,), (1,)), ((), ())), preferred_element_type=F32)
        s = jnp.where(valid2, s, NEG_INF)
        m = jnp.max(s, axis=-1, keepdims=True)
        e = jnp.exp(s - m)
        den = jnp.sum(e, axis=-1, keepdims=True)
        pv = jnp.dot(e.astype(BF16), vv, preferred_element_type=F32) / den
        o_ref[:, sl] = jnp.where(lo, pv[:blk], pv[blk:]).astype(BF16)
        lse = m + jnp.log(den)
        lse_tile = jnp.where(lane == 2 * g, lse[:blk], lse_tile)
        lse_tile = jnp.where(lane == 2 * g + 1, lse[blk:], lse_tile)
    lse_ref[...] = lse_tile


def _attn_branch(qkv_d, batch, seq, dil):
    blk = ATTN_BLOCK
    nb = seq // dil // blk
    rows = qkv_d.shape[0]

    def cur(col):
        return pl.BlockSpec((blk, A_WIDTH), lambda b, r, n: (b * nb + n, 3 * r + col))

    def prev(col):
        return pl.BlockSpec((blk, A_WIDTH), lambda b, r, n: (b * nb + jnp.maximum(n - 1, 0), 3 * r + col))

    return pl.pallas_call(
        _attn_kernel,
        grid=(batch, dil, nb),
        in_specs=[cur(0), prev(1), cur(1), prev(2), cur(2)],
        out_specs=[
            pl.BlockSpec((blk, A_WIDTH), lambda b, r, n: (b * nb + n, r)),
            pl.BlockSpec((blk, LANES), lambda b, r, n: (b * nb + n, r)),
        ],
        out_shape=[
            jax.ShapeDtypeStruct((rows, dil * A_WIDTH), BF16),
            jax.ShapeDtypeStruct((rows, dil * LANES), F32),
        ],
        compiler_params=_cparams(("parallel", "parallel", "arbitrary")),
        name=f"dilated_attn_d{dil}",
    )(qkv_d, qkv_d, qkv_d, qkv_d, qkv_d)


def _conv_kernel(val_ref, gate_ref, w_ref, b_ref, g_ref, beta_ref, o_ref, abuf):
    n = pl.program_id(1)
    tc = CONV_TILE

    @pl.when(n == 0)
    def _():
        abuf[0:CONV_HALO, :] = jnp.zeros((CONV_HALO, B_WIDTH), F32)

    @pl.when(n > 0)
    def _():
        abuf[0:CONV_HALO, :] = abuf[tc:tc + CONV_HALO, :]

    val = val_ref[...].astype(F32)
    gate = gate_ref[...].astype(F32)
    abuf[CONV_HALO:CONV_HALO + tc, :] = val * jax.nn.sigmoid(gate)
    off = CONV_HALO - (CONV_WIDTH - 1)
    for c in range(tc // CONV_CHUNK):
        acc = jnp.broadcast_to(b_ref[...], (CONV_CHUNK, B_WIDTH))
        for j in range(CONV_WIDTH):
            r0 = c * CONV_CHUNK + off + j
            acc = acc + w_ref[j:j + 1, :] * abuf[r0:r0 + CONV_CHUNK, :]
        mu = jnp.mean(acc, axis=-1, keepdims=True)
        cen = acc - mu
        var = jnp.mean(cen * cen, axis=-1, keepdims=True)
        yn = cen * lax.rsqrt(var + NORM_EPS) * g_ref[...] + beta_ref[...]
        o_ref[c * CONV_CHUNK:(c + 1) * CONV_CHUNK, :] = (yn * jax.nn.sigmoid(yn)).astype(BF16)


def _conv(conv_in, w, b, g, beta, batch, seq):
    t = conv_in.shape[0]
    tc = CONV_TILE
    nt = seq // tc
    w_pad = jnp.concatenate([w, jnp.zeros((CONV_HALO - CONV_WIDTH, B_WIDTH), F32)], axis=0)
    vec = pl.BlockSpec((1, B_WIDTH), lambda bb, n: (0, 0))
    return pl.pallas_call(
        _conv_kernel,
        grid=(batch, nt),
        in_specs=[
            pl.BlockSpec((tc, B_WIDTH), lambda bb, n: (bb * nt + n, 0)),
            pl.BlockSpec((tc, B_WIDTH), lambda bb, n: (bb * nt + n, 1)),
            pl.BlockSpec((CONV_HALO, B_WIDTH), lambda bb, n: (0, 0)),
            vec, vec, vec,
        ],
        out_specs=pl.BlockSpec((tc, B_WIDTH), lambda bb, n: (bb * nt + n, 0)),
        out_shape=jax.ShapeDtypeStruct((t, B_WIDTH), BF16),
        scratch_shapes=[pltpu.VMEM((tc + CONV_HALO, B_WIDTH), F32)],
        compiler_params=_cparams(("parallel", "arbitrary")),
        name="conformer_conv",
    )(conv_in, conv_in, w_pad, b.reshape(1, -1), g.reshape(1, -1), beta.reshape(1, -1))


def _even_out_kernel(h_ref, o1_ref, o2_ref, o3_ref, l1_ref, l2_ref, l3_ref, bc_ref, w_ref, out_ref, wb_ref):
    _cast_once(w_ref, wb_ref, pl.program_id(0) == 0)
    l1, l2, l3 = l1_ref[...], l2_ref[...], l3_ref[...]
    m = jnp.maximum(jnp.maximum(l1, l2), l3)
    e1, e2, e3 = jnp.exp(l1 - m), jnp.exp(l2 - m), jnp.exp(l3 - m)
    inv = 1.0 / (e1 + e2 + e3)
    row = lax.broadcasted_iota(I32, (LANES, A_WIDTH), 0)
    col = lax.broadcasted_iota(I32, (LANES, A_WIDTH), 1)
    expand = (col // A_HEAD_DIM == row).astype(F32)
    a = jnp.zeros(o1_ref.shape, F32)
    for e, o_ref in ((e1, o1_ref), (e2, o2_ref), (e3, o3_ref)):
        wfull = jnp.dot(e * inv, expand, precision=lax.Precision.HIGHEST, preferred_element_type=F32)
        a = a + wfull * o_ref[...].astype(F32)
    acc = jnp.dot(a.astype(BF16), wb_ref[0:A_WIDTH, :], preferred_element_type=F32)
    acc = acc + jnp.dot(bc_ref[...], wb_ref[A_WIDTH:, :], preferred_element_type=F32)
    out_ref[...] = h_ref[...] + acc


def _even_out(h, outs, lses, bconv, w):
    t = h.shape[0]
    tm = ROW_TILE
    row = lambda width: pl.BlockSpec((tm, width), lambda i: (i, 0))
    return pl.pallas_call(
        _even_out_kernel,
        grid=(t // tm,),
        in_specs=[row(D_MODEL), row(A_WIDTH), row(A_WIDTH), row(A_WIDTH), row(LANES), row(LANES), row(LANES),
                  row(B_WIDTH), pl.BlockSpec((None, D_MODEL, D_MODEL), lambda i: (0, 0, 0))],
        out_specs=row(D_MODEL),
        out_shape=jax.ShapeDtypeStruct((t, D_MODEL), F32),
        scratch_shapes=[pltpu.VMEM((D_MODEL, D_MODEL), BF16)],
        compiler_params=_cparams(("arbitrary",)),
        name="even_out_proj",
    )(h, *outs, *lses, bconv, w)


def _odd_in_kernel(h_ref, g_ref, w_ref, cos_ref, sin_ref, o_ref, wb_ref):
    j = pl.program_id(0)
    _cast_once(w_ref, wb_ref, pl.program_id(1) == 0)
    u = _rms(h_ref[...], g_ref[...]).astype(BF16)
    acc = jnp.dot(u, wb_ref[...], preferred_element_type=F32)

    @pl.when(j == 0)
    def _():
        cos, sin = cos_ref[...], sin_ref[...]
        half = C_QK_DIM // 2
        for hd in range(2 * C_HEADS):
            x1 = acc[:, hd * C_QK_DIM:hd * C_QK_DIM + half]
            x2 = acc[:, hd * C_QK_DIM + half:(hd + 1) * C_QK_DIM]
            r1 = x1 * cos - x2 * sin
            r2 = x2 * cos + x1 * sin
            if hd >= C_HEADS:
                r1 = r1 * (C_QK_DIM ** -0.5)
                r2 = r2 * (C_QK_DIM ** -0.5)
            o_ref[:, hd * C_QK_DIM:hd * C_QK_DIM + half] = r1.astype(BF16)
            o_ref[:, hd * C_QK_DIM + half:(hd + 1) * C_QK_DIM] = r2.astype(BF16)

    @pl.when(j > 0)
    def _():
        o_ref[...] = acc.astype(BF16)


def _odd_in(h, gain, w, cos, sin, seq):
    t = h.shape[0]
    tm = ROW_TILE
    tn = 2 * C_QK_WIDTH
    nseq = seq // tm
    tab = pl.BlockSpec((tm, C_QK_DIM // 2), lambda j, i: (i % nseq, 0))
    return pl.pallas_call(
        _odd_in_kernel,
        grid=(ODD_IN_WIDTH // tn, t // tm),
        in_specs=[
            pl.BlockSpec((tm, D_MODEL), lambda j, i: (i, 0)),
            pl.BlockSpec((1, D_MODEL), lambda j, i: (0, 0)),
            pl.BlockSpec((None, D_MODEL, tn), lambda j, i: (0, 0, j)),
            tab, tab,
        ],
        out_specs=pl.BlockSpec((tm, tn), lambda j, i: (i, j)),
        out_shape=jax.ShapeDtypeStruct((t, ODD_IN_WIDTH), BF16),
        scratch_shapes=[pltpu.VMEM((D_MODEL, tn), BF16)],
        compiler_params=_cparams(("arbitrary", "arbitrary")),
        name="odd_in_proj",
    )(h, gain.reshape(1, D_MODEL), w, cos, sin)


def _ret_kernel(ld_ref, q_ref, k_ref, v_ref, g_ref, o_ref, state):
    hd = pl.program_id(1)
    c = pl.program_id(2)
    ch = RET_CHUNK

    @pl.when(c == 0)
    def _():
        state[...] = jnp.zeros(state.shape, F32)

    ld = ld_ref[hd]
    q, k, v = q_ref[...], k_ref[...], v_ref[...]
    ii = lax.broadcasted_iota(I32, (ch, ch), 0)
    jj = lax.broadcasted_iota(I32, (ch, ch), 1)
    diff = (ii - jj).astype(F32)
    intra = jnp.where(diff >= 0, jnp.exp(ld * jnp.maximum(diff, 0.0)), 0.0)
    s = lax.dot_general(q, k, (((1,), (1,)), ((), ())), preferred_element_type=F32) * intra
    inner = jnp.dot(s.astype(BF16), v, preferred_element_type=F32)
    pos = lax.broadcasted_iota(I32, (ch, 1), 0).astype(F32)
    q_decay = jnp.exp(ld * (pos + 1.0))
    k_decay = jnp.exp(ld * (ch - 1.0 - pos))
    chunk_decay = jnp.exp(ld * jnp.full((1, 1), float(ch), F32))
    st = state[...]
    cross = jnp.dot(q, st.astype(BF16), preferred_element_type=F32) * q_decay
    kd_t = jnp.transpose(k.astype(F32) * k_decay).astype(BF16)
    state[...] = st * chunk_decay + jnp.dot(kd_t, v, preferred_element_type=F32)
    out = inner + cross
    mu = jnp.mean(out, axis=-1, keepdims=True)
    cen = out - mu
    var = jnp.mean(cen * cen, axis=-1, keepdims=True)
    o = cen * lax.rsqrt(var + NORM_EPS)
    gf = g_ref[...].astype(F32)
    o_ref[...] = (gf * jax.nn.sigmoid(gf) * o).astype(BF16)


def _retention(proj, log_decay, batch, seq):
    t = proj.shape[0]
    ch = RET_CHUNK
    nc = seq // ch
    qk_blocks = C_QK_WIDTH // C_QK_DIM
    v_blocks = C_V_WIDTH // C_V_DIM
    v0 = 2 * C_QK_WIDTH // C_V_DIM
    grid_spec = pltpu.PrefetchScalarGridSpec(
        num_scalar_prefetch=1,
        grid=(batch, C_HEADS, nc),
        in_specs=[
            pl.BlockSpec((ch, C_QK_DIM), lambda b, hd, c, ld: (b * nc + c, hd)),
            pl.BlockSpec((ch, C_QK_DIM), lambda b, hd, c, ld: (b * nc + c, qk_blocks + hd)),
            pl.BlockSpec((ch, C_V_DIM), lambda b, hd, c, ld: (b * nc + c, v0 + hd)),
            pl.BlockSpec((ch, C_V_DIM), lambda b, hd, c, ld: (b * nc + c, v0 + v_blocks + hd)),
        ],
        out_specs=pl.BlockSpec((ch, C_V_DIM), lambda b, hd, c, ld: (b * nc + c, hd)),
        scratch_shapes=[pltpu.VMEM((C_QK_DIM, C_V_DIM), F32)],
    )
    return pl.pallas_call(
        _ret_kernel,
        grid_spec=grid_spec,
        out_shape=jax.ShapeDtypeStruct((t, C_V_WIDTH), BF16),
        compiler_params=_cparams(("parallel", "parallel", "arbitrary")),
        name="retention",
    )(log_decay, proj, proj, proj, proj)


def _odd_out_kernel(h_ref, y_ref, w_ref, out_ref, wb_ref):
    _cast_once(w_ref, wb_ref, pl.program_id(0) == 0)
    out_ref[...] = h_ref[...] + jnp.dot(y_ref[...], wb_ref[...], preferred_element_type=F32)


def _odd_out(h, y, w):
    t = h.shape[0]
    tm = ROW_TILE
    return pl.pallas_call(
        _odd_out_kernel,
        grid=(t // tm,),
        in_specs=[pl.BlockSpec((tm, D_MODEL), lambda i: (i, 0)),
                  pl.BlockSpec((tm, C_V_WIDTH), lambda i: (i, 0)),
                  pl.BlockSpec((None, C_V_WIDTH, D_MODEL), lambda i: (0, 0, 0))],
        out_specs=pl.BlockSpec((tm, D_MODEL), lambda i: (i, 0)),
        out_shape=jax.ShapeDtypeStruct((t, D_MODEL), F32),
        scratch_shapes=[pltpu.VMEM((C_V_WIDTH, D_MODEL), BF16)],
        compiler_params=_cparams(("arbitrary",)),
        name="odd_out_proj",
    )(h, y, w)


def _router_kernel(h_ref, g_ref, wr_ref, br_ref, x2s_ref, codes_ref, gates_ref, cnt_ref, base_ref, *,
                   tiles_per_batch):
    i = pl.program_id(0)
    tm = h_ref.shape[0]
    batch = i // tiles_per_batch

    @pl.when(i % tiles_per_batch == 0)
    def _():
        base_ref[...] = jnp.zeros(base_ref.shape, F32)

    x2 = _rms(h_ref[...], g_ref[...])
    for j in range(SLABS):
        x2s_ref[pl.ds(j, tm, stride=SLABS), :] = x2[:, j * LANES:(j + 1) * LANES]
    logits = jnp.dot(x2, wr_ref[...], precision=lax.Precision.HIGHEST, preferred_element_type=F32) + br_ref[...]
    lane = lax.broadcasted_iota(I32, (tm, LANES), 1)
    is_grp = lane < MOE_GROUPS
    lg = jnp.where(is_grp, logits, -jnp.inf)
    gmax = jnp.max(lg, axis=-1, keepdims=True)
    gsum = jnp.sum(jnp.where(is_grp, jnp.exp(logits - gmax), 0.0), axis=-1, keepdims=True)
    gp = 1.0 / gsum
    gi = jnp.min(jnp.where(lg == gmax, lane, LANES), axis=-1, keepdims=True)
    ex = lane - ROUTER_LANE0
    in_grp = (ex >= 0) & (ex < MOE_EXPERTS) & ((ex // MOE_EXPERTS_PER_GROUP) == gi)
    sel = jnp.where(in_grp, logits, -jnp.inf)
    v1 = jnp.max(sel, axis=-1, keepdims=True)
    i1 = jnp.min(jnp.where(sel == v1, lane, LANES), axis=-1, keepdims=True)
    sel2 = jnp.where(lane == i1, -jnp.inf, sel)
    v2 = jnp.max(sel2, axis=-1, keepdims=True)
    i2 = jnp.min(jnp.where(sel2 == v2, lane, LANES), axis=-1, keepdims=True)
    tt = jnp.exp(v2 - v1)
    g0 = gp / (1.0 + tt)
    g1 = gp * tt / (1.0 + tt)
    oh0 = (lane == i1).astype(F32)
    oh1 = (lane == i2).astype(F32)
    oh = oh0 + oh1
    rr = lax.broadcasted_iota(I32, (tm, tm), 0)
    cc = lax.broadcasted_iota(I32, (tm, tm), 1)
    tri = (cc < rr).astype(BF16)
    prefix = jnp.dot(tri, oh.astype(BF16), preferred_element_type=F32)
    tot = base_ref[...] + prefix
    rank0 = jnp.sum(oh0 * tot, axis=-1, keepdims=True).astype(I32)
    rank1 = jnp.sum(oh1 * tot, axis=-1, keepdims=True).astype(I32)
    base_ref[...] = base_ref[...] + jnp.sum(oh, axis=0, keepdims=True)
    eoff = batch * MOE_EXPERTS - ROUTER_LANE0
    code0 = (i1 + eoff) * (1 << RANK_BITS) + rank0
    code1 = (i2 + eoff) * (1 << RANK_BITS) + rank1
    meta = jnp.where(lane == 0, code0, jnp.where(lane == 1, code1, 0))
    codes_ref[...] = jnp.transpose(meta)[0:8, :]
    gates_ref[...] = jnp.where(lane == 0, g0, jnp.where(lane == 1, g1, 0.0))
    cnt_ref[...] = jnp.broadcast_to(base_ref[...], cnt_ref.shape)


def _router(h, gain, wr, br, batch):
    t = h.shape[0]
    tm = ROW_TILE
    tpb = t // batch // tm
    row = lambda width: pl.BlockSpec((tm, width), lambda i: (i, 0))
    fixed = lambda shape: pl.BlockSpec(shape, lambda i: (0, 0))
    return pl.pallas_call(
        functools.partial(_router_kernel, tiles_per_batch=tpb),
        grid=(t // tm,),
        in_specs=[row(D_MODEL), fixed((1, D_MODEL)), fixed((D_MODEL, LANES)), fixed((1, LANES))],
        out_specs=[pl.BlockSpec((tm * SLABS, LANES), lambda i: (i, 0)),
                   pl.BlockSpec((8, tm), lambda i: (0, i)), row(LANES),
                   pl.BlockSpec((8, LANES), lambda i: (i // tpb, 0))],
        out_shape=[
            jax.ShapeDtypeStruct((t * SLABS, LANES), F32),
            jax.ShapeDtypeStruct((8, t), I32),
            jax.ShapeDtypeStruct((t, LANES), F32),
            jax.ShapeDtypeStruct((batch * 8, LANES), F32),
        ],
        scratch_shapes=[pltpu.VMEM((1, LANES), F32)],
        compiler_params=_cparams(("arbitrary",)),
        name="moe_router",
    )(h, gain.reshape(1, D_MODEL), wr, br)


INVERT_UNROLL = 8


def _invert_kernel(trips_ref, codes_ref, pstart_ref, fill_lo_ref, fill_hi_ref, slot_ref, *, batch):
    per_batch = codes_ref.shape[0] // batch
    blk_shift = MOE_BLOCK.bit_length() - 1
    assert MOE_BLOCK == 1 << blk_shift and 2 * MOE_BLOCK <= DUMMY_SLOTS

    def fill_segment(sgm, carry):
        def fill_row(r, c):
            slot_ref[r] = per_batch + (lax.shift_right_logical(r, blk_shift) & 1) * MOE_BLOCK + (r & (MOE_BLOCK - 1))
            return c

        return lax.fori_loop(fill_lo_ref[sgm], fill_hi_ref[sgm], fill_row, carry)

    lax.fori_loop(0, fill_lo_ref.shape[0], fill_segment, 0)
    for b in range(batch):
        def place(i, carry, b=b):
            for v in range(INVERT_UNROLL):
                a = i * INVERT_UNROLL + v
                code = codes_ref[b * per_batch + a]
                row = pstart_ref[lax.shift_right_logical(code, RANK_BITS)] + (code & ((1 << RANK_BITS) - 1))
                slot_ref[row] = a
            return carry

        lax.fori_loop(0, trips_ref[0], place, 0)


def _invert(codes_flat, pad_start, fill_lo, fill_hi, n_rows, batch):
    smem = pl.BlockSpec(memory_space=pltpu.SMEM)
    per_batch = codes_flat.shape[0] // batch
    assert per_batch % INVERT_UNROLL == 0
    trips = jnp.full((1,), per_batch // INVERT_UNROLL, I32)
    return pl.pallas_call(
        functools.partial(_invert_kernel, batch=batch),
        in_specs=[smem] * 5,
        out_specs=smem,
        out_shape=jax.ShapeDtypeStruct((n_rows,), I32),
        name="moe_invert_rows",
    )(trips, codes_flat, pad_start, fill_lo, fill_hi)


def _expert_kernel(be_ref, run_ref, nxt_ref, slot_ref, x2s_ref, wg_hbm, wu_hbm, wd_hbm, ys_ref,
                   xres, wbuf_g, wbuf_u, wbuf_d, wgb_ref, wub_ref, wdb_ref, tile, ybuf, sems, wsems, *,
                   batch, nb, layer):
    b = pl.program_id(0)
    n = pl.program_id(1)
    g = b * nb + n
    last = batch * nb - 1
    blk = MOE_BLOCK
    seq = xres.shape[0] // SLABS
    bstride = 2 * seq + DUMMY_SLOTS
    tcur = g % 2
    ycur = g % 3
    yprev = (g + 2) % 3

    def weight_copies(e, ws):
        pairs = ((wg_hbm, wbuf_g), (wu_hbm, wbuf_u), (wd_hbm, wbuf_d))
        return [pltpu.make_async_copy(src.at[layer, e], dst.at[ws], wsems.at[ws, k])
                for k, (src, dst) in enumerate(pairs)]

    def wait_block(q):
        pltpu.make_async_copy(ybuf.at[q], ys_ref.at[pl.ds(0, blk * SLABS), :], sems.at[q]).wait()

    def gather_row(block, tslot, mi):
        s = slot_ref[block * blk + mi]
        tok = jnp.minimum(jnp.where(s >= seq, s - seq, s), seq - 1)
        slab = xres[pl.ds(pl.multiple_of(tok * SLABS, SLABS), SLABS), :]
        tile[tslot, pl.ds(mi, SLABS, stride=GATHER_PITCH), :] = slab

    def scatter_row(yslot, slot, row0, mi):
        dst = pl.multiple_of((row0 + slot) * SLABS, SLABS)
        pltpu.make_async_copy(ybuf.at[yslot, pl.ds(mi * SLABS, SLABS), :], ys_ref.at[pl.ds(dst, SLABS), :],
                              sems.at[yslot]).start()

    @pl.when(g == 0)
    def _():
        for c in weight_copies(be_ref[0], 0):
            c.start()
        ybuf[...] = jnp.zeros(ybuf.shape, F32)
        for bb in range(batch):
            for c in range(DUMMY_SLOTS // blk):
                start = (bb * bstride + 2 * seq + c * blk) * SLABS
                zero = pltpu.make_async_copy(ybuf.at[0], ys_ref.at[pl.ds(start, blk * SLABS), :], sems.at[0])
                zero.start()
                zero.wait()

    @pl.when(n == 0)
    def _():
        pltpu.sync_copy(x2s_ref.at[pl.ds(pl.multiple_of(b * (seq * SLABS), SLABS), seq * SLABS), :], xres)
        for mi in range(blk):
            gather_row(g, tcur, mi)

    run = run_ref[g]
    ws = run % 2

    @pl.when((g == 0) | (run != run_ref[jnp.maximum(g - 1, 0)]))
    def _():
        for c in weight_copies(be_ref[g], ws):
            c.wait()
        wgb_ref[...] = wbuf_g[ws].astype(BF16)
        wub_ref[...] = wbuf_u[ws].astype(BF16)
        wdb_ref[...] = wbuf_d[ws].astype(BF16)

        @pl.when(nxt_ref[g] >= 0)
        def _():
            for c in weight_copies(nxt_ref[g], 1 - ws):
                c.start()

    @pl.when(g >= 2)
    def _():
        wait_block(ycur)

    nxt_block = jnp.minimum(g + 1, last)
    prev_block = jnp.maximum(g - 1, 0)
    prev_row0 = jnp.where(n == 0, jnp.maximum(b - 1, 0), b) * bstride
    x = jnp.concatenate([tile[tcur, j * GATHER_PITCH:j * GATHER_PITCH + blk, :].astype(BF16)
                         for j in range(SLABS)], axis=1)
    n_chunks = 8
    per = blk // n_chunks
    half = EXPERT_HIDDEN // 2
    quarter = D_MODEL // 4
    acts = []
    hid = None
    for c in range(n_chunks):
        for mi in range(c * per, (c + 1) * per):
            gather_row(nxt_block, 1 - tcur, mi)
        for mi in range(c * per, (c + 1) * per):
            slot = jnp.where(g == 0, 2 * seq + 2 * blk + mi, slot_ref[prev_block * blk + mi])
            scatter_row(yprev, slot, prev_row0, mi)
        if c < 4:
            w_ref = wgb_ref if c < 2 else wub_ref
            acts.append(jnp.dot(x, w_ref[:, (c % 2) * half:(c % 2 + 1) * half], preferred_element_type=F32))
        if c == 3:
            a = jnp.concatenate(acts[0:2], axis=1)
            u = jnp.concatenate(acts[2:4], axis=1)
            hid = (a * jax.nn.sigmoid(a) * u).astype(BF16)
        if c >= 4:
            q = c - 4
            yq = jnp.dot(hid, wdb_ref[:, q * quarter:(q + 1) * quarter], preferred_element_type=F32)
            for jj in range(quarter // LANES):
                j = q * (quarter // LANES) + jj
                ybuf[ycur, pl.ds(j, blk, stride=SLABS), :] = yq[:, jj * LANES:(jj + 1) * LANES]

    @pl.when(g == last)
    def _():
        wait_block((last - 2) % 3)
        for mi in range(blk):
            scatter_row(last % 3, slot_ref[last * blk + mi], (batch - 1) * bstride, mi)
        wait_block((last - 1) % 3)
        wait_block(last % 3)


def _experts(block_expert, row_slot, x2s, wg, wu, wd, layer, batch, seq):
    blk = MOE_BLOCK
    nb = row_slot.shape[0] // blk // batch
    steps = batch * nb
    assert steps >= 3
    idx = jnp.arange(steps, dtype=I32)
    change = jnp.concatenate([jnp.zeros((1,), I32), (block_expert[1:] != block_expert[:-1]).astype(I32)])
    run = jnp.cumsum(change).astype(I32)
    later_change = (idx[None, :] > idx[:, None]) & (change[None, :] > 0)
    nxt_idx = jnp.min(jnp.where(later_change, idx[None, :], steps), axis=1)
    nxt = jnp.where(nxt_idx < steps, block_expert[jnp.minimum(nxt_idx, steps - 1)], -1).astype(I32)
    hbm = pl.BlockSpec(memory_space=pl.ANY)
    grid_spec = pltpu.PrefetchScalarGridSpec(
        num_scalar_prefetch=4,
        grid=(batch, nb),
        in_specs=[hbm, hbm, hbm, hbm],
        out_specs=hbm,
        scratch_shapes=[
            pltpu.VMEM((seq * SLABS, LANES), F32),
            pltpu.VMEM((2, D_MODEL, EXPERT_HIDDEN), F32), pltpu.VMEM((2, D_MODEL, EXPERT_HIDDEN), F32),
            pltpu.VMEM((2, EXPERT_HIDDEN, D_MODEL), F32),
            pltpu.VMEM((D_MODEL, EXPERT_HIDDEN), BF16), pltpu.VMEM((D_MODEL, EXPERT_HIDDEN), BF16),
            pltpu.VMEM((EXPERT_HIDDEN, D_MODEL), BF16),
            pltpu.VMEM((2, SLABS * GATHER_PITCH, LANES), F32),
            pltpu.VMEM((3, blk * SLABS, LANES), F32),
            pltpu.SemaphoreType.DMA((3,)),
            pltpu.SemaphoreType.DMA((2, 3)),
        ],
    )
    return pl.pallas_call(
        functools.partial(_expert_kernel, batch=batch, nb=nb, layer=layer),
        grid_spec=grid_spec,
        out_shape=jax.ShapeDtypeStruct((batch * (2 * seq + DUMMY_SLOTS) * SLABS, LANES), F32),
        compiler_params=_cparams(("arbitrary", "arbitrary"), VMEM_LIMIT_EXPERTS),
        name="moe_experts",
    )(block_expert, run, nxt, row_slot, x2s, wg, wu, wd)


def _combine_kernel(h_ref, gates_ref, y0_ref, y1_ref, *rest, final):
    if final:
        fg_ref, out_ref = rest
    else:
        (out_ref,) = rest
    tm = h_ref.shape[0]
    gates = gates_ref[...]
    g0, g1 = gates[:, 0:1], gates[:, 1:2]
    parts = []
    for j in range(SLABS):
        y0 = y0_ref[pl.ds(j, tm, stride=SLABS), :]
        y1 = y1_ref[pl.ds(j, tm, stride=SLABS), :]
        parts.append(g0 * y0 + g1 * y1)
    out = h_ref[...] + jnp.concatenate(parts, axis=1)
    if final:
        out = _rms(out, fg_ref[...])
    out_ref[...] = out


def _combine(h, gates, ys, batch, seq, final_gain=None):
    t = h.shape[0]
    tm = ROW_TILE
    tpb = seq // tm
    bstride = 2 * tpb + DUMMY_SLOTS // tm
    final = final_gain is not None
    in_specs = [pl.BlockSpec((tm, D_MODEL), lambda i: (i, 0)),
                pl.BlockSpec((tm, LANES), lambda i: (i, 0)),
                pl.BlockSpec((tm * SLABS, LANES), lambda i: (i // tpb * bstride + i % tpb, 0)),
                pl.BlockSpec((tm * SLABS, LANES), lambda i: (i // tpb * bstride + tpb + i % tpb, 0))]
    args = [h, gates, ys, ys]
    if final:
        in_specs.append(pl.BlockSpec((1, D_MODEL), lambda i: (0, 0)))
        args.append(final_gain.reshape(1, D_MODEL))
    return pl.pallas_call(
        functools.partial(_combine_kernel, final=final),
        grid=(t // tm,),
        in_specs=in_specs,
        out_specs=pl.BlockSpec((tm, D_MODEL), lambda i: (i, 0)),
        out_shape=jax.ShapeDtypeStruct((t, D_MODEL), F32),
        compiler_params=_cparams(("parallel",)),
        name="moe_combine_final" if final else "moe_combine",
    )(*args)


def _moe(h, gain, w_r1, b_r1, w_r2, b_r2, wg, wu, wd, layer, batch, final_gain=None):
    t = h.shape[0]
    seq = t // batch
    assert seq <= (1 << RANK_BITS) and seq % ROW_TILE == 0 and DUMMY_SLOTS % ROW_TILE == 0
    pad_w = jnp.zeros((D_MODEL, LANES - MOE_GROUPS - MOE_EXPERTS), F32)
    wr = jnp.concatenate([w_r1, w_r2, pad_w], axis=1)
    br = jnp.concatenate([b_r1, b_r2, jnp.zeros((LANES - MOE_GROUPS - MOE_EXPERTS,), F32)]).reshape(1, LANES)
    x2s, codes, gates, counts = _router(h, gain, wr, br, batch)
    cnt = counts.reshape(batch, 8, LANES)[:, 0, ROUTER_LANE0:ROUTER_LANE0 + MOE_EXPERTS].astype(I32)
    padded = (cnt + MOE_BLOCK - 1) // MOE_BLOCK * MOE_BLOCK
    pad_end = jnp.cumsum(padded, axis=1)
    rows_pb = seq * 2 + MOE_EXPERTS * MOE_BLOCK
    nb = rows_pb // MOE_BLOCK
    pad_start = pad_end - padded + (jnp.arange(batch, dtype=I32) * rows_pb)[:, None]
    block_start = jnp.arange(nb, dtype=I32) * MOE_BLOCK
    block_expert = jnp.minimum(
        jnp.sum((pad_end[:, None, :] <= block_start[None, :, None]).astype(I32), axis=2), MOE_EXPERTS - 1)
    codes_flat = codes[0:2].reshape(2, batch, seq).transpose(1, 0, 2).reshape(-1)
    batch_end = ((jnp.arange(batch, dtype=I32) + 1) * rows_pb)[:, None]
    fill_lo = jnp.concatenate([pad_start + cnt, pad_start[:, -1:] + padded[:, -1:]], axis=1)
    fill_hi = jnp.concatenate([pad_start + padded, batch_end], axis=1)
    row_slot = _invert(codes_flat, pad_start.reshape(-1), fill_lo.reshape(-1), fill_hi.reshape(-1),
                       batch * rows_pb, batch)
    ys = _experts(block_expert.reshape(-1), row_slot, x2s, wg, wu, wd, layer, batch, seq)
    return _combine(h, gates, ys, batch, seq, final_gain)


def kernel(x, mix_norm, ffn_norm, final_norm, even_w_in, even_w_out, conv_w, conv_b, conv_norm_g, conv_norm_b,
           odd_w_in, odd_w_out, router_w1, router_b1, router_w2, router_b2, expert_w_gate, expert_w_up,
           expert_w_down):
    batch, seq, d = x.shape
    assert d == D_MODEL and seq % (ATTN_BLOCK * max(dl for _, dl in A_BRANCHES)) == 0
    assert all(w // dl == ATTN_BLOCK for w, dl in A_BRANCHES)
    t = batch * seq
    h = x.reshape(t, d)

    qkv, conv_in = _even_in(h, mix_norm[0], even_w_in, _attn_rope_tables(seq), seq)
    outs, lses = [], []
    for _, dil in A_BRANCHES:
        o_d, lse_d = _attn_branch(qkv.reshape(t // dil, dil * 3 * A_WIDTH), batch, seq, dil)
        outs.append(o_d.reshape(t, A_WIDTH))
        lses.append(lse_d.reshape(t, LANES))
    bconv = _conv(conv_in, conv_w[0], conv_b[0], conv_norm_g[0], conv_norm_b[0], batch, seq)
    h = _even_out(h, outs, lses, bconv, even_w_out)
    h = _moe(h, ffn_norm[0], router_w1[0], router_b1[0], router_w2[0], router_b2[0],
             expert_w_gate, expert_w_up, expert_w_down, 0, batch)

    inv_freq = RET_ROT_THETA ** (-jnp.linspace(0.0, 1.0, C_QK_DIM // 2, dtype=F32))
    ang = jnp.arange(seq, dtype=F32)[:, None] * inv_freq[None, :]
    proj = _odd_in(h, mix_norm[1], odd_w_in, jnp.cos(ang), jnp.sin(ang), seq)
    log_decay = jnp.log(1.0 - jnp.exp2(-5.0 - jnp.arange(C_HEADS, dtype=F32)))
    y = _retention(proj, log_decay, batch, seq)
    h = _odd_out(h, y, odd_w_out)
    out = _moe(h, ffn_norm[1], router_w1[1], router_b1[1], router_w2[1], router_b2[1],
               expert_w_gate, expert_w_up, expert_w_down, 1, batch, final_gain=final_norm)
    return out.reshape(batch, seq, d)
```

```python
import functools

import jax
import jax.numpy as jnp
from jax import lax
from jax.experimental import pallas as pl
from jax.experimental.pallas import tpu as pltpu

F32 = jnp.float32
BF16 = jnp.bfloat16
I32 = jnp.int32

NORM_EPS = 1e-6
NEG_INF = -1e30

D_MODEL = 1024
A_HEADS = 8
A_HEAD_DIM = 64
A_WIDTH = A_HEADS * A_HEAD_DIM
A_BRANCHES = ((128, 1), (512, 4), (2048, 16))
ATTN_BLOCK = 128
ROPE_THETA = 500000.0
ROPE_DIM = A_HEAD_DIM // 4
B_WIDTH = D_MODEL - A_WIDTH
CONV_WIDTH = 31
C_HEADS = 4
C_QK_DIM = 256
C_V_DIM = 512
C_QK_WIDTH = C_HEADS * C_QK_DIM
C_V_WIDTH = C_HEADS * C_V_DIM
RET_CHUNK = 128
RET_ROT_THETA = 10000.0
MOE_GROUPS = 4
MOE_EXPERTS_PER_GROUP = 8
MOE_EXPERTS = MOE_GROUPS * MOE_EXPERTS_PER_GROUP
EXPERT_HIDDEN = 512
MOE_BLOCK = 128
EVEN_IN_WIDTH = 3 * A_WIDTH + 2 * B_WIDTH
ODD_IN_WIDTH = 2 * C_QK_WIDTH + 2 * C_V_WIDTH

LANES = 128
ROW_TILE = 512
CONV_TILE = 512
CONV_HALO = 32
CONV_CHUNK = 64
ROUTER_LANE0 = MOE_GROUPS
VMEM_LIMIT = 56 * 1024 * 1024
VMEM_LIMIT_EXPERTS = 60 * 1024 * 1024
SLABS = D_MODEL // LANES
RANK_BITS = 16
GATHER_PITCH = MOE_BLOCK + 8
DUMMY_SLOTS = ROW_TILE


def _cparams(sem, vmem=VMEM_LIMIT):
    return pltpu.CompilerParams(dimension_semantics=sem, vmem_limit_bytes=vmem)


def _cast_once(w_ref, wb_ref, first):
    @pl.when(first)
    def _():
        wb_ref[...] = w_ref[...].astype(BF16)


def _rms(x, gain):
    ms = jnp.mean(x * x, axis=-1, keepdims=True)
    return x * lax.rsqrt(ms + NORM_EPS) * gain


def _even_in_kernel(h_ref, g_ref, w_ref, c_ref, s1_ref, s2_ref, qkv_ref, conv_ref, wb_ref):
    _cast_once(w_ref, wb_ref, pl.program_id(0) == 0)
    u = _rms(h_ref[...], g_ref[...]).astype(BF16)
    acc = jnp.dot(u, wb_ref[...], preferred_element_type=F32)
    c, s1, s2 = c_ref[...], s1_ref[...], s2_ref[...]
    for j in range(2 * A_WIDTH // LANES):
        xg = acc[:, j * LANES:(j + 1) * LANES]
        if j < A_WIDTH // LANES:
            xg = xg * (A_HEAD_DIM ** -0.5)
        r = xg * c + pltpu.roll(xg, LANES - ROPE_DIM // 2, 1) * s1 + pltpu.roll(xg, ROPE_DIM // 2, 1) * s2
        qkv_ref[:, j * LANES:(j + 1) * LANES] = r.astype(BF16)
    qkv_ref[:, 2 * A_WIDTH:3 * A_WIDTH] = acc[:, 2 * A_WIDTH:3 * A_WIDTH].astype(BF16)
    conv_ref[...] = acc[:, 3 * A_WIDTH:].astype(BF16)


def _even_in(h, gain, w, tabs, seq):
    t = h.shape[0]
    tm = ROW_TILE
    nseq = seq // tm
    tab_spec = pl.BlockSpec((tm, LANES), lambda i: (i % nseq, 0))
    return pl.pallas_call(
        _even_in_kernel,
        grid=(t // tm,),
        in_specs=[
            pl.BlockSpec((tm, D_MODEL), lambda i: (i, 0)),
            pl.BlockSpec((1, D_MODEL), lambda i: (0, 0)),
            pl.BlockSpec((None, D_MODEL, EVEN_IN_WIDTH), lambda i: (0, 0, 0)),
            tab_spec, tab_spec, tab_spec,
        ],
        out_specs=[
            pl.BlockSpec((tm, 3 * A_WIDTH), lambda i: (i, 0)),
            pl.BlockSpec((tm, 2 * B_WIDTH), lambda i: (i, 0)),
        ],
        out_shape=[
            jax.ShapeDtypeStruct((t, 3 * A_WIDTH), BF16),
            jax.ShapeDtypeStruct((t, 2 * B_WIDTH), BF16),
        ],
        scratch_shapes=[pltpu.VMEM((D_MODEL, EVEN_IN_WIDTH), BF16)],
        compiler_params=_cparams(("arbitrary",)),
        name="even_in_proj",
    )(h, gain.reshape(1, D_MODEL), w, *tabs)


def _attn_rope_tables(seq):
    half = ROPE_DIM // 2
    inv_freq = ROPE_THETA ** (-jnp.arange(0, ROPE_DIM, 2, dtype=F32) / ROPE_DIM)
    ang = jnp.arange(seq, dtype=F32)[:, None] * inv_freq[None, :]
    cos, sin = jnp.cos(ang), jnp.sin(ang)
    rest = A_HEAD_DIM - ROPE_DIM
    ones = jnp.ones((seq, rest), F32)
    z_rest = jnp.zeros((seq, rest), F32)
    z_half = jnp.zeros((seq, half), F32)
    c = jnp.concatenate([cos, cos, ones], axis=1)
    s1 = jnp.concatenate([-sin, z_half, z_rest], axis=1)
    s2 = jnp.concatenate([z_half, sin, z_rest], axis=1)
    rep = LANES // A_HEAD_DIM
    return tuple(jnp.tile(a, (1, rep)) for a in (c, s1, s2))


def _attn_kernel(q_ref, kp_ref, kc_ref, vp_ref, vc_ref, o_ref, lse_ref):
    n = pl.program_id(2)
    blk = ATTN_BLOCK
    qi = lax.broadcasted_iota(I32, (blk, 2 * blk), 0)
    kj = lax.broadcasted_iota(I32, (blk, 2 * blk), 1)
    dist = blk + qi - kj
    valid = (dist >= 0) & (dist <= blk) & ((kj >= blk) | (n > 0))
    valid2 = jnp.concatenate([valid, valid], axis=0)
    lane = lax.broadcasted_iota(I32, (blk, LANES), 1)
    lo = lane < A_HEAD_DIM
    lse_tile = jnp.zeros((blk, LANES), F32)
    for g in range(A_WIDTH // LANES):
        sl = slice(g * LANES, (g + 1) * LANES)
        q = q_ref[:, sl].astype(F32)
        q2 = jnp.concatenate([jnp.where(lo, q, 0.0), jnp.where(lo, 0.0, q)], axis=0).astype(BF16)
        kk = jnp.concatenate([kp_ref[:, sl], kc_ref[:, sl]], axis=0)
        vv = jnp.concatenate([vp_ref[:, sl], vc_ref[:, sl]], axis=0)
        s = lax.dot_general(q2, kk, (((1,), (1,)), ((), ())), preferred_element_type=F32)
        s = jnp.where(valid2, s, NEG_INF)
        m = jnp.max(s, axis=-1, keepdims=True)
        e = jnp.exp(s - m)
        den = jnp.sum(e, axis=-1, keepdims=True)
        pv = jnp.dot(e.astype(BF16), vv, preferred_element_type=F32) / den
        o_ref[:, sl] = jnp.where(lo, pv[:blk], pv[blk:]).astype(BF16)
        lse = m + jnp.log(den)
        lse_tile = jnp.where(lane == 2 * g, lse[:blk], lse_tile)
        lse_tile = jnp.where(lane == 2 * g + 1, lse[blk:], lse_tile)
    lse_ref[...] = lse_tile


def _attn_branch(qkv_d, batch, seq, dil):
    blk = ATTN_BLOCK
    nb = seq // dil // blk
    rows = qkv_d.shape[0]

    def cur(col):
        return pl.BlockSpec((blk, A_WIDTH), lambda b, r, n: (b * nb + n, 3 * r + col))

    def prev(col):
        return pl.BlockSpec((blk, A_WIDTH), lambda b, r, n: (b * nb + jnp.maximum(n - 1, 0), 3 * r + col))

    return pl.pallas_call(
        _attn_kernel,
        grid=(batch, dil, nb),
        in_specs=[cur(0), prev(1), cur(1), prev(2), cur(2)],
        out_specs=[
            pl.BlockSpec((blk, A_WIDTH), lambda b, r, n: (b * nb + n, r)),
            pl.BlockSpec((blk, LANES), lambda b, r, n: (b * nb + n, r)),
        ],
        out_shape=[
            jax.ShapeDtypeStruct((rows, dil * A_WIDTH), BF16),
            jax.ShapeDtypeStruct((rows, dil * LANES), F32),
        ],
        compiler_params=_cparams(("parallel", "parallel", "arbitrary")),
        name=f"dilated_attn_d{dil}",
    )(qkv_d, qkv_d, qkv_d, qkv_d, qkv_d)


def _conv_kernel(val_ref, gate_ref, w_ref, b_ref, g_ref, beta_ref, o_ref, abuf, shifted):
    n = pl.program_id(1)
    tc = CONV_TILE
    sub = 8

    @pl.when(n == 0)
    def _():
        abuf[0:CONV_HALO, :] = jnp.zeros((CONV_HALO, B_WIDTH), F32)

    @pl.when(n > 0)
    def _():
        abuf[0:CONV_HALO, :] = abuf[tc:tc + CONV_HALO, :]

    val = val_ref[...].astype(F32)
    gate = gate_ref[...].astype(F32)
    abuf[CONV_HALO:CONV_HALO + tc, :] = val * jax.nn.sigmoid(gate)
    off = CONV_HALO - (CONV_WIDTH - 1)
    span = shifted.shape[1]
    for s in range(1, sub):
        shifted[s - 1] = abuf[s:s + span, :]
    for c in range(tc // CONV_CHUNK):
        acc = jnp.broadcast_to(b_ref[...], (CONV_CHUNK, B_WIDTH))
        for j in range(CONV_WIDTH):
            s = (off + j) % sub
            r0 = c * CONV_CHUNK + off + j - s
            src = abuf[r0:r0 + CONV_CHUNK, :] if s == 0 else shifted[s - 1, r0:r0 + CONV_CHUNK, :]
            acc = acc + w_ref[j:j + 1, :] * src
        mu = jnp.mean(acc, axis=-1, keepdims=True)
        cen = acc - mu
        var = jnp.mean(cen * cen, axis=-1, keepdims=True)
        yn = cen * lax.rsqrt(var + NORM_EPS) * g_ref[...] + beta_ref[...]
        o_ref[c * CONV_CHUNK:(c + 1) * CONV_CHUNK, :] = (yn * jax.nn.sigmoid(yn)).astype(BF16)


def _conv(conv_in, w, b, g, beta, batch, seq):
    t = conv_in.shape[0]
    tc = CONV_TILE
    nt = seq // tc
    w_pad = jnp.concatenate([w, jnp.zeros((CONV_HALO - CONV_WIDTH, B_WIDTH), F32)], axis=0)
    vec = pl.BlockSpec((1, B_WIDTH), lambda bb, n: (0, 0))
    return pl.pallas_call(
        _conv_kernel,
        grid=(batch, nt),
        in_specs=[
            pl.BlockSpec((tc, B_WIDTH), lambda bb, n: (bb * nt + n, 0)),
            pl.BlockSpec((tc, B_WIDTH), lambda bb, n: (bb * nt + n, 1)),
            pl.BlockSpec((CONV_HALO, B_WIDTH), lambda bb, n: (0, 0)),
            vec, vec, vec,
        ],
        out_specs=pl.BlockSpec((tc, B_WIDTH), lambda bb, n: (bb * nt + n, 0)),
        out_shape=jax.ShapeDtypeStruct((t, B_WIDTH), BF16),
        scratch_shapes=[pltpu.VMEM((tc + CONV_HALO, B_WIDTH), F32),
                        pltpu.VMEM((7, tc + CONV_HALO - 8, B_WIDTH), F32)],
        compiler_params=_cparams(("parallel", "arbitrary")),
        name="conformer_conv",
    )(conv_in, conv_in, w_pad, b.reshape(1, -1), g.reshape(1, -1), beta.reshape(1, -1))


def _even_out_kernel(h_ref, o1_ref, o2_ref, o3_ref, l1_ref, l2_ref, l3_ref, bc_ref, w_ref, out_ref, wb_ref):
    _cast_once(w_ref, wb_ref, pl.program_id(0) == 0)
    l1, l2, l3 = l1_ref[...], l2_ref[...], l3_ref[...]
    m = jnp.maximum(jnp.maximum(l1, l2), l3)
    e1, e2, e3 = jnp.exp(l1 - m), jnp.exp(l2 - m), jnp.exp(l3 - m)
    inv = 1.0 / (e1 + e2 + e3)
    row = lax.broadcasted_iota(I32, (LANES, A_WIDTH), 0)
    col = lax.broadcasted_iota(I32, (LANES, A_WIDTH), 1)
    expand = (col // A_HEAD_DIM == row).astype(F32)
    a = jnp.zeros(o1_ref.shape, F32)
    for e, o_ref in ((e1, o1_ref), (e2, o2_ref), (e3, o3_ref)):
        wfull = jnp.dot(e * inv, expand, precision=lax.Precision.HIGHEST, preferred_element_type=F32)
        a = a + wfull * o_ref[...].astype(F32)
    acc = jnp.dot(a.astype(BF16), wb_ref[0:A_WIDTH, :], preferred_element_type=F32)
    acc = acc + jnp.dot(bc_ref[...], wb_ref[A_WIDTH:, :], preferred_element_type=F32)
    out_ref[...] = h_ref[...] + acc


def _even_out(h, outs, lses, bconv, w):
    t = h.shape[0]
    tm = ROW_TILE
    row = lambda width: pl.BlockSpec((tm, width), lambda i: (i, 0))
    return pl.pallas_call(
        _even_out_kernel,
        grid=(t // tm,),
        in_specs=[row(D_MODEL), row(A_WIDTH), row(A_WIDTH), row(A_WIDTH), row(LANES), row(LANES), row(LANES),
                  row(B_WIDTH), pl.BlockSpec((None, D_MODEL, D_MODEL), lambda i: (0, 0, 0))],
        out_specs=row(D_MODEL),
        out_shape=jax.ShapeDtypeStruct((t, D_MODEL), F32),
        scratch_shapes=[pltpu.VMEM((D_MODEL, D_MODEL), BF16)],
        compiler_params=_cparams(("arbitrary",)),
        name="even_out_proj",
    )(h, *outs, *lses, bconv, w)


def _odd_in_kernel(h_ref, g_ref, w_ref, cos_ref, sin_ref, o_ref, wb_ref):
    j = pl.program_id(0)
    _cast_once(w_ref, wb_ref, pl.program_id(1) == 0)
    u = _rms(h_ref[...], g_ref[...]).astype(BF16)
    acc = jnp.dot(u, wb_ref[...], preferred_element_type=F32)

    @pl.when(j == 0)
    def _():
        cos, sin = cos_ref[...], sin_ref[...]
        half = C_QK_DIM // 2
        for hd in range(2 * C_HEADS):
            x1 = acc[:, hd * C_QK_DIM:hd * C_QK_DIM + half]
            x2 = acc[:, hd * C_QK_DIM + half:(hd + 1) * C_QK_DIM]
            r1 = x1 * cos - x2 * sin
            r2 = x2 * cos + x1 * sin
            if hd >= C_HEADS:
                r1 = r1 * (C_QK_DIM ** -0.5)
                r2 = r2 * (C_QK_DIM ** -0.5)
            o_ref[:, hd * C_QK_DIM:hd * C_QK_DIM + half] = r1.astype(BF16)
            o_ref[:, hd * C_QK_DIM + half:(hd + 1) * C_QK_DIM] = r2.astype(BF16)

    @pl.when(j > 0)
    def _():
        o_ref[...] = acc.astype(BF16)


def _odd_in(h, gain, w, cos, sin, seq):
    t = h.shape[0]
    tm = ROW_TILE
    tn = 2 * C_QK_WIDTH
    nseq = seq // tm
    tab = pl.BlockSpec((tm, C_QK_DIM // 2), lambda j, i: (i % nseq, 0))
    return pl.pallas_call(
        _odd_in_kernel,
        grid=(ODD_IN_WIDTH // tn, t // tm),
        in_specs=[
            pl.BlockSpec((tm, D_MODEL), lambda j, i: (i, 0)),
            pl.BlockSpec((1, D_MODEL), lambda j, i: (0, 0)),
            pl.BlockSpec((None, D_MODEL, tn), lambda j, i: (0, 0, j)),
            tab, tab,
        ],
        out_specs=pl.BlockSpec((tm, tn), lambda j, i: (i, j)),
        out_shape=jax.ShapeDtypeStruct((t, ODD_IN_WIDTH), BF16),
        scratch_shapes=[pltpu.VMEM((D_MODEL, tn), BF16)],
        compiler_params=_cparams(("arbitrary", "arbitrary")),
        name="odd_in_proj",
    )(h, gain.reshape(1, D_MODEL), w, cos, sin)


def _ret_kernel(ld_ref, q_ref, k_ref, v_ref, g_ref, o_ref, state):
    c = pl.program_id(1)
    ch = RET_CHUNK

    @pl.when(c == 0)
    def _():
        state[...] = jnp.zeros(state.shape, F32)

    ii = lax.broadcasted_iota(I32, (ch, ch), 0)
    jj = lax.broadcasted_iota(I32, (ch, ch), 1)
    diff = (ii - jj).astype(F32)
    pos = lax.broadcasted_iota(I32, (ch, 1), 0).astype(F32)
    for hd in range(C_HEADS):
        ld = ld_ref[hd]
        q = q_ref[:, hd * C_QK_DIM:(hd + 1) * C_QK_DIM]
        k = k_ref[:, hd * C_QK_DIM:(hd + 1) * C_QK_DIM]
        v = v_ref[:, hd * C_V_DIM:(hd + 1) * C_V_DIM]
        intra = jnp.where(diff >= 0, jnp.exp(ld * jnp.maximum(diff, 0.0)), 0.0)
        s = lax.dot_general(q, k, (((1,), (1,)), ((), ())), preferred_element_type=F32) * intra
        inner = jnp.dot(s.astype(BF16), v, preferred_element_type=F32)
        q_decay = jnp.exp(ld * (pos + 1.0))
        k_decay = jnp.exp(ld * (ch - 1.0 - pos))
        chunk_decay = jnp.exp(ld * jnp.full((1, 1), float(ch), F32))
        st = state[hd]
        cross = jnp.dot(q, st.astype(BF16), preferred_element_type=F32) * q_decay
        kd_t = jnp.transpose(k.astype(F32) * k_decay).astype(BF16)
        state[hd] = st * chunk_decay + jnp.dot(kd_t, v, preferred_element_type=F32)
        out = inner + cross
        mu = jnp.mean(out, axis=-1, keepdims=True)
        cen = out - mu
        var = jnp.mean(cen * cen, axis=-1, keepdims=True)
        o = cen * lax.rsqrt(var + NORM_EPS)
        gf = g_ref[:, hd * C_V_DIM:(hd + 1) * C_V_DIM].astype(F32)
        o_ref[:, hd * C_V_DIM:(hd + 1) * C_V_DIM] = (gf * jax.nn.sigmoid(gf) * o).astype(BF16)


def _retention(proj, log_decay, batch, seq):
    t = proj.shape[0]
    ch = RET_CHUNK
    nc = seq // ch
    v0 = 2 * C_QK_WIDTH // C_V_WIDTH
    grid_spec = pltpu.PrefetchScalarGridSpec(
        num_scalar_prefetch=1,
        grid=(batch, nc),
        in_specs=[
            pl.BlockSpec((ch, C_QK_WIDTH), lambda b, c, ld: (b * nc + c, 0)),
            pl.BlockSpec((ch, C_QK_WIDTH), lambda b, c, ld: (b * nc + c, 1)),
            pl.BlockSpec((ch, C_V_WIDTH), lambda b, c, ld: (b * nc + c, v0)),
            pl.BlockSpec((ch, C_V_WIDTH), lambda b, c, ld: (b * nc + c, v0 + 1)),
        ],
        out_specs=pl.BlockSpec((ch, C_V_WIDTH), lambda b, c, ld: (b * nc + c, 0)),
        scratch_shapes=[pltpu.VMEM((C_HEADS, C_QK_DIM, C_V_DIM), F32)],
    )
    return pl.pallas_call(
        _ret_kernel,
        grid_spec=grid_spec,
        out_shape=jax.ShapeDtypeStruct((t, C_V_WIDTH), BF16),
        compiler_params=_cparams(("parallel", "arbitrary")),
        name="retention",
    )(log_decay, proj, proj, proj, proj)


def _odd_out_kernel(h_ref, y_ref, w_ref, out_ref, wb_ref):
    _cast_once(w_ref, wb_ref, pl.program_id(0) == 0)
    out_ref[...] = h_ref[...] + jnp.dot(y_ref[...], wb_ref[...], preferred_element_type=F32)


def _odd_out(h, y, w):
    t = h.shape[0]
    tm = ROW_TILE
    return pl.pallas_call(
        _odd_out_kernel,
        grid=(t // tm,),
        in_specs=[pl.BlockSpec((tm, D_MODEL), lambda i: (i, 0)),
                  pl.BlockSpec((tm, C_V_WIDTH), lambda i: (i, 0)),
                  pl.BlockSpec((None, C_V_WIDTH, D_MODEL), lambda i: (0, 0, 0))],
        out_specs=pl.BlockSpec((tm, D_MODEL), lambda i: (i, 0)),
        out_shape=jax.ShapeDtypeStruct((t, D_MODEL), F32),
        scratch_shapes=[pltpu.VMEM((C_V_WIDTH, D_MODEL), BF16)],
        compiler_params=_cparams(("arbitrary",)),
        name="odd_out_proj",
    )(h, y, w)


def _router_kernel(h_ref, g_ref, wr_ref, br_ref, x2s_ref, codes_ref, gates_ref, cnt_ref, base_ref, *,
                   tiles_per_batch):
    i = pl.program_id(0)
    tm = h_ref.shape[0]
    batch = i // tiles_per_batch

    @pl.when(i % tiles_per_batch == 0)
    def _():
        base_ref[...] = jnp.zeros(base_ref.shape, F32)

    x2 = _rms(h_ref[...], g_ref[...])
    for j in range(SLABS):
        x2s_ref[pl.ds(j, tm, stride=SLABS), :] = x2[:, j * LANES:(j + 1) * LANES]
    logits = jnp.dot(x2, wr_ref[...], precision=lax.Precision.HIGHEST, preferred_element_type=F32) + br_ref[...]
    lane = lax.broadcasted_iota(I32, (tm, LANES), 1)
    is_grp = lane < MOE_GROUPS
    lg = jnp.where(is_grp, logits, -jnp.inf)
    gmax = jnp.max(lg, axis=-1, keepdims=True)
    gsum = jnp.sum(jnp.where(is_grp, jnp.exp(logits - gmax), 0.0), axis=-1, keepdims=True)
    gp = 1.0 / gsum
    gi = jnp.min(jnp.where(lg == gmax, lane, LANES), axis=-1, keepdims=True)
    ex = lane - ROUTER_LANE0
    in_grp = (ex >= 0) & (ex < MOE_EXPERTS) & ((ex // MOE_EXPERTS_PER_GROUP) == gi)
    sel = jnp.where(in_grp, logits, -jnp.inf)
    v1 = jnp.max(sel, axis=-1, keepdims=True)
    i1 = jnp.min(jnp.where(sel == v1, lane, LANES), axis=-1, keepdims=True)
    sel2 = jnp.where(lane == i1, -jnp.inf, sel)
    v2 = jnp.max(sel2, axis=-1, keepdims=True)
    i2 = jnp.min(jnp.where(sel2 == v2, lane, LANES), axis=-1, keepdims=True)
    tt = jnp.exp(v2 - v1)
    g0 = gp / (1.0 + tt)
    g1 = gp * tt / (1.0 + tt)
    oh0 = (lane == i1).astype(F32)
    oh1 = (lane == i2).astype(F32)
    oh = oh0 + oh1
    rr = lax.broadcasted_iota(I32, (tm, tm), 0)
    cc = lax.broadcasted_iota(I32, (tm, tm), 1)
    tri = (cc < rr).astype(BF16)
    prefix = jnp.dot(tri, oh.astype(BF16), preferred_element_type=F32)
    tot = base_ref[...] + prefix
    rank0 = jnp.sum(oh0 * tot, axis=-1, keepdims=True).astype(I32)
    rank1 = jnp.sum(oh1 * tot, axis=-1, keepdims=True).astype(I32)
    base_ref[...] = base_ref[...] + jnp.sum(oh, axis=0, keepdims=True)
    eoff = batch * MOE_EXPERTS - ROUTER_LANE0
    code0 = (i1 + eoff) * (1 << RANK_BITS) + rank0
    code1 = (i2 + eoff) * (1 << RANK_BITS) + rank1
    meta = jnp.where(lane == 0, code0, jnp.where(lane == 1, code1, 0))
    codes_ref[...] = jnp.transpose(meta)[0:8, :]
    gates_ref[...] = jnp.where(lane == 0, g0, jnp.where(lane == 1, g1, 0.0))
    cnt_ref[...] = jnp.broadcast_to(base_ref[...], cnt_ref.shape)


def _router(h, gain, wr, br, batch):
    t = h.shape[0]
    tm = ROW_TILE
    tpb = t // batch // tm
    row = lambda width: pl.BlockSpec((tm, width), lambda i: (i, 0))
    fixed = lambda shape: pl.BlockSpec(shape, lambda i: (0, 0))
    return pl.pallas_call(
        functools.partial(_router_kernel, tiles_per_batch=tpb),
        grid=(t // tm,),
        in_specs=[row(D_MODEL), fixed((1, D_MODEL)), fixed((D_MODEL, LANES)), fixed((1, LANES))],
        out_specs=[pl.BlockSpec((tm * SLABS, LANES), lambda i: (i, 0)),
                   pl.BlockSpec((8, tm), lambda i: (0, i)), row(LANES),
                   pl.BlockSpec((8, LANES), lambda i: (i // tpb, 0))],
        out_shape=[
            jax.ShapeDtypeStruct((t * SLABS, LANES), F32),
            jax.ShapeDtypeStruct((8, t), I32),
            jax.ShapeDtypeStruct((t, LANES), F32),
            jax.ShapeDtypeStruct((batch * 8, LANES), F32),
        ],
        scratch_shapes=[pltpu.VMEM((1, LANES), F32)],
        compiler_params=_cparams(("arbitrary",)),
        name="moe_router",
    )(h, gain.reshape(1, D_MODEL), wr, br)


INVERT_UNROLL = 16


def _dest_kernel(pstart_ref, codes_ref, dest_ref):
    codes = codes_ref[...]
    seg = lax.shift_right_logical(codes, RANK_BITS)
    dest = codes & ((1 << RANK_BITS) - 1)
    for i in range(pstart_ref.shape[0]):
        dest = dest + jnp.where(seg == i, pstart_ref[i], 0)
    dest_ref[...] = dest


def _dest_rows(pad_start, codes):
    grid_spec = pltpu.PrefetchScalarGridSpec(
        num_scalar_prefetch=1,
        grid=(1,),
        in_specs=[pl.BlockSpec(codes.shape, lambda i, ps: (0, 0))],
        out_specs=pl.BlockSpec(codes.shape, lambda i, ps: (0, 0)),
    )
    return pl.pallas_call(
        _dest_kernel,
        grid_spec=grid_spec,
        out_shape=jax.ShapeDtypeStruct(codes.shape, I32),
        compiler_params=_cparams(("arbitrary",)),
        name="moe_dest_rows",
    )(pad_start, codes)


def _invert_kernel(trips_ref, dest_ref, fill_lo_ref, fill_hi_ref, slot_ref, *, batch):
    per_batch = dest_ref.shape[0] // batch
    blk_shift = MOE_BLOCK.bit_length() - 1
    assert MOE_BLOCK == 1 << blk_shift and 2 * MOE_BLOCK <= DUMMY_SLOTS

    def fill_segment(sgm, carry):
        def fill_row(r, c):
            slot_ref[r] = per_batch + (lax.shift_right_logical(r, blk_shift) & 1) * MOE_BLOCK + (r & (MOE_BLOCK - 1))
            return c

        return lax.fori_loop(fill_lo_ref[sgm], fill_hi_ref[sgm], fill_row, carry)

    lax.fori_loop(0, fill_lo_ref.shape[0], fill_segment, 0)
    for b in range(batch):
        def place(i, carry, b=b):
            for v in range(INVERT_UNROLL):
                a = i * INVERT_UNROLL + v
                slot_ref[dest_ref[b * per_batch + a]] = a
            return carry

        lax.fori_loop(0, trips_ref[0], place, 0)


def _invert(dest_flat, fill_lo, fill_hi, n_rows, batch):
    smem = pl.BlockSpec(memory_space=pltpu.SMEM)
    per_batch = dest_flat.shape[0] // batch
    assert per_batch % INVERT_UNROLL == 0
    trips = jnp.full((1,), per_batch // INVERT_UNROLL, I32)
    return pl.pallas_call(
        functools.partial(_invert_kernel, batch=batch),
        in_specs=[smem] * 4,
        out_specs=smem,
        out_shape=jax.ShapeDtypeStruct((n_rows,), I32),
        name="moe_invert_rows",
    )(trips, dest_flat, fill_lo, fill_hi)


def _expert_kernel(be_ref, run_ref, nxt_ref, slot_ref, x2s_ref, wg_hbm, wu_hbm, wd_hbm, ys_ref,
                   xres, wbuf_g, wbuf_u, wbuf_d, wgb_ref, wub_ref, wdb_ref, tile, ybuf, sems, wsems, *,
                   batch, nb, layer):
    b = pl.program_id(0)
    n = pl.program_id(1)
    g = b * nb + n
    last = batch * nb - 1
    blk = MOE_BLOCK
    seq = xres.shape[0] // SLABS
    bstride = 2 * seq + DUMMY_SLOTS
    tcur = g % 2
    ycur = g % 3
    yprev = (g + 2) % 3

    def weight_copies(e, ws):
        pairs = ((wg_hbm, wbuf_g), (wu_hbm, wbuf_u), (wd_hbm, wbuf_d))
        return [pltpu.make_async_copy(src.at[layer, e], dst.at[ws], wsems.at[ws, k])
                for k, (src, dst) in enumerate(pairs)]

    def wait_block(q):
        pltpu.make_async_copy(ybuf.at[q], ys_ref.at[pl.ds(0, blk * SLABS), :], sems.at[q]).wait()

    def gather_row(block, tslot, mi):
        s = slot_ref[block * blk + mi]
        tok = jnp.minimum(jnp.where(s >= seq, s - seq, s), seq - 1)
        slab = xres[pl.ds(pl.multiple_of(tok * SLABS, SLABS), SLABS), :]
        tile[tslot, pl.ds(mi, SLABS, stride=GATHER_PITCH), :] = slab

    def scatter_row(yslot, slot, row0, mi):
        dst = pl.multiple_of((row0 + slot) * SLABS, SLABS)
        pltpu.make_async_copy(ybuf.at[yslot, pl.ds(mi * SLABS, SLABS), :], ys_ref.at[pl.ds(dst, SLABS), :],
                              sems.at[yslot]).start()

    @pl.when(g == 0)
    def _():
        for c in weight_copies(be_ref[0], 0):
            c.start()
        ybuf[...] = jnp.zeros(ybuf.shape, F32)
        for bb in range(batch):
            for c in range(DUMMY_SLOTS // blk):
                start = (bb * bstride + 2 * seq + c * blk) * SLABS
                zero = pltpu.make_async_copy(ybuf.at[0], ys_ref.at[pl.ds(start, blk * SLABS), :], sems.at[0])
                zero.start()
                zero.wait()

    @pl.when(n == 0)
    def _():
        pltpu.sync_copy(x2s_ref.at[pl.ds(pl.multiple_of(b * (seq * SLABS), SLABS), seq * SLABS), :], xres)
        for mi in range(blk):
            gather_row(g, tcur, mi)

    run = run_ref[g]
    ws = run % 2

    @pl.when((g == 0) | (run != run_ref[jnp.maximum(g - 1, 0)]))
    def _():
        for c in weight_copies(be_ref[g], ws):
            c.wait()
        wgb_ref[...] = wbuf_g[ws].astype(BF16)
        wub_ref[...] = wbuf_u[ws].astype(BF16)
        wdb_ref[...] = wbuf_d[ws].astype(BF16)

        @pl.when(nxt_ref[g] >= 0)
        def _():
            for c in weight_copies(nxt_ref[g], 1 - ws):
                c.start()

    @pl.when(g >= 2)
    def _():
        wait_block(ycur)

    nxt_block = jnp.minimum(g + 1, last)
    prev_block = jnp.maximum(g - 1, 0)
    prev_row0 = jnp.where(n == 0, jnp.maximum(b - 1, 0), b) * bstride
    x = jnp.concatenate([tile[tcur, j * GATHER_PITCH:j * GATHER_PITCH + blk, :].astype(BF16)
                         for j in range(SLABS)], axis=1)
    n_chunks = 8
    per = blk // n_chunks
    half = EXPERT_HIDDEN // 2
    quarter = D_MODEL // 4
    acts = []
    hid = None
    for c in range(n_chunks):
        for mi in range(c * per, (c + 1) * per):
            gather_row(nxt_block, 1 - tcur, mi)
        for mi in range(c * per, (c + 1) * per):
            slot = jnp.where(g == 0, 2 * seq + 2 * blk + mi, slot_ref[prev_block * blk + mi])
            scatter_row(yprev, slot, prev_row0, mi)
        if c < 4:
            w_ref = wgb_ref if c < 2 else wub_ref
            acts.append(jnp.dot(x, w_ref[:, (c % 2) * half:(c % 2 + 1) * half], preferred_element_type=F32))
        if c == 3:
            a = jnp.concatenate(acts[0:2], axis=1)
            u = jnp.concatenate(acts[2:4], axis=1)
            hid = (a * jax.nn.sigmoid(a) * u).astype(BF16)
        if c >= 4:
            q = c - 4
            yq = jnp.dot(hid, wdb_ref[:, q * quarter:(q + 1) * quarter], preferred_element_type=F32)
            for jj in range(quarter // LANES):
                j = q * (quarter // LANES) + jj
                ybuf[ycur, pl.ds(j, blk, stride=SLABS), :] = yq[:, jj * LANES:(jj + 1) * LANES]

    @pl.when(g == last)
    def _():
        wait_block((last - 2) % 3)
        for mi in range(blk):
            scatter_row(last % 3, slot_ref[last * blk + mi], (batch - 1) * bstride, mi)
        wait_block((last - 1) % 3)
        wait_block(last % 3)


def _experts(block_expert, row_slot, x2s, wg, wu, wd, layer, batch, seq):
    blk = MOE_BLOCK
    nb = row_slot.shape[0] // blk // batch
    steps = batch * nb
    assert steps >= 3
    idx = jnp.arange(steps, dtype=I32)
    change = jnp.concatenate([jnp.zeros((1,), I32), (block_expert[1:] != block_expert[:-1]).astype(I32)])
    run = jnp.cumsum(change).astype(I32)
    later_change = (idx[None, :] > idx[:, None]) & (change[None, :] > 0)
    nxt_idx = jnp.min(jnp.where(later_change, idx[None, :], steps), axis=1)
    nxt = jnp.where(nxt_idx < steps, block_expert[jnp.minimum(nxt_idx, steps - 1)], -1).astype(I32)
    hbm = pl.BlockSpec(memory_space=pl.ANY)
    grid_spec = pltpu.PrefetchScalarGridSpec(
        num_scalar_prefetch=4,
        grid=(batch, nb),
        in_specs=[hbm, hbm, hbm, hbm],
        out_specs=hbm,
        scratch_shapes=[
            pltpu.VMEM((seq * SLABS, LANES), F32),
            pltpu.VMEM((2, D_MODEL, EXPERT_HIDDEN), F32), pltpu.VMEM((2, D_MODEL, EXPERT_HIDDEN), F32),
            pltpu.VMEM((2, EXPERT_HIDDEN, D_MODEL), F32),
            pltpu.VMEM((D_MODEL, EXPERT_HIDDEN), BF16), pltpu.VMEM((D_MODEL, EXPERT_HIDDEN), BF16),
            pltpu.VMEM((EXPERT_HIDDEN, D_MODEL), BF16),
            pltpu.VMEM((2, SLABS * GATHER_PITCH, LANES), F32),
            pltpu.VMEM((3, blk * SLABS, LANES), F32),
            pltpu.SemaphoreType.DMA((3,)),
            pltpu.SemaphoreType.DMA((2, 3)),
        ],
    )
    return pl.pallas_call(
        functools.partial(_expert_kernel, batch=batch, nb=nb, layer=layer),
        grid_spec=grid_spec,
        out_shape=jax.ShapeDtypeStruct((batch * (2 * seq + DUMMY_SLOTS) * SLABS, LANES), F32),
        compiler_params=_cparams(("arbitrary", "arbitrary"), VMEM_LIMIT_EXPERTS),
        name="moe_experts",
    )(block_expert, run, nxt, row_slot, x2s, wg, wu, wd)


def _combine_kernel(h_ref, gates_ref, y0_ref, y1_ref, *rest, final):
    if final:
        fg_ref, out_ref = rest
    else:
        (out_ref,) = rest
    tm = h_ref.shape[0]
    gates = gates_ref[...]
    g0, g1 = gates[:, 0:1], gates[:, 1:2]
    parts = []
    for j in range(SLABS):
        y0 = y0_ref[pl.ds(j, tm, stride=SLABS), :]
        y1 = y1_ref[pl.ds(j, tm, stride=SLABS), :]
        parts.append(g0 * y0 + g1 * y1)
    out = h_ref[...] + jnp.concatenate(parts, axis=1)
    if final:
        out = _rms(out, fg_ref[...])
    out_ref[...] = out


def _combine(h, gates, ys, batch, seq, final_gain=None):
    t = h.shape[0]
    tm = ROW_TILE
    tpb = seq // tm
    bstride = 2 * tpb + DUMMY_SLOTS // tm
    final = final_gain is not None
    in_specs = [pl.BlockSpec((tm, D_MODEL), lambda i: (i, 0)),
                pl.BlockSpec((tm, LANES), lambda i: (i, 0)),
                pl.BlockSpec((tm * SLABS, LANES), lambda i: (i // tpb * bstride + i % tpb, 0)),
                pl.BlockSpec((tm * SLABS, LANES), lambda i: (i // tpb * bstride + tpb + i % tpb, 0))]
    args = [h, gates, ys, ys]
    if final:
        in_specs.append(pl.BlockSpec((1, D_MODEL), lambda i: (0, 0)))
        args.append(final_gain.reshape(1, D_MODEL))
    return pl.pallas_call(
        functools.partial(_combine_kernel, final=final),
        grid=(t // tm,),
        in_specs=in_specs,
        out_specs=pl.BlockSpec((tm, D_MODEL), lambda i: (i, 0)),
        out_shape=jax.ShapeDtypeStruct((t, D_MODEL), F32),
        compiler_params=_cparams(("parallel",)),
        name="moe_combine_final" if final else "moe_combine",
    )(*args)


def _moe(h, gain, w_r1, b_r1, w_r2, b_r2, wg, wu, wd, layer, batch, final_gain=None):
    t = h.shape[0]
    seq = t // batch
    assert seq <= (1 << RANK_BITS) and seq % ROW_TILE == 0 and DUMMY_SLOTS % ROW_TILE == 0
    pad_w = jnp.zeros((D_MODEL, LANES - MOE_GROUPS - MOE_EXPERTS), F32)
    wr = jnp.concatenate([w_r1, w_r2, pad_w], axis=1)
    br = jnp.concatenate([b_r1, b_r2, jnp.zeros((LANES - MOE_GROUPS - MOE_EXPERTS,), F32)]).reshape(1, LANES)
    x2s, codes, gates, counts = _router(h, gain, wr, br, batch)
    cnt = counts.reshape(batch, 8, LANES)[:, 0, ROUTER_LANE0:ROUTER_LANE0 + MOE_EXPERTS].astype(I32)
    padded = (cnt + MOE_BLOCK - 1) // MOE_BLOCK * MOE_BLOCK
    pad_end = jnp.cumsum(padded, axis=1)
    rows_pb = seq * 2 + MOE_EXPERTS * MOE_BLOCK
    nb = rows_pb // MOE_BLOCK
    pad_start = pad_end - padded + (jnp.arange(batch, dtype=I32) * rows_pb)[:, None]
    block_start = jnp.arange(nb, dtype=I32) * MOE_BLOCK
    block_expert = jnp.minimum(
        jnp.sum((pad_end[:, None, :] <= block_start[None, :, None]).astype(I32), axis=2), MOE_EXPERTS - 1)
    dest = _dest_rows(pad_start.reshape(-1), codes)
    dest_flat = dest[0:2].reshape(2, batch, seq).transpose(1, 0, 2).reshape(-1)
    batch_end = ((jnp.arange(batch, dtype=I32) + 1) * rows_pb)[:, None]
    fill_lo = jnp.concatenate([pad_start + cnt, pad_start[:, -1:] + padded[:, -1:]], axis=1)
    fill_hi = jnp.concatenate([pad_start + padded, batch_end], axis=1)
    row_slot = _invert(dest_flat, fill_lo.reshape(-1), fill_hi.reshape(-1), batch * rows_pb, batch)
    ys = _experts(block_expert.reshape(-1), row_slot, x2s, wg, wu, wd, layer, batch, seq)
    return _combine(h, gates, ys, batch, seq, final_gain)


def kernel(x, mix_norm, ffn_norm, final_norm, even_w_in, even_w_out, conv_w, conv_b, conv_norm_g, conv_norm_b,
           odd_w_in, odd_w_out, router_w1, router_b1, router_w2, router_b2, expert_w_gate, expert_w_up,
           expert_w_down):
    batch, seq, d = x.shape
    assert d == D_MODEL and seq % (ATTN_BLOCK * max(dl for _, dl in A_BRANCHES)) == 0
    assert all(w // dl == ATTN_BLOCK for w, dl in A_BRANCHES)
    t = batch * seq
    h = x.reshape(t, d)

    qkv, conv_in = _even_in(h, mix_norm[0], even_w_in, _attn_rope_tables(seq), seq)
    outs, lses = [], []
    for _, dil in A_BRANCHES:
        o_d, lse_d = _attn_branch(qkv.reshape(t // dil, dil * 3 * A_WIDTH), batch, seq, dil)
        outs.append(o_d.reshape(t, A_WIDTH))
        lses.append(lse_d.reshape(t, LANES))
    bconv = _conv(conv_in, conv_w[0], conv_b[0], conv_norm_g[0], conv_norm_b[0], batch, seq)
    h = _even_out(h, outs, lses, bconv, even_w_out)
    h = _moe(h, ffn_norm[0], router_w1[0], router_b1[0], router_w2[0], router_b2[0],
             expert_w_gate, expert_w_up, expert_w_down, 0, batch)

    inv_freq = RET_ROT_THETA ** (-jnp.linspace(0.0, 1.0, C_QK_DIM // 2, dtype=F32))
    ang = jnp.arange(seq, dtype=F32)[:, None] * inv_freq[None, :]
    proj = _odd_in(h, mix_norm[1], odd_w_in, jnp.cos(ang), jnp.sin(ang), seq)
    log_decay = jnp.log(1.0 - jnp.exp2(-5.0 - jnp.arange(C_HEADS, dtype=F32)))
    y = _retention(proj, log_decay, batch, seq)
    h = _odd_out(h, y, odd_w_out)
    out = _moe(h, ffn_norm[1], router_w1[1], router_b1[1], router_w2[1], router_b2[1],
               expert_w_gate, expert_w_up, expert_w_down, 1, batch, final_gain=final_norm)
    return out.reshape(batch, seq, d)
```

```python
import functools

import jax
import jax.numpy as jnp
from jax import lax
from jax.experimental import pallas as pl
from jax.experimental.pallas import tpu as pltpu

F32 = jnp.float32
BF16 = jnp.bfloat16
I32 = jnp.int32

NORM_EPS = 1e-6
NEG_INF = -1e30

D_MODEL = 1024
A_HEADS = 8
A_HEAD_DIM = 64
A_WIDTH = A_HEADS * A_HEAD_DIM
A_BRANCHES = ((128, 1), (512, 4), (2048, 16))
ATTN_BLOCK = 128
ROPE_THETA = 500000.0
ROPE_DIM = A_HEAD_DIM // 4
B_WIDTH = D_MODEL - A_WIDTH
CONV_WIDTH = 31
C_HEADS = 4
C_QK_DIM = 256
C_V_DIM = 512
C_QK_WIDTH = C_HEADS * C_QK_DIM
C_V_WIDTH = C_HEADS * C_V_DIM
RET_CHUNK = 128
RET_ROT_THETA = 10000.0
MOE_GROUPS = 4
MOE_EXPERTS_PER_GROUP = 8
MOE_EXPERTS = MOE_GROUPS * MOE_EXPERTS_PER_GROUP
EXPERT_HIDDEN = 512
MOE_BLOCK = 128
EVEN_IN_WIDTH = 3 * A_WIDTH + 2 * B_WIDTH
ODD_IN_WIDTH = 2 * C_QK_WIDTH + 2 * C_V_WIDTH

LANES = 128
ROW_TILE = 512
CONV_TILE = 512
CONV_HALO = 32
CONV_CHUNK = 64
ROUTER_LANE0 = MOE_GROUPS
VMEM_LIMIT = 56 * 1024 * 1024
VMEM_LIMIT_EXPERTS = 60 * 1024 * 1024
SLABS = D_MODEL // LANES
RANK_BITS = 16
GATHER_PITCH = MOE_BLOCK + 8
DUMMY_SLOTS = ROW_TILE


def _cparams(sem, vmem=VMEM_LIMIT):
    return pltpu.CompilerParams(dimension_semantics=sem, vmem_limit_bytes=vmem)


def _cast_once(w_ref, wb_ref, first):
    @pl.when(first)
    def _():
        wb_ref[...] = w_ref[...].astype(BF16)


def _rms(x, gain):
    ms = jnp.mean(x * x, axis=-1, keepdims=True)
    return x * lax.rsqrt(ms + NORM_EPS) * gain


def _even_in_kernel(h_ref, g_ref, w_ref, c_ref, s1_ref, s2_ref, qkv_ref, conv_ref, wb_ref):
    _cast_once(w_ref, wb_ref, pl.program_id(0) == 0)
    u = _rms(h_ref[...], g_ref[...]).astype(BF16)
    acc = jnp.dot(u, wb_ref[...], preferred_element_type=F32)
    c, s1, s2 = c_ref[...], s1_ref[...], s2_ref[...]
    for j in range(2 * A_WIDTH // LANES):
        xg = acc[:, j * LANES:(j + 1) * LANES]
        if j < A_WIDTH // LANES:
            xg = xg * (A_HEAD_DIM ** -0.5)
        qkv_ref[j] = xg * c + pltpu.roll(xg, LANES - ROPE_DIM // 2, 1) * s1 + pltpu.roll(xg, ROPE_DIM // 2, 1) * s2
    for j in range(2 * A_WIDTH // LANES, 3 * A_WIDTH // LANES):
        qkv_ref[j] = acc[:, j * LANES:(j + 1) * LANES]
    conv_ref[...] = acc[:, 3 * A_WIDTH:].astype(BF16)


def _even_in(h, gain, w, tabs, seq):
    t = h.shape[0]
    tm = ROW_TILE
    nseq = seq // tm
    tab_spec = pl.BlockSpec((tm, LANES), lambda i: (i % nseq, 0))
    return pl.pallas_call(
        _even_in_kernel,
        grid=(t // tm,),
        in_specs=[
            pl.BlockSpec((tm, D_MODEL), lambda i: (i, 0)),
            pl.BlockSpec((1, D_MODEL), lambda i: (0, 0)),
            pl.BlockSpec((None, D_MODEL, EVEN_IN_WIDTH), lambda i: (0, 0, 0)),
            tab_spec, tab_spec, tab_spec,
        ],
        out_specs=[
            pl.BlockSpec((3 * A_WIDTH // LANES, tm, LANES), lambda i: (0, i, 0)),
            pl.BlockSpec((tm, 2 * B_WIDTH), lambda i: (i, 0)),
        ],
        out_shape=[
            jax.ShapeDtypeStruct((3 * A_WIDTH // LANES, t, LANES), F32),
            jax.ShapeDtypeStruct((t, 2 * B_WIDTH), BF16),
        ],
        scratch_shapes=[pltpu.VMEM((D_MODEL, EVEN_IN_WIDTH), BF16)],
        compiler_params=_cparams(("arbitrary",)),
        name="even_in_proj",
    )(h, gain.reshape(1, D_MODEL), w, *tabs)


def _attn_rope_tables(seq):
    half = ROPE_DIM // 2
    inv_freq = ROPE_THETA ** (-jnp.arange(0, ROPE_DIM, 2, dtype=F32) / ROPE_DIM)
    ang = jnp.arange(seq, dtype=F32)[:, None] * inv_freq[None, :]
    cos, sin = jnp.cos(ang), jnp.sin(ang)
    rest = A_HEAD_DIM - ROPE_DIM
    ones = jnp.ones((seq, rest), F32)
    z_rest = jnp.zeros((seq, rest), F32)
    z_half = jnp.zeros((seq, half), F32)
    c = jnp.concatenate([cos, cos, ones], axis=1)
    s1 = jnp.concatenate([-sin, z_half, z_rest], axis=1)
    s2 = jnp.concatenate([z_half, sin, z_rest], axis=1)
    rep = LANES // A_HEAD_DIM
    return tuple(jnp.tile(a, (1, rep)) for a in (c, s1, s2))


ATTN_STEP = ATTN_BLOCK * max(d for _, d in A_BRANCHES)
ATTN_PAIRS = A_WIDTH // LANES
ATTN_MERGE_ROWS = 256


def _attn_kernel(q_ref, k_ref, v_ref, o_ref, kprev, vprev, oml):
    n = pl.program_id(1)
    blk = ATTN_BLOCK
    n_blocks = ATTN_STEP // blk

    @pl.when(n == 0)
    def _():
        kprev[...] = jnp.zeros(kprev.shape, BF16)
        vprev[...] = jnp.zeros(vprev.shape, BF16)

    qi = lax.broadcasted_iota(I32, (blk, 2 * blk), 0)
    kj = lax.broadcasted_iota(I32, (blk, 2 * blk), 1)
    dist = blk + qi - kj
    band = (dist >= 0) & (dist <= blk)
    in_cur = kj >= blk
    lane = lax.broadcasted_iota(I32, (blk, LANES), 1)
    lo = lane < A_HEAD_DIM

    off = 0
    for bi, (_, dil) in enumerate(A_BRANCHES):
        shift = dil.bit_length() - 1
        assert dil == 1 << shift

        def block_body(idx, c, bi=bi, dil=dil, shift=shift, off=off):
            u = lax.shift_right_logical(idx, shift)
            r = idx & (dil - 1)
            start = u * (blk * dil) + r
            rows = pl.ds(pl.multiple_of(start, blk), blk) if dil == 1 else pl.ds(start, blk, stride=dil)
            has_prev = (n > 0) | (u > 0)
            valid = band & (in_cur | has_prev)
            valid2 = jnp.concatenate([valid, valid], axis=0)
            for g in range(ATTN_PAIRS):
                q = q_ref[g, rows, :]
                kc = k_ref[g, rows, :].astype(BF16)
                vc = v_ref[g, rows, :].astype(BF16)
                q2 = jnp.concatenate([jnp.where(lo, q, 0.0), jnp.where(lo, 0.0, q)], axis=0).astype(BF16)
                kk = jnp.concatenate([kprev[g, off + r], kc], axis=0)
                vv = jnp.concatenate([vprev[g, off + r], vc], axis=0)
                s = lax.dot_general(q2, kk, (((1,), (1,)), ((), ())), preferred_element_type=F32)
                s = jnp.where(valid2, s, NEG_INF)
                m = jnp.max(s, axis=-1, keepdims=True)
                e = jnp.exp(s - m)
                den = jnp.sum(e, axis=-1, keepdims=True)
                pv = jnp.dot(e.astype(BF16), vv, preferred_element_type=F32)
                o_new = jnp.where(lo, pv[:blk], pv[blk:])
                m_new = jnp.where(lo, m[:blk], m[blk:])
                l_new = jnp.where(lo, den[:blk], den[blk:])
                if bi > 0:
                    o_run, m_run, l_run = oml[0, g, rows, :], oml[1, g, rows, :], oml[2, g, rows, :]
                    m_both = jnp.maximum(m_run, m_new)
                    w_run, w_new = jnp.exp(m_run - m_both), jnp.exp(m_new - m_both)
                    o_new = w_run * o_run + w_new * o_new
                    l_new = w_run * l_run + w_new * l_new
                    m_new = m_both
                oml[0, g, rows, :] = o_new
                oml[1, g, rows, :] = m_new
                oml[2, g, rows, :] = l_new
                kprev[g, off + r] = kc
                vprev[g, off + r] = vc
            return c

        lax.fori_loop(0, n_blocks, block_body, 0)
        off += dil
    for g in range(ATTN_PAIRS):
        for c in range(ATTN_STEP // ATTN_MERGE_ROWS):
            sl = slice(c * ATTN_MERGE_ROWS, (c + 1) * ATTN_MERGE_ROWS)
            o_ref[g, sl, :] = (oml[0, g, sl, :] / oml[2, g, sl, :]).astype(BF16)


def _attention(qkv, batch, seq):
    t = qkv.shape[1]
    steps = seq // ATTN_STEP
    n_res = sum(d for _, d in A_BRANCHES)

    def slabs(which):
        return pl.BlockSpec((ATTN_PAIRS, ATTN_STEP, LANES), lambda b, n: (which, b * steps + n, 0))

    return pl.pallas_call(
        _attn_kernel,
        grid=(batch, steps),
        in_specs=[slabs(0), slabs(1), slabs(2)],
        out_specs=pl.BlockSpec((ATTN_PAIRS, ATTN_STEP, LANES), lambda b, n: (0, b * steps + n, 0)),
        out_shape=jax.ShapeDtypeStruct((ATTN_PAIRS, t, LANES), BF16),
        scratch_shapes=[
            pltpu.VMEM((ATTN_PAIRS, n_res, ATTN_BLOCK, LANES), BF16),
            pltpu.VMEM((ATTN_PAIRS, n_res, ATTN_BLOCK, LANES), BF16),
            pltpu.VMEM((3, ATTN_PAIRS, ATTN_STEP, LANES), F32),
        ],
        compiler_params=_cparams(("parallel", "arbitrary")),
        name="dilated_attention",
    )(qkv, qkv, qkv)


def _conv_kernel(val_ref, gate_ref, w_ref, b_ref, g_ref, beta_ref, o_ref, abuf, shifted):
    n = pl.program_id(1)
    tc = CONV_TILE
    sub = 8

    @pl.when(n == 0)
    def _():
        abuf[0:CONV_HALO, :] = jnp.zeros((CONV_HALO, B_WIDTH), F32)

    @pl.when(n > 0)
    def _():
        abuf[0:CONV_HALO, :] = abuf[tc:tc + CONV_HALO, :]

    val = val_ref[...].astype(F32)
    gate = gate_ref[...].astype(F32)
    abuf[CONV_HALO:CONV_HALO + tc, :] = val * jax.nn.sigmoid(gate)
    off = CONV_HALO - (CONV_WIDTH - 1)
    span = shifted.shape[1]
    for s in range(1, sub):
        shifted[s - 1] = abuf[s:s + span, :]
    for c in range(tc // CONV_CHUNK):
        acc = jnp.broadcast_to(b_ref[...], (CONV_CHUNK, B_WIDTH))
        for j in range(CONV_WIDTH):
            s = (off + j) % sub
            r0 = c * CONV_CHUNK + off + j - s
            src = abuf[r0:r0 + CONV_CHUNK, :] if s == 0 else shifted[s - 1, r0:r0 + CONV_CHUNK, :]
            acc = acc + w_ref[j:j + 1, :] * src
        mu = jnp.mean(acc, axis=-1, keepdims=True)
        cen = acc - mu
        var = jnp.mean(cen * cen, axis=-1, keepdims=True)
        yn = cen * lax.rsqrt(var + NORM_EPS) * g_ref[...] + beta_ref[...]
        o_ref[c * CONV_CHUNK:(c + 1) * CONV_CHUNK, :] = (yn * jax.nn.sigmoid(yn)).astype(BF16)


def _conv(conv_in, w, b, g, beta, batch, seq):
    t = conv_in.shape[0]
    tc = CONV_TILE
    nt = seq // tc
    w_pad = jnp.concatenate([w, jnp.zeros((CONV_HALO - CONV_WIDTH, B_WIDTH), F32)], axis=0)
    vec = pl.BlockSpec((1, B_WIDTH), lambda bb, n: (0, 0))
    return pl.pallas_call(
        _conv_kernel,
        grid=(batch, nt),
        in_specs=[
            pl.BlockSpec((tc, B_WIDTH), lambda bb, n: (bb * nt + n, 0)),
            pl.BlockSpec((tc, B_WIDTH), lambda bb, n: (bb * nt + n, 1)),
            pl.BlockSpec((CONV_HALO, B_WIDTH), lambda bb, n: (0, 0)),
            vec, vec, vec,
        ],
        out_specs=pl.BlockSpec((tc, B_WIDTH), lambda bb, n: (bb * nt + n, 0)),
        out_shape=jax.ShapeDtypeStruct((t, B_WIDTH), BF16),
        scratch_shapes=[pltpu.VMEM((tc + CONV_HALO, B_WIDTH), F32),
                        pltpu.VMEM((7, tc + CONV_HALO - 8, B_WIDTH), F32)],
        compiler_params=_cparams(("parallel", "arbitrary")),
        name="conformer_conv",
    )(conv_in, conv_in, w_pad, b.reshape(1, -1), g.reshape(1, -1), beta.reshape(1, -1))


def _even_out_kernel(h_ref, a_ref, bc_ref, w_ref, out_ref, wb_ref):
    _cast_once(w_ref, wb_ref, pl.program_id(0) == 0)
    a = jnp.concatenate([a_ref[g] for g in range(ATTN_PAIRS)], axis=1)
    acc = jnp.dot(a, wb_ref[0:A_WIDTH, :], preferred_element_type=F32)
    acc = acc + jnp.dot(bc_ref[...], wb_ref[A_WIDTH:, :], preferred_element_type=F32)
    out_ref[...] = h_ref[...] + acc


def _even_out(h, attn, bconv, w):
    t = h.shape[0]
    tm = ROW_TILE
    row = lambda width: pl.BlockSpec((tm, width), lambda i: (i, 0))
    return pl.pallas_call(
        _even_out_kernel,
        grid=(t // tm,),
        in_specs=[row(D_MODEL), pl.BlockSpec((ATTN_PAIRS, tm, LANES), lambda i: (0, i, 0)), row(B_WIDTH),
                  pl.BlockSpec((None, D_MODEL, D_MODEL), lambda i: (0, 0, 0))],
        out_specs=row(D_MODEL),
        out_shape=jax.ShapeDtypeStruct((t, D_MODEL), F32),
        scratch_shapes=[pltpu.VMEM((D_MODEL, D_MODEL), BF16)],
        compiler_params=_cparams(("arbitrary",)),
        name="even_out_proj",
    )(h, attn, bconv, w)


def _odd_in_kernel(h_ref, g_ref, w_ref, cos_ref, sin_ref, o_ref, wb_ref):
    j = pl.program_id(0)
    _cast_once(w_ref, wb_ref, pl.program_id(1) == 0)
    u = _rms(h_ref[...], g_ref[...]).astype(BF16)
    acc = jnp.dot(u, wb_ref[...], preferred_element_type=F32)

    @pl.when(j == 0)
    def _():
        cos, sin = cos_ref[...], sin_ref[...]
        half = C_QK_DIM // 2
        for hd in range(2 * C_HEADS):
            x1 = acc[:, hd * C_QK_DIM:hd * C_QK_DIM + half]
            x2 = acc[:, hd * C_QK_DIM + half:(hd + 1) * C_QK_DIM]
            r1 = x1 * cos - x2 * sin
            r2 = x2 * cos + x1 * sin
            if hd >= C_HEADS:
                r1 = r1 * (C_QK_DIM ** -0.5)
                r2 = r2 * (C_QK_DIM ** -0.5)
            o_ref[:, hd * C_QK_DIM:hd * C_QK_DIM + half] = r1.astype(BF16)
            o_ref[:, hd * C_QK_DIM + half:(hd + 1) * C_QK_DIM] = r2.astype(BF16)

    @pl.when(j > 0)
    def _():
        o_ref[...] = acc.astype(BF16)


def _odd_in(h, gain, w, cos, sin, seq):
    t = h.shape[0]
    tm = ROW_TILE
    tn = 2 * C_QK_WIDTH
    nseq = seq // tm
    tab = pl.BlockSpec((tm, C_QK_DIM // 2), lambda j, i: (i % nseq, 0))
    return pl.pallas_call(
        _odd_in_kernel,
        grid=(ODD_IN_WIDTH // tn, t // tm),
        in_specs=[
            pl.BlockSpec((tm, D_MODEL), lambda j, i: (i, 0)),
            pl.BlockSpec((1, D_MODEL), lambda j, i: (0, 0)),
            pl.BlockSpec((None, D_MODEL, tn), lambda j, i: (0, 0, j)),
            tab, tab,
        ],
        out_specs=pl.BlockSpec((tm, tn), lambda j, i: (i, j)),
        out_shape=jax.ShapeDtypeStruct((t, ODD_IN_WIDTH), BF16),
        scratch_shapes=[pltpu.VMEM((D_MODEL, tn), BF16)],
        compiler_params=_cparams(("arbitrary", "arbitrary")),
        name="odd_in_proj",
    )(h, gain.reshape(1, D_MODEL), w, cos, sin)


def _ret_kernel(ld_ref, q_ref, k_ref, v_ref, g_ref, o_ref, state):
    c = pl.program_id(1)
    ch = RET_CHUNK

    @pl.when(c == 0)
    def _():
        state[...] = jnp.zeros(state.shape, F32)

    ii = lax.broadcasted_iota(I32, (ch, ch), 0)
    jj = lax.broadcasted_iota(I32, (ch, ch), 1)
    diff = (ii - jj).astype(F32)
    pos = lax.broadcasted_iota(I32, (ch, 1), 0).astype(F32)
    for hd in range(C_HEADS):
        ld = ld_ref[hd]
        q = q_ref[:, hd * C_QK_DIM:(hd + 1) * C_QK_DIM]
        k = k_ref[:, hd * C_QK_DIM:(hd + 1) * C_QK_DIM]
        v = v_ref[:, hd * C_V_DIM:(hd + 1) * C_V_DIM]
        intra = jnp.where(diff >= 0, jnp.exp(ld * jnp.maximum(diff, 0.0)), 0.0)
        s = lax.dot_general(q, k, (((1,), (1,)), ((), ())), preferred_element_type=F32) * intra
        inner = jnp.dot(s.astype(BF16), v, preferred_element_type=F32)
        q_decay = jnp.exp(ld * (pos + 1.0))
        k_decay = jnp.exp(ld * (ch - 1.0 - pos))
        chunk_decay = jnp.exp(ld * jnp.full((1, 1), float(ch), F32))
        st = state[hd]
        cross = jnp.dot(q, st.astype(BF16), preferred_element_type=F32) * q_decay
        kd_t = jnp.transpose(k.astype(F32) * k_decay).astype(BF16)
        state[hd] = st * chunk_decay + jnp.dot(kd_t, v, preferred_element_type=F32)
        out = inner + cross
        mu = jnp.mean(out, axis=-1, keepdims=True)
        cen = out - mu
        var = jnp.mean(cen * cen, axis=-1, keepdims=True)
        o = cen * lax.rsqrt(var + NORM_EPS)
        gf = g_ref[:, hd * C_V_DIM:(hd + 1) * C_V_DIM].astype(F32)
        o_ref[:, hd * C_V_DIM:(hd + 1) * C_V_DIM] = (gf * jax.nn.sigmoid(gf) * o).astype(BF16)


def _retention(proj, log_decay, batch, seq):
    t = proj.shape[0]
    ch = RET_CHUNK
    nc = seq // ch
    v0 = 2 * C_QK_WIDTH // C_V_WIDTH
    grid_spec = pltpu.PrefetchScalarGridSpec(
        num_scalar_prefetch=1,
        grid=(batch, nc),
        in_specs=[
            pl.BlockSpec((ch, C_QK_WIDTH), lambda b, c, ld: (b * nc + c, 0)),
            pl.BlockSpec((ch, C_QK_WIDTH), lambda b, c, ld: (b * nc + c, 1)),
            pl.BlockSpec((ch, C_V_WIDTH), lambda b, c, ld: (b * nc + c, v0)),
            pl.BlockSpec((ch, C_V_WIDTH), lambda b, c, ld: (b * nc + c, v0 + 1)),
        ],
        out_specs=pl.BlockSpec((ch, C_V_WIDTH), lambda b, c, ld: (b * nc + c, 0)),
        scratch_shapes=[pltpu.VMEM((C_HEADS, C_QK_DIM, C_V_DIM), F32)],
    )
    return pl.pallas_call(
        _ret_kernel,
        grid_spec=grid_spec,
        out_shape=jax.ShapeDtypeStruct((t, C_V_WIDTH), BF16),
        compiler_params=_cparams(("parallel", "arbitrary")),
        name="retention",
    )(log_decay, proj, proj, proj, proj)


def _odd_out_kernel(h_ref, y_ref, w_ref, out_ref, wb_ref):
    _cast_once(w_ref, wb_ref, pl.program_id(0) == 0)
    out_ref[...] = h_ref[...] + jnp.dot(y_ref[...], wb_ref[...], preferred_element_type=F32)


def _odd_out(h, y, w):
    t = h.shape[0]
    tm = ROW_TILE
    return pl.pallas_call(
        _odd_out_kernel,
        grid=(t // tm,),
        in_specs=[pl.BlockSpec((tm, D_MODEL), lambda i: (i, 0)),
                  pl.BlockSpec((tm, C_V_WIDTH), lambda i: (i, 0)),
                  pl.BlockSpec((None, C_V_WIDTH, D_MODEL), lambda i: (0, 0, 0))],
        out_specs=pl.BlockSpec((tm, D_MODEL), lambda i: (i, 0)),
        out_shape=jax.ShapeDtypeStruct((t, D_MODEL), F32),
        scratch_shapes=[pltpu.VMEM((C_V_WIDTH, D_MODEL), BF16)],
        compiler_params=_cparams(("arbitrary",)),
        name="odd_out_proj",
    )(h, y, w)


def _router_kernel(h_ref, g_ref, wr_ref, br_ref, x2s_ref, codes_ref, gates_ref, cnt_ref, base_ref, *,
                   tiles_per_batch):
    i = pl.program_id(0)
    tm = h_ref.shape[0]
    batch = i // tiles_per_batch

    @pl.when(i % tiles_per_batch == 0)
    def _():
        base_ref[...] = jnp.zeros(base_ref.shape, F32)

    x2 = _rms(h_ref[...], g_ref[...])
    for j in range(SLABS):
        x2s_ref[pl.ds(j, tm, stride=SLABS), :] = x2[:, j * LANES:(j + 1) * LANES]
    logits = jnp.dot(x2, wr_ref[...], precision=lax.Precision.HIGHEST, preferred_element_type=F32) + br_ref[...]
    lane = lax.broadcasted_iota(I32, (tm, LANES), 1)
    is_grp = lane < MOE_GROUPS
    lg = jnp.where(is_grp, logits, -jnp.inf)
    gmax = jnp.max(lg, axis=-1, keepdims=True)
    gsum = jnp.sum(jnp.where(is_grp, jnp.exp(logits - gmax), 0.0), axis=-1, keepdims=True)
    gp = 1.0 / gsum
    gi = jnp.min(jnp.where(lg == gmax, lane, LANES), axis=-1, keepdims=True)
    ex = lane - ROUTER_LANE0
    in_grp = (ex >= 0) & (ex < MOE_EXPERTS) & ((ex // MOE_EXPERTS_PER_GROUP) == gi)
    sel = jnp.where(in_grp, logits, -jnp.inf)
    v1 = jnp.max(sel, axis=-1, keepdims=True)
    i1 = jnp.min(jnp.where(sel == v1, lane, LANES), axis=-1, keepdims=True)
    sel2 = jnp.where(lane == i1, -jnp.inf, sel)
    v2 = jnp.max(sel2, axis=-1, keepdims=True)
    i2 = jnp.min(jnp.where(sel2 == v2, lane, LANES), axis=-1, keepdims=True)
    tt = jnp.exp(v2 - v1)
    g0 = gp / (1.0 + tt)
    g1 = gp * tt / (1.0 + tt)
    oh0 = (lane == i1).astype(F32)
    oh1 = (lane == i2).astype(F32)
    oh = oh0 + oh1
    rr = lax.broadcasted_iota(I32, (tm, tm), 0)
    cc = lax.broadcasted_iota(I32, (tm, tm), 1)
    tri = (cc < rr).astype(BF16)
    prefix = jnp.dot(tri, oh.astype(BF16), preferred_element_type=F32)
    tot = base_ref[...] + prefix
    rank0 = jnp.sum(oh0 * tot, axis=-1, keepdims=True).astype(I32)
    rank1 = jnp.sum(oh1 * tot, axis=-1, keepdims=True).astype(I32)
    base_ref[...] = base_ref[...] + jnp.sum(oh, axis=0, keepdims=True)
    eoff = batch * MOE_EXPERTS - ROUTER_LANE0
    code0 = (i1 + eoff) * (1 << RANK_BITS) + rank0
    code1 = (i2 + eoff) * (1 << RANK_BITS) + rank1
    meta = jnp.where(lane == 0, code0, jnp.where(lane == 1, code1, 0))
    codes_ref[...] = jnp.transpose(meta)[0:8, :]
    gates_ref[...] = jnp.where(lane == 0, g0, jnp.where(lane == 1, g1, 0.0))
    cnt_ref[...] = jnp.broadcast_to(base_ref[...], cnt_ref.shape)


def _router(h, gain, wr, br, batch):
    t = h.shape[0]
    tm = ROW_TILE
    tpb = t // batch // tm
    row = lambda width: pl.BlockSpec((tm, width), lambda i: (i, 0))
    fixed = lambda shape: pl.BlockSpec(shape, lambda i: (0, 0))
    return pl.pallas_call(
        functools.partial(_router_kernel, tiles_per_batch=tpb),
        grid=(t // tm,),
        in_specs=[row(D_MODEL), fixed((1, D_MODEL)), fixed((D_MODEL, LANES)), fixed((1, LANES))],
        out_specs=[pl.BlockSpec((tm * SLABS, LANES), lambda i: (i, 0)),
                   pl.BlockSpec((8, tm), lambda i: (0, i)), row(LANES),
                   pl.BlockSpec((8, LANES), lambda i: (i // tpb, 0))],
        out_shape=[
            jax.ShapeDtypeStruct((t * SLABS, LANES), F32),
            jax.ShapeDtypeStruct((8, t), I32),
            jax.ShapeDtypeStruct((t, LANES), F32),
            jax.ShapeDtypeStruct((batch * 8, LANES), F32),
        ],
        scratch_shapes=[pltpu.VMEM((1, LANES), F32)],
        compiler_params=_cparams(("arbitrary",)),
        name="moe_router",
    )(h, gain.reshape(1, D_MODEL), wr, br)


INVERT_UNROLL = 16


def _dest_kernel(pstart_ref, codes_ref, dest_ref):
    codes = codes_ref[...]
    seg = lax.shift_right_logical(codes, RANK_BITS)
    dest = codes & ((1 << RANK_BITS) - 1)
    for i in range(pstart_ref.shape[0]):
        dest = dest + jnp.where(seg == i, pstart_ref[i], 0)
    dest_ref[...] = dest


def _dest_rows(pad_start, codes):
    grid_spec = pltpu.PrefetchScalarGridSpec(
        num_scalar_prefetch=1,
        grid=(1,),
        in_specs=[pl.BlockSpec(codes.shape, lambda i, ps: (0, 0))],
        out_specs=pl.BlockSpec(codes.shape, lambda i, ps: (0, 0)),
    )
    return pl.pallas_call(
        _dest_kernel,
        grid_spec=grid_spec,
        out_shape=jax.ShapeDtypeStruct(codes.shape, I32),
        compiler_params=_cparams(("arbitrary",)),
        name="moe_dest_rows",
    )(pad_start, codes)


def _invert_kernel(trips_ref, dest_ref, fill_lo_ref, fill_hi_ref, slot_ref, *, batch):
    per_batch = dest_ref.shape[0] // batch
    blk_shift = MOE_BLOCK.bit_length() - 1
    assert MOE_BLOCK == 1 << blk_shift and 2 * MOE_BLOCK <= DUMMY_SLOTS

    def fill_segment(sgm, carry):
        def fill_row(r, c):
            slot_ref[r] = per_batch + (lax.shift_right_logical(r, blk_shift) & 1) * MOE_BLOCK + (r & (MOE_BLOCK - 1))
            return c

        return lax.fori_loop(fill_lo_ref[sgm], fill_hi_ref[sgm], fill_row, carry)

    lax.fori_loop(0, fill_lo_ref.shape[0], fill_segment, 0)
    for b in range(batch):
        def place(i, carry, b=b):
            for v in range(INVERT_UNROLL):
                a = i * INVERT_UNROLL + v
                slot_ref[dest_ref[b * per_batch + a]] = a
            return carry

        lax.fori_loop(0, trips_ref[0], place, 0)


def _invert(dest_flat, fill_lo, fill_hi, n_rows, batch):
    smem = pl.BlockSpec(memory_space=pltpu.SMEM)
    per_batch = dest_flat.shape[0] // batch
    assert per_batch % INVERT_UNROLL == 0
    trips = jnp.full((1,), per_batch // INVERT_UNROLL, I32)
    return pl.pallas_call(
        functools.partial(_invert_kernel, batch=batch),
        in_specs=[smem] * 4,
        out_specs=smem,
        out_shape=jax.ShapeDtypeStruct((n_rows,), I32),
        name="moe_invert_rows",
    )(trips, dest_flat, fill_lo, fill_hi)


def _expert_kernel(be_ref, run_ref, nxt_ref, slot_ref, x2s_ref, wg_hbm, wu_hbm, wd_hbm, ys_ref,
                   xres, wbuf_g, wbuf_u, wbuf_d, wgb_ref, wub_ref, wdb_ref, tile, ybuf, sems, wsems, *,
                   batch, nb, layer):
    b = pl.program_id(0)
    n = pl.program_id(1)
    g = b * nb + n
    last = batch * nb - 1
    blk = MOE_BLOCK
    seq = xres.shape[0] // SLABS
    bstride = 2 * seq + DUMMY_SLOTS
    tcur = g % 2
    ycur = g % 3
    yprev = (g + 2) % 3

    def weight_copies(e, ws):
        pairs = ((wg_hbm, wbuf_g), (wu_hbm, wbuf_u), (wd_hbm, wbuf_d))
        return [pltpu.make_async_copy(src.at[layer, e], dst.at[ws], wsems.at[ws, k])
                for k, (src, dst) in enumerate(pairs)]

    def wait_block(q):
        pltpu.make_async_copy(ybuf.at[q], ys_ref.at[pl.ds(0, blk * SLABS), :], sems.at[q]).wait()

    def gather_row(block, tslot, mi):
        s = slot_ref[block * blk + mi]
        tok = jnp.minimum(jnp.where(s >= seq, s - seq, s), seq - 1)
        slab = xres[pl.ds(pl.multiple_of(tok * SLABS, SLABS), SLABS), :]
        tile[tslot, pl.ds(mi, SLABS, stride=GATHER_PITCH), :] = slab

    def scatter_row(yslot, slot, row0, mi):
        dst = pl.multiple_of((row0 + slot) * SLABS, SLABS)
        pltpu.make_async_copy(ybuf.at[yslot, pl.ds(mi * SLABS, SLABS), :], ys_ref.at[pl.ds(dst, SLABS), :],
                              sems.at[yslot]).start()

    @pl.when(g == 0)
    def _():
        for c in weight_copies(be_ref[0], 0):
            c.start()
        ybuf[...] = jnp.zeros(ybuf.shape, F32)
        for bb in range(batch):
            for c in range(DUMMY_SLOTS // blk):
                start = (bb * bstride + 2 * seq + c * blk) * SLABS
                zero = pltpu.make_async_copy(ybuf.at[0], ys_ref.at[pl.ds(start, blk * SLABS), :], sems.at[0])
                zero.start()
                zero.wait()

    @pl.when(n == 0)
    def _():
        pltpu.sync_copy(x2s_ref.at[pl.ds(pl.multiple_of(b * (seq * SLABS), SLABS), seq * SLABS), :], xres)
        for mi in range(blk):
            gather_row(g, tcur, mi)

    run = run_ref[g]
    ws = run % 2

    @pl.when((g == 0) | (run != run_ref[jnp.maximum(g - 1, 0)]))
    def _():
        for c in weight_copies(be_ref[g], ws):
            c.wait()
        wgb_ref[...] = wbuf_g[ws].astype(BF16)
        wub_ref[...] = wbuf_u[ws].astype(BF16)
        wdb_ref[...] = wbuf_d[ws].astype(BF16)

        @pl.when(nxt_ref[g] >= 0)
        def _():
            for c in weight_copies(nxt_ref[g], 1 - ws):
                c.start()

    @pl.when(g >= 2)
    def _():
        wait_block(ycur)

    nxt_block = jnp.minimum(g + 1, last)
    prev_block = jnp.maximum(g - 1, 0)
    prev_row0 = jnp.where(n == 0, jnp.maximum(b - 1, 0), b) * bstride
    x = jnp.concatenate([tile[tcur, j * GATHER_PITCH:j * GATHER_PITCH + blk, :].astype(BF16)
                         for j in range(SLABS)], axis=1)
    n_chunks = 8
    per = blk // n_chunks
    half = EXPERT_HIDDEN // 2
    quarter = D_MODEL // 4
    acts = []
    hid = None
    for c in range(n_chunks):
        for mi in range(c * per, (c + 1) * per):
            gather_row(nxt_block, 1 - tcur, mi)
        for mi in range(c * per, (c + 1) * per):
            slot = jnp.where(g == 0, 2 * seq + 2 * blk + mi, slot_ref[prev_block * blk + mi])
            scatter_row(yprev, slot, prev_row0, mi)
        if c < 4:
            w_ref = wgb_ref if c < 2 else wub_ref
            acts.append(jnp.dot(x, w_ref[:, (c % 2) * half:(c % 2 + 1) * half], preferred_element_type=F32))
        if c == 3:
            a = jnp.concatenate(acts[0:2], axis=1)
            u = jnp.concatenate(acts[2:4], axis=1)
            hid = (a * jax.nn.sigmoid(a) * u).astype(BF16)
        if c >= 4:
            q = c - 4
            yq = jnp.dot(hid, wdb_ref[:, q * quarter:(q + 1) * quarter], preferred_element_type=F32)
            for jj in range(quarter // LANES):
                j = q * (quarter // LANES) + jj
                ybuf[ycur, pl.ds(j, blk, stride=SLABS), :] = yq[:, jj * LANES:(jj + 1) * LANES]

    @pl.when(g == last)
    def _():
        wait_block((last - 2) % 3)
        for mi in range(blk):
            scatter_row(last % 3, slot_ref[last * blk + mi], (batch - 1) * bstride, mi)
        wait_block((last - 1) % 3)
        wait_block(last % 3)


def _experts(block_expert, row_slot, x2s, wg, wu, wd, layer, batch, seq):
    blk = MOE_BLOCK
    nb = row_slot.shape[0] // blk // batch
    steps = batch * nb
    assert steps >= 3
    idx = jnp.arange(steps, dtype=I32)
    change = jnp.concatenate([jnp.zeros((1,), I32), (block_expert[1:] != block_expert[:-1]).astype(I32)])
    run = jnp.cumsum(change).astype(I32)
    later_change = (idx[None, :] > idx[:, None]) & (change[None, :] > 0)
    nxt_idx = jnp.min(jnp.where(later_change, idx[None, :], steps), axis=1)
    nxt = jnp.where(nxt_idx < steps, block_expert[jnp.minimum(nxt_idx, steps - 1)], -1).astype(I32)
    hbm = pl.BlockSpec(memory_space=pl.ANY)
    grid_spec = pltpu.PrefetchScalarGridSpec(
        num_scalar_prefetch=4,
        grid=(batch, nb),
        in_specs=[hbm, hbm, hbm, hbm],
        out_specs=hbm,
        scratch_shapes=[
            pltpu.VMEM((seq * SLABS, LANES), F32),
            pltpu.VMEM((2, D_MODEL, EXPERT_HIDDEN), F32), pltpu.VMEM((2, D_MODEL, EXPERT_HIDDEN), F32),
            pltpu.VMEM((2, EXPERT_HIDDEN, D_MODEL), F32),
            pltpu.VMEM((D_MODEL, EXPERT_HIDDEN), BF16), pltpu.VMEM((D_MODEL, EXPERT_HIDDEN), BF16),
            pltpu.VMEM((EXPERT_HIDDEN, D_MODEL), BF16),
            pltpu.VMEM((2, SLABS * GATHER_PITCH, LANES), F32),
            pltpu.VMEM((3, blk * SLABS, LANES), F32),
            pltpu.SemaphoreType.DMA((3,)),
            pltpu.SemaphoreType.DMA((2, 3)),
        ],
    )
    return pl.pallas_call(
        functools.partial(_expert_kernel, batch=batch, nb=nb, layer=layer),
        grid_spec=grid_spec,
        out_shape=jax.ShapeDtypeStruct((batch * (2 * seq + DUMMY_SLOTS) * SLABS, LANES), F32),
        compiler_params=_cparams(("arbitrary", "arbitrary"), VMEM_LIMIT_EXPERTS),
        name="moe_experts",
    )(block_expert, run, nxt, row_slot, x2s, wg, wu, wd)


def _combine_kernel(h_ref, gates_ref, y0_ref, y1_ref, *rest, final):
    if final:
        fg_ref, out_ref = rest
    else:
        (out_ref,) = rest
    tm = h_ref.shape[0]
    gates = gates_ref[...]
    g0, g1 = gates[:, 0:1], gates[:, 1:2]
    parts = []
    for j in range(SLABS):
        y0 = y0_ref[pl.ds(j, tm, stride=SLABS), :]
        y1 = y1_ref[pl.ds(j, tm, stride=SLABS), :]
        parts.append(g0 * y0 + g1 * y1)
    out = h_ref[...] + jnp.concatenate(parts, axis=1)
    if final:
        out = _rms(out, fg_ref[...])
    out_ref[...] = out


def _combine(h, gates, ys, batch, seq, final_gain=None):
    t = h.shape[0]
    tm = ROW_TILE
    tpb = seq // tm
    bstride = 2 * tpb + DUMMY_SLOTS // tm
    final = final_gain is not None
    in_specs = [pl.BlockSpec((tm, D_MODEL), lambda i: (i, 0)),
                pl.BlockSpec((tm, LANES), lambda i: (i, 0)),
                pl.BlockSpec((tm * SLABS, LANES), lambda i: (i // tpb * bstride + i % tpb, 0)),
                pl.BlockSpec((tm * SLABS, LANES), lambda i: (i // tpb * bstride + tpb + i % tpb, 0))]
    args = [h, gates, ys, ys]
    if final:
        in_specs.append(pl.BlockSpec((1, D_MODEL), lambda i: (0, 0)))
        args.append(final_gain.reshape(1, D_MODEL))
    return pl.pallas_call(
        functools.partial(_combine_kernel, final=final),
        grid=(t // tm,),
        in_specs=in_specs,
        out_specs=pl.BlockSpec((tm, D_MODEL), lambda i: (i, 0)),
        out_shape=jax.ShapeDtypeStruct((t, D_MODEL), F32),
        compiler_params=_cparams(("parallel",)),
        name="moe_combine_final" if final else "moe_combine",
    )(*args)


def _moe(h, gain, w_r1, b_r1, w_r2, b_r2, wg, wu, wd, layer, batch, final_gain=None):
    t = h.shape[0]
    seq = t // batch
    assert seq <= (1 << RANK_BITS) and seq % ROW_TILE == 0 and DUMMY_SLOTS % ROW_TILE == 0
    pad_w = jnp.zeros((D_MODEL, LANES - MOE_GROUPS - MOE_EXPERTS), F32)
    wr = jnp.concatenate([w_r1, w_r2, pad_w], axis=1)
    br = jnp.concatenate([b_r1, b_r2, jnp.zeros((LANES - MOE_GROUPS - MOE_EXPERTS,), F32)]).reshape(1, LANES)
    x2s, codes, gates, counts = _router(h, gain, wr, br, batch)
    cnt = counts.reshape(batch, 8, LANES)[:, 0, ROUTER_LANE0:ROUTER_LANE0 + MOE_EXPERTS].astype(I32)
    padded = (cnt + MOE_BLOCK - 1) // MOE_BLOCK * MOE_BLOCK
    pad_end = jnp.cumsum(padded, axis=1)
    rows_pb = seq * 2 + MOE_EXPERTS * MOE_BLOCK
    nb = rows_pb // MOE_BLOCK
    pad_start = pad_end - padded + (jnp.arange(batch, dtype=I32) * rows_pb)[:, None]
    block_start = jnp.arange(nb, dtype=I32) * MOE_BLOCK
    block_expert = jnp.minimum(
        jnp.sum((pad_end[:, None, :] <= block_start[None, :, None]).astype(I32), axis=2), MOE_EXPERTS - 1)
    dest = _dest_rows(pad_start.reshape(-1), codes)
    dest_flat = dest[0:2].reshape(2, batch, seq).transpose(1, 0, 2).reshape(-1)
    batch_end = ((jnp.arange(batch, dtype=I32) + 1) * rows_pb)[:, None]
    fill_lo = jnp.concatenate([pad_start + cnt, pad_start[:, -1:] + padded[:, -1:]], axis=1)
    fill_hi = jnp.concatenate([pad_start + padded, batch_end], axis=1)
    row_slot = _invert(dest_flat, fill_lo.reshape(-1), fill_hi.reshape(-1), batch * rows_pb, batch)
    ys = _experts(block_expert.reshape(-1), row_slot, x2s, wg, wu, wd, layer, batch, seq)
    return _combine(h, gates, ys, batch, seq, final_gain)


def kernel(x, mix_norm, ffn_norm, final_norm, even_w_in, even_w_out, conv_w, conv_b, conv_norm_g, conv_norm_b,
           odd_w_in, odd_w_out, router_w1, router_b1, router_w2, router_b2, expert_w_gate, expert_w_up,
           expert_w_down):
    batch, seq, d = x.shape
    assert d == D_MODEL and seq % (ATTN_BLOCK * max(dl for _, dl in A_BRANCHES)) == 0
    assert all(w // dl == ATTN_BLOCK for w, dl in A_BRANCHES)
    t = batch * seq
    h = x.reshape(t, d)

    qkv, conv_in = _even_in(h, mix_norm[0], even_w_in, _attn_rope_tables(seq), seq)
    attn = _attention(qkv, batch, seq)
    bconv = _conv(conv_in, conv_w[0], conv_b[0], conv_norm_g[0], conv_norm_b[0], batch, seq)
    h = _even_out(h, attn, bconv, even_w_out)
    h = _moe(h, ffn_norm[0], router_w1[0], router_b1[0], router_w2[0], router_b2[0],
             expert_w_gate, expert_w_up, expert_w_down, 0, batch)

    inv_freq = RET_ROT_THETA ** (-jnp.linspace(0.0, 1.0, C_QK_DIM // 2, dtype=F32))
    ang = jnp.arange(seq, dtype=F32)[:, None] * inv_freq[None, :]
    proj = _odd_in(h, mix_norm[1], odd_w_in, jnp.cos(ang), jnp.sin(ang), seq)
    log_decay = jnp.log(1.0 - jnp.exp2(-5.0 - jnp.arange(C_HEADS, dtype=F32)))
    y = _retention(proj, log_decay, batch, seq)
    h = _odd_out(h, y, odd_w_out)
    out = _moe(h, ffn_norm[1], router_w1[1], router_b1[1], router_w2[1], router_b2[1],
               expert_w_gate, expert_w_up, expert_w_down, 1, batch, final_gain=final_norm)
    return out.reshape(batch, seq, d)
```

```python
import functools

import jax
import jax.numpy as jnp
from jax import lax
from jax.experimental import pallas as pl
from jax.experimental.pallas import tpu as pltpu

F32 = jnp.float32
BF16 = jnp.bfloat16
I32 = jnp.int32

NORM_EPS = 1e-6
NEG_INF = -1e30

D_MODEL = 1024
A_HEADS = 8
A_HEAD_DIM = 64
A_WIDTH = A_HEADS * A_HEAD_DIM
A_BRANCHES = ((128, 1), (512, 4), (2048, 16))
ATTN_BLOCK = 128
ROPE_THETA = 500000.0
ROPE_DIM = A_HEAD_DIM // 4
B_WIDTH = D_MODEL - A_WIDTH
CONV_WIDTH = 31
C_HEADS = 4
C_QK_DIM = 256
C_V_DIM = 512
C_QK_WIDTH = C_HEADS * C_QK_DIM
C_V_WIDTH = C_HEADS * C_V_DIM
RET_CHUNK = 128
RET_ROT_THETA = 10000.0
MOE_GROUPS = 4
MOE_EXPERTS_PER_GROUP = 8
MOE_EXPERTS = MOE_GROUPS * MOE_EXPERTS_PER_GROUP
EXPERT_HIDDEN = 512
MOE_BLOCK = 256
EVEN_IN_WIDTH = 3 * A_WIDTH + 2 * B_WIDTH
ODD_IN_WIDTH = 2 * C_QK_WIDTH + 2 * C_V_WIDTH

LANES = 128
ROW_TILE = 512
CONV_TILE = 512
CONV_HALO = 32
CONV_CHUNK = 64
ROUTER_LANE0 = MOE_GROUPS
VMEM_LIMIT = 56 * 1024 * 1024
VMEM_LIMIT_EXPERTS = 60 * 1024 * 1024
SLABS = D_MODEL // LANES
RANK_BITS = 16
GATHER_PITCH = MOE_BLOCK + 8
DUMMY_SLOTS = 2 * ROW_TILE


def _cparams(sem, vmem=VMEM_LIMIT):
    return pltpu.CompilerParams(dimension_semantics=sem, vmem_limit_bytes=vmem)


def _cast_once(w_ref, wb_ref, first):
    @pl.when(first)
    def _():
        wb_ref[...] = w_ref[...].astype(BF16)


def _rms(x, gain):
    ms = jnp.mean(x * x, axis=-1, keepdims=True)
    return x * lax.rsqrt(ms + NORM_EPS) * gain


def _even_in_kernel(h_ref, g_ref, w_ref, c_ref, s1_ref, s2_ref, qkv_ref, conv_ref, wb_ref):
    _cast_once(w_ref, wb_ref, pl.program_id(0) == 0)
    u = _rms(h_ref[...], g_ref[...]).astype(BF16)
    acc = jnp.dot(u, wb_ref[...], preferred_element_type=F32)
    c, s1, s2 = c_ref[...], s1_ref[...], s2_ref[...]
    for j in range(2 * A_WIDTH // LANES):
        xg = acc[:, j * LANES:(j + 1) * LANES]
        if j < A_WIDTH // LANES:
            xg = xg * (A_HEAD_DIM ** -0.5)
        qkv_ref[j] = xg * c + pltpu.roll(xg, LANES - ROPE_DIM // 2, 1) * s1 + pltpu.roll(xg, ROPE_DIM // 2, 1) * s2
    for j in range(2 * A_WIDTH // LANES, 3 * A_WIDTH // LANES):
        qkv_ref[j] = acc[:, j * LANES:(j + 1) * LANES]
    conv_ref[...] = acc[:, 3 * A_WIDTH:].astype(BF16)


def _even_in(h, gain, w, tabs, seq):
    t = h.shape[0]
    tm = ROW_TILE
    nseq = seq // tm
    tab_spec = pl.BlockSpec((tm, LANES), lambda i: (i % nseq, 0))
    return pl.pallas_call(
        _even_in_kernel,
        grid=(t // tm,),
        in_specs=[
            pl.BlockSpec((tm, D_MODEL), lambda i: (i, 0)),
            pl.BlockSpec((1, D_MODEL), lambda i: (0, 0)),
            pl.BlockSpec((None, D_MODEL, EVEN_IN_WIDTH), lambda i: (0, 0, 0)),
            tab_spec, tab_spec, tab_spec,
        ],
        out_specs=[
            pl.BlockSpec((3 * A_WIDTH // LANES, tm, LANES), lambda i: (0, i, 0)),
            pl.BlockSpec((tm, 2 * B_WIDTH), lambda i: (i, 0)),
        ],
        out_shape=[
            jax.ShapeDtypeStruct((3 * A_WIDTH // LANES, t, LANES), F32),
            jax.ShapeDtypeStruct((t, 2 * B_WIDTH), BF16),
        ],
        scratch_shapes=[pltpu.VMEM((D_MODEL, EVEN_IN_WIDTH), BF16)],
        compiler_params=_cparams(("arbitrary",)),
        name="even_in_proj",
    )(h, gain.reshape(1, D_MODEL), w, *tabs)


def _attn_rope_tables(seq):
    half = ROPE_DIM // 2
    inv_freq = ROPE_THETA ** (-jnp.arange(0, ROPE_DIM, 2, dtype=F32) / ROPE_DIM)
    ang = jnp.arange(seq, dtype=F32)[:, None] * inv_freq[None, :]
    cos, sin = jnp.cos(ang), jnp.sin(ang)
    rest = A_HEAD_DIM - ROPE_DIM
    ones = jnp.ones((seq, rest), F32)
    z_rest = jnp.zeros((seq, rest), F32)
    z_half = jnp.zeros((seq, half), F32)
    c = jnp.concatenate([cos, cos, ones], axis=1)
    s1 = jnp.concatenate([-sin, z_half, z_rest], axis=1)
    s2 = jnp.concatenate([z_half, sin, z_rest], axis=1)
    rep = LANES // A_HEAD_DIM
    return tuple(jnp.tile(a, (1, rep)) for a in (c, s1, s2))


ATTN_STEP = ATTN_BLOCK * max(d for _, d in A_BRANCHES)
ATTN_PAIRS = A_WIDTH // LANES
ATTN_MERGE_ROWS = 256


def _attn_kernel(q_ref, k_ref, v_ref, o_ref, kprev, vprev, oml):
    n = pl.program_id(1)
    blk = ATTN_BLOCK
    n_blocks = ATTN_STEP // blk

    @pl.when(n == 0)
    def _():
        kprev[...] = jnp.zeros(kprev.shape, BF16)
        vprev[...] = jnp.zeros(vprev.shape, BF16)

    qi = lax.broadcasted_iota(I32, (blk, 2 * blk), 0)
    kj = lax.broadcasted_iota(I32, (blk, 2 * blk), 1)
    dist = blk + qi - kj
    band = (dist >= 0) & (dist <= blk)
    in_cur = kj >= blk
    lane = lax.broadcasted_iota(I32, (blk, LANES), 1)
    lo = lane < A_HEAD_DIM

    off = 0
    for bi, (_, dil) in enumerate(A_BRANCHES):
        shift = dil.bit_length() - 1
        assert dil == 1 << shift

        def block_body(idx, c, bi=bi, dil=dil, shift=shift, off=off):
            u = lax.shift_right_logical(idx, shift)
            r = idx & (dil - 1)
            start = u * (blk * dil) + r
            rows = pl.ds(pl.multiple_of(start, blk), blk) if dil == 1 else pl.ds(start, blk, stride=dil)
            has_prev = (n > 0) | (u > 0)
            valid = band & (in_cur | has_prev)
            valid2 = jnp.concatenate([valid, valid], axis=0)
            for g in range(ATTN_PAIRS):
                q = q_ref[g, rows, :]
                kc = k_ref[g, rows, :].astype(BF16)
                vc = v_ref[g, rows, :].astype(BF16)
                q2 = jnp.concatenate([jnp.where(lo, q, 0.0), jnp.where(lo, 0.0, q)], axis=0).astype(BF16)
                kk = jnp.concatenate([kprev[g, off + r], kc], axis=0)
                vv = jnp.concatenate([vprev[g, off + r], vc], axis=0)
                s = lax.dot_general(q2, kk, (((1,), (1,)), ((), ())), preferred_element_type=F32)
                s = jnp.where(valid2, s, NEG_INF)
                m = jnp.max(s, axis=-1, keepdims=True)
                e = jnp.exp(s - m)
                den = jnp.sum(e, axis=-1, keepdims=True)
                pv = jnp.dot(e.astype(BF16), vv, preferred_element_type=F32)
                o_new = jnp.where(lo, pv[:blk], pv[blk:])
                m_new = jnp.where(lo, m[:blk], m[blk:])
                l_new = jnp.where(lo, den[:blk], den[blk:])
                if bi > 0:
                    o_run, m_run, l_run = oml[0, g, rows, :], oml[1, g, rows, :], oml[2, g, rows, :]
                    m_both = jnp.maximum(m_run, m_new)
                    w_run, w_new = jnp.exp(m_run - m_both), jnp.exp(m_new - m_both)
                    o_new = w_run * o_run + w_new * o_new
                    l_new = w_run * l_run + w_new * l_new
                    m_new = m_both
                oml[0, g, rows, :] = o_new
                oml[1, g, rows, :] = m_new
                oml[2, g, rows, :] = l_new
                kprev[g, off + r] = kc
                vprev[g, off + r] = vc
            return c

        lax.fori_loop(0, n_blocks, block_body, 0)
        off += dil
    for g in range(ATTN_PAIRS):
        for c in range(ATTN_STEP // ATTN_MERGE_ROWS):
            sl = slice(c * ATTN_MERGE_ROWS, (c + 1) * ATTN_MERGE_ROWS)
            o_ref[g, sl, :] = (oml[0, g, sl, :] / oml[2, g, sl, :]).astype(BF16)


def _attention(qkv, batch, seq):
    t = qkv.shape[1]
    steps = seq // ATTN_STEP
    n_res = sum(d for _, d in A_BRANCHES)

    def slabs(which):
        return pl.BlockSpec((ATTN_PAIRS, ATTN_STEP, LANES), lambda b, n: (which, b * steps + n, 0))

    return pl.pallas_call(
        _attn_kernel,
        grid=(batch, steps),
        in_specs=[slabs(0), slabs(1), slabs(2)],
        out_specs=pl.BlockSpec((ATTN_PAIRS, ATTN_STEP, LANES), lambda b, n: (0, b * steps + n, 0)),
        out_shape=jax.ShapeDtypeStruct((ATTN_PAIRS, t, LANES), BF16),
        scratch_shapes=[
            pltpu.VMEM((ATTN_PAIRS, n_res, ATTN_BLOCK, LANES), BF16),
            pltpu.VMEM((ATTN_PAIRS, n_res, ATTN_BLOCK, LANES), BF16),
            pltpu.VMEM((3, ATTN_PAIRS, ATTN_STEP, LANES), F32),
        ],
        compiler_params=_cparams(("parallel", "arbitrary")),
        name="dilated_attention",
    )(qkv, qkv, qkv)


def _conv_kernel(val_ref, gate_ref, w_ref, b_ref, g_ref, beta_ref, o_ref, abuf, shifted):
    n = pl.program_id(1)
    tc = CONV_TILE
    sub = 8

    @pl.when(n == 0)
    def _():
        abuf[0:CONV_HALO, :] = jnp.zeros((CONV_HALO, B_WIDTH), F32)

    @pl.when(n > 0)
    def _():
        abuf[0:CONV_HALO, :] = abuf[tc:tc + CONV_HALO, :]

    val = val_ref[...].astype(F32)
    gate = gate_ref[...].astype(F32)
    abuf[CONV_HALO:CONV_HALO + tc, :] = val * jax.nn.sigmoid(gate)
    off = CONV_HALO - (CONV_WIDTH - 1)
    span = shifted.shape[1]
    for s in range(1, sub):
        shifted[s - 1] = abuf[s:s + span, :]
    for c in range(tc // CONV_CHUNK):
        acc = jnp.broadcast_to(b_ref[...], (CONV_CHUNK, B_WIDTH))
        for j in range(CONV_WIDTH):
            s = (off + j) % sub
            r0 = c * CONV_CHUNK + off + j - s
            src = abuf[r0:r0 + CONV_CHUNK, :] if s == 0 else shifted[s - 1, r0:r0 + CONV_CHUNK, :]
            acc = acc + w_ref[j:j + 1, :] * src
        mu = jnp.mean(acc, axis=-1, keepdims=True)
        cen = acc - mu
        var = jnp.mean(cen * cen, axis=-1, keepdims=True)
        yn = cen * lax.rsqrt(var + NORM_EPS) * g_ref[...] + beta_ref[...]
        o_ref[c * CONV_CHUNK:(c + 1) * CONV_CHUNK, :] = (yn * jax.nn.sigmoid(yn)).astype(BF16)


def _conv(conv_in, w, b, g, beta, batch, seq):
    t = conv_in.shape[0]
    tc = CONV_TILE
    nt = seq // tc
    w_pad = jnp.concatenate([w, jnp.zeros((CONV_HALO - CONV_WIDTH, B_WIDTH), F32)], axis=0)
    vec = pl.BlockSpec((1, B_WIDTH), lambda bb, n: (0, 0))
    return pl.pallas_call(
        _conv_kernel,
        grid=(batch, nt),
        in_specs=[
            pl.BlockSpec((tc, B_WIDTH), lambda bb, n: (bb * nt + n, 0)),
            pl.BlockSpec((tc, B_WIDTH), lambda bb, n: (bb * nt + n, 1)),
            pl.BlockSpec((CONV_HALO, B_WIDTH), lambda bb, n: (0, 0)),
            vec, vec, vec,
        ],
        out_specs=pl.BlockSpec((tc, B_WIDTH), lambda bb, n: (bb * nt + n, 0)),
        out_shape=jax.ShapeDtypeStruct((t, B_WIDTH), BF16),
        scratch_shapes=[pltpu.VMEM((tc + CONV_HALO, B_WIDTH), F32),
                        pltpu.VMEM((7, tc + CONV_HALO - 8, B_WIDTH), F32)],
        compiler_params=_cparams(("parallel", "arbitrary")),
        name="conformer_conv",
    )(conv_in, conv_in, w_pad, b.reshape(1, -1), g.reshape(1, -1), beta.reshape(1, -1))


def _even_out_kernel(h_ref, a_ref, bc_ref, w_ref, out_ref, wb_ref):
    _cast_once(w_ref, wb_ref, pl.program_id(0) == 0)
    a = jnp.concatenate([a_ref[g] for g in range(ATTN_PAIRS)], axis=1)
    acc = jnp.dot(a, wb_ref[0:A_WIDTH, :], preferred_element_type=F32)
    acc = acc + jnp.dot(bc_ref[...], wb_ref[A_WIDTH:, :], preferred_element_type=F32)
    out_ref[...] = h_ref[...] + acc


def _even_out(h, attn, bconv, w):
    t = h.shape[0]
    tm = ROW_TILE
    row = lambda width: pl.BlockSpec((tm, width), lambda i: (i, 0))
    return pl.pallas_call(
        _even_out_kernel,
        grid=(t // tm,),
        in_specs=[row(D_MODEL), pl.BlockSpec((ATTN_PAIRS, tm, LANES), lambda i: (0, i, 0)), row(B_WIDTH),
                  pl.BlockSpec((None, D_MODEL, D_MODEL), lambda i: (0, 0, 0))],
        out_specs=row(D_MODEL),
        out_shape=jax.ShapeDtypeStruct((t, D_MODEL), F32),
        scratch_shapes=[pltpu.VMEM((D_MODEL, D_MODEL), BF16)],
        compiler_params=_cparams(("arbitrary",)),
        name="even_out_proj",
    )(h, attn, bconv, w)


def _odd_in_kernel(h_ref, g_ref, w_ref, cos_ref, sin_ref, o_ref, wb_ref):
    j = pl.program_id(0)
    _cast_once(w_ref, wb_ref, pl.program_id(1) == 0)
    u = _rms(h_ref[...], g_ref[...]).astype(BF16)
    acc = jnp.dot(u, wb_ref[...], preferred_element_type=F32)

    @pl.when(j == 0)
    def _():
        cos, sin = cos_ref[...], sin_ref[...]
        half = C_QK_DIM // 2
        for hd in range(2 * C_HEADS):
            x1 = acc[:, hd * C_QK_DIM:hd * C_QK_DIM + half]
            x2 = acc[:, hd * C_QK_DIM + half:(hd + 1) * C_QK_DIM]
            r1 = x1 * cos - x2 * sin
            r2 = x2 * cos + x1 * sin
            if hd >= C_HEADS:
                r1 = r1 * (C_QK_DIM ** -0.5)
                r2 = r2 * (C_QK_DIM ** -0.5)
            o_ref[:, hd * C_QK_DIM:hd * C_QK_DIM + half] = r1.astype(BF16)
            o_ref[:, hd * C_QK_DIM + half:(hd + 1) * C_QK_DIM] = r2.astype(BF16)

    @pl.when(j > 0)
    def _():
        o_ref[...] = acc.astype(BF16)


def _odd_in(h, gain, w, cos, sin, seq):
    t = h.shape[0]
    tm = ROW_TILE
    tn = 2 * C_QK_WIDTH
    nseq = seq // tm
    tab = pl.BlockSpec((tm, C_QK_DIM // 2), lambda j, i: (i % nseq, 0))
    return pl.pallas_call(
        _odd_in_kernel,
        grid=(ODD_IN_WIDTH // tn, t // tm),
        in_specs=[
            pl.BlockSpec((tm, D_MODEL), lambda j, i: (i, 0)),
            pl.BlockSpec((1, D_MODEL), lambda j, i: (0, 0)),
            pl.BlockSpec((None, D_MODEL, tn), lambda j, i: (0, 0, j)),
            tab, tab,
        ],
        out_specs=pl.BlockSpec((tm, tn), lambda j, i: (i, j)),
        out_shape=jax.ShapeDtypeStruct((t, ODD_IN_WIDTH), BF16),
        scratch_shapes=[pltpu.VMEM((D_MODEL, tn), BF16)],
        compiler_params=_cparams(("arbitrary", "arbitrary")),
        name="odd_in_proj",
    )(h, gain.reshape(1, D_MODEL), w, cos, sin)


def _ret_kernel(ld_ref, q_ref, k_ref, v_ref, g_ref, o_ref, state):
    c = pl.program_id(1)
    ch = RET_CHUNK

    @pl.when(c == 0)
    def _():
        state[...] = jnp.zeros(state.shape, F32)

    ii = lax.broadcasted_iota(I32, (ch, ch), 0)
    jj = lax.broadcasted_iota(I32, (ch, ch), 1)
    diff = (ii - jj).astype(F32)
    pos = lax.broadcasted_iota(I32, (ch, 1), 0).astype(F32)
    for hd in range(C_HEADS):
        ld = ld_ref[hd]
        q = q_ref[:, hd * C_QK_DIM:(hd + 1) * C_QK_DIM]
        k = k_ref[:, hd * C_QK_DIM:(hd + 1) * C_QK_DIM]
        v = v_ref[:, hd * C_V_DIM:(hd + 1) * C_V_DIM]
        intra = jnp.where(diff >= 0, jnp.exp(ld * jnp.maximum(diff, 0.0)), 0.0)
        s = lax.dot_general(q, k, (((1,), (1,)), ((), ())), preferred_element_type=F32) * intra
        inner = jnp.dot(s.astype(BF16), v, preferred_element_type=F32)
        q_decay = jnp.exp(ld * (pos + 1.0))
        k_decay = jnp.exp(ld * (ch - 1.0 - pos))
        chunk_decay = jnp.exp(ld * jnp.full((1, 1), float(ch), F32))
        st = state[hd]
        cross = jnp.dot(q, st.astype(BF16), preferred_element_type=F32) * q_decay
        kd_t = jnp.transpose(k.astype(F32) * k_decay).astype(BF16)
        state[hd] = st * chunk_decay + jnp.dot(kd_t, v, preferred_element_type=F32)
        out = inner + cross
        mu = jnp.mean(out, axis=-1, keepdims=True)
        cen = out - mu
        var = jnp.mean(cen * cen, axis=-1, keepdims=True)
        o = cen * lax.rsqrt(var + NORM_EPS)
        gf = g_ref[:, hd * C_V_DIM:(hd + 1) * C_V_DIM].astype(F32)
        o_ref[:, hd * C_V_DIM:(hd + 1) * C_V_DIM] = (gf * jax.nn.sigmoid(gf) * o).astype(BF16)


def _retention(proj, log_decay, batch, seq):
    t = proj.shape[0]
    ch = RET_CHUNK
    nc = seq // ch
    v0 = 2 * C_QK_WIDTH // C_V_WIDTH
    grid_spec = pltpu.PrefetchScalarGridSpec(
        num_scalar_prefetch=1,
        grid=(batch, nc),
        in_specs=[
            pl.BlockSpec((ch, C_QK_WIDTH), lambda b, c, ld: (b * nc + c, 0)),
            pl.BlockSpec((ch, C_QK_WIDTH), lambda b, c, ld: (b * nc + c, 1)),
            pl.BlockSpec((ch, C_V_WIDTH), lambda b, c, ld: (b * nc + c, v0)),
            pl.BlockSpec((ch, C_V_WIDTH), lambda b, c, ld: (b * nc + c, v0 + 1)),
        ],
        out_specs=pl.BlockSpec((ch, C_V_WIDTH), lambda b, c, ld: (b * nc + c, 0)),
        scratch_shapes=[pltpu.VMEM((C_HEADS, C_QK_DIM, C_V_DIM), F32)],
    )
    return pl.pallas_call(
        _ret_kernel,
        grid_spec=grid_spec,
        out_shape=jax.ShapeDtypeStruct((t, C_V_WIDTH), BF16),
        compiler_params=_cparams(("parallel", "arbitrary")),
        name="retention",
    )(log_decay, proj, proj, proj, proj)


def _odd_out_kernel(h_ref, y_ref, w_ref, out_ref, wb_ref):
    _cast_once(w_ref, wb_ref, pl.program_id(0) == 0)
    out_ref[...] = h_ref[...] + jnp.dot(y_ref[...], wb_ref[...], preferred_element_type=F32)


def _odd_out(h, y, w):
    t = h.shape[0]
    tm = ROW_TILE
    return pl.pallas_call(
        _odd_out_kernel,
        grid=(t // tm,),
        in_specs=[pl.BlockSpec((tm, D_MODEL), lambda i: (i, 0)),
                  pl.BlockSpec((tm, C_V_WIDTH), lambda i: (i, 0)),
                  pl.BlockSpec((None, C_V_WIDTH, D_MODEL), lambda i: (0, 0, 0))],
        out_specs=pl.BlockSpec((tm, D_MODEL), lambda i: (i, 0)),
        out_shape=jax.ShapeDtypeStruct((t, D_MODEL), F32),
        scratch_shapes=[pltpu.VMEM((C_V_WIDTH, D_MODEL), BF16)],
        compiler_params=_cparams(("arbitrary",)),
        name="odd_out_proj",
    )(h, y, w)


def _router_kernel(h_ref, g_ref, wr_ref, br_ref, x2s_ref, codes_ref, gates_ref, cnt_ref, base_ref, *,
                   tiles_per_batch):
    i = pl.program_id(0)
    tm = h_ref.shape[0]
    batch = i // tiles_per_batch

    @pl.when(i % tiles_per_batch == 0)
    def _():
        base_ref[...] = jnp.zeros(base_ref.shape, F32)

    x2 = _rms(h_ref[...], g_ref[...])
    for j in range(SLABS):
        x2s_ref[pl.ds(j, tm, stride=SLABS), :] = x2[:, j * LANES:(j + 1) * LANES]
    logits = jnp.dot(x2, wr_ref[...], precision=lax.Precision.HIGHEST, preferred_element_type=F32) + br_ref[...]
    lane = lax.broadcasted_iota(I32, (tm, LANES), 1)
    is_grp = lane < MOE_GROUPS
    lg = jnp.where(is_grp, logits, -jnp.inf)
    gmax = jnp.max(lg, axis=-1, keepdims=True)
    gsum = jnp.sum(jnp.where(is_grp, jnp.exp(logits - gmax), 0.0), axis=-1, keepdims=True)
    gp = 1.0 / gsum
    gi = jnp.min(jnp.where(lg == gmax, lane, LANES), axis=-1, keepdims=True)
    ex = lane - ROUTER_LANE0
    in_grp = (ex >= 0) & (ex < MOE_EXPERTS) & ((ex // MOE_EXPERTS_PER_GROUP) == gi)
    sel = jnp.where(in_grp, logits, -jnp.inf)
    v1 = jnp.max(sel, axis=-1, keepdims=True)
    i1 = jnp.min(jnp.where(sel == v1, lane, LANES), axis=-1, keepdims=True)
    sel2 = jnp.where(lane == i1, -jnp.inf, sel)
    v2 = jnp.max(sel2, axis=-1, keepdims=True)
    i2 = jnp.min(jnp.where(sel2 == v2, lane, LANES), axis=-1, keepdims=True)
    tt = jnp.exp(v2 - v1)
    g0 = gp / (1.0 + tt)
    g1 = gp * tt / (1.0 + tt)
    oh0 = (lane == i1).astype(F32)
    oh1 = (lane == i2).astype(F32)
    oh = oh0 + oh1
    rr = lax.broadcasted_iota(I32, (tm, tm), 0)
    cc = lax.broadcasted_iota(I32, (tm, tm), 1)
    tri = (cc < rr).astype(BF16)
    prefix = jnp.dot(tri, oh.astype(BF16), preferred_element_type=F32)
    tot = base_ref[...] + prefix
    rank0 = jnp.sum(oh0 * tot, axis=-1, keepdims=True).astype(I32)
    rank1 = jnp.sum(oh1 * tot, axis=-1, keepdims=True).astype(I32)
    base_ref[...] = base_ref[...] + jnp.sum(oh, axis=0, keepdims=True)
    eoff = batch * MOE_EXPERTS - ROUTER_LANE0
    code0 = (i1 + eoff) * (1 << RANK_BITS) + rank0
    code1 = (i2 + eoff) * (1 << RANK_BITS) + rank1
    meta = jnp.where(lane == 0, code0, jnp.where(lane == 1, code1, 0))
    codes_ref[...] = jnp.transpose(meta)[0:8, :]
    gates_ref[...] = jnp.where(lane == 0, g0, jnp.where(lane == 1, g1, 0.0))
    cnt_ref[...] = jnp.broadcast_to(base_ref[...], cnt_ref.shape)


def _router(h, gain, wr, br, batch):
    t = h.shape[0]
    tm = ROW_TILE
    tpb = t // batch // tm
    row = lambda width: pl.BlockSpec((tm, width), lambda i: (i, 0))
    fixed = lambda shape: pl.BlockSpec(shape, lambda i: (0, 0))
    return pl.pallas_call(
        functools.partial(_router_kernel, tiles_per_batch=tpb),
        grid=(t // tm,),
        in_specs=[row(D_MODEL), fixed((1, D_MODEL)), fixed((D_MODEL, LANES)), fixed((1, LANES))],
        out_specs=[pl.BlockSpec((tm * SLABS, LANES), lambda i: (i, 0)),
                   pl.BlockSpec((8, tm), lambda i: (0, i)), row(LANES),
                   pl.BlockSpec((8, LANES), lambda i: (i // tpb, 0))],
        out_shape=[
            jax.ShapeDtypeStruct((t * SLABS, LANES), F32),
            jax.ShapeDtypeStruct((8, t), I32),
            jax.ShapeDtypeStruct((t, LANES), F32),
            jax.ShapeDtypeStruct((batch * 8, LANES), F32),
        ],
        scratch_shapes=[pltpu.VMEM((1, LANES), F32)],
        compiler_params=_cparams(("arbitrary",)),
        name="moe_router",
    )(h, gain.reshape(1, D_MODEL), wr, br)


INVERT_UNROLL = 16


def _dest_kernel(pstart_ref, codes_ref, dest_ref):
    codes = codes_ref[...]
    seg = lax.shift_right_logical(codes, RANK_BITS)
    dest = codes & ((1 << RANK_BITS) - 1)
    for i in range(pstart_ref.shape[0]):
        dest = dest + jnp.where(seg == i, pstart_ref[i], 0)
    dest_ref[...] = dest


def _dest_rows(pad_start, codes):
    grid_spec = pltpu.PrefetchScalarGridSpec(
        num_scalar_prefetch=1,
        grid=(1,),
        in_specs=[pl.BlockSpec(codes.shape, lambda i, ps: (0, 0))],
        out_specs=pl.BlockSpec(codes.shape, lambda i, ps: (0, 0)),
    )
    return pl.pallas_call(
        _dest_kernel,
        grid_spec=grid_spec,
        out_shape=jax.ShapeDtypeStruct(codes.shape, I32),
        compiler_params=_cparams(("arbitrary",)),
        name="moe_dest_rows",
    )(pad_start, codes)


def _invert_kernel(trips_ref, dest_ref, fill_lo_ref, fill_hi_ref, slot_ref, *, batch):
    per_batch = dest_ref.shape[0] // batch
    blk_shift = MOE_BLOCK.bit_length() - 1
    assert MOE_BLOCK == 1 << blk_shift and 3 * MOE_BLOCK <= DUMMY_SLOTS

    def fill_segment(sgm, carry):
        def fill_row(r, c):
            slot_ref[r] = per_batch + (lax.shift_right_logical(r, blk_shift) & 1) * MOE_BLOCK + (r & (MOE_BLOCK - 1))
            return c

        return lax.fori_loop(fill_lo_ref[sgm], fill_hi_ref[sgm], fill_row, carry)

    lax.fori_loop(0, fill_lo_ref.shape[0], fill_segment, 0)
    n_rows = slot_ref.shape[0] - MOE_BLOCK

    def spare_row(r, carry):
        slot_ref[r] = per_batch + 2 * MOE_BLOCK + (r & (MOE_BLOCK - 1))
        return carry

    lax.fori_loop(n_rows, n_rows + MOE_BLOCK, spare_row, 0)
    for b in range(batch):
        def place(i, carry, b=b):
            for v in range(INVERT_UNROLL):
                a = i * INVERT_UNROLL + v
                slot_ref[dest_ref[b * per_batch + a]] = a
            return carry

        lax.fori_loop(0, trips_ref[0], place, 0)


def _invert(dest_flat, fill_lo, fill_hi, n_rows, batch):
    smem = pl.BlockSpec(memory_space=pltpu.SMEM)
    per_batch = dest_flat.shape[0] // batch
    assert per_batch % INVERT_UNROLL == 0
    trips = jnp.full((1,), per_batch // INVERT_UNROLL, I32)
    return pl.pallas_call(
        functools.partial(_invert_kernel, batch=batch),
        in_specs=[smem] * 4,
        out_specs=smem,
        out_shape=jax.ShapeDtypeStruct((n_rows + MOE_BLOCK,), I32),
        name="moe_invert_rows",
    )(trips, dest_flat, fill_lo, fill_hi)


def _expert_kernel(be_ref, run_ref, nxt_ref, slot_ref, x2s_ref, wg_hbm, wu_hbm, wd_hbm, ys_ref,
                   xres, wbuf_g, wbuf_u, wbuf_d, wgb_ref, wub_ref, wdb_ref, tile, ybuf, sems, wsems, *,
                   batch, nb, layer):
    b = pl.program_id(0)
    n = pl.program_id(1)
    g = b * nb + n
    last = batch * nb - 1
    blk = MOE_BLOCK
    seq = xres.shape[0] // SLABS
    bstride = 2 * seq + DUMMY_SLOTS
    tcur = g % 2
    ycur = g % 3
    yprev = (g + 2) % 3

    def weight_copies(e, ws):
        pairs = ((wg_hbm, wbuf_g), (wu_hbm, wbuf_u), (wd_hbm, wbuf_d))
        return [pltpu.make_async_copy(src.at[layer, e], dst.at[ws], wsems.at[ws, k])
                for k, (src, dst) in enumerate(pairs)]

    def wait_block(q):
        pltpu.make_async_copy(ybuf.at[q], ys_ref.at[pl.ds(0, blk * SLABS), :], sems.at[q]).wait()

    def gather_row(block, tslot, mi):
        s = slot_ref[block * blk + mi]
        if seq & (seq - 1) == 0:
            tok = s & (seq - 1)
        else:
            tok = jnp.minimum(jnp.where(s >= seq, s - seq, s), seq - 1)
        slab = xres[pl.ds(pl.multiple_of(tok * SLABS, SLABS), SLABS), :]
        tile[tslot, pl.ds(mi, SLABS, stride=GATHER_PITCH), :] = slab

    def scatter_row(yslot, slot, row0, mi):
        dst = pl.multiple_of((row0 + slot) * SLABS, SLABS)
        pltpu.make_async_copy(ybuf.at[yslot, pl.ds(mi * SLABS, SLABS), :], ys_ref.at[pl.ds(dst, SLABS), :],
                              sems.at[yslot]).start()

    @pl.when(g == 0)
    def _():
        for c in weight_copies(be_ref[0], 0):
            c.start()
        ybuf[...] = jnp.zeros(ybuf.shape, F32)
        for bb in range(batch):
            for c in range(DUMMY_SLOTS // blk):
                start = (bb * bstride + 2 * seq + c * blk) * SLABS
                zero = pltpu.make_async_copy(ybuf.at[0], ys_ref.at[pl.ds(start, blk * SLABS), :], sems.at[0])
                zero.start()
                zero.wait()

    @pl.when(n == 0)
    def _():
        pltpu.sync_copy(x2s_ref.at[pl.ds(pl.multiple_of(b * (seq * SLABS), SLABS), seq * SLABS), :], xres)
        for mi in range(blk):
            gather_row(g, tcur, mi)

    run = run_ref[g]
    ws = run % 2

    @pl.when((g == 0) | (run != run_ref[jnp.maximum(g - 1, 0)]))
    def _():
        for c in weight_copies(be_ref[g], ws):
            c.wait()
        wgb_ref[...] = wbuf_g[ws].astype(BF16)
        wub_ref[...] = wbuf_u[ws].astype(BF16)
        wdb_ref[...] = wbuf_d[ws].astype(BF16)

        @pl.when(nxt_ref[g] >= 0)
        def _():
            for c in weight_copies(nxt_ref[g], 1 - ws):
                c.start()

    @pl.when(g >= 2)
    def _():
        wait_block(ycur)

    nxt_block = jnp.minimum(g + 1, last)
    prev_block = jnp.where(g == 0, batch * nb, g - 1)
    prev_row0 = jnp.where(n == 0, jnp.maximum(b - 1, 0), b) * bstride
    x = jnp.concatenate([tile[tcur, j * GATHER_PITCH:j * GATHER_PITCH + blk, :].astype(BF16)
                         for j in range(SLABS)], axis=1)
    n_chunks = 8
    per = blk // n_chunks
    half = EXPERT_HIDDEN // 2
    quarter = D_MODEL // 4
    acts = []
    hid = None
    for c in range(n_chunks):
        for mi in range(c * per, (c + 1) * per):
            gather_row(nxt_block, 1 - tcur, mi)
        for mi in range(c * per, (c + 1) * per):
            scatter_row(yprev, slot_ref[prev_block * blk + mi], prev_row0, mi)
        if c < 4:
            w_ref = wgb_ref if c < 2 else wub_ref
            acts.append(jnp.dot(x, w_ref[:, (c % 2) * half:(c % 2 + 1) * half], preferred_element_type=F32))
        if c == 3:
            a = jnp.concatenate(acts[0:2], axis=1)
            u = jnp.concatenate(acts[2:4], axis=1)
            hid = (a * jax.nn.sigmoid(a) * u).astype(BF16)
        if c >= 4:
            q = c - 4
            yq = jnp.dot(hid, wdb_ref[:, q * quarter:(q + 1) * quarter], preferred_element_type=F32)
            for jj in range(quarter // LANES):
                j = q * (quarter // LANES) + jj
                ybuf[ycur, pl.ds(j, blk, stride=SLABS), :] = yq[:, jj * LANES:(jj + 1) * LANES]

    @pl.when(g == last)
    def _():
        wait_block((last - 2) % 3)
        for mi in range(blk):
            scatter_row(last % 3, slot_ref[last * blk + mi], (batch - 1) * bstride, mi)
        wait_block((last - 1) % 3)
        wait_block(last % 3)


def _experts(block_expert, row_slot, x2s, wg, wu, wd, layer, batch, seq):
    blk = MOE_BLOCK
    nb = (row_slot.shape[0] // blk - 1) // batch
    steps = batch * nb
    assert steps >= 3
    idx = jnp.arange(steps, dtype=I32)
    change = jnp.concatenate([jnp.zeros((1,), I32), (block_expert[1:] != block_expert[:-1]).astype(I32)])
    run = jnp.cumsum(change).astype(I32)
    later_change = (idx[None, :] > idx[:, None]) & (change[None, :] > 0)
    nxt_idx = jnp.min(jnp.where(later_change, idx[None, :], steps), axis=1)
    nxt = jnp.where(nxt_idx < steps, block_expert[jnp.minimum(nxt_idx, steps - 1)], -1).astype(I32)
    hbm = pl.BlockSpec(memory_space=pl.ANY)
    grid_spec = pltpu.PrefetchScalarGridSpec(
        num_scalar_prefetch=4,
        grid=(batch, nb),
        in_specs=[hbm, hbm, hbm, hbm],
        out_specs=hbm,
        scratch_shapes=[
            pltpu.VMEM((seq * SLABS, LANES), F32),
            pltpu.VMEM((2, D_MODEL, EXPERT_HIDDEN), F32), pltpu.VMEM((2, D_MODEL, EXPERT_HIDDEN), F32),
            pltpu.VMEM((2, EXPERT_HIDDEN, D_MODEL), F32),
            pltpu.VMEM((D_MODEL, EXPERT_HIDDEN), BF16), pltpu.VMEM((D_MODEL, EXPERT_HIDDEN), BF16),
            pltpu.VMEM((EXPERT_HIDDEN, D_MODEL), BF16),
            pltpu.VMEM((2, SLABS * GATHER_PITCH, LANES), F32),
            pltpu.VMEM((3, blk * SLABS, LANES), F32),
            pltpu.SemaphoreType.DMA((3,)),
            pltpu.SemaphoreType.DMA((2, 3)),
        ],
    )
    return pl.pallas_call(
        functools.partial(_expert_kernel, batch=batch, nb=nb, layer=layer),
        grid_spec=grid_spec,
        out_shape=jax.ShapeDtypeStruct((batch * (2 * seq + DUMMY_SLOTS) * SLABS, LANES), F32),
        compiler_params=_cparams(("arbitrary", "arbitrary"), VMEM_LIMIT_EXPERTS),
        name="moe_experts",
    )(block_expert, run, nxt, row_slot, x2s, wg, wu, wd)


def _combine_kernel(h_ref, gates_ref, y0_ref, y1_ref, *rest, final):
    if final:
        fg_ref, out_ref = rest
    else:
        (out_ref,) = rest
    tm = h_ref.shape[0]
    gates = gates_ref[...]
    g0, g1 = gates[:, 0:1], gates[:, 1:2]
    parts = []
    for j in range(SLABS):
        y0 = y0_ref[pl.ds(j, tm, stride=SLABS), :]
        y1 = y1_ref[pl.ds(j, tm, stride=SLABS), :]
        parts.append(g0 * y0 + g1 * y1)
    out = h_ref[...] + jnp.concatenate(parts, axis=1)
    if final:
        out = _rms(out, fg_ref[...])
    out_ref[...] = out


def _combine(h, gates, ys, batch, seq, final_gain=None):
    t = h.shape[0]
    tm = ROW_TILE
    tpb = seq // tm
    bstride = 2 * tpb + DUMMY_SLOTS // tm
    final = final_gain is not None
    in_specs = [pl.BlockSpec((tm, D_MODEL), lambda i: (i, 0)),
                pl.BlockSpec((tm, LANES), lambda i: (i, 0)),
                pl.BlockSpec((tm * SLABS, LANES), lambda i: (i // tpb * bstride + i % tpb, 0)),
                pl.BlockSpec((tm * SLABS, LANES), lambda i: (i // tpb * bstride + tpb + i % tpb, 0))]
    args = [h, gates, ys, ys]
    if final:
        in_specs.append(pl.BlockSpec((1, D_MODEL), lambda i: (0, 0)))
        args.append(final_gain.reshape(1, D_MODEL))
    return pl.pallas_call(
        functools.partial(_combine_kernel, final=final),
        grid=(t // tm,),
        in_specs=in_specs,
        out_specs=pl.BlockSpec((tm, D_MODEL), lambda i: (i, 0)),
        out_shape=jax.ShapeDtypeStruct((t, D_MODEL), F32),
        compiler_params=_cparams(("parallel",)),
        name="moe_combine_final" if final else "moe_combine",
    )(*args)


def _moe(h, gain, w_r1, b_r1, w_r2, b_r2, wg, wu, wd, layer, batch, final_gain=None):
    t = h.shape[0]
    seq = t // batch
    assert seq <= (1 << RANK_BITS) and seq % ROW_TILE == 0 and DUMMY_SLOTS % ROW_TILE == 0
    pad_w = jnp.zeros((D_MODEL, LANES - MOE_GROUPS - MOE_EXPERTS), F32)
    wr = jnp.concatenate([w_r1, w_r2, pad_w], axis=1)
    br = jnp.concatenate([b_r1, b_r2, jnp.zeros((LANES - MOE_GROUPS - MOE_EXPERTS,), F32)]).reshape(1, LANES)
    x2s, codes, gates, counts = _router(h, gain, wr, br, batch)
    cnt = counts.reshape(batch, 8, LANES)[:, 0, ROUTER_LANE0:ROUTER_LANE0 + MOE_EXPERTS].astype(I32)
    padded = (cnt + MOE_BLOCK - 1) // MOE_BLOCK * MOE_BLOCK
    pad_end = jnp.cumsum(padded, axis=1)
    rows_pb = seq * 2 + MOE_EXPERTS * MOE_BLOCK
    nb = rows_pb // MOE_BLOCK
    pad_start = pad_end - padded + (jnp.arange(batch, dtype=I32) * rows_pb)[:, None]
    block_start = jnp.arange(nb, dtype=I32) * MOE_BLOCK
    block_expert = jnp.minimum(
        jnp.sum((pad_end[:, None, :] <= block_start[None, :, None]).astype(I32), axis=2), MOE_EXPERTS - 1)
    dest = _dest_rows(pad_start.reshape(-1), codes)
    dest_flat = dest[0:2].reshape(2, batch, seq).transpose(1, 0, 2).reshape(-1)
    batch_end = ((jnp.arange(batch, dtype=I32) + 1) * rows_pb)[:, None]
    fill_lo = jnp.concatenate([pad_start + cnt, pad_start[:, -1:] + padded[:, -1:]], axis=1)
    fill_hi = jnp.concatenate([pad_start + padded, batch_end], axis=1)
    row_slot = _invert(dest_flat, fill_lo.reshape(-1), fill_hi.reshape(-1), batch * rows_pb, batch)
    ys = _experts(block_expert.reshape(-1), row_slot, x2s, wg, wu, wd, layer, batch, seq)
    return _combine(h, gates, ys, batch, seq, final_gain)


def kernel(x, mix_norm, ffn_norm, final_norm, even_w_in, even_w_out, conv_w, conv_b, conv_norm_g, conv_norm_b,
           odd_w_in, odd_w_out, router_w1, router_b1, router_w2, router_b2, expert_w_gate, expert_w_up,
           expert_w_down):
    batch, seq, d = x.shape
    assert d == D_MODEL and seq % (ATTN_BLOCK * max(dl for _, dl in A_BRANCHES)) == 0
    assert all(w // dl == ATTN_BLOCK for w, dl in A_BRANCHES)
    t = batch * seq
    h = x.reshape(t, d)

    qkv, conv_in = _even_in(h, mix_norm[0], even_w_in, _attn_rope_tables(seq), seq)
    attn = _attention(qkv, batch, seq)
    bconv = _conv(conv_in, conv_w[0], conv_b[0], conv_norm_g[0], conv_norm_b[0], batch, seq)
    h = _even_out(h, attn, bconv, even_w_out)
    h = _moe(h, ffn_norm[0], router_w1[0], router_b1[0], router_w2[0], router_b2[0],
             expert_w_gate, expert_w_up, expert_w_down, 0, batch)

    inv_freq = RET_ROT_THETA ** (-jnp.linspace(0.0, 1.0, C_QK_DIM // 2, dtype=F32))
    ang = jnp.arange(seq, dtype=F32)[:, None] * inv_freq[None, :]
    proj = _odd_in(h, mix_norm[1], odd_w_in, jnp.cos(ang), jnp.sin(ang), seq)
    log_decay = jnp.log(1.0 - jnp.exp2(-5.0 - jnp.arange(C_HEADS, dtype=F32)))
    y = _retention(proj, log_decay, batch, seq)
    h = _odd_out(h, y, odd_w_out)
    out = _moe(h, ffn_norm[1], router_w1[1], router_b1[1], router_w2[1], router_b2[1],
               expert_w_gate, expert_w_up, expert_w_down, 1, batch, final_gain=final_norm)
    return out.reshape(batch, seq, d)
```

```python
import functools

import jax
import jax.numpy as jnp
from jax import lax
from jax.experimental import pallas as pl
from jax.experimental.pallas import tpu as pltpu

F32 = jnp.float32
BF16 = jnp.bfloat16
I32 = jnp.int32

NORM_EPS = 1e-6
NEG_INF = -1e30

D_MODEL = 1024
A_HEADS = 8
A_HEAD_DIM = 64
A_WIDTH = A_HEADS * A_HEAD_DIM
A_BRANCHES = ((128, 1), (512, 4), (2048, 16))
ATTN_BLOCK = 128
ROPE_THETA = 500000.0
ROPE_DIM = A_HEAD_DIM // 4
B_WIDTH = D_MODEL - A_WIDTH
CONV_WIDTH = 31
C_HEADS = 4
C_QK_DIM = 256
C_V_DIM = 512
C_QK_WIDTH = C_HEADS * C_QK_DIM
C_V_WIDTH = C_HEADS * C_V_DIM
RET_CHUNK = 128
RET_ROT_THETA = 10000.0
MOE_GROUPS = 4
MOE_EXPERTS_PER_GROUP = 8
MOE_EXPERTS = MOE_GROUPS * MOE_EXPERTS_PER_GROUP
EXPERT_HIDDEN = 512
MOE_BLOCK = 128
EVEN_IN_WIDTH = 3 * A_WIDTH + 2 * B_WIDTH
ODD_IN_WIDTH = 2 * C_QK_WIDTH + 2 * C_V_WIDTH

LANES = 128
ROW_TILE = 512
CONV_TILE = 512
CONV_HALO = 32
CONV_CHUNK = 64
ROUTER_LANE0 = MOE_GROUPS
VMEM_LIMIT = 56 * 1024 * 1024
VMEM_LIMIT_EXPERTS = 60 * 1024 * 1024
SLABS = D_MODEL // LANES
RANK_BITS = 16
GATHER_PITCH = MOE_BLOCK + 8
DUMMY_SLOTS = ROW_TILE


def _cparams(sem, vmem=VMEM_LIMIT):
    return pltpu.CompilerParams(dimension_semantics=sem, vmem_limit_bytes=vmem)


def _cast_once(w_ref, wb_ref, first):
    @pl.when(first)
    def _():
        wb_ref[...] = w_ref[...].astype(BF16)


def _rms(x, gain):
    ms = jnp.mean(x * x, axis=-1, keepdims=True)
    return x * lax.rsqrt(ms + NORM_EPS) * gain


def _even_in_kernel(h_ref, g_ref, w_ref, c_ref, s1_ref, s2_ref, qkv_ref, conv_ref, wb_ref):
    _cast_once(w_ref, wb_ref, pl.program_id(0) == 0)
    u = _rms(h_ref[...], g_ref[...]).astype(BF16)
    acc = jnp.dot(u, wb_ref[...], preferred_element_type=F32)
    c, s1, s2 = c_ref[...], s1_ref[...], s2_ref[...]
    for j in range(2 * A_WIDTH // LANES):
        xg = acc[:, j * LANES:(j + 1) * LANES]
        if j < A_WIDTH // LANES:
            xg = xg * (A_HEAD_DIM ** -0.5)
        qkv_ref[j] = xg * c + pltpu.roll(xg, LANES - ROPE_DIM // 2, 1) * s1 + pltpu.roll(xg, ROPE_DIM // 2, 1) * s2
    for j in range(2 * A_WIDTH // LANES, 3 * A_WIDTH // LANES):
        qkv_ref[j] = acc[:, j * LANES:(j + 1) * LANES]
    conv_ref[...] = acc[:, 3 * A_WIDTH:].astype(BF16)


def _even_in(h, gain, w, tabs, seq):
    t = h.shape[0]
    tm = ROW_TILE
    nseq = seq // tm
    tab_spec = pl.BlockSpec((tm, LANES), lambda i: (i % nseq, 0))
    return pl.pallas_call(
        _even_in_kernel,
        grid=(t // tm,),
        in_specs=[
            pl.BlockSpec((tm, D_MODEL), lambda i: (i, 0)),
            pl.BlockSpec((1, D_MODEL), lambda i: (0, 0)),
            pl.BlockSpec((None, D_MODEL, EVEN_IN_WIDTH), lambda i: (0, 0, 0)),
            tab_spec, tab_spec, tab_spec,
        ],
        out_specs=[
            pl.BlockSpec((3 * A_WIDTH // LANES, tm, LANES), lambda i: (0, i, 0)),
            pl.BlockSpec((tm, 2 * B_WIDTH), lambda i: (i, 0)),
        ],
        out_shape=[
            jax.ShapeDtypeStruct((3 * A_WIDTH // LANES, t, LANES), F32),
            jax.ShapeDtypeStruct((t, 2 * B_WIDTH), BF16),
        ],
        scratch_shapes=[pltpu.VMEM((D_MODEL, EVEN_IN_WIDTH), BF16)],
        compiler_params=_cparams(("arbitrary",)),
        name="even_in_proj",
    )(h, gain.reshape(1, D_MODEL), w, *tabs)


def _attn_rope_tables(seq):
    half = ROPE_DIM // 2
    inv_freq = ROPE_THETA ** (-jnp.arange(0, ROPE_DIM, 2, dtype=F32) / ROPE_DIM)
    ang = jnp.arange(seq, dtype=F32)[:, None] * inv_freq[None, :]
    cos, sin = jnp.cos(ang), jnp.sin(ang)
    rest = A_HEAD_DIM - ROPE_DIM
    ones = jnp.ones((seq, rest), F32)
    z_rest = jnp.zeros((seq, rest), F32)
    z_half = jnp.zeros((seq, half), F32)
    c = jnp.concatenate([cos, cos, ones], axis=1)
    s1 = jnp.concatenate([-sin, z_half, z_rest], axis=1)
    s2 = jnp.concatenate([z_half, sin, z_rest], axis=1)
    rep = LANES // A_HEAD_DIM
    return tuple(jnp.tile(a, (1, rep)) for a in (c, s1, s2))


ATTN_STEP = ATTN_BLOCK * max(d for _, d in A_BRANCHES)
ATTN_PAIRS = A_WIDTH // LANES
ATTN_MERGE_ROWS = 256


def _attn_kernel(q_ref, k_ref, v_ref, o_ref, kprev, vprev, oml):
    n = pl.program_id(1)
    blk = ATTN_BLOCK
    n_blocks = ATTN_STEP // blk

    @pl.when(n == 0)
    def _():
        kprev[...] = jnp.zeros(kprev.shape, BF16)
        vprev[...] = jnp.zeros(vprev.shape, BF16)

    qi = lax.broadcasted_iota(I32, (blk, 2 * blk), 0)
    kj = lax.broadcasted_iota(I32, (blk, 2 * blk), 1)
    dist = blk + qi - kj
    band = (dist >= 0) & (dist <= blk)
    in_cur = kj >= blk
    lane = lax.broadcasted_iota(I32, (blk, LANES), 1)
    lo = lane < A_HEAD_DIM

    off = 0
    for bi, (_, dil) in enumerate(A_BRANCHES):
        shift = dil.bit_length() - 1
        assert dil == 1 << shift

        def block_body(idx, c, bi=bi, dil=dil, shift=shift, off=off):
            u = lax.shift_right_logical(idx, shift)
            r = idx & (dil - 1)
            start = u * (blk * dil) + r
            rows = pl.ds(pl.multiple_of(start, blk), blk) if dil == 1 else pl.ds(start, blk, stride=dil)
            has_prev = (n > 0) | (u > 0)
            valid = band & (in_cur | has_prev)
            valid2 = jnp.concatenate([valid, valid], axis=0)
            for g in range(ATTN_PAIRS):
                q = q_ref[g, rows, :]
                kc = k_ref[g, rows, :].astype(BF16)
                vc = v_ref[g, rows, :].astype(BF16)
                q2 = jnp.concatenate([jnp.where(lo, q, 0.0), jnp.where(lo, 0.0, q)], axis=0).astype(BF16)
                kk = jnp.concatenate([kprev[g, off + r], kc], axis=0)
                vv = jnp.concatenate([vprev[g, off + r], vc], axis=0)
                s = lax.dot_general(q2, kk, (((1,), (1,)), ((), ())), preferred_element_type=F32)
                s = jnp.where(valid2, s, NEG_INF)
                m = jnp.max(s, axis=-1, keepdims=True)
                e = jnp.exp(s - m)
                den = jnp.sum(e, axis=-1, keepdims=True)
                pv = jnp.dot(e.astype(BF16), vv, preferred_element_type=F32)
                o_new = jnp.where(lo, pv[:blk], pv[blk:])
                m_new = jnp.where(lo, m[:blk], m[blk:])
                l_new = jnp.where(lo, den[:blk], den[blk:])
                if bi > 0:
                    o_run, m_run, l_run = oml[0, g, rows, :], oml[1, g, rows, :], oml[2, g, rows, :]
                    m_both = jnp.maximum(m_run, m_new)
                    w_run, w_new = jnp.exp(m_run - m_both), jnp.exp(m_new - m_both)
                    o_new = w_run * o_run + w_new * o_new
                    l_new = w_run * l_run + w_new * l_new
                    m_new = m_both
                oml[0, g, rows, :] = o_new
                oml[1, g, rows, :] = m_new
                oml[2, g, rows, :] = l_new
                kprev[g, off + r] = kc
                vprev[g, off + r] = vc
            return c

        lax.fori_loop(0, n_blocks, block_body, 0)
        off += dil
    for g in range(ATTN_PAIRS):
        for c in range(ATTN_STEP // ATTN_MERGE_ROWS):
            sl = slice(c * ATTN_MERGE_ROWS, (c + 1) * ATTN_MERGE_ROWS)
            o_ref[g, sl, :] = (oml[0, g, sl, :] / oml[2, g, sl, :]).astype(BF16)


def _attention(qkv, batch, seq):
    t = qkv.shape[1]
    steps = seq // ATTN_STEP
    n_res = sum(d for _, d in A_BRANCHES)

    def slabs(which):
        return pl.BlockSpec((ATTN_PAIRS, ATTN_STEP, LANES), lambda b, n: (which, b * steps + n, 0))

    return pl.pallas_call(
        _attn_kernel,
        grid=(batch, steps),
        in_specs=[slabs(0), slabs(1), slabs(2)],
        out_specs=pl.BlockSpec((ATTN_PAIRS, ATTN_STEP, LANES), lambda b, n: (0, b * steps + n, 0)),
        out_shape=jax.ShapeDtypeStruct((ATTN_PAIRS, t, LANES), BF16),
        scratch_shapes=[
            pltpu.VMEM((ATTN_PAIRS, n_res, ATTN_BLOCK, LANES), BF16),
            pltpu.VMEM((ATTN_PAIRS, n_res, ATTN_BLOCK, LANES), BF16),
            pltpu.VMEM((3, ATTN_PAIRS, ATTN_STEP, LANES), F32),
        ],
        compiler_params=_cparams(("parallel", "arbitrary")),
        name="dilated_attention",
    )(qkv, qkv, qkv)


def _conv_kernel(val_ref, gate_ref, w_ref, b_ref, g_ref, beta_ref, o_ref, abuf, shifted):
    n = pl.program_id(1)
    tc = CONV_TILE
    sub = 8

    @pl.when(n == 0)
    def _():
        abuf[0:CONV_HALO, :] = jnp.zeros((CONV_HALO, B_WIDTH), F32)

    @pl.when(n > 0)
    def _():
        abuf[0:CONV_HALO, :] = abuf[tc:tc + CONV_HALO, :]

    val = val_ref[...].astype(F32)
    gate = gate_ref[...].astype(F32)
    abuf[CONV_HALO:CONV_HALO + tc, :] = val * jax.nn.sigmoid(gate)
    off = CONV_HALO - (CONV_WIDTH - 1)
    span = shifted.shape[1]
    for s in range(1, sub):
        shifted[s - 1] = abuf[s:s + span, :]
    for c in range(tc // CONV_CHUNK):
        acc = jnp.broadcast_to(b_ref[...], (CONV_CHUNK, B_WIDTH))
        for j in range(CONV_WIDTH):
            s = (off + j) % sub
            r0 = c * CONV_CHUNK + off + j - s
            src = abuf[r0:r0 + CONV_CHUNK, :] if s == 0 else shifted[s - 1, r0:r0 + CONV_CHUNK, :]
            acc = acc + w_ref[j:j + 1, :] * src
        mu = jnp.mean(acc, axis=-1, keepdims=True)
        cen = acc - mu
        var = jnp.mean(cen * cen, axis=-1, keepdims=True)
        yn = cen * lax.rsqrt(var + NORM_EPS) * g_ref[...] + beta_ref[...]
        o_ref[c * CONV_CHUNK:(c + 1) * CONV_CHUNK, :] = (yn * jax.nn.sigmoid(yn)).astype(BF16)


def _conv(conv_in, w, b, g, beta, batch, seq):
    t = conv_in.shape[0]
    tc = CONV_TILE
    nt = seq // tc
    w_pad = jnp.concatenate([w, jnp.zeros((CONV_HALO - CONV_WIDTH, B_WIDTH), F32)], axis=0)
    vec = pl.BlockSpec((1, B_WIDTH), lambda bb, n: (0, 0))
    return pl.pallas_call(
        _conv_kernel,
        grid=(batch, nt),
        in_specs=[
            pl.BlockSpec((tc, B_WIDTH), lambda bb, n: (bb * nt + n, 0)),
            pl.BlockSpec((tc, B_WIDTH), lambda bb, n: (bb * nt + n, 1)),
            pl.BlockSpec((CONV_HALO, B_WIDTH), lambda bb, n: (0, 0)),
            vec, vec, vec,
        ],
        out_specs=pl.BlockSpec((tc, B_WIDTH), lambda bb, n: (bb * nt + n, 0)),
        out_shape=jax.ShapeDtypeStruct((t, B_WIDTH), BF16),
        scratch_shapes=[pltpu.VMEM((tc + CONV_HALO, B_WIDTH), F32),
                        pltpu.VMEM((7, tc + CONV_HALO - 8, B_WIDTH), F32)],
        compiler_params=_cparams(("parallel", "arbitrary")),
        name="conformer_conv",
    )(conv_in, conv_in, w_pad, b.reshape(1, -1), g.reshape(1, -1), beta.reshape(1, -1))


def _even_out_kernel(h_ref, a_ref, bc_ref, w_ref, out_ref, wb_ref):
    _cast_once(w_ref, wb_ref, pl.program_id(0) == 0)
    a = jnp.concatenate([a_ref[g] for g in range(ATTN_PAIRS)], axis=1)
    acc = jnp.dot(a, wb_ref[0:A_WIDTH, :], preferred_element_type=F32)
    acc = acc + jnp.dot(bc_ref[...], wb_ref[A_WIDTH:, :], preferred_element_type=F32)
    out_ref[...] = h_ref[...] + acc


def _even_out(h, attn, bconv, w):
    t = h.shape[0]
    tm = ROW_TILE
    row = lambda width: pl.BlockSpec((tm, width), lambda i: (i, 0))
    return pl.pallas_call(
        _even_out_kernel,
        grid=(t // tm,),
        in_specs=[row(D_MODEL), pl.BlockSpec((ATTN_PAIRS, tm, LANES), lambda i: (0, i, 0)), row(B_WIDTH),
                  pl.BlockSpec((None, D_MODEL, D_MODEL), lambda i: (0, 0, 0))],
        out_specs=row(D_MODEL),
        out_shape=jax.ShapeDtypeStruct((t, D_MODEL), F32),
        scratch_shapes=[pltpu.VMEM((D_MODEL, D_MODEL), BF16)],
        compiler_params=_cparams(("arbitrary",)),
        name="even_out_proj",
    )(h, attn, bconv, w)


def _odd_in_kernel(h_ref, g_ref, w_ref, cos_ref, sin_ref, o_ref, wb_ref):
    j = pl.program_id(0)
    _cast_once(w_ref, wb_ref, pl.program_id(1) == 0)
    u = _rms(h_ref[...], g_ref[...]).astype(BF16)
    acc = jnp.dot(u, wb_ref[...], preferred_element_type=F32)

    @pl.when(j == 0)
    def _():
        cos, sin = cos_ref[...], sin_ref[...]
        half = C_QK_DIM // 2
        for hd in range(2 * C_HEADS):
            x1 = acc[:, hd * C_QK_DIM:hd * C_QK_DIM + half]
            x2 = acc[:, hd * C_QK_DIM + half:(hd + 1) * C_QK_DIM]
            r1 = x1 * cos - x2 * sin
            r2 = x2 * cos + x1 * sin
            if hd >= C_HEADS:
                r1 = r1 * (C_QK_DIM ** -0.5)
                r2 = r2 * (C_QK_DIM ** -0.5)
            o_ref[:, hd * C_QK_DIM:hd * C_QK_DIM + half] = r1.astype(BF16)
            o_ref[:, hd * C_QK_DIM + half:(hd + 1) * C_QK_DIM] = r2.astype(BF16)

    @pl.when(j > 0)
    def _():
        o_ref[...] = acc.astype(BF16)


def _odd_in(h, gain, w, cos, sin, seq):
    t = h.shape[0]
    tm = ROW_TILE
    tn = 2 * C_QK_WIDTH
    nseq = seq // tm
    tab = pl.BlockSpec((tm, C_QK_DIM // 2), lambda j, i: (i % nseq, 0))
    return pl.pallas_call(
        _odd_in_kernel,
        grid=(ODD_IN_WIDTH // tn, t // tm),
        in_specs=[
            pl.BlockSpec((tm, D_MODEL), lambda j, i: (i, 0)),
            pl.BlockSpec((1, D_MODEL), lambda j, i: (0, 0)),
            pl.BlockSpec((None, D_MODEL, tn), lambda j, i: (0, 0, j)),
            tab, tab,
        ],
        out_specs=pl.BlockSpec((tm, tn), lambda j, i: (i, j)),
        out_shape=jax.ShapeDtypeStruct((t, ODD_IN_WIDTH), BF16),
        scratch_shapes=[pltpu.VMEM((D_MODEL, tn), BF16)],
        compiler_params=_cparams(("arbitrary", "arbitrary")),
        name="odd_in_proj",
    )(h, gain.reshape(1, D_MODEL), w, cos, sin)


def _ret_kernel(ld_ref, q_ref, k_ref, v_ref, g_ref, o_ref, state):
    c = pl.program_id(1)
    ch = RET_CHUNK

    @pl.when(c == 0)
    def _():
        state[...] = jnp.zeros(state.shape, F32)

    ii = lax.broadcasted_iota(I32, (ch, ch), 0)
    jj = lax.broadcasted_iota(I32, (ch, ch), 1)
    diff = (ii - jj).astype(F32)
    pos = lax.broadcasted_iota(I32, (ch, 1), 0).astype(F32)
    for hd in range(C_HEADS):
        ld = ld_ref[hd]
        q = q_ref[:, hd * C_QK_DIM:(hd + 1) * C_QK_DIM]
        k = k_ref[:, hd * C_QK_DIM:(hd + 1) * C_QK_DIM]
        v = v_ref[:, hd * C_V_DIM:(hd + 1) * C_V_DIM]
        intra = jnp.where(diff >= 0, jnp.exp(ld * jnp.maximum(diff, 0.0)), 0.0)
        s = lax.dot_general(q, k, (((1,), (1,)), ((), ())), preferred_element_type=F32) * intra
        inner = jnp.dot(s.astype(BF16), v, preferred_element_type=F32)
        q_decay = jnp.exp(ld * (pos + 1.0))
        k_decay = jnp.exp(ld * (ch - 1.0 - pos))
        chunk_decay = jnp.exp(ld * jnp.full((1, 1), float(ch), F32))
        st = state[hd]
        cross = jnp.dot(q, st.astype(BF16), preferred_element_type=F32) * q_decay
        kd_t = jnp.transpose(k.astype(F32) * k_decay).astype(BF16)
        state[hd] = st * chunk_decay + jnp.dot(kd_t, v, preferred_element_type=F32)
        out = inner + cross
        mu = jnp.mean(out, axis=-1, keepdims=True)
        cen = out - mu
        var = jnp.mean(cen * cen, axis=-1, keepdims=True)
        o = cen * lax.rsqrt(var + NORM_EPS)
        gf = g_ref[:, hd * C_V_DIM:(hd + 1) * C_V_DIM].astype(F32)
        o_ref[:, hd * C_V_DIM:(hd + 1) * C_V_DIM] = (gf * jax.nn.sigmoid(gf) * o).astype(BF16)


def _retention(proj, log_decay, batch, seq):
    t = proj.shape[0]
    ch = RET_CHUNK
    nc = seq // ch
    v0 = 2 * C_QK_WIDTH // C_V_WIDTH
    grid_spec = pltpu.PrefetchScalarGridSpec(
        num_scalar_prefetch=1,
        grid=(batch, nc),
        in_specs=[
            pl.BlockSpec((ch, C_QK_WIDTH), lambda b, c, ld: (b * nc + c, 0)),
            pl.BlockSpec((ch, C_QK_WIDTH), lambda b, c, ld: (b * nc + c, 1)),
            pl.BlockSpec((ch, C_V_WIDTH), lambda b, c, ld: (b * nc + c, v0)),
            pl.BlockSpec((ch, C_V_WIDTH), lambda b, c, ld: (b * nc + c, v0 + 1)),
        ],
        out_specs=pl.BlockSpec((ch, C_V_WIDTH), lambda b, c, ld: (b * nc + c, 0)),
        scratch_shapes=[pltpu.VMEM((C_HEADS, C_QK_DIM, C_V_DIM), F32)],
    )
    return pl.pallas_call(
        _ret_kernel,
        grid_spec=grid_spec,
        out_shape=jax.ShapeDtypeStruct((t, C_V_WIDTH), BF16),
        compiler_params=_cparams(("parallel", "arbitrary")),
        name="retention",
    )(log_decay, proj, proj, proj, proj)


def _odd_out_kernel(h_ref, y_ref, w_ref, out_ref, wb_ref):
    _cast_once(w_ref, wb_ref, pl.program_id(0) == 0)
    out_ref[...] = h_ref[...] + jnp.dot(y_ref[...], wb_ref[...], preferred_element_type=F32)


def _odd_out(h, y, w):
    t = h.shape[0]
    tm = ROW_TILE
    return pl.pallas_call(
        _odd_out_kernel,
        grid=(t // tm,),
        in_specs=[pl.BlockSpec((tm, D_MODEL), lambda i: (i, 0)),
                  pl.BlockSpec((tm, C_V_WIDTH), lambda i: (i, 0)),
                  pl.BlockSpec((None, C_V_WIDTH, D_MODEL), lambda i: (0, 0, 0))],
        out_specs=pl.BlockSpec((tm, D_MODEL), lambda i: (i, 0)),
        out_shape=jax.ShapeDtypeStruct((t, D_MODEL), F32),
        scratch_shapes=[pltpu.VMEM((C_V_WIDTH, D_MODEL), BF16)],
        compiler_params=_cparams(("arbitrary",)),
        name="odd_out_proj",
    )(h, y, w)


def _router_kernel(h_ref, g_ref, wr_ref, br_ref, x2s_ref, codes_ref, gates_ref, cnt_ref, base_ref, *,
                   tiles_per_batch):
    i = pl.program_id(0)
    tm = h_ref.shape[0]
    batch = i // tiles_per_batch

    @pl.when(i % tiles_per_batch == 0)
    def _():
        base_ref[...] = jnp.zeros(base_ref.shape, F32)

    x2 = _rms(h_ref[...], g_ref[...])
    for j in range(SLABS):
        x2s_ref[pl.ds(j, tm, stride=SLABS), :] = x2[:, j * LANES:(j + 1) * LANES]
    wr = wr_ref[...]
    x_hi, w_hi = x2.astype(BF16), wr.astype(BF16)
    x_lo = (x2 - x_hi.astype(F32)).astype(BF16)
    w_lo = (wr - w_hi.astype(F32)).astype(BF16)
    logits = (jnp.dot(x_hi, w_hi, preferred_element_type=F32) + jnp.dot(x_hi, w_lo, preferred_element_type=F32)
              + jnp.dot(x_lo, w_hi, preferred_element_type=F32) + br_ref[...])
    lane = lax.broadcasted_iota(I32, (tm, LANES), 1)
    is_grp = lane < MOE_GROUPS
    lg = jnp.where(is_grp, logits, -jnp.inf)
    gmax = jnp.max(lg, axis=-1, keepdims=True)
    gsum = jnp.sum(jnp.where(is_grp, jnp.exp(logits - gmax), 0.0), axis=-1, keepdims=True)
    gp = 1.0 / gsum
    gi = jnp.min(jnp.where(lg == gmax, lane, LANES), axis=-1, keepdims=True)
    ex = lane - ROUTER_LANE0
    in_grp = (ex >= 0) & (ex < MOE_EXPERTS) & ((ex // MOE_EXPERTS_PER_GROUP) == gi)
    sel = jnp.where(in_grp, logits, -jnp.inf)
    v1 = jnp.max(sel, axis=-1, keepdims=True)
    i1 = jnp.min(jnp.where(sel == v1, lane, LANES), axis=-1, keepdims=True)
    sel2 = jnp.where(lane == i1, -jnp.inf, sel)
    v2 = jnp.max(sel2, axis=-1, keepdims=True)
    i2 = jnp.min(jnp.where(sel2 == v2, lane, LANES), axis=-1, keepdims=True)
    tt = jnp.exp(v2 - v1)
    g0 = gp / (1.0 + tt)
    g1 = gp * tt / (1.0 + tt)
    oh0 = (lane == i1).astype(F32)
    oh1 = (lane == i2).astype(F32)
    oh = oh0 + oh1
    rr = lax.broadcasted_iota(I32, (tm, tm), 0)
    cc = lax.broadcasted_iota(I32, (tm, tm), 1)
    tri = (cc < rr).astype(BF16)
    prefix = jnp.dot(tri, oh.astype(BF16), preferred_element_type=F32)
    tot = base_ref[...] + prefix
    rank0 = jnp.sum(oh0 * tot, axis=-1, keepdims=True).astype(I32)
    rank1 = jnp.sum(oh1 * tot, axis=-1, keepdims=True).astype(I32)
    base_ref[...] = base_ref[...] + jnp.sum(oh, axis=0, keepdims=True)
    eoff = batch * MOE_EXPERTS - ROUTER_LANE0
    code0 = (i1 + eoff) * (1 << RANK_BITS) + rank0
    code1 = (i2 + eoff) * (1 << RANK_BITS) + rank1
    meta = jnp.where(lane == 0, code0, jnp.where(lane == 1, code1, 0))
    codes_ref[...] = jnp.transpose(meta)[0:8, :]
    gates_ref[...] = jnp.where(lane == 0, g0, jnp.where(lane == 1, g1, 0.0))
    cnt_ref[...] = jnp.broadcast_to(base_ref[...], cnt_ref.shape)


def _router(h, gain, wr, br, batch):
    t = h.shape[0]
    tm = ROW_TILE
    tpb = t // batch // tm
    row = lambda width: pl.BlockSpec((tm, width), lambda i: (i, 0))
    fixed = lambda shape: pl.BlockSpec(shape, lambda i: (0, 0))
    return pl.pallas_call(
        functools.partial(_router_kernel, tiles_per_batch=tpb),
        grid=(t // tm,),
        in_specs=[row(D_MODEL), fixed((1, D_MODEL)), fixed((D_MODEL, LANES)), fixed((1, LANES))],
        out_specs=[pl.BlockSpec((tm * SLABS, LANES), lambda i: (i, 0)),
                   pl.BlockSpec((8, tm), lambda i: (0, i)), row(LANES),
                   pl.BlockSpec((8, LANES), lambda i: (i // tpb, 0))],
        out_shape=[
            jax.ShapeDtypeStruct((t * SLABS, LANES), F32),
            jax.ShapeDtypeStruct((8, t), I32),
            jax.ShapeDtypeStruct((t, LANES), F32),
            jax.ShapeDtypeStruct((batch * 8, LANES), F32),
        ],
        scratch_shapes=[pltpu.VMEM((1, LANES), F32)],
        compiler_params=_cparams(("arbitrary",)),
        name="moe_router",
    )(h, gain.reshape(1, D_MODEL), wr, br)


INVERT_UNROLL = 16


def _dest_kernel(pstart_ref, codes_ref, dest_ref):
    codes = codes_ref[...]
    seg = lax.shift_right_logical(codes, RANK_BITS)
    dest = codes & ((1 << RANK_BITS) - 1)
    for i in range(pstart_ref.shape[0]):
        dest = dest + jnp.where(seg == i, pstart_ref[i], 0)
    dest_ref[...] = dest


def _dest_rows(pad_start, codes):
    grid_spec = pltpu.PrefetchScalarGridSpec(
        num_scalar_prefetch=1,
        grid=(1,),
        in_specs=[pl.BlockSpec(codes.shape, lambda i, ps: (0, 0))],
        out_specs=pl.BlockSpec(codes.shape, lambda i, ps: (0, 0)),
    )
    return pl.pallas_call(
        _dest_kernel,
        grid_spec=grid_spec,
        out_shape=jax.ShapeDtypeStruct(codes.shape, I32),
        compiler_params=_cparams(("arbitrary",)),
        name="moe_dest_rows",
    )(pad_start, codes)


def _invert_kernel(trips_ref, dest_ref, dummy_ref, slot_ref, *, batch):
    per_batch = dest_ref.shape[0] // batch
    pltpu.sync_copy(dummy_ref, slot_ref)
    for b in range(batch):
        def place(i, carry, b=b):
            for v in range(INVERT_UNROLL):
                a = i * INVERT_UNROLL + v
                slot_ref[dest_ref[b * per_batch + a]] = a
            return carry

        lax.fori_loop(0, trips_ref[0], place, 0)


def _dummy_slots(n_rows, per_batch):
    assert MOE_BLOCK & (MOE_BLOCK - 1) == 0 and 3 * MOE_BLOCK <= DUMMY_SLOTS
    r = jnp.arange(n_rows + MOE_BLOCK, dtype=I32)
    pattern = per_batch + (r // MOE_BLOCK) % 2 * MOE_BLOCK + r % MOE_BLOCK
    return jnp.where(r < n_rows, pattern, per_batch + 2 * MOE_BLOCK + r % MOE_BLOCK)


def _invert(dest_flat, n_rows, batch):
    smem = pl.BlockSpec(memory_space=pltpu.SMEM)
    per_batch = dest_flat.shape[0] // batch
    assert per_batch % INVERT_UNROLL == 0
    trips = jnp.full((1,), per_batch // INVERT_UNROLL, I32)
    return pl.pallas_call(
        functools.partial(_invert_kernel, batch=batch),
        in_specs=[smem, smem, pl.BlockSpec(memory_space=pl.ANY)],
        out_specs=smem,
        out_shape=jax.ShapeDtypeStruct((n_rows + MOE_BLOCK,), I32),
        name="moe_invert_rows",
    )(trips, dest_flat, _dummy_slots(n_rows, per_batch))


def _expert_kernel(be_ref, run_ref, nxt_ref, slot_ref, x2s_ref, wg_hbm, wu_hbm, wd_hbm, ys_ref,
                   xres, wbuf_g, wbuf_u, wbuf_d, wgb_ref, wub_ref, wdb_ref, tile, ybuf, sems, wsems, *,
                   batch, nb, layer):
    b = pl.program_id(0)
    n = pl.program_id(1)
    g = b * nb + n
    last = batch * nb - 1
    blk = MOE_BLOCK
    seq = xres.shape[0] // SLABS
    bstride = 2 * seq + DUMMY_SLOTS
    tcur = g % 2
    ycur = g % 3
    yprev = (g + 2) % 3

    def weight_copies(e, ws):
        pairs = ((wg_hbm, wbuf_g), (wu_hbm, wbuf_u), (wd_hbm, wbuf_d))
        return [pltpu.make_async_copy(src.at[layer, e], dst.at[ws], wsems.at[ws, k])
                for k, (src, dst) in enumerate(pairs)]

    def wait_block(q):
        pltpu.make_async_copy(ybuf.at[q], ys_ref.at[pl.ds(0, blk * SLABS), :], sems.at[q]).wait()

    def gather_row(block, tslot, mi):
        s = slot_ref[block * blk + mi]
        if seq & (seq - 1) == 0:
            tok = s & (seq - 1)
        else:
            tok = jnp.minimum(jnp.where(s >= seq, s - seq, s), seq - 1)
        slab = xres[pl.ds(pl.multiple_of(tok * SLABS, SLABS), SLABS), :]
        tile[tslot, pl.ds(mi, SLABS, stride=GATHER_PITCH), :] = slab

    def scatter_row(yslot, slot, row0, mi):
        dst = pl.multiple_of((row0 + slot) * SLABS, SLABS)
        pltpu.make_async_copy(ybuf.at[yslot, pl.ds(mi * SLABS, SLABS), :], ys_ref.at[pl.ds(dst, SLABS), :],
                              sems.at[yslot]).start()

    @pl.when(g == 0)
    def _():
        for c in weight_copies(be_ref[0], 0):
            c.start()
        ybuf[...] = jnp.zeros(ybuf.shape, F32)
        for bb in range(batch):
            for c in range(DUMMY_SLOTS // blk):
                start = (bb * bstride + 2 * seq + c * blk) * SLABS
                zero = pltpu.make_async_copy(ybuf.at[0], ys_ref.at[pl.ds(start, blk * SLABS), :], sems.at[0])
                zero.start()
                zero.wait()

    @pl.when(n == 0)
    def _():
        pltpu.sync_copy(x2s_ref.at[pl.ds(pl.multiple_of(b * (seq * SLABS), SLABS), seq * SLABS), :], xres)
        for mi in range(blk):
            gather_row(g, tcur, mi)

    run = run_ref[g]
    ws = run % 2

    @pl.when((g == 0) | (run != run_ref[jnp.maximum(g - 1, 0)]))
    def _():
        for c in weight_copies(be_ref[g], ws):
            c.wait()
        wgb_ref[...] = wbuf_g[ws].astype(BF16)
        wub_ref[...] = wbuf_u[ws].astype(BF16)
        wdb_ref[...] = wbuf_d[ws].astype(BF16)

        @pl.when(nxt_ref[g] >= 0)
        def _():
            for c in weight_copies(nxt_ref[g], 1 - ws):
                c.start()

    @pl.when(g >= 2)
    def _():
        wait_block(ycur)

    nxt_block = jnp.minimum(g + 1, last)
    prev_block = jnp.where(g == 0, batch * nb, g - 1)
    prev_row0 = jnp.where(n == 0, jnp.maximum(b - 1, 0), b) * bstride
    x = jnp.concatenate([tile[tcur, j * GATHER_PITCH:j * GATHER_PITCH + blk, :].astype(BF16)
                         for j in range(SLABS)], axis=1)
    n_chunks = 8
    per = blk // n_chunks
    half = EXPERT_HIDDEN // 2
    quarter = D_MODEL // 4
    acts = []
    hid = None
    for c in range(n_chunks):
        for mi in range(c * per, (c + 1) * per):
            gather_row(nxt_block, 1 - tcur, mi)
        if c < 4:
            for mi in range(2 * c * per, 2 * (c + 1) * per):
                scatter_row(yprev, slot_ref[prev_block * blk + mi], prev_row0, mi)
        if c < 4:
            w_ref = wgb_ref if c < 2 else wub_ref
            acts.append(jnp.dot(x, w_ref[:, (c % 2) * half:(c % 2 + 1) * half], preferred_element_type=F32))
        if c == 3:
            a = jnp.concatenate(acts[0:2], axis=1)
            u = jnp.concatenate(acts[2:4], axis=1)
            hid = (a * jax.nn.sigmoid(a) * u).astype(BF16)
        if c >= 4:
            q = c - 4
            yq = jnp.dot(hid, wdb_ref[:, q * quarter:(q + 1) * quarter], preferred_element_type=F32)
            for jj in range(quarter // LANES):
                j = q * (quarter // LANES) + jj
                ybuf[ycur, pl.ds(j, blk, stride=SLABS), :] = yq[:, jj * LANES:(jj + 1) * LANES]

    @pl.when(g == last)
    def _():
        wait_block((last - 2) % 3)
        for mi in range(blk):
            scatter_row(last % 3, slot_ref[last * blk + mi], (batch - 1) * bstride, mi)
        wait_block((last - 1) % 3)
        wait_block(last % 3)


def _experts(block_expert, row_slot, x2s, wg, wu, wd, layer, batch, seq):
    blk = MOE_BLOCK
    nb = (row_slot.shape[0] // blk - 1) // batch
    steps = batch * nb
    assert steps >= 3
    idx = jnp.arange(steps, dtype=I32)
    change = jnp.concatenate([jnp.zeros((1,), I32), (block_expert[1:] != block_expert[:-1]).astype(I32)])
    run = jnp.cumsum(change).astype(I32)
    later_change = (idx[None, :] > idx[:, None]) & (change[None, :] > 0)
    nxt_idx = jnp.min(jnp.where(later_change, idx[None, :], steps), axis=1)
    nxt = jnp.where(nxt_idx < steps, block_expert[jnp.minimum(nxt_idx, steps - 1)], -1).astype(I32)
    hbm = pl.BlockSpec(memory_space=pl.ANY)
    grid_spec = pltpu.PrefetchScalarGridSpec(
        num_scalar_prefetch=4,
        grid=(batch, nb),
        in_specs=[hbm, hbm, hbm, hbm],
        out_specs=hbm,
        scratch_shapes=[
            pltpu.VMEM((seq * SLABS, LANES), F32),
            pltpu.VMEM((2, D_MODEL, EXPERT_HIDDEN), F32), pltpu.VMEM((2, D_MODEL, EXPERT_HIDDEN), F32),
            pltpu.VMEM((2, EXPERT_HIDDEN, D_MODEL), F32),
            pltpu.VMEM((D_MODEL, EXPERT_HIDDEN), BF16), pltpu.VMEM((D_MODEL, EXPERT_HIDDEN), BF16),
            pltpu.VMEM((EXPERT_HIDDEN, D_MODEL), BF16),
            pltpu.VMEM((2, SLABS * GATHER_PITCH, LANES), F32),
            pltpu.VMEM((3, blk * SLABS, LANES), F32),
            pltpu.SemaphoreType.DMA((3,)),
            pltpu.SemaphoreType.DMA((2, 3)),
        ],
    )
    return pl.pallas_call(
        functools.partial(_expert_kernel, batch=batch, nb=nb, layer=layer),
        grid_spec=grid_spec,
        out_shape=jax.ShapeDtypeStruct((batch * (2 * seq + DUMMY_SLOTS) * SLABS, LANES), F32),
        compiler_params=_cparams(("arbitrary", "arbitrary"), VMEM_LIMIT_EXPERTS),
        name="moe_experts",
    )(block_expert, run, nxt, row_slot, x2s, wg, wu, wd)


def _combine_kernel(h_ref, gates_ref, y0_ref, y1_ref, *rest, final):
    if final:
        fg_ref, out_ref = rest
    else:
        (out_ref,) = rest
    tm = h_ref.shape[0]
    gates = gates_ref[...]
    g0, g1 = gates[:, 0:1], gates[:, 1:2]
    parts = []
    for j in range(SLABS):
        y0 = y0_ref[pl.ds(j, tm, stride=SLABS), :]
        y1 = y1_ref[pl.ds(j, tm, stride=SLABS), :]
        parts.append(g0 * y0 + g1 * y1)
    out = h_ref[...] + jnp.concatenate(parts, axis=1)
    if final:
        out = _rms(out, fg_ref[...])
    out_ref[...] = out


def _combine(h, gates, ys, batch, seq, final_gain=None):
    t = h.shape[0]
    tm = ROW_TILE
    tpb = seq // tm
    bstride = 2 * tpb + DUMMY_SLOTS // tm
    final = final_gain is not None
    in_specs = [pl.BlockSpec((tm, D_MODEL), lambda i: (i, 0)),
                pl.BlockSpec((tm, LANES), lambda i: (i, 0)),
                pl.BlockSpec((tm * SLABS, LANES), lambda i: (i // tpb * bstride + i % tpb, 0)),
                pl.BlockSpec((tm * SLABS, LANES), lambda i: (i // tpb * bstride + tpb + i % tpb, 0))]
    args = [h, gates, ys, ys]
    if final:
        in_specs.append(pl.BlockSpec((1, D_MODEL), lambda i: (0, 0)))
        args.append(final_gain.reshape(1, D_MODEL))
    return pl.pallas_call(
        functools.partial(_combine_kernel, final=final),
        grid=(t // tm,),
        in_specs=in_specs,
        out_specs=pl.BlockSpec((tm, D_MODEL), lambda i: (i, 0)),
        out_shape=jax.ShapeDtypeStruct((t, D_MODEL), F32),
        compiler_params=_cparams(("parallel",)),
        name="moe_combine_final" if final else "moe_combine",
    )(*args)


def _moe(h, gain, w_r1, b_r1, w_r2, b_r2, wg, wu, wd, layer, batch, final_gain=None):
    t = h.shape[0]
    seq = t // batch
    assert seq <= (1 << RANK_BITS) and seq % ROW_TILE == 0 and DUMMY_SLOTS % ROW_TILE == 0
    pad_w = jnp.zeros((D_MODEL, LANES - MOE_GROUPS - MOE_EXPERTS), F32)
    wr = jnp.concatenate([w_r1, w_r2, pad_w], axis=1)
    br = jnp.concatenate([b_r1, b_r2, jnp.zeros((LANES - MOE_GROUPS - MOE_EXPERTS,), F32)]).reshape(1, LANES)
    x2s, codes, gates, counts = _router(h, gain, wr, br, batch)
    cnt = counts.reshape(batch, 8, LANES)[:, 0, ROUTER_LANE0:ROUTER_LANE0 + MOE_EXPERTS].astype(I32)
    padded = (cnt + MOE_BLOCK - 1) // MOE_BLOCK * MOE_BLOCK
    pad_end = jnp.cumsum(padded, axis=1)
    rows_pb = seq * 2 + MOE_EXPERTS * MOE_BLOCK
    nb = rows_pb // MOE_BLOCK
    pad_start = pad_end - padded + (jnp.arange(batch, dtype=I32) * rows_pb)[:, None]
    block_start = jnp.arange(nb, dtype=I32) * MOE_BLOCK
    block_expert = jnp.minimum(
        jnp.sum((pad_end[:, None, :] <= block_start[None, :, None]).astype(I32), axis=2), MOE_EXPERTS - 1)
    dest = _dest_rows(pad_start.reshape(-1), codes)
    dest_flat = dest[0:2].reshape(2, batch, seq).transpose(1, 0, 2).reshape(-1)
    row_slot = _invert(dest_flat, batch * rows_pb, batch)
    ys = _experts(block_expert.reshape(-1), row_slot, x2s, wg, wu, wd, layer, batch, seq)
    return _combine(h, gates, ys, batch, seq, final_gain)


def kernel(x, mix_norm, ffn_norm, final_norm, even_w_in, even_w_out, conv_w, conv_b, conv_norm_g, conv_norm_b,
           odd_w_in, odd_w_out, router_w1, router_b1, router_w2, router_b2, expert_w_gate, expert_w_up,
           expert_w_down):
    batch, seq, d = x.shape
    assert d == D_MODEL and seq % (ATTN_BLOCK * max(dl for _, dl in A_BRANCHES)) == 0
    assert all(w // dl == ATTN_BLOCK for w, dl in A_BRANCHES)
    t = batch * seq
    h = x.reshape(t, d)

    qkv, conv_in = _even_in(h, mix_norm[0], even_w_in, _attn_rope_tables(seq), seq)
    attn = _attention(qkv, batch, seq)
    bconv = _conv(conv_in, conv_w[0], conv_b[0], conv_norm_g[0], conv_norm_b[0], batch, seq)
    h = _even_out(h, attn, bconv, even_w_out)
    h = _moe(h, ffn_norm[0], router_w1[0], router_b1[0], router_w2[0], router_b2[0],
             expert_w_gate, expert_w_up, expert_w_down, 0, batch)

    inv_freq = RET_ROT_THETA ** (-jnp.linspace(0.0, 1.0, C_QK_DIM // 2, dtype=F32))
    ang = jnp.arange(seq, dtype=F32)[:, None] * inv_freq[None, :]
    proj = _odd_in(h, mix_norm[1], odd_w_in, jnp.cos(ang), jnp.sin(ang), seq)
    log_decay = jnp.log(1.0 - jnp.exp2(-5.0 - jnp.arange(C_HEADS, dtype=F32)))
    y = _retention(proj, log_decay, batch, seq)
    h = _odd_out(h, y, odd_w_out)
    out = _moe(h, ffn_norm[1], router_w1[1], router_b1[1], router_w2[1], router_b2[1],
               expert_w_gate, expert_w_up, expert_w_down, 1, batch, final_gain=final_norm)
    return out.reshape(batch, seq, d)
```

```python
import functools

import jax
import jax.numpy as jnp
from jax import lax
from jax.experimental import pallas as pl
from jax.experimental.pallas import tpu as pltpu

F32 = jnp.float32
BF16 = jnp.bfloat16
I32 = jnp.int32

NORM_EPS = 1e-6
NEG_INF = -1e30

D_MODEL = 1024
A_HEADS = 8
A_HEAD_DIM = 64
A_WIDTH = A_HEADS * A_HEAD_DIM
A_BRANCHES = ((128, 1), (512, 4), (2048, 16))
ATTN_BLOCK = 128
ROPE_THETA = 500000.0
ROPE_DIM = A_HEAD_DIM // 4
B_WIDTH = D_MODEL - A_WIDTH
CONV_WIDTH = 31
C_HEADS = 4
C_QK_DIM = 256
C_V_DIM = 512
C_QK_WIDTH = C_HEADS * C_QK_DIM
C_V_WIDTH = C_HEADS * C_V_DIM
RET_CHUNK = 128
RET_ROT_THETA = 10000.0
MOE_GROUPS = 4
MOE_EXPERTS_PER_GROUP = 8
MOE_EXPERTS = MOE_GROUPS * MOE_EXPERTS_PER_GROUP
EXPERT_HIDDEN = 512
MOE_BLOCK = 128
EVEN_IN_WIDTH = 3 * A_WIDTH + 2 * B_WIDTH
ODD_IN_WIDTH = 2 * C_QK_WIDTH + 2 * C_V_WIDTH

LANES = 128
ROW_TILE = 512
WIDE_ROW_TILE = 1024
CONV_TILE = 512
CONV_HALO = 32
CONV_CHUNK = 64
ROUTER_LANE0 = MOE_GROUPS
VMEM_LIMIT = 56 * 1024 * 1024
VMEM_LIMIT_EXPERTS = 60 * 1024 * 1024
SLABS = D_MODEL // LANES
RANK_BITS = 16
GATHER_PITCH = MOE_BLOCK + 8
DUMMY_SLOTS = ROW_TILE


def _cparams(sem, vmem=VMEM_LIMIT):
    return pltpu.CompilerParams(dimension_semantics=sem, vmem_limit_bytes=vmem)


def _cast_once(w_ref, wb_ref, first):
    @pl.when(first)
    def _():
        wb_ref[...] = w_ref[...].astype(BF16)


def _rms(x, gain):
    ms = jnp.mean(x * x, axis=-1, keepdims=True)
    return x * lax.rsqrt(ms + NORM_EPS) * gain


def _even_in_kernel(h_ref, g_ref, w_ref, c_ref, s1_ref, s2_ref, qkv_ref, conv_ref, wb_ref):
    _cast_once(w_ref, wb_ref, pl.program_id(0) == 0)
    u = _rms(h_ref[...], g_ref[...]).astype(BF16)
    acc = jnp.dot(u, wb_ref[...], preferred_element_type=F32)
    c, s1, s2 = c_ref[...], s1_ref[...], s2_ref[...]
    for j in range(2 * A_WIDTH // LANES):
        xg = acc[:, j * LANES:(j + 1) * LANES]
        if j < A_WIDTH // LANES:
            xg = xg * (A_HEAD_DIM ** -0.5)
        qkv_ref[j] = xg * c + pltpu.roll(xg, LANES - ROPE_DIM // 2, 1) * s1 + pltpu.roll(xg, ROPE_DIM // 2, 1) * s2
    for j in range(2 * A_WIDTH // LANES, 3 * A_WIDTH // LANES):
        qkv_ref[j] = acc[:, j * LANES:(j + 1) * LANES]
    conv_ref[...] = acc[:, 3 * A_WIDTH:].astype(BF16)


def _even_in(h, gain, w, tabs, seq):
    t = h.shape[0]
    tm = ROW_TILE
    nseq = seq // tm
    tab_spec = pl.BlockSpec((tm, LANES), lambda i: (i % nseq, 0))
    return pl.pallas_call(
        _even_in_kernel,
        grid=(t // tm,),
        in_specs=[
            pl.BlockSpec((tm, D_MODEL), lambda i: (i, 0)),
            pl.BlockSpec((1, D_MODEL), lambda i: (0, 0)),
            pl.BlockSpec((None, D_MODEL, EVEN_IN_WIDTH), lambda i: (0, 0, 0)),
            tab_spec, tab_spec, tab_spec,
        ],
        out_specs=[
            pl.BlockSpec((3 * A_WIDTH // LANES, tm, LANES), lambda i: (0, i, 0)),
            pl.BlockSpec((tm, 2 * B_WIDTH), lambda i: (i, 0)),
        ],
        out_shape=[
            jax.ShapeDtypeStruct((3 * A_WIDTH // LANES, t, LANES), F32),
            jax.ShapeDtypeStruct((t, 2 * B_WIDTH), BF16),
        ],
        scratch_shapes=[pltpu.VMEM((D_MODEL, EVEN_IN_WIDTH), BF16)],
        compiler_params=_cparams(("arbitrary",)),
        name="even_in_proj",
    )(h, gain.reshape(1, D_MODEL), w, *tabs)


def _attn_rope_tables(seq):
    half = ROPE_DIM // 2
    inv_freq = ROPE_THETA ** (-jnp.arange(0, ROPE_DIM, 2, dtype=F32) / ROPE_DIM)
    ang = jnp.arange(seq, dtype=F32)[:, None] * inv_freq[None, :]
    cos, sin = jnp.cos(ang), jnp.sin(ang)
    rest = A_HEAD_DIM - ROPE_DIM
    ones = jnp.ones((seq, rest), F32)
    z_rest = jnp.zeros((seq, rest), F32)
    z_half = jnp.zeros((seq, half), F32)
    c = jnp.concatenate([cos, cos, ones], axis=1)
    s1 = jnp.concatenate([-sin, z_half, z_rest], axis=1)
    s2 = jnp.concatenate([z_half, sin, z_rest], axis=1)
    rep = LANES // A_HEAD_DIM
    return tuple(jnp.tile(a, (1, rep)) for a in (c, s1, s2))


ATTN_STEP = ATTN_BLOCK * max(d for _, d in A_BRANCHES)
ATTN_PAIRS = A_WIDTH // LANES
ATTN_MERGE_ROWS = 256


def _attn_kernel(q_ref, k_ref, v_ref, o_ref, kprev, vprev, oml):
    n = pl.program_id(1)
    blk = ATTN_BLOCK
    n_blocks = ATTN_STEP // blk

    @pl.when(n == 0)
    def _():
        kprev[...] = jnp.zeros(kprev.shape, BF16)
        vprev[...] = jnp.zeros(vprev.shape, BF16)

    qi = lax.broadcasted_iota(I32, (blk, 2 * blk), 0)
    kj = lax.broadcasted_iota(I32, (blk, 2 * blk), 1)
    dist = blk + qi - kj
    band = (dist >= 0) & (dist <= blk)
    in_cur = kj >= blk
    lane = lax.broadcasted_iota(I32, (blk, LANES), 1)
    lo = lane < A_HEAD_DIM

    off = 0
    for bi, (_, dil) in enumerate(A_BRANCHES):
        shift = dil.bit_length() - 1
        assert dil == 1 << shift

        def block_body(idx, c, bi=bi, dil=dil, shift=shift, off=off):
            u = lax.shift_right_logical(idx, shift)
            r = idx & (dil - 1)
            start = u * (blk * dil) + r
            rows = pl.ds(pl.multiple_of(start, blk), blk) if dil == 1 else pl.ds(start, blk, stride=dil)
            has_prev = (n > 0) | (u > 0)
            valid = band & (in_cur | has_prev)
            valid2 = jnp.concatenate([valid, valid], axis=0)
            for g in range(ATTN_PAIRS):
                q = q_ref[g, rows, :]
                kc = k_ref[g, rows, :].astype(BF16)
                vc = v_ref[g, rows, :].astype(BF16)
                q2 = jnp.concatenate([jnp.where(lo, q, 0.0), jnp.where(lo, 0.0, q)], axis=0).astype(BF16)
                kk = jnp.concatenate([kprev[g, off + r], kc], axis=0)
                vv = jnp.concatenate([vprev[g, off + r], vc], axis=0)
                s = lax.dot_general(q2, kk, (((1,), (1,)), ((), ())), preferred_element_type=F32)
                s = jnp.where(valid2, s, NEG_INF)
                m = jnp.max(s, axis=-1, keepdims=True)
                e = jnp.exp(s - m)
                den = jnp.sum(e, axis=-1, keepdims=True)
                pv = jnp.dot(e.astype(BF16), vv, preferred_element_type=F32)
                o_new = jnp.where(lo, pv[:blk], pv[blk:])
                m_new = jnp.where(lo, m[:blk], m[blk:])
                l_new = jnp.where(lo, den[:blk], den[blk:])
                if bi > 0:
                    o_run, m_run, l_run = oml[0, g, rows, :], oml[1, g, rows, :], oml[2, g, rows, :]
                    m_both = jnp.maximum(m_run, m_new)
                    w_run, w_new = jnp.exp(m_run - m_both), jnp.exp(m_new - m_both)
                    o_new = w_run * o_run + w_new * o_new
                    l_new = w_run * l_run + w_new * l_new
                    m_new = m_both
                oml[0, g, rows, :] = o_new
                oml[1, g, rows, :] = m_new
                oml[2, g, rows, :] = l_new
                kprev[g, off + r] = kc
                vprev[g, off + r] = vc
            return c

        lax.fori_loop(0, n_blocks, block_body, 0)
        off += dil
    for g in range(ATTN_PAIRS):
        for c in range(ATTN_STEP // ATTN_MERGE_ROWS):
            sl = slice(c * ATTN_MERGE_ROWS, (c + 1) * ATTN_MERGE_ROWS)
            o_ref[g, sl, :] = (oml[0, g, sl, :] / oml[2, g, sl, :]).astype(BF16)


def _attention(qkv, batch, seq):
    t = qkv.shape[1]
    steps = seq // ATTN_STEP
    n_res = sum(d for _, d in A_BRANCHES)

    def slabs(which):
        return pl.BlockSpec((ATTN_PAIRS, ATTN_STEP, LANES), lambda b, n: (which, b * steps + n, 0))

    return pl.pallas_call(
        _attn_kernel,
        grid=(batch, steps),
        in_specs=[slabs(0), slabs(1), slabs(2)],
        out_specs=pl.BlockSpec((ATTN_PAIRS, ATTN_STEP, LANES), lambda b, n: (0, b * steps + n, 0)),
        out_shape=jax.ShapeDtypeStruct((ATTN_PAIRS, t, LANES), BF16),
        scratch_shapes=[
            pltpu.VMEM((ATTN_PAIRS, n_res, ATTN_BLOCK, LANES), BF16),
            pltpu.VMEM((ATTN_PAIRS, n_res, ATTN_BLOCK, LANES), BF16),
            pltpu.VMEM((3, ATTN_PAIRS, ATTN_STEP, LANES), F32),
        ],
        compiler_params=_cparams(("parallel", "arbitrary")),
        name="dilated_attention",
    )(qkv, qkv, qkv)


def _conv_kernel(val_ref, gate_ref, w_ref, b_ref, g_ref, beta_ref, o_ref, abuf, shifted):
    n = pl.program_id(1)
    tc = CONV_TILE
    sub = 8

    @pl.when(n == 0)
    def _():
        abuf[0:CONV_HALO, :] = jnp.zeros((CONV_HALO, B_WIDTH), F32)

    @pl.when(n > 0)
    def _():
        abuf[0:CONV_HALO, :] = abuf[tc:tc + CONV_HALO, :]

    val = val_ref[...].astype(F32)
    gate = gate_ref[...].astype(F32)
    abuf[CONV_HALO:CONV_HALO + tc, :] = val * jax.nn.sigmoid(gate)
    off = CONV_HALO - (CONV_WIDTH - 1)
    span = shifted.shape[1]
    for s in range(1, sub):
        shifted[s - 1] = abuf[s:s + span, :]
    for c in range(tc // CONV_CHUNK):
        acc = jnp.broadcast_to(b_ref[...], (CONV_CHUNK, B_WIDTH))
        for j in range(CONV_WIDTH):
            s = (off + j) % sub
            r0 = c * CONV_CHUNK + off + j - s
            src = abuf[r0:r0 + CONV_CHUNK, :] if s == 0 else shifted[s - 1, r0:r0 + CONV_CHUNK, :]
            acc = acc + w_ref[j:j + 1, :] * src
        mu = jnp.mean(acc, axis=-1, keepdims=True)
        cen = acc - mu
        var = jnp.mean(cen * cen, axis=-1, keepdims=True)
        yn = cen * lax.rsqrt(var + NORM_EPS) * g_ref[...] + beta_ref[...]
        o_ref[c * CONV_CHUNK:(c + 1) * CONV_CHUNK, :] = (yn * jax.nn.sigmoid(yn)).astype(BF16)


def _conv(conv_in, w, b, g, beta, batch, seq):
    t = conv_in.shape[0]
    tc = CONV_TILE
    nt = seq // tc
    w_pad = jnp.concatenate([w, jnp.zeros((CONV_HALO - CONV_WIDTH, B_WIDTH), F32)], axis=0)
    vec = pl.BlockSpec((1, B_WIDTH), lambda bb, n: (0, 0))
    return pl.pallas_call(
        _conv_kernel,
        grid=(batch, nt),
        in_specs=[
            pl.BlockSpec((tc, B_WIDTH), lambda bb, n: (bb * nt + n, 0)),
            pl.BlockSpec((tc, B_WIDTH), lambda bb, n: (bb * nt + n, 1)),
            pl.BlockSpec((CONV_HALO, B_WIDTH), lambda bb, n: (0, 0)),
            vec, vec, vec,
        ],
        out_specs=pl.BlockSpec((tc, B_WIDTH), lambda bb, n: (bb * nt + n, 0)),
        out_shape=jax.ShapeDtypeStruct((t, B_WIDTH), BF16),
        scratch_shapes=[pltpu.VMEM((tc + CONV_HALO, B_WIDTH), F32),
                        pltpu.VMEM((7, tc + CONV_HALO - 8, B_WIDTH), F32)],
        compiler_params=_cparams(("parallel", "arbitrary")),
        name="conformer_conv",
    )(conv_in, conv_in, w_pad, b.reshape(1, -1), g.reshape(1, -1), beta.reshape(1, -1))


def _even_out_kernel(h_ref, a_ref, bc_ref, w_ref, out_ref, wb_ref):
    _cast_once(w_ref, wb_ref, pl.program_id(0) == 0)
    a = jnp.concatenate([a_ref[g] for g in range(ATTN_PAIRS)], axis=1)
    acc = jnp.dot(a, wb_ref[0:A_WIDTH, :], preferred_element_type=F32)
    acc = acc + jnp.dot(bc_ref[...], wb_ref[A_WIDTH:, :], preferred_element_type=F32)
    out_ref[...] = h_ref[...] + acc


def _even_out(h, attn, bconv, w):
    t = h.shape[0]
    tm = ROW_TILE
    row = lambda width: pl.BlockSpec((tm, width), lambda i: (i, 0))
    return pl.pallas_call(
        _even_out_kernel,
        grid=(t // tm,),
        in_specs=[row(D_MODEL), pl.BlockSpec((ATTN_PAIRS, tm, LANES), lambda i: (0, i, 0)), row(B_WIDTH),
                  pl.BlockSpec((None, D_MODEL, D_MODEL), lambda i: (0, 0, 0))],
        out_specs=row(D_MODEL),
        out_shape=jax.ShapeDtypeStruct((t, D_MODEL), F32),
        scratch_shapes=[pltpu.VMEM((D_MODEL, D_MODEL), BF16)],
        compiler_params=_cparams(("arbitrary",)),
        name="even_out_proj",
    )(h, attn, bconv, w)


def _odd_in_kernel(h_ref, g_ref, w_ref, cos_ref, sin_ref, o_ref, wb_ref):
    j = pl.program_id(0)
    _cast_once(w_ref, wb_ref, pl.program_id(1) == 0)
    u = _rms(h_ref[...], g_ref[...]).astype(BF16)
    acc = jnp.dot(u, wb_ref[...], preferred_element_type=F32)

    @pl.when(j == 0)
    def _():
        cos, sin = cos_ref[...], sin_ref[...]
        half = C_QK_DIM // 2
        for hd in range(2 * C_HEADS):
            x1 = acc[:, hd * C_QK_DIM:hd * C_QK_DIM + half]
            x2 = acc[:, hd * C_QK_DIM + half:(hd + 1) * C_QK_DIM]
            r1 = x1 * cos - x2 * sin
            r2 = x2 * cos + x1 * sin
            if hd >= C_HEADS:
                r1 = r1 * (C_QK_DIM ** -0.5)
                r2 = r2 * (C_QK_DIM ** -0.5)
            o_ref[:, hd * C_QK_DIM:hd * C_QK_DIM + half] = r1.astype(BF16)
            o_ref[:, hd * C_QK_DIM + half:(hd + 1) * C_QK_DIM] = r2.astype(BF16)

    @pl.when(j > 0)
    def _():
        o_ref[...] = acc.astype(BF16)


def _odd_in(h, gain, w, cos, sin, seq):
    t = h.shape[0]
    tm = WIDE_ROW_TILE
    tn = 2 * C_QK_WIDTH
    nseq = seq // tm
    tab = pl.BlockSpec((tm, C_QK_DIM // 2), lambda j, i: (i % nseq, 0))
    return pl.pallas_call(
        _odd_in_kernel,
        grid=(ODD_IN_WIDTH // tn, t // tm),
        in_specs=[
            pl.BlockSpec((tm, D_MODEL), lambda j, i: (i, 0)),
            pl.BlockSpec((1, D_MODEL), lambda j, i: (0, 0)),
            pl.BlockSpec((None, D_MODEL, tn), lambda j, i: (0, 0, j), pipeline_mode=pl.Buffered(1)),
            tab, tab,
        ],
        out_specs=pl.BlockSpec((tm, tn), lambda j, i: (i, j)),
        out_shape=jax.ShapeDtypeStruct((t, ODD_IN_WIDTH), BF16),
        scratch_shapes=[pltpu.VMEM((D_MODEL, tn), BF16)],
        compiler_params=_cparams(("arbitrary", "arbitrary")),
        name="odd_in_proj",
    )(h, gain.reshape(1, D_MODEL), w, cos, sin)


def _ret_kernel(ld_ref, q_ref, k_ref, v_ref, g_ref, o_ref, state):
    c = pl.program_id(1)
    ch = RET_CHUNK

    @pl.when(c == 0)
    def _():
        state[...] = jnp.zeros(state.shape, F32)

    ii = lax.broadcasted_iota(I32, (ch, ch), 0)
    jj = lax.broadcasted_iota(I32, (ch, ch), 1)
    diff = (ii - jj).astype(F32)
    pos = lax.broadcasted_iota(I32, (ch, 1), 0).astype(F32)
    for hd in range(C_HEADS):
        ld = ld_ref[hd]
        q = q_ref[:, hd * C_QK_DIM:(hd + 1) * C_QK_DIM]
        k = k_ref[:, hd * C_QK_DIM:(hd + 1) * C_QK_DIM]
        v = v_ref[:, hd * C_V_DIM:(hd + 1) * C_V_DIM]
        intra = jnp.where(diff >= 0, jnp.exp(ld * jnp.maximum(diff, 0.0)), 0.0)
        s = lax.dot_general(q, k, (((1,), (1,)), ((), ())), preferred_element_type=F32) * intra
        inner = jnp.dot(s.astype(BF16), v, preferred_element_type=F32)
        q_decay = jnp.exp(ld * (pos + 1.0))
        k_decay = jnp.exp(ld * (ch - 1.0 - pos))
        chunk_decay = jnp.exp(ld * jnp.full((1, 1), float(ch), F32))
        st = state[hd]
        cross = jnp.dot(q, st.astype(BF16), preferred_element_type=F32) * q_decay
        kd_t = jnp.transpose(k.astype(F32) * k_decay).astype(BF16)
        state[hd] = st * chunk_decay + jnp.dot(kd_t, v, preferred_element_type=F32)
        out = inner + cross
        mu = jnp.mean(out, axis=-1, keepdims=True)
        cen = out - mu
        var = jnp.mean(cen * cen, axis=-1, keepdims=True)
        o = cen * lax.rsqrt(var + NORM_EPS)
        gf = g_ref[:, hd * C_V_DIM:(hd + 1) * C_V_DIM].astype(F32)
        o_ref[:, hd * C_V_DIM:(hd + 1) * C_V_DIM] = (gf * jax.nn.sigmoid(gf) * o).astype(BF16)


def _retention(proj, log_decay, batch, seq):
    t = proj.shape[0]
    ch = RET_CHUNK
    nc = seq // ch
    v0 = 2 * C_QK_WIDTH // C_V_WIDTH
    grid_spec = pltpu.PrefetchScalarGridSpec(
        num_scalar_prefetch=1,
        grid=(batch, nc),
        in_specs=[
            pl.BlockSpec((ch, C_QK_WIDTH), lambda b, c, ld: (b * nc + c, 0)),
            pl.BlockSpec((ch, C_QK_WIDTH), lambda b, c, ld: (b * nc + c, 1)),
            pl.BlockSpec((ch, C_V_WIDTH), lambda b, c, ld: (b * nc + c, v0)),
            pl.BlockSpec((ch, C_V_WIDTH), lambda b, c, ld: (b * nc + c, v0 + 1)),
        ],
        out_specs=pl.BlockSpec((ch, C_V_WIDTH), lambda b, c, ld: (b * nc + c, 0)),
        scratch_shapes=[pltpu.VMEM((C_HEADS, C_QK_DIM, C_V_DIM), F32)],
    )
    return pl.pallas_call(
        _ret_kernel,
        grid_spec=grid_spec,
        out_shape=jax.ShapeDtypeStruct((t, C_V_WIDTH), BF16),
        compiler_params=_cparams(("parallel", "arbitrary")),
        name="retention",
    )(log_decay, proj, proj, proj, proj)


def _odd_out_kernel(h_ref, y_ref, w_ref, out_ref, wb_ref):
    _cast_once(w_ref, wb_ref, pl.program_id(0) == 0)
    out_ref[...] = h_ref[...] + jnp.dot(y_ref[...], wb_ref[...], preferred_element_type=F32)


def _odd_out(h, y, w):
    t = h.shape[0]
    tm = WIDE_ROW_TILE
    return pl.pallas_call(
        _odd_out_kernel,
        grid=(t // tm,),
        in_specs=[pl.BlockSpec((tm, D_MODEL), lambda i: (i, 0)),
                  pl.BlockSpec((tm, C_V_WIDTH), lambda i: (i, 0)),
                  pl.BlockSpec((None, C_V_WIDTH, D_MODEL), lambda i: (0, 0, 0), pipeline_mode=pl.Buffered(1))],
        out_specs=pl.BlockSpec((tm, D_MODEL), lambda i: (i, 0)),
        out_shape=jax.ShapeDtypeStruct((t, D_MODEL), F32),
        scratch_shapes=[pltpu.VMEM((C_V_WIDTH, D_MODEL), BF16)],
        compiler_params=_cparams(("arbitrary",)),
        name="odd_out_proj",
    )(h, y, w)


def _router_kernel(h_ref, g_ref, wr_ref, br_ref, x2s_ref, codes_ref, gates_ref, cnt_ref, base_ref, *,
                   tiles_per_batch):
    i = pl.program_id(0)
    tm = h_ref.shape[0]
    batch = i // tiles_per_batch

    @pl.when(i % tiles_per_batch == 0)
    def _():
        base_ref[...] = jnp.zeros(base_ref.shape, F32)

    x2 = _rms(h_ref[...], g_ref[...])
    for j in range(SLABS):
        x2s_ref[pl.ds(j, tm, stride=SLABS), :] = x2[:, j * LANES:(j + 1) * LANES]
    wr = wr_ref[...]
    x_hi, w_hi = x2.astype(BF16), wr.astype(BF16)
    x_lo = (x2 - x_hi.astype(F32)).astype(BF16)
    w_lo = (wr - w_hi.astype(F32)).astype(BF16)
    logits = (jnp.dot(x_hi, w_hi, preferred_element_type=F32) + jnp.dot(x_hi, w_lo, preferred_element_type=F32)
              + jnp.dot(x_lo, w_hi, preferred_element_type=F32) + br_ref[...])
    lane = lax.broadcasted_iota(I32, (tm, LANES), 1)
    is_grp = lane < MOE_GROUPS
    lg = jnp.where(is_grp, logits, -jnp.inf)
    gmax = jnp.max(lg, axis=-1, keepdims=True)
    gsum = jnp.sum(jnp.where(is_grp, jnp.exp(logits - gmax), 0.0), axis=-1, keepdims=True)
    gp = 1.0 / gsum
    gi = jnp.min(jnp.where(lg == gmax, lane, LANES), axis=-1, keepdims=True)
    ex = lane - ROUTER_LANE0
    in_grp = (ex >= 0) & (ex < MOE_EXPERTS) & ((ex // MOE_EXPERTS_PER_GROUP) == gi)
    sel = jnp.where(in_grp, logits, -jnp.inf)
    v1 = jnp.max(sel, axis=-1, keepdims=True)
    i1 = jnp.min(jnp.where(sel == v1, lane, LANES), axis=-1, keepdims=True)
    sel2 = jnp.where(lane == i1, -jnp.inf, sel)
    v2 = jnp.max(sel2, axis=-1, keepdims=True)
    i2 = jnp.min(jnp.where(sel2 == v2, lane, LANES), axis=-1, keepdims=True)
    tt = jnp.exp(v2 - v1)
    g0 = gp / (1.0 + tt)
    g1 = gp * tt / (1.0 + tt)
    oh0 = (lane == i1).astype(F32)
    oh1 = (lane == i2).astype(F32)
    oh = oh0 + oh1
    rr = lax.broadcasted_iota(I32, (tm, tm), 0)
    cc = lax.broadcasted_iota(I32, (tm, tm), 1)
    tri = (cc < rr).astype(BF16)
    prefix = jnp.dot(tri, oh.astype(BF16), preferred_element_type=F32)
    tot = base_ref[...] + prefix
    rank0 = jnp.sum(oh0 * tot, axis=-1, keepdims=True).astype(I32)
    rank1 = jnp.sum(oh1 * tot, axis=-1, keepdims=True).astype(I32)
    base_ref[...] = base_ref[...] + jnp.sum(oh, axis=0, keepdims=True)
    eoff = batch * MOE_EXPERTS - ROUTER_LANE0
    code0 = (i1 + eoff) * (1 << RANK_BITS) + rank0
    code1 = (i2 + eoff) * (1 << RANK_BITS) + rank1
    meta = jnp.where(lane == 0, code0, jnp.where(lane == 1, code1, 0))
    codes_ref[...] = jnp.transpose(meta)[0:8, :]
    gates_ref[...] = jnp.where(lane == 0, g0, jnp.where(lane == 1, g1, 0.0))
    cnt_ref[...] = jnp.broadcast_to(base_ref[...], cnt_ref.shape)


def _router(h, gain, wr, br, batch):
    t = h.shape[0]
    tm = ROW_TILE
    tpb = t // batch // tm
    row = lambda width: pl.BlockSpec((tm, width), lambda i: (i, 0))
    fixed = lambda shape: pl.BlockSpec(shape, lambda i: (0, 0))
    return pl.pallas_call(
        functools.partial(_router_kernel, tiles_per_batch=tpb),
        grid=(t // tm,),
        in_specs=[row(D_MODEL), fixed((1, D_MODEL)), fixed((D_MODEL, LANES)), fixed((1, LANES))],
        out_specs=[pl.BlockSpec((tm * SLABS, LANES), lambda i: (i, 0)),
                   pl.BlockSpec((8, tm), lambda i: (0, i)), row(LANES),
                   pl.BlockSpec((8, LANES), lambda i: (i // tpb, 0))],
        out_shape=[
            jax.ShapeDtypeStruct((t * SLABS, LANES), F32),
            jax.ShapeDtypeStruct((8, t), I32),
            jax.ShapeDtypeStruct((t, LANES), F32),
            jax.ShapeDtypeStruct((batch * 8, LANES), F32),
        ],
        scratch_shapes=[pltpu.VMEM((1, LANES), F32)],
        compiler_params=_cparams(("arbitrary",)),
        name="moe_router",
    )(h, gain.reshape(1, D_MODEL), wr, br)


INVERT_UNROLL = 16


def _dest_kernel(pstart_ref, codes_ref, dest_ref):
    codes = codes_ref[...]
    seg = lax.shift_right_logical(codes, RANK_BITS)
    dest = codes & ((1 << RANK_BITS) - 1)
    for i in range(pstart_ref.shape[0]):
        dest = dest + jnp.where(seg == i, pstart_ref[i], 0)
    dest_ref[...] = dest


def _dest_rows(pad_start, codes):
    grid_spec = pltpu.PrefetchScalarGridSpec(
        num_scalar_prefetch=1,
        grid=(1,),
        in_specs=[pl.BlockSpec(codes.shape, lambda i, ps: (0, 0))],
        out_specs=pl.BlockSpec(codes.shape, lambda i, ps: (0, 0)),
    )
    return pl.pallas_call(
        _dest_kernel,
        grid_spec=grid_spec,
        out_shape=jax.ShapeDtypeStruct(codes.shape, I32),
        compiler_params=_cparams(("arbitrary",)),
        name="moe_dest_rows",
    )(pad_start, codes)


def _invert_kernel(trips_ref, dest_ref, dummy_ref, slot_ref, *, batch):
    per_batch = dest_ref.shape[0] // batch
    pltpu.sync_copy(dummy_ref, slot_ref)
    for b in range(batch):
        def place(i, carry, b=b):
            for v in range(INVERT_UNROLL):
                a = i * INVERT_UNROLL + v
                slot_ref[dest_ref[b * per_batch + a]] = a
            return carry

        lax.fori_loop(0, trips_ref[0], place, 0)


def _dummy_slots(n_rows, per_batch):
    assert MOE_BLOCK & (MOE_BLOCK - 1) == 0 and 3 * MOE_BLOCK <= DUMMY_SLOTS
    r = jnp.arange(n_rows + MOE_BLOCK, dtype=I32)
    pattern = per_batch + (r // MOE_BLOCK) % 2 * MOE_BLOCK + r % MOE_BLOCK
    return jnp.where(r < n_rows, pattern, per_batch + 2 * MOE_BLOCK + r % MOE_BLOCK)


def _invert(dest_flat, n_rows, batch):
    smem = pl.BlockSpec(memory_space=pltpu.SMEM)
    per_batch = dest_flat.shape[0] // batch
    assert per_batch % INVERT_UNROLL == 0
    trips = jnp.full((1,), per_batch // INVERT_UNROLL, I32)
    return pl.pallas_call(
        functools.partial(_invert_kernel, batch=batch),
        in_specs=[smem, smem, pl.BlockSpec(memory_space=pl.ANY)],
        out_specs=smem,
        out_shape=jax.ShapeDtypeStruct((n_rows + MOE_BLOCK,), I32),
        name="moe_invert_rows",
    )(trips, dest_flat, _dummy_slots(n_rows, per_batch))


def _expert_kernel(be_ref, run_ref, nxt_ref, slot_ref, x2s_ref, wg_hbm, wu_hbm, wd_hbm, ys_ref,
                   xres, wbuf_g, wbuf_u, wbuf_d, wgb_ref, wub_ref, wdb_ref, tile, xb, ybuf, sems, wsems, *,
                   batch, nb, layer):
    b = pl.program_id(0)
    n = pl.program_id(1)
    g = b * nb + n
    last = batch * nb - 1
    blk = MOE_BLOCK
    seq = xres.shape[0] // SLABS
    bstride = 2 * seq + DUMMY_SLOTS
    tcur = g % 2
    ycur = g % 3
    yprev = (g + 2) % 3

    def weight_copies(e, ws):
        pairs = ((wg_hbm, wbuf_g), (wu_hbm, wbuf_u), (wd_hbm, wbuf_d))
        return [pltpu.make_async_copy(src.at[layer, e], dst.at[ws], wsems.at[ws, k])
                for k, (src, dst) in enumerate(pairs)]

    def wait_block(q):
        pltpu.make_async_copy(ybuf.at[q], ys_ref.at[pl.ds(0, blk * SLABS), :], sems.at[q]).wait()

    def gather_row(block, tslot, mi):
        s = slot_ref[block * blk + mi]
        if seq & (seq - 1) == 0:
            tok = s & (seq - 1)
        else:
            tok = jnp.minimum(jnp.where(s >= seq, s - seq, s), seq - 1)
        slab = xres[pl.ds(pl.multiple_of(tok * SLABS, SLABS), SLABS), :]
        tile[tslot, pl.ds(mi, SLABS, stride=GATHER_PITCH), :] = slab

    def to_bf16(tslot):
        for j in range(SLABS):
            xb[tslot, :, j * LANES:(j + 1) * LANES] = tile[tslot, j * GATHER_PITCH:j * GATHER_PITCH + blk, :].astype(BF16)

    def scatter_row(yslot, slot, row0, mi):
        dst = pl.multiple_of((row0 + slot) * SLABS, SLABS)
        pltpu.make_async_copy(ybuf.at[yslot, pl.ds(mi * SLABS, SLABS), :], ys_ref.at[pl.ds(dst, SLABS), :],
                              sems.at[yslot]).start()

    @pl.when(g == 0)
    def _():
        for c in weight_copies(be_ref[0], 0):
            c.start()
        ybuf[...] = jnp.zeros(ybuf.shape, F32)
        for bb in range(batch):
            for c in range(DUMMY_SLOTS // blk):
                start = (bb * bstride + 2 * seq + c * blk) * SLABS
                zero = pltpu.make_async_copy(ybuf.at[0], ys_ref.at[pl.ds(start, blk * SLABS), :], sems.at[0])
                zero.start()
                zero.wait()

    @pl.when(n == 0)
    def _():
        pltpu.sync_copy(x2s_ref.at[pl.ds(pl.multiple_of(b * (seq * SLABS), SLABS), seq * SLABS), :], xres)
        for mi in range(blk):
            gather_row(g, tcur, mi)
        to_bf16(tcur)

    run = run_ref[g]
    ws = run % 2

    @pl.when((g == 0) | (run != run_ref[jnp.maximum(g - 1, 0)]))
    def _():
        for c in weight_copies(be_ref[g], ws):
            c.wait()
        wgb_ref[...] = wbuf_g[ws].astype(BF16)
        wub_ref[...] = wbuf_u[ws].astype(BF16)
        wdb_ref[...] = wbuf_d[ws].astype(BF16)

        @pl.when(nxt_ref[g] >= 0)
        def _():
            for c in weight_copies(nxt_ref[g], 1 - ws):
                c.start()

    @pl.when(g >= 2)
    def _():
        wait_block(ycur)

    nxt_block = jnp.minimum(g + 1, last)
    prev_block = jnp.where(g == 0, batch * nb, g - 1)
    prev_row0 = jnp.where(n == 0, jnp.maximum(b - 1, 0), b) * bstride
    x = xb[tcur]
    n_chunks = 8
    per = blk // n_chunks
    half = EXPERT_HIDDEN // 2
    quarter = D_MODEL // 4
    acts = []
    hid = None
    for c in range(n_chunks):
        for mi in range(c * per, (c + 1) * per):
            gather_row(nxt_block, 1 - tcur, mi)
        if c < 4:
            for mi in range(2 * c * per, 2 * (c + 1) * per):
                scatter_row(yprev, slot_ref[prev_block * blk + mi], prev_row0, mi)
        if c < 4:
            w_ref = wgb_ref if c < 2 else wub_ref
            acts.append(jnp.dot(x, w_ref[:, (c % 2) * half:(c % 2 + 1) * half], preferred_element_type=F32))
        if c == 3:
            a = jnp.concatenate(acts[0:2], axis=1)
            u = jnp.concatenate(acts[2:4], axis=1)
            hid = (a * jax.nn.sigmoid(a) * u).astype(BF16)
        if c >= 4:
            q = c - 4
            yq = jnp.dot(hid, wdb_ref[:, q * quarter:(q + 1) * quarter], preferred_element_type=F32)
            for jj in range(quarter // LANES):
                j = q * (quarter // LANES) + jj
                ybuf[ycur, pl.ds(j, blk, stride=SLABS), :] = yq[:, jj * LANES:(jj + 1) * LANES]
    to_bf16(1 - tcur)

    @pl.when(g == last)
    def _():
        wait_block((last - 2) % 3)
        for mi in range(blk):
            scatter_row(last % 3, slot_ref[last * blk + mi], (batch - 1) * bstride, mi)
        wait_block((last - 1) % 3)
        wait_block(last % 3)


def _experts(block_expert, row_slot, x2s, wg, wu, wd, layer, batch, seq):
    blk = MOE_BLOCK
    nb = (row_slot.shape[0] // blk - 1) // batch
    steps = batch * nb
    assert steps >= 3
    idx = jnp.arange(steps, dtype=I32)
    change = jnp.concatenate([jnp.zeros((1,), I32), (block_expert[1:] != block_expert[:-1]).astype(I32)])
    run = jnp.cumsum(change).astype(I32)
    later_change = (idx[None, :] > idx[:, None]) & (change[None, :] > 0)
    nxt_idx = jnp.min(jnp.where(later_change, idx[None, :], steps), axis=1)
    nxt = jnp.where(nxt_idx < steps, block_expert[jnp.minimum(nxt_idx, steps - 1)], -1).astype(I32)
    hbm = pl.BlockSpec(memory_space=pl.ANY)
    grid_spec = pltpu.PrefetchScalarGridSpec(
        num_scalar_prefetch=4,
        grid=(batch, nb),
        in_specs=[hbm, hbm, hbm, hbm],
        out_specs=hbm,
        scratch_shapes=[
            pltpu.VMEM((seq * SLABS, LANES), F32),
            pltpu.VMEM((2, D_MODEL, EXPERT_HIDDEN), F32), pltpu.VMEM((2, D_MODEL, EXPERT_HIDDEN), F32),
            pltpu.VMEM((2, EXPERT_HIDDEN, D_MODEL), F32),
            pltpu.VMEM((D_MODEL, EXPERT_HIDDEN), BF16), pltpu.VMEM((D_MODEL, EXPERT_HIDDEN), BF16),
            pltpu.VMEM((EXPERT_HIDDEN, D_MODEL), BF16),
            pltpu.VMEM((2, SLABS * GATHER_PITCH, LANES), F32),
            pltpu.VMEM((2, blk, D_MODEL), BF16),
            pltpu.VMEM((3, blk * SLABS, LANES), F32),
            pltpu.SemaphoreType.DMA((3,)),
            pltpu.SemaphoreType.DMA((2, 3)),
        ],
    )
    return pl.pallas_call(
        functools.partial(_expert_kernel, batch=batch, nb=nb, layer=layer),
        grid_spec=grid_spec,
        out_shape=jax.ShapeDtypeStruct((batch * (2 * seq + DUMMY_SLOTS) * SLABS, LANES), F32),
        compiler_params=_cparams(("arbitrary", "arbitrary"), VMEM_LIMIT_EXPERTS),
        name="moe_experts",
    )(block_expert, run, nxt, row_slot, x2s, wg, wu, wd)


def _combine_kernel(h_ref, gates_ref, y0_ref, y1_ref, *rest, final):
    if final:
        fg_ref, out_ref = rest
    else:
        (out_ref,) = rest
    tm = h_ref.shape[0]
    gates = gates_ref[...]
    g0, g1 = gates[:, 0:1], gates[:, 1:2]
    parts = []
    for j in range(SLABS):
        y0 = y0_ref[pl.ds(j, tm, stride=SLABS), :]
        y1 = y1_ref[pl.ds(j, tm, stride=SLABS), :]
        parts.append(g0 * y0 + g1 * y1)
    out = h_ref[...] + jnp.concatenate(parts, axis=1)
    if final:
        out = _rms(out, fg_ref[...])
    out_ref[...] = out


def _combine(h, gates, ys, batch, seq, final_gain=None):
    t = h.shape[0]
    tm = ROW_TILE
    tpb = seq // tm
    bstride = 2 * tpb + DUMMY_SLOTS // tm
    final = final_gain is not None
    in_specs = [pl.BlockSpec((tm, D_MODEL), lambda i: (i, 0)),
                pl.BlockSpec((tm, LANES), lambda i: (i, 0)),
                pl.BlockSpec((tm * SLABS, LANES), lambda i: (i // tpb * bstride + i % tpb, 0)),
                pl.BlockSpec((tm * SLABS, LANES), lambda i: (i // tpb * bstride + tpb + i % tpb, 0))]
    args = [h, gates, ys, ys]
    if final:
        in_specs.append(pl.BlockSpec((1, D_MODEL), lambda i: (0, 0)))
        args.append(final_gain.reshape(1, D_MODEL))
    return pl.pallas_call(
        functools.partial(_combine_kernel, final=final),
        grid=(t // tm,),
        in_specs=in_specs,
        out_specs=pl.BlockSpec((tm, D_MODEL), lambda i: (i, 0)),
        out_shape=jax.ShapeDtypeStruct((t, D_MODEL), F32),
        compiler_params=_cparams(("parallel",)),
        name="moe_combine_final" if final else "moe_combine",
    )(*args)


def _moe(h, gain, w_r1, b_r1, w_r2, b_r2, wg, wu, wd, layer, batch, final_gain=None):
    t = h.shape[0]
    seq = t // batch
    assert seq <= (1 << RANK_BITS) and seq % ROW_TILE == 0 and DUMMY_SLOTS % ROW_TILE == 0
    pad_w = jnp.zeros((D_MODEL, LANES - MOE_GROUPS - MOE_EXPERTS), F32)
    wr = jnp.concatenate([w_r1, w_r2, pad_w], axis=1)
    br = jnp.concatenate([b_r1, b_r2, jnp.zeros((LANES - MOE_GROUPS - MOE_EXPERTS,), F32)]).reshape(1, LANES)
    x2s, codes, gates, counts = _router(h, gain, wr, br, batch)
    cnt = counts.reshape(batch, 8, LANES)[:, 0, ROUTER_LANE0:ROUTER_LANE0 + MOE_EXPERTS].astype(I32)
    padded = (cnt + MOE_BLOCK - 1) // MOE_BLOCK * MOE_BLOCK
    pad_end = jnp.cumsum(padded, axis=1)
    rows_pb = seq * 2 + MOE_EXPERTS * MOE_BLOCK
    nb = rows_pb // MOE_BLOCK
    pad_start = pad_end - padded + (jnp.arange(batch, dtype=I32) * rows_pb)[:, None]
    block_start = jnp.arange(nb, dtype=I32) * MOE_BLOCK
    block_expert = jnp.minimum(
        jnp.sum((pad_end[:, None, :] <= block_start[None, :, None]).astype(I32), axis=2), MOE_EXPERTS - 1)
    dest = _dest_rows(pad_start.reshape(-1), codes)
    dest_flat = dest[0:2].reshape(2, batch, seq).transpose(1, 0, 2).reshape(-1)
    row_slot = _invert(dest_flat, batch * rows_pb, batch)
    ys = _experts(block_expert.reshape(-1), row_slot, x2s, wg, wu, wd, layer, batch, seq)
    return _combine(h, gates, ys, batch, seq, final_gain)


def kernel(x, mix_norm, ffn_norm, final_norm, even_w_in, even_w_out, conv_w, conv_b, conv_norm_g, conv_norm_b,
           odd_w_in, odd_w_out, router_w1, router_b1, router_w2, router_b2, expert_w_gate, expert_w_up,
           expert_w_down):
    batch, seq, d = x.shape
    assert d == D_MODEL and seq % (ATTN_BLOCK * max(dl for _, dl in A_BRANCHES)) == 0
    assert all(w // dl == ATTN_BLOCK for w, dl in A_BRANCHES)
    t = batch * seq
    h = x.reshape(t, d)

    qkv, conv_in = _even_in(h, mix_norm[0], even_w_in, _attn_rope_tables(seq), seq)
    attn = _attention(qkv, batch, seq)
    bconv = _conv(conv_in, conv_w[0], conv_b[0], conv_norm_g[0], conv_norm_b[0], batch, seq)
    h = _even_out(h, attn, bconv, even_w_out)
    h = _moe(h, ffn_norm[0], router_w1[0], router_b1[0], router_w2[0], router_b2[0],
             expert_w_gate, expert_w_up, expert_w_down, 0, batch)

    inv_freq = RET_ROT_THETA ** (-jnp.linspace(0.0, 1.0, C_QK_DIM // 2, dtype=F32))
    ang = jnp.arange(seq, dtype=F32)[:, None] * inv_freq[None, :]
    proj = _odd_in(h, mix_norm[1], odd_w_in, jnp.cos(ang), jnp.sin(ang), seq)
    log_decay = jnp.log(1.0 - jnp.exp2(-5.0 - jnp.arange(C_HEADS, dtype=F32)))
    y = _retention(proj, log_decay, batch, seq)
    h = _odd_out(h, y, odd_w_out)
    out = _moe(h, ffn_norm[1], router_w1[1], router_b1[1], router_w2[1], router_b2[1],
               expert_w_gate, expert_w_up, expert_w_down, 1, batch, final_gain=final_norm)
    return out.reshape(batch, seq, d)
```

```python
import functools

import jax
import jax.numpy as jnp
from jax import lax
from jax.experimental import pallas as pl
from jax.experimental.pallas import tpu as pltpu

F32 = jnp.float32
BF16 = jnp.bfloat16
I32 = jnp.int32

NORM_EPS = 1e-6
NEG_INF = -1e30

D_MODEL = 1024
A_HEADS = 8
A_HEAD_DIM = 64
A_WIDTH = A_HEADS * A_HEAD_DIM
A_BRANCHES = ((128, 1), (512, 4), (2048, 16))
ATTN_BLOCK = 128
ROPE_THETA = 500000.0
ROPE_DIM = A_HEAD_DIM // 4
B_WIDTH = D_MODEL - A_WIDTH
CONV_WIDTH = 31
C_HEADS = 4
C_QK_DIM = 256
C_V_DIM = 512
C_QK_WIDTH = C_HEADS * C_QK_DIM
C_V_WIDTH = C_HEADS * C_V_DIM
RET_CHUNK = 128
RET_ROT_THETA = 10000.0
MOE_GROUPS = 4
MOE_EXPERTS_PER_GROUP = 8
MOE_EXPERTS = MOE_GROUPS * MOE_EXPERTS_PER_GROUP
EXPERT_HIDDEN = 512
MOE_BLOCK = 128
EVEN_IN_WIDTH = 3 * A_WIDTH + 2 * B_WIDTH
ODD_IN_WIDTH = 2 * C_QK_WIDTH + 2 * C_V_WIDTH

LANES = 128
ROW_TILE = 512
WIDE_ROW_TILE = 1024
CONV_TILE = 512
CONV_HALO = 32
CONV_CHUNK = 64
ROUTER_LANE0 = MOE_GROUPS
VMEM_LIMIT = 56 * 1024 * 1024
VMEM_LIMIT_EXPERTS = 60 * 1024 * 1024
SLABS = D_MODEL // LANES
RANK_BITS = 16
GATHER_PITCH = MOE_BLOCK + 8
DUMMY_SLOTS = ROW_TILE


def _cparams(sem, vmem=VMEM_LIMIT):
    return pltpu.CompilerParams(dimension_semantics=sem, vmem_limit_bytes=vmem)


def _cast_once(w_ref, wb_ref, first):
    @pl.when(first)
    def _():
        wb_ref[...] = w_ref[...].astype(BF16)


def _rms(x, gain):
    ms = jnp.mean(x * x, axis=-1, keepdims=True)
    return x * lax.rsqrt(ms + NORM_EPS) * gain


def _even_in_kernel(h_ref, g_ref, w_ref, c_ref, s1_ref, s2_ref, qkv_ref, conv_ref, wb_ref):
    _cast_once(w_ref, wb_ref, pl.program_id(0) == 0)
    u = _rms(h_ref[...], g_ref[...]).astype(BF16)
    acc = jnp.dot(u, wb_ref[...], preferred_element_type=F32)
    c, s1, s2 = c_ref[...], s1_ref[...], s2_ref[...]
    for j in range(2 * A_WIDTH // LANES):
        xg = acc[:, j * LANES:(j + 1) * LANES]
        if j < A_WIDTH // LANES:
            xg = xg * (A_HEAD_DIM ** -0.5)
        qkv_ref[j] = xg * c + pltpu.roll(xg, LANES - ROPE_DIM // 2, 1) * s1 + pltpu.roll(xg, ROPE_DIM // 2, 1) * s2
    for j in range(2 * A_WIDTH // LANES, 3 * A_WIDTH // LANES):
        qkv_ref[j] = acc[:, j * LANES:(j + 1) * LANES]
    conv_ref[...] = acc[:, 3 * A_WIDTH:].astype(BF16)


def _even_in(h, gain, w, tabs, seq):
    t = h.shape[0]
    tm = ROW_TILE
    nseq = seq // tm
    tab_spec = pl.BlockSpec((tm, LANES), lambda i: (i % nseq, 0))
    return pl.pallas_call(
        _even_in_kernel,
        grid=(t // tm,),
        in_specs=[
            pl.BlockSpec((tm, D_MODEL), lambda i: (i, 0)),
            pl.BlockSpec((1, D_MODEL), lambda i: (0, 0)),
            pl.BlockSpec((None, D_MODEL, EVEN_IN_WIDTH), lambda i: (0, 0, 0)),
            tab_spec, tab_spec, tab_spec,
        ],
        out_specs=[
            pl.BlockSpec((3 * A_WIDTH // LANES, tm, LANES), lambda i: (0, i, 0)),
            pl.BlockSpec((tm, 2 * B_WIDTH), lambda i: (i, 0)),
        ],
        out_shape=[
            jax.ShapeDtypeStruct((3 * A_WIDTH // LANES, t, LANES), F32),
            jax.ShapeDtypeStruct((t, 2 * B_WIDTH), BF16),
        ],
        scratch_shapes=[pltpu.VMEM((D_MODEL, EVEN_IN_WIDTH), BF16)],
        compiler_params=_cparams(("arbitrary",)),
        name="even_in_proj",
    )(h, gain.reshape(1, D_MODEL), w, *tabs)


def _attn_rope_tables(seq):
    half = ROPE_DIM // 2
    inv_freq = ROPE_THETA ** (-jnp.arange(0, ROPE_DIM, 2, dtype=F32) / ROPE_DIM)
    ang = jnp.arange(seq, dtype=F32)[:, None] * inv_freq[None, :]
    cos, sin = jnp.cos(ang), jnp.sin(ang)
    rest = A_HEAD_DIM - ROPE_DIM
    ones = jnp.ones((seq, rest), F32)
    z_rest = jnp.zeros((seq, rest), F32)
    z_half = jnp.zeros((seq, half), F32)
    c = jnp.concatenate([cos, cos, ones], axis=1)
    s1 = jnp.concatenate([-sin, z_half, z_rest], axis=1)
    s2 = jnp.concatenate([z_half, sin, z_rest], axis=1)
    rep = LANES // A_HEAD_DIM
    return tuple(jnp.tile(a, (1, rep)) for a in (c, s1, s2))


ATTN_STEP = ATTN_BLOCK * max(d for _, d in A_BRANCHES)
ATTN_PAIRS = A_WIDTH // LANES
ATTN_MERGE_ROWS = 256


def _attn_kernel(q_ref, k_ref, v_ref, o_ref, kprev, vprev, oml):
    n = pl.program_id(1)
    blk = ATTN_BLOCK
    n_blocks = ATTN_STEP // blk

    @pl.when(n == 0)
    def _():
        kprev[...] = jnp.zeros(kprev.shape, BF16)
        vprev[...] = jnp.zeros(vprev.shape, BF16)

    qi = lax.broadcasted_iota(I32, (blk, 2 * blk), 0)
    kj = lax.broadcasted_iota(I32, (blk, 2 * blk), 1)
    dist = blk + qi - kj
    band = (dist >= 0) & (dist <= blk)
    in_cur = kj >= blk
    lane = lax.broadcasted_iota(I32, (blk, LANES), 1)
    lo = lane < A_HEAD_DIM

    off = 0
    for bi, (_, dil) in enumerate(A_BRANCHES):
        shift = dil.bit_length() - 1
        assert dil == 1 << shift

        def one_block(idx, carried, bi=bi, dil=dil, shift=shift, off=off):
            u = lax.shift_right_logical(idx, shift)
            r = idx & (dil - 1)
            start = u * (blk * dil) + r
            rows = pl.ds(pl.multiple_of(start, blk), blk) if dil == 1 else pl.ds(start, blk, stride=dil)
            has_prev = (n > 0) | (u > 0)
            valid = band & (in_cur | has_prev)
            valid2 = jnp.concatenate([valid, valid], axis=0)
            current = []
            for g in range(ATTN_PAIRS):
                q = q_ref[g, rows, :]
                kc = k_ref[g, rows, :].astype(BF16)
                vc = v_ref[g, rows, :].astype(BF16)
                current.append((kc, vc))
                kp, vp = (kprev[g, off + r], vprev[g, off + r]) if carried is None else carried[g]
                q2 = jnp.concatenate([jnp.where(lo, q, 0.0), jnp.where(lo, 0.0, q)], axis=0).astype(BF16)
                kk = jnp.concatenate([kp, kc], axis=0)
                vv = jnp.concatenate([vp, vc], axis=0)
                s = lax.dot_general(q2, kk, (((1,), (1,)), ((), ())), preferred_element_type=F32)
                s = jnp.where(valid2, s, NEG_INF)
                m = jnp.max(s, axis=-1, keepdims=True)
                e = jnp.exp(s - m)
                den = jnp.sum(e, axis=-1, keepdims=True)
                pv = jnp.dot(e.astype(BF16), vv, preferred_element_type=F32)
                o_new = jnp.where(lo, pv[:blk], pv[blk:])
                m_new = jnp.where(lo, m[:blk], m[blk:])
                l_new = jnp.where(lo, den[:blk], den[blk:])
                if bi > 0:
                    o_run, m_run, l_run = oml[0, g, rows, :], oml[1, g, rows, :], oml[2, g, rows, :]
                    m_both = jnp.maximum(m_run, m_new)
                    w_run, w_new = jnp.exp(m_run - m_both), jnp.exp(m_new - m_both)
                    o_new = w_run * o_run + w_new * o_new
                    l_new = w_run * l_run + w_new * l_new
                    m_new = m_both
                oml[0, g, rows, :] = o_new
                oml[1, g, rows, :] = m_new
                oml[2, g, rows, :] = l_new
            return r, current

        def save_prev(r, current, off=off):
            for g, (kc, vc) in enumerate(current):
                kprev[g, off + r] = kc
                vprev[g, off + r] = vc

        def block_body(i, c, dil=dil):
            r0, cur0 = one_block(2 * i, None)
            if dil == 1:
                r1, cur1 = one_block(2 * i + 1, cur0)
            else:
                save_prev(r0, cur0)
                r1, cur1 = one_block(2 * i + 1, None)
            save_prev(r1, cur1)
            return c

        lax.fori_loop(0, n_blocks // 2, block_body, 0)
        off += dil
    for g in range(ATTN_PAIRS):
        for c in range(ATTN_STEP // ATTN_MERGE_ROWS):
            sl = slice(c * ATTN_MERGE_ROWS, (c + 1) * ATTN_MERGE_ROWS)
            o_ref[g, sl, :] = (oml[0, g, sl, :] / oml[2, g, sl, :]).astype(BF16)


def _attention(qkv, batch, seq):
    t = qkv.shape[1]
    steps = seq // ATTN_STEP
    n_res = sum(d for _, d in A_BRANCHES)

    def slabs(which):
        return pl.BlockSpec((ATTN_PAIRS, ATTN_STEP, LANES), lambda b, n: (which, b * steps + n, 0))

    return pl.pallas_call(
        _attn_kernel,
        grid=(batch, steps),
        in_specs=[slabs(0), slabs(1), slabs(2)],
        out_specs=pl.BlockSpec((ATTN_PAIRS, ATTN_STEP, LANES), lambda b, n: (0, b * steps + n, 0)),
        out_shape=jax.ShapeDtypeStruct((ATTN_PAIRS, t, LANES), BF16),
        scratch_shapes=[
            pltpu.VMEM((ATTN_PAIRS, n_res, ATTN_BLOCK, LANES), BF16),
            pltpu.VMEM((ATTN_PAIRS, n_res, ATTN_BLOCK, LANES), BF16),
            pltpu.VMEM((3, ATTN_PAIRS, ATTN_STEP, LANES), F32),
        ],
        compiler_params=_cparams(("parallel", "arbitrary")),
        name="dilated_attention",
    )(qkv, qkv, qkv)


def _conv_kernel(val_ref, gate_ref, w_ref, b_ref, g_ref, beta_ref, o_ref, abuf, shifted):
    n = pl.program_id(1)
    tc = CONV_TILE
    sub = 8

    @pl.when(n == 0)
    def _():
        abuf[0:CONV_HALO, :] = jnp.zeros((CONV_HALO, B_WIDTH), F32)

    @pl.when(n > 0)
    def _():
        abuf[0:CONV_HALO, :] = abuf[tc:tc + CONV_HALO, :]

    val = val_ref[...].astype(F32)
    gate = gate_ref[...].astype(F32)
    abuf[CONV_HALO:CONV_HALO + tc, :] = val * jax.nn.sigmoid(gate)
    off = CONV_HALO - (CONV_WIDTH - 1)
    span = shifted.shape[1]
    for s in range(1, sub):
        shifted[s - 1] = abuf[s:s + span, :]
    for c in range(tc // CONV_CHUNK):
        acc = jnp.broadcast_to(b_ref[...], (CONV_CHUNK, B_WIDTH))
        for j in range(CONV_WIDTH):
            s = (off + j) % sub
            r0 = c * CONV_CHUNK + off + j - s
            src = abuf[r0:r0 + CONV_CHUNK, :] if s == 0 else shifted[s - 1, r0:r0 + CONV_CHUNK, :]
            acc = acc + w_ref[j:j + 1, :] * src
        mu = jnp.mean(acc, axis=-1, keepdims=True)
        cen = acc - mu
        var = jnp.mean(cen * cen, axis=-1, keepdims=True)
        yn = cen * lax.rsqrt(var + NORM_EPS) * g_ref[...] + beta_ref[...]
        o_ref[c * CONV_CHUNK:(c + 1) * CONV_CHUNK, :] = (yn * jax.nn.sigmoid(yn)).astype(BF16)


def _conv(conv_in, w, b, g, beta, batch, seq):
    t = conv_in.shape[0]
    tc = CONV_TILE
    nt = seq // tc
    w_pad = jnp.concatenate([w, jnp.zeros((CONV_HALO - CONV_WIDTH, B_WIDTH), F32)], axis=0)
    vec = pl.BlockSpec((1, B_WIDTH), lambda bb, n: (0, 0))
    return pl.pallas_call(
        _conv_kernel,
        grid=(batch, nt),
        in_specs=[
            pl.BlockSpec((tc, B_WIDTH), lambda bb, n: (bb * nt + n, 0)),
            pl.BlockSpec((tc, B_WIDTH), lambda bb, n: (bb * nt + n, 1)),
            pl.BlockSpec((CONV_HALO, B_WIDTH), lambda bb, n: (0, 0)),
            vec, vec, vec,
        ],
        out_specs=pl.BlockSpec((tc, B_WIDTH), lambda bb, n: (bb * nt + n, 0)),
        out_shape=jax.ShapeDtypeStruct((t, B_WIDTH), BF16),
        scratch_shapes=[pltpu.VMEM((tc + CONV_HALO, B_WIDTH), F32),
                        pltpu.VMEM((7, tc + CONV_HALO - 8, B_WIDTH), F32)],
        compiler_params=_cparams(("parallel", "arbitrary")),
        name="conformer_conv",
    )(conv_in, conv_in, w_pad, b.reshape(1, -1), g.reshape(1, -1), beta.reshape(1, -1))


def _even_out_kernel(h_ref, a_ref, bc_ref, w_ref, out_ref, wb_ref):
    _cast_once(w_ref, wb_ref, pl.program_id(0) == 0)
    a = jnp.concatenate([a_ref[g] for g in range(ATTN_PAIRS)], axis=1)
    acc = jnp.dot(a, wb_ref[0:A_WIDTH, :], preferred_element_type=F32)
    acc = acc + jnp.dot(bc_ref[...], wb_ref[A_WIDTH:, :], preferred_element_type=F32)
    out_ref[...] = h_ref[...] + acc


def _even_out(h, attn, bconv, w):
    t = h.shape[0]
    tm = ROW_TILE
    row = lambda width: pl.BlockSpec((tm, width), lambda i: (i, 0))
    return pl.pallas_call(
        _even_out_kernel,
        grid=(t // tm,),
        in_specs=[row(D_MODEL), pl.BlockSpec((ATTN_PAIRS, tm, LANES), lambda i: (0, i, 0)), row(B_WIDTH),
                  pl.BlockSpec((None, D_MODEL, D_MODEL), lambda i: (0, 0, 0))],
        out_specs=row(D_MODEL),
        out_shape=jax.ShapeDtypeStruct((t, D_MODEL), F32),
        scratch_shapes=[pltpu.VMEM((D_MODEL, D_MODEL), BF16)],
        compiler_params=_cparams(("arbitrary",)),
        name="even_out_proj",
    )(h, attn, bconv, w)


def _odd_in_kernel(h_ref, g_ref, w_ref, cos_ref, sin_ref, o_ref, wb_ref):
    j = pl.program_id(0)
    _cast_once(w_ref, wb_ref, pl.program_id(1) == 0)
    u = _rms(h_ref[...], g_ref[...]).astype(BF16)
    acc = jnp.dot(u, wb_ref[...], preferred_element_type=F32)

    @pl.when(j == 0)
    def _():
        cos, sin = cos_ref[...], sin_ref[...]
        half = C_QK_DIM // 2
        for hd in range(2 * C_HEADS):
            x1 = acc[:, hd * C_QK_DIM:hd * C_QK_DIM + half]
            x2 = acc[:, hd * C_QK_DIM + half:(hd + 1) * C_QK_DIM]
            r1 = x1 * cos - x2 * sin
            r2 = x2 * cos + x1 * sin
            if hd >= C_HEADS:
                r1 = r1 * (C_QK_DIM ** -0.5)
                r2 = r2 * (C_QK_DIM ** -0.5)
            o_ref[:, hd * C_QK_DIM:hd * C_QK_DIM + half] = r1.astype(BF16)
            o_ref[:, hd * C_QK_DIM + half:(hd + 1) * C_QK_DIM] = r2.astype(BF16)

    @pl.when(j > 0)
    def _():
        o_ref[...] = acc.astype(BF16)


def _odd_in(h, gain, w, cos, sin, seq):
    t = h.shape[0]
    tm = WIDE_ROW_TILE
    tn = 2 * C_QK_WIDTH
    nseq = seq // tm
    tab = pl.BlockSpec((tm, C_QK_DIM // 2), lambda j, i: (i % nseq, 0))
    return pl.pallas_call(
        _odd_in_kernel,
        grid=(ODD_IN_WIDTH // tn, t // tm),
        in_specs=[
            pl.BlockSpec((tm, D_MODEL), lambda j, i: (i, 0)),
            pl.BlockSpec((1, D_MODEL), lambda j, i: (0, 0)),
            pl.BlockSpec((None, D_MODEL, tn), lambda j, i: (0, 0, j), pipeline_mode=pl.Buffered(1)),
            tab, tab,
        ],
        out_specs=pl.BlockSpec((tm, tn), lambda j, i: (i, j)),
        out_shape=jax.ShapeDtypeStruct((t, ODD_IN_WIDTH), BF16),
        scratch_shapes=[pltpu.VMEM((D_MODEL, tn), BF16)],
        compiler_params=_cparams(("arbitrary", "arbitrary")),
        name="odd_in_proj",
    )(h, gain.reshape(1, D_MODEL), w, cos, sin)


def _ret_kernel(ld_ref, q_ref, k_ref, v_ref, g_ref, o_ref, state):
    c = pl.program_id(0)
    ch = RET_CHUNK
    batch = q_ref.shape[0]

    @pl.when(c == 0)
    def _():
        state[...] = jnp.zeros(state.shape, F32)

    ii = lax.broadcasted_iota(I32, (ch, ch), 0)
    jj = lax.broadcasted_iota(I32, (ch, ch), 1)
    diff = (ii - jj).astype(F32)
    pos = lax.broadcasted_iota(I32, (ch, 1), 0).astype(F32)
    for hd in range(C_HEADS):
        ld = ld_ref[hd]
        intra = jnp.where(diff >= 0, jnp.exp(ld * jnp.maximum(diff, 0.0)), 0.0)
        q_decay = jnp.exp(ld * (pos + 1.0))
        k_decay = jnp.exp(ld * (ch - 1.0 - pos))
        chunk_decay = jnp.exp(ld * jnp.full((1, 1), float(ch), F32))
        for bb in range(batch):
            q = q_ref[bb, :, hd * C_QK_DIM:(hd + 1) * C_QK_DIM]
            k = k_ref[bb, :, hd * C_QK_DIM:(hd + 1) * C_QK_DIM]
            v = v_ref[bb, :, hd * C_V_DIM:(hd + 1) * C_V_DIM]
            s = lax.dot_general(q, k, (((1,), (1,)), ((), ())), preferred_element_type=F32) * intra
            inner = jnp.dot(s.astype(BF16), v, preferred_element_type=F32)
            st = state[bb, hd]
            cross = jnp.dot(q, st.astype(BF16), preferred_element_type=F32) * q_decay
            kd_t = jnp.transpose(k.astype(F32) * k_decay).astype(BF16)
            state[bb, hd] = st * chunk_decay + jnp.dot(kd_t, v, preferred_element_type=F32)
            out = inner + cross
            mu = jnp.mean(out, axis=-1, keepdims=True)
            cen = out - mu
            var = jnp.mean(cen * cen, axis=-1, keepdims=True)
            o = cen * lax.rsqrt(var + NORM_EPS)
            gf = g_ref[bb, :, hd * C_V_DIM:(hd + 1) * C_V_DIM].astype(F32)
            o_ref[bb, :, hd * C_V_DIM:(hd + 1) * C_V_DIM] = (gf * jax.nn.sigmoid(gf) * o).astype(BF16)


def _retention(proj, log_decay, batch, seq):
    ch = RET_CHUNK
    nc = seq // ch
    proj3 = proj.reshape(batch, seq, ODD_IN_WIDTH)
    v0 = 2 * C_QK_WIDTH // C_V_WIDTH
    grid_spec = pltpu.PrefetchScalarGridSpec(
        num_scalar_prefetch=1,
        grid=(nc,),
        in_specs=[
            pl.BlockSpec((batch, ch, C_QK_WIDTH), lambda c, ld: (0, c, 0)),
            pl.BlockSpec((batch, ch, C_QK_WIDTH), lambda c, ld: (0, c, 1)),
            pl.BlockSpec((batch, ch, C_V_WIDTH), lambda c, ld: (0, c, v0)),
            pl.BlockSpec((batch, ch, C_V_WIDTH), lambda c, ld: (0, c, v0 + 1)),
        ],
        out_specs=pl.BlockSpec((batch, ch, C_V_WIDTH), lambda c, ld: (0, c, 0)),
        scratch_shapes=[pltpu.VMEM((batch, C_HEADS, C_QK_DIM, C_V_DIM), F32)],
    )
    y = pl.pallas_call(
        _ret_kernel,
        grid_spec=grid_spec,
        out_shape=jax.ShapeDtypeStruct((batch, seq, C_V_WIDTH), BF16),
        compiler_params=_cparams(("arbitrary",)),
        name="retention",
    )(log_decay, proj3, proj3, proj3, proj3)
    return y.reshape(batch * seq, C_V_WIDTH)


def _odd_out_kernel(h_ref, y_ref, w_ref, out_ref, wb_ref):
    _cast_once(w_ref, wb_ref, pl.program_id(0) == 0)
    out_ref[...] = h_ref[...] + jnp.dot(y_ref[...], wb_ref[...], preferred_element_type=F32)


def _odd_out(h, y, w):
    t = h.shape[0]
    tm = WIDE_ROW_TILE
    return pl.pallas_call(
        _odd_out_kernel,
        grid=(t // tm,),
        in_specs=[pl.BlockSpec((tm, D_MODEL), lambda i: (i, 0)),
                  pl.BlockSpec((tm, C_V_WIDTH), lambda i: (i, 0)),
                  pl.BlockSpec((None, C_V_WIDTH, D_MODEL), lambda i: (0, 0, 0), pipeline_mode=pl.Buffered(1))],
        out_specs=pl.BlockSpec((tm, D_MODEL), lambda i: (i, 0)),
        out_shape=jax.ShapeDtypeStruct((t, D_MODEL), F32),
        scratch_shapes=[pltpu.VMEM((C_V_WIDTH, D_MODEL), BF16)],
        compiler_params=_cparams(("arbitrary",)),
        name="odd_out_proj",
    )(h, y, w)


def _router_kernel(h_ref, g_ref, wr_ref, br_ref, x2s_ref, codes_ref, gates_ref, cnt_ref, base_ref, *,
                   tiles_per_batch):
    i = pl.program_id(0)
    tm = h_ref.shape[0]
    batch = i // tiles_per_batch

    @pl.when(i % tiles_per_batch == 0)
    def _():
        base_ref[...] = jnp.zeros(base_ref.shape, F32)

    x2 = _rms(h_ref[...], g_ref[...])
    for j in range(SLABS):
        x2s_ref[pl.ds(j, tm, stride=SLABS), :] = x2[:, j * LANES:(j + 1) * LANES]
    wr = wr_ref[...]
    x_hi, w_hi = x2.astype(BF16), wr.astype(BF16)
    x_lo = (x2 - x_hi.astype(F32)).astype(BF16)
    w_lo = (wr - w_hi.astype(F32)).astype(BF16)
    logits = (jnp.dot(x_hi, w_hi, preferred_element_type=F32) + jnp.dot(x_hi, w_lo, preferred_element_type=F32)
              + jnp.dot(x_lo, w_hi, preferred_element_type=F32) + br_ref[...])
    lane = lax.broadcasted_iota(I32, (tm, LANES), 1)
    is_grp = lane < MOE_GROUPS
    lg = jnp.where(is_grp, logits, -jnp.inf)
    gmax = jnp.max(lg, axis=-1, keepdims=True)
    gsum = jnp.sum(jnp.where(is_grp, jnp.exp(logits - gmax), 0.0), axis=-1, keepdims=True)
    gp = 1.0 / gsum
    gi = jnp.min(jnp.where(lg == gmax, lane, LANES), axis=-1, keepdims=True)
    ex = lane - ROUTER_LANE0
    in_grp = (ex >= 0) & (ex < MOE_EXPERTS) & ((ex // MOE_EXPERTS_PER_GROUP) == gi)
    sel = jnp.where(in_grp, logits, -jnp.inf)
    v1 = jnp.max(sel, axis=-1, keepdims=True)
    i1 = jnp.min(jnp.where(sel == v1, lane, LANES), axis=-1, keepdims=True)
    sel2 = jnp.where(lane == i1, -jnp.inf, sel)
    v2 = jnp.max(sel2, axis=-1, keepdims=True)
    i2 = jnp.min(jnp.where(sel2 == v2, lane, LANES), axis=-1, keepdims=True)
    tt = jnp.exp(v2 - v1)
    g0 = gp / (1.0 + tt)
    g1 = gp * tt / (1.0 + tt)
    oh0 = (lane == i1).astype(F32)
    oh1 = (lane == i2).astype(F32)
    oh = oh0 + oh1
    rr = lax.broadcasted_iota(I32, (tm, tm), 0)
    cc = lax.broadcasted_iota(I32, (tm, tm), 1)
    tri = (cc < rr).astype(BF16)
    prefix = jnp.dot(tri, oh.astype(BF16), preferred_element_type=F32)
    tot = base_ref[...] + prefix
    rank0 = jnp.sum(oh0 * tot, axis=-1, keepdims=True).astype(I32)
    rank1 = jnp.sum(oh1 * tot, axis=-1, keepdims=True).astype(I32)
    base_ref[...] = base_ref[...] + jnp.sum(oh, axis=0, keepdims=True)
    eoff = batch * MOE_EXPERTS - ROUTER_LANE0
    code0 = (i1 + eoff) * (1 << RANK_BITS) + rank0
    code1 = (i2 + eoff) * (1 << RANK_BITS) + rank1
    meta = jnp.where(lane == 0, code0, jnp.where(lane == 1, code1, 0))
    codes_ref[...] = jnp.transpose(meta)[0:8, :]
    gates_ref[...] = jnp.where(lane == 0, g0, jnp.where(lane == 1, g1, 0.0))
    cnt_ref[...] = jnp.broadcast_to(base_ref[...], cnt_ref.shape)


def _router(h, gain, wr, br, batch):
    t = h.shape[0]
    tm = ROW_TILE
    tpb = t // batch // tm
    row = lambda width: pl.BlockSpec((tm, width), lambda i: (i, 0))
    fixed = lambda shape: pl.BlockSpec(shape, lambda i: (0, 0))
    return pl.pallas_call(
        functools.partial(_router_kernel, tiles_per_batch=tpb),
        grid=(t // tm,),
        in_specs=[row(D_MODEL), fixed((1, D_MODEL)), fixed((D_MODEL, LANES)), fixed((1, LANES))],
        out_specs=[pl.BlockSpec((tm * SLABS, LANES), lambda i: (i, 0)),
                   pl.BlockSpec((8, tm), lambda i: (0, i)), row(LANES),
                   pl.BlockSpec((8, LANES), lambda i: (i // tpb, 0))],
        out_shape=[
            jax.ShapeDtypeStruct((t * SLABS, LANES), F32),
            jax.ShapeDtypeStruct((8, t), I32),
            jax.ShapeDtypeStruct((t, LANES), F32),
            jax.ShapeDtypeStruct((batch * 8, LANES), F32),
        ],
        scratch_shapes=[pltpu.VMEM((1, LANES), F32)],
        compiler_params=_cparams(("arbitrary",)),
        name="moe_router",
    )(h, gain.reshape(1, D_MODEL), wr, br)


INVERT_UNROLL = 16


def _dest_kernel(pstart_ref, codes_ref, dest_ref):
    codes = codes_ref[...]
    seg = lax.shift_right_logical(codes, RANK_BITS)
    dest = codes & ((1 << RANK_BITS) - 1)
    for i in range(pstart_ref.shape[0]):
        dest = dest + jnp.where(seg == i, pstart_ref[i], 0)
    dest_ref[...] = dest


def _dest_rows(pad_start, codes):
    grid_spec = pltpu.PrefetchScalarGridSpec(
        num_scalar_prefetch=1,
        grid=(1,),
        in_specs=[pl.BlockSpec(codes.shape, lambda i, ps: (0, 0))],
        out_specs=pl.BlockSpec(codes.shape, lambda i, ps: (0, 0)),
    )
    return pl.pallas_call(
        _dest_kernel,
        grid_spec=grid_spec,
        out_shape=jax.ShapeDtypeStruct(codes.shape, I32),
        compiler_params=_cparams(("arbitrary",)),
        name="moe_dest_rows",
    )(pad_start, codes)


def _invert_kernel(trips_ref, dest_ref, dummy_ref, slot_ref, *, batch):
    per_batch = dest_ref.shape[0] // batch
    pltpu.sync_copy(dummy_ref, slot_ref)
    for b in range(batch):
        def place(i, carry, b=b):
            for v in range(INVERT_UNROLL):
                a = i * INVERT_UNROLL + v
                slot_ref[dest_ref[b * per_batch + a]] = a
            return carry

        lax.fori_loop(0, trips_ref[0], place, 0)


def _dummy_slots(n_rows, per_batch):
    assert MOE_BLOCK & (MOE_BLOCK - 1) == 0 and 3 * MOE_BLOCK <= DUMMY_SLOTS
    r = jnp.arange(n_rows + MOE_BLOCK, dtype=I32)
    pattern = per_batch + (r // MOE_BLOCK) % 2 * MOE_BLOCK + r % MOE_BLOCK
    return jnp.where(r < n_rows, pattern, per_batch + 2 * MOE_BLOCK + r % MOE_BLOCK)


def _invert(dest_flat, n_rows, batch):
    smem = pl.BlockSpec(memory_space=pltpu.SMEM)
    per_batch = dest_flat.shape[0] // batch
    assert per_batch % INVERT_UNROLL == 0
    trips = jnp.full((1,), per_batch // INVERT_UNROLL, I32)
    return pl.pallas_call(
        functools.partial(_invert_kernel, batch=batch),
        in_specs=[smem, smem, pl.BlockSpec(memory_space=pl.ANY)],
        out_specs=smem,
        out_shape=jax.ShapeDtypeStruct((n_rows + MOE_BLOCK,), I32),
        name="moe_invert_rows",
    )(trips, dest_flat, _dummy_slots(n_rows, per_batch))


def _expert_kernel(be_ref, run_ref, nxt_ref, slot_ref, x2s_ref, wg_hbm, wu_hbm, wd_hbm, ys_ref,
                   xres, wbuf_g, wbuf_u, wbuf_d, wgb_ref, wub_ref, wdb_ref, tile, xb, ybuf, sems, wsems, *,
                   batch, nb, layer):
    b = pl.program_id(0)
    n = pl.program_id(1)
    g = b * nb + n
    last = batch * nb - 1
    blk = MOE_BLOCK
    seq = xres.shape[0] // SLABS
    bstride = 2 * seq + DUMMY_SLOTS
    tcur = g % 2
    ycur = g % 3
    yprev = (g + 2) % 3

    def weight_copies(e, ws):
        pairs = ((wg_hbm, wbuf_g), (wu_hbm, wbuf_u), (wd_hbm, wbuf_d))
        return [pltpu.make_async_copy(src.at[layer, e], dst.at[ws], wsems.at[ws, k])
                for k, (src, dst) in enumerate(pairs)]

    def wait_block(q):
        pltpu.make_async_copy(ybuf.at[q], ys_ref.at[pl.ds(0, blk * SLABS), :], sems.at[q]).wait()

    def gather_row(block, tslot, mi):
        s = slot_ref[block * blk + mi]
        if seq & (seq - 1) == 0:
            tok = s & (seq - 1)
        else:
            tok = jnp.minimum(jnp.where(s >= seq, s - seq, s), seq - 1)
        slab = xres[pl.ds(pl.multiple_of(tok * SLABS, SLABS), SLABS), :]
        tile[tslot, pl.ds(mi, SLABS, stride=GATHER_PITCH), :] = slab

    def to_bf16(tslot):
        for j in range(SLABS):
            xb[tslot, :, j * LANES:(j + 1) * LANES] = tile[tslot, j * GATHER_PITCH:j * GATHER_PITCH + blk, :].astype(BF16)

    def scatter_row(yslot, slot, row0, mi):
        dst = pl.multiple_of((row0 + slot) * SLABS, SLABS)
        pltpu.make_async_copy(ybuf.at[yslot, pl.ds(mi * SLABS, SLABS), :], ys_ref.at[pl.ds(dst, SLABS), :],
                              sems.at[yslot]).start()

    @pl.when(g == 0)
    def _():
        for c in weight_copies(be_ref[0], 0):
            c.start()
        ybuf[...] = jnp.zeros(ybuf.shape, F32)
        for bb in range(batch):
            for c in range(DUMMY_SLOTS // blk):
                start = (bb * bstride + 2 * seq + c * blk) * SLABS
                zero = pltpu.make_async_copy(ybuf.at[0], ys_ref.at[pl.ds(start, blk * SLABS), :], sems.at[0])
                zero.start()
                zero.wait()

    @pl.when(n == 0)
    def _():
        pltpu.sync_copy(x2s_ref.at[pl.ds(pl.multiple_of(b * (seq * SLABS), SLABS), seq * SLABS), :], xres)
        for mi in range(blk):
            gather_row(g, tcur, mi)
        to_bf16(tcur)

    run = run_ref[g]
    ws = run % 2

    @pl.when((g == 0) | (run != run_ref[jnp.maximum(g - 1, 0)]))
    def _():
        for c in weight_copies(be_ref[g], ws):
            c.wait()
        wgb_ref[...] = wbuf_g[ws].astype(BF16)
        wub_ref[...] = wbuf_u[ws].astype(BF16)
        wdb_ref[...] = wbuf_d[ws].astype(BF16)

        @pl.when(nxt_ref[g] >= 0)
        def _():
            for c in weight_copies(nxt_ref[g], 1 - ws):
                c.start()

    @pl.when(g >= 2)
    def _():
        wait_block(ycur)

    nxt_block = jnp.minimum(g + 1, last)
    prev_block = jnp.where(g == 0, batch * nb, g - 1)
    prev_row0 = jnp.where(n == 0, jnp.maximum(b - 1, 0), b) * bstride
    x = xb[tcur]
    n_chunks = 8
    per = blk // n_chunks
    half = EXPERT_HIDDEN // 2
    quarter = D_MODEL // 4
    acts = []
    hid = None
    for c in range(n_chunks):
        for mi in range(c * per, (c + 1) * per):
            gather_row(nxt_block, 1 - tcur, mi)
        if c < 4:
            for mi in range(2 * c * per, 2 * (c + 1) * per):
                scatter_row(yprev, slot_ref[prev_block * blk + mi], prev_row0, mi)
        if c < 4:
            w_ref = wgb_ref if c < 2 else wub_ref
            acts.append(jnp.dot(x, w_ref[:, (c % 2) * half:(c % 2 + 1) * half], preferred_element_type=F32))
        if c == 3:
            a = jnp.concatenate(acts[0:2], axis=1)
            u = jnp.concatenate(acts[2:4], axis=1)
            hid = (a * jax.nn.sigmoid(a) * u).astype(BF16)
        if c >= 4:
            q = c - 4
            yq = jnp.dot(hid, wdb_ref[:, q * quarter:(q + 1) * quarter], preferred_element_type=F32)
            for jj in range(quarter // LANES):
                j = q * (quarter // LANES) + jj
                ybuf[ycur, pl.ds(j, blk, stride=SLABS), :] = yq[:, jj * LANES:(jj + 1) * LANES]
    to_bf16(1 - tcur)

    @pl.when(g == last)
    def _():
        wait_block((last - 2) % 3)
        for mi in range(blk):
            scatter_row(last % 3, slot_ref[last * blk + mi], (batch - 1) * bstride, mi)
        wait_block((last - 1) % 3)
        wait_block(last % 3)


def _experts(block_expert, row_slot, x2s, wg, wu, wd, layer, batch, seq):
    blk = MOE_BLOCK
    nb = (row_slot.shape[0] // blk - 1) // batch
    steps = batch * nb
    assert steps >= 3
    idx = jnp.arange(steps, dtype=I32)
    change = jnp.concatenate([jnp.zeros((1,), I32), (block_expert[1:] != block_expert[:-1]).astype(I32)])
    run = jnp.cumsum(change).astype(I32)
    later_change = (idx[None, :] > idx[:, None]) & (change[None, :] > 0)
    nxt_idx = jnp.min(jnp.where(later_change, idx[None, :], steps), axis=1)
    nxt = jnp.where(nxt_idx < steps, block_expert[jnp.minimum(nxt_idx, steps - 1)], -1).astype(I32)
    hbm = pl.BlockSpec(memory_space=pl.ANY)
    grid_spec = pltpu.PrefetchScalarGridSpec(
        num_scalar_prefetch=4,
        grid=(batch, nb),
        in_specs=[hbm, hbm, hbm, hbm],
        out_specs=hbm,
        scratch_shapes=[
            pltpu.VMEM((seq * SLABS, LANES), F32),
            pltpu.VMEM((2, D_MODEL, EXPERT_HIDDEN), F32), pltpu.VMEM((2, D_MODEL, EXPERT_HIDDEN), F32),
            pltpu.VMEM((2, EXPERT_HIDDEN, D_MODEL), F32),
            pltpu.VMEM((D_MODEL, EXPERT_HIDDEN), BF16), pltpu.VMEM((D_MODEL, EXPERT_HIDDEN), BF16),
            pltpu.VMEM((EXPERT_HIDDEN, D_MODEL), BF16),
            pltpu.VMEM((2, SLABS * GATHER_PITCH, LANES), F32),
            pltpu.VMEM((2, blk, D_MODEL), BF16),
            pltpu.VMEM((3, blk * SLABS, LANES), F32),
            pltpu.SemaphoreType.DMA((3,)),
            pltpu.SemaphoreType.DMA((2, 3)),
        ],
    )
    return pl.pallas_call(
        functools.partial(_expert_kernel, batch=batch, nb=nb, layer=layer),
        grid_spec=grid_spec,
        out_shape=jax.ShapeDtypeStruct((batch * (2 * seq + DUMMY_SLOTS) * SLABS, LANES), F32),
        compiler_params=_cparams(("arbitrary", "arbitrary"), VMEM_LIMIT_EXPERTS),
        name="moe_experts",
    )(block_expert, run, nxt, row_slot, x2s, wg, wu, wd)


def _combine_kernel(h_ref, gates_ref, y0_ref, y1_ref, *rest, final):
    if final:
        fg_ref, out_ref = rest
    else:
        (out_ref,) = rest
    tm = h_ref.shape[0]
    gates = gates_ref[...]
    g0, g1 = gates[:, 0:1], gates[:, 1:2]
    parts = []
    for j in range(SLABS):
        y0 = y0_ref[pl.ds(j, tm, stride=SLABS), :]
        y1 = y1_ref[pl.ds(j, tm, stride=SLABS), :]
        parts.append(g0 * y0 + g1 * y1)
    out = h_ref[...] + jnp.concatenate(parts, axis=1)
    if final:
        out = _rms(out, fg_ref[...])
    out_ref[...] = out


def _combine(h, gates, ys, batch, seq, final_gain=None):
    t = h.shape[0]
    tm = ROW_TILE
    tpb = seq // tm
    bstride = 2 * tpb + DUMMY_SLOTS // tm
    final = final_gain is not None
    in_specs = [pl.BlockSpec((tm, D_MODEL), lambda i: (i, 0)),
                pl.BlockSpec((tm, LANES), lambda i: (i, 0)),
                pl.BlockSpec((tm * SLABS, LANES), lambda i: (i // tpb * bstride + i % tpb, 0)),
                pl.BlockSpec((tm * SLABS, LANES), lambda i: (i // tpb * bstride + tpb + i % tpb, 0))]
    args = [h, gates, ys, ys]
    if final:
        in_specs.append(pl.BlockSpec((1, D_MODEL), lambda i: (0, 0)))
        args.append(final_gain.reshape(1, D_MODEL))
    return pl.pallas_call(
        functools.partial(_combine_kernel, final=final),
        grid=(t // tm,),
        in_specs=in_specs,
        out_specs=pl.BlockSpec((tm, D_MODEL), lambda i: (i, 0)),
        out_shape=jax.ShapeDtypeStruct((t, D_MODEL), F32),
        compiler_params=_cparams(("parallel",)),
        name="moe_combine_final" if final else "moe_combine",
    )(*args)


def _moe(h, gain, w_r1, b_r1, w_r2, b_r2, wg, wu, wd, layer, batch, final_gain=None):
    t = h.shape[0]
    seq = t // batch
    assert seq <= (1 << RANK_BITS) and seq % ROW_TILE == 0 and DUMMY_SLOTS % ROW_TILE == 0
    pad_w = jnp.zeros((D_MODEL, LANES - MOE_GROUPS - MOE_EXPERTS), F32)
    wr = jnp.concatenate([w_r1, w_r2, pad_w], axis=1)
    br = jnp.concatenate([b_r1, b_r2, jnp.zeros((LANES - MOE_GROUPS - MOE_EXPERTS,), F32)]).reshape(1, LANES)
    x2s, codes, gates, counts = _router(h, gain, wr, br, batch)
    cnt = counts.reshape(batch, 8, LANES)[:, 0, ROUTER_LANE0:ROUTER_LANE0 + MOE_EXPERTS].astype(I32)
    padded = (cnt + MOE_BLOCK - 1) // MOE_BLOCK * MOE_BLOCK
    pad_end = jnp.cumsum(padded, axis=1)
    rows_pb = seq * 2 + MOE_EXPERTS * MOE_BLOCK
    nb = rows_pb // MOE_BLOCK
    pad_start = pad_end - padded + (jnp.arange(batch, dtype=I32) * rows_pb)[:, None]
    block_start = jnp.arange(nb, dtype=I32) * MOE_BLOCK
    block_expert = jnp.minimum(
        jnp.sum((pad_end[:, None, :] <= block_start[None, :, None]).astype(I32), axis=2), MOE_EXPERTS - 1)
    dest = _dest_rows(pad_start.reshape(-1), codes)
    dest_flat = dest[0:2].reshape(2, batch, seq).transpose(1, 0, 2).reshape(-1)
    row_slot = _invert(dest_flat, batch * rows_pb, batch)
    ys = _experts(block_expert.reshape(-1), row_slot, x2s, wg, wu, wd, layer, batch, seq)
    return _combine(h, gates, ys, batch, seq, final_gain)


def kernel(x, mix_norm, ffn_norm, final_norm, even_w_in, even_w_out, conv_w, conv_b, conv_norm_g, conv_norm_b,
           odd_w_in, odd_w_out, router_w1, router_b1, router_w2, router_b2, expert_w_gate, expert_w_up,
           expert_w_down):
    batch, seq, d = x.shape
    assert d == D_MODEL and seq % (ATTN_BLOCK * max(dl for _, dl in A_BRANCHES)) == 0
    assert all(w // dl == ATTN_BLOCK for w, dl in A_BRANCHES)
    t = batch * seq
    h = x.reshape(t, d)

    qkv, conv_in = _even_in(h, mix_norm[0], even_w_in, _attn_rope_tables(seq), seq)
    attn = _attention(qkv, batch, seq)
    bconv = _conv(conv_in, conv_w[0], conv_b[0], conv_norm_g[0], conv_norm_b[0], batch, seq)
    h = _even_out(h, attn, bconv, even_w_out)
    h = _moe(h, ffn_norm[0], router_w1[0], router_b1[0], router_w2[0], router_b2[0],
             expert_w_gate, expert_w_up, expert_w_down, 0, batch)

    inv_freq = RET_ROT_THETA ** (-jnp.linspace(0.0, 1.0, C_QK_DIM // 2, dtype=F32))
    ang = jnp.arange(seq, dtype=F32)[:, None] * inv_freq[None, :]
    proj = _odd_in(h, mix_norm[1], odd_w_in, jnp.cos(ang), jnp.sin(ang), seq)
    log_decay = jnp.log(1.0 - jnp.exp2(-5.0 - jnp.arange(C_HEADS, dtype=F32)))
    y = _retention(proj, log_decay, batch, seq)
    h = _odd_out(h, y, odd_w_out)
    out = _moe(h, ffn_norm[1], router_w1[1], router_b1[1], router_w2[1], router_b2[1],
               expert_w_gate, expert_w_up, expert_w_down, 1, batch, final_gain=final_norm)
    return out.reshape(batch, seq, d)
```

```python
import functools

import jax
import jax.numpy as jnp
from jax import lax
from jax.experimental import pallas as pl
from jax.experimental.pallas import tpu as pltpu

F32 = jnp.float32
BF16 = jnp.bfloat16
I32 = jnp.int32

NORM_EPS = 1e-6
NEG_INF = -1e30

D_MODEL = 1024
A_HEADS = 8
A_HEAD_DIM = 64
A_WIDTH = A_HEADS * A_HEAD_DIM
A_BRANCHES = ((128, 1), (512, 4), (2048, 16))
ATTN_BLOCK = 128
ROPE_THETA = 500000.0
ROPE_DIM = A_HEAD_DIM // 4
B_WIDTH = D_MODEL - A_WIDTH
CONV_WIDTH = 31
C_HEADS = 4
C_QK_DIM = 256
C_V_DIM = 512
C_QK_WIDTH = C_HEADS * C_QK_DIM
C_V_WIDTH = C_HEADS * C_V_DIM
RET_CHUNK = 128
RET_ROT_THETA = 10000.0
MOE_GROUPS = 4
MOE_EXPERTS_PER_GROUP = 8
MOE_EXPERTS = MOE_GROUPS * MOE_EXPERTS_PER_GROUP
EXPERT_HIDDEN = 512
MOE_BLOCK = 128
EVEN_IN_WIDTH = 3 * A_WIDTH + 2 * B_WIDTH
ODD_IN_WIDTH = 2 * C_QK_WIDTH + 2 * C_V_WIDTH

LANES = 128
ROW_TILE = 512
WIDE_ROW_TILE = 1024
CONV_TILE = 512
CONV_HALO = 32
CONV_CHUNK = 64
ROUTER_LANE0 = MOE_GROUPS
VMEM_LIMIT = 56 * 1024 * 1024
VMEM_LIMIT_EXPERTS = 60 * 1024 * 1024
SLABS = D_MODEL // LANES
RANK_BITS = 16
GATHER_PITCH = MOE_BLOCK + 8
DUMMY_SLOTS = ROW_TILE


def _cparams(sem, vmem=VMEM_LIMIT):
    return pltpu.CompilerParams(dimension_semantics=sem, vmem_limit_bytes=vmem)


def _cast_once(w_ref, wb_ref, first):
    @pl.when(first)
    def _():
        wb_ref[...] = w_ref[...].astype(BF16)


def _rms(x, gain):
    ms = jnp.mean(x * x, axis=-1, keepdims=True)
    return x * lax.rsqrt(ms + NORM_EPS) * gain


def _even_in_kernel(h_ref, g_ref, w_ref, c_ref, s1_ref, s2_ref, qkv_ref, conv_ref, wb_ref):
    _cast_once(w_ref, wb_ref, pl.program_id(0) == 0)
    u = _rms(h_ref[...], g_ref[...]).astype(BF16)
    acc = jnp.dot(u, wb_ref[...], preferred_element_type=F32)
    c, s1, s2 = c_ref[...], s1_ref[...], s2_ref[...]
    for j in range(2 * A_WIDTH // LANES):
        xg = acc[:, j * LANES:(j + 1) * LANES]
        if j < A_WIDTH // LANES:
            xg = xg * (A_HEAD_DIM ** -0.5)
        qkv_ref[j] = xg * c + pltpu.roll(xg, LANES - ROPE_DIM // 2, 1) * s1 + pltpu.roll(xg, ROPE_DIM // 2, 1) * s2
    for j in range(2 * A_WIDTH // LANES, 3 * A_WIDTH // LANES):
        qkv_ref[j] = acc[:, j * LANES:(j + 1) * LANES]
    conv_ref[...] = acc[:, 3 * A_WIDTH:].astype(BF16)


def _even_in(h, gain, w, tabs, seq):
    t = h.shape[0]
    tm = ROW_TILE
    nseq = seq // tm
    tab_spec = pl.BlockSpec((tm, LANES), lambda i: (i % nseq, 0))
    return pl.pallas_call(
        _even_in_kernel,
        grid=(t // tm,),
        in_specs=[
            pl.BlockSpec((tm, D_MODEL), lambda i: (i, 0)),
            pl.BlockSpec((1, D_MODEL), lambda i: (0, 0)),
            pl.BlockSpec((None, D_MODEL, EVEN_IN_WIDTH), lambda i: (0, 0, 0)),
            tab_spec, tab_spec, tab_spec,
        ],
        out_specs=[
            pl.BlockSpec((3 * A_WIDTH // LANES, tm, LANES), lambda i: (0, i, 0)),
            pl.BlockSpec((tm, 2 * B_WIDTH), lambda i: (i, 0)),
        ],
        out_shape=[
            jax.ShapeDtypeStruct((3 * A_WIDTH // LANES, t, LANES), F32),
            jax.ShapeDtypeStruct((t, 2 * B_WIDTH), BF16),
        ],
        scratch_shapes=[pltpu.VMEM((D_MODEL, EVEN_IN_WIDTH), BF16)],
        compiler_params=_cparams(("arbitrary",)),
        name="even_in_proj",
    )(h, gain.reshape(1, D_MODEL), w, *tabs)


def _attn_rope_tables(seq):
    half = ROPE_DIM // 2
    inv_freq = ROPE_THETA ** (-jnp.arange(0, ROPE_DIM, 2, dtype=F32) / ROPE_DIM)
    ang = jnp.arange(seq, dtype=F32)[:, None] * inv_freq[None, :]
    cos, sin = jnp.cos(ang), jnp.sin(ang)
    rest = A_HEAD_DIM - ROPE_DIM
    ones = jnp.ones((seq, rest), F32)
    z_rest = jnp.zeros((seq, rest), F32)
    z_half = jnp.zeros((seq, half), F32)
    c = jnp.concatenate([cos, cos, ones], axis=1)
    s1 = jnp.concatenate([-sin, z_half, z_rest], axis=1)
    s2 = jnp.concatenate([z_half, sin, z_rest], axis=1)
    rep = LANES // A_HEAD_DIM
    return tuple(jnp.tile(a, (1, rep)) for a in (c, s1, s2))


ATTN_STEP = ATTN_BLOCK * max(d for _, d in A_BRANCHES)
ATTN_PAIRS = A_WIDTH // LANES
ATTN_MERGE_ROWS = 256


def _attn_kernel(q_ref, k_ref, v_ref, o_ref, kprev, vprev, oml):
    n = pl.program_id(1)
    blk = ATTN_BLOCK
    n_blocks = ATTN_STEP // blk

    @pl.when(n == 0)
    def _():
        kprev[...] = jnp.zeros(kprev.shape, BF16)
        vprev[...] = jnp.zeros(vprev.shape, BF16)

    qi = lax.broadcasted_iota(I32, (blk, 2 * blk), 0)
    kj = lax.broadcasted_iota(I32, (blk, 2 * blk), 1)
    dist = blk + qi - kj
    band = (dist >= 0) & (dist <= blk)
    in_cur = kj >= blk
    lane = lax.broadcasted_iota(I32, (blk, LANES), 1)
    lo = lane < A_HEAD_DIM

    off = 0
    for bi, (_, dil) in enumerate(A_BRANCHES):
        shift = dil.bit_length() - 1
        assert dil == 1 << shift

        def one_block(idx, carried, bi=bi, dil=dil, shift=shift, off=off):
            u = lax.shift_right_logical(idx, shift)
            r = idx & (dil - 1)
            start = u * (blk * dil) + r
            rows = pl.ds(pl.multiple_of(start, blk), blk) if dil == 1 else pl.ds(start, blk, stride=dil)
            has_prev = (n > 0) | (u > 0)
            valid = band & (in_cur | has_prev)
            valid2 = jnp.concatenate([valid, valid], axis=0)
            current = []
            for g in range(ATTN_PAIRS):
                q = q_ref[g, rows, :]
                kc = k_ref[g, rows, :].astype(BF16)
                vc = v_ref[g, rows, :].astype(BF16)
                current.append((kc, vc))
                kp, vp = (kprev[g, off + r], vprev[g, off + r]) if carried is None else carried[g]
                q2 = jnp.concatenate([jnp.where(lo, q, 0.0), jnp.where(lo, 0.0, q)], axis=0).astype(BF16)
                kk = jnp.concatenate([kp, kc], axis=0)
                vv = jnp.concatenate([vp, vc], axis=0)
                s = lax.dot_general(q2, kk, (((1,), (1,)), ((), ())), preferred_element_type=F32)
                s = jnp.where(valid2, s, NEG_INF)
                m = jnp.max(s, axis=-1, keepdims=True)
                e = jnp.exp(s - m)
                den = jnp.sum(e, axis=-1, keepdims=True)
                pv = jnp.dot(e.astype(BF16), vv, preferred_element_type=F32)
                o_new = jnp.where(lo, pv[:blk], pv[blk:])
                m_new = jnp.where(lo, m[:blk], m[blk:])
                l_new = jnp.where(lo, den[:blk], den[blk:])
                if bi > 0:
                    o_run, m_run, l_run = oml[0, g, rows, :], oml[1, g, rows, :], oml[2, g, rows, :]
                    m_both = jnp.maximum(m_run, m_new)
                    w_run, w_new = jnp.exp(m_run - m_both), jnp.exp(m_new - m_both)
                    o_new = w_run * o_run + w_new * o_new
                    l_new = w_run * l_run + w_new * l_new
                    m_new = m_both
                oml[0, g, rows, :] = o_new
                oml[1, g, rows, :] = m_new
                oml[2, g, rows, :] = l_new
            return r, current

        def save_prev(r, current, off=off):
            for g, (kc, vc) in enumerate(current):
                kprev[g, off + r] = kc
                vprev[g, off + r] = vc

        def block_body(i, c, dil=dil):
            r0, cur0 = one_block(2 * i, None)
            if dil == 1:
                r1, cur1 = one_block(2 * i + 1, cur0)
            else:
                save_prev(r0, cur0)
                r1, cur1 = one_block(2 * i + 1, None)
            save_prev(r1, cur1)
            return c

        lax.fori_loop(0, n_blocks // 2, block_body, 0)
        off += dil
    for g in range(ATTN_PAIRS):
        for c in range(ATTN_STEP // ATTN_MERGE_ROWS):
            sl = slice(c * ATTN_MERGE_ROWS, (c + 1) * ATTN_MERGE_ROWS)
            o_ref[g, sl, :] = (oml[0, g, sl, :] / oml[2, g, sl, :]).astype(BF16)


def _attention(qkv, batch, seq):
    t = qkv.shape[1]
    steps = seq // ATTN_STEP
    n_res = sum(d for _, d in A_BRANCHES)

    def slabs(which):
        return pl.BlockSpec((ATTN_PAIRS, ATTN_STEP, LANES), lambda b, n: (which, b * steps + n, 0))

    return pl.pallas_call(
        _attn_kernel,
        grid=(batch, steps),
        in_specs=[slabs(0), slabs(1), slabs(2)],
        out_specs=pl.BlockSpec((ATTN_PAIRS, ATTN_STEP, LANES), lambda b, n: (0, b * steps + n, 0)),
        out_shape=jax.ShapeDtypeStruct((ATTN_PAIRS, t, LANES), BF16),
        scratch_shapes=[
            pltpu.VMEM((ATTN_PAIRS, n_res, ATTN_BLOCK, LANES), BF16),
            pltpu.VMEM((ATTN_PAIRS, n_res, ATTN_BLOCK, LANES), BF16),
            pltpu.VMEM((3, ATTN_PAIRS, ATTN_STEP, LANES), F32),
        ],
        compiler_params=_cparams(("parallel", "arbitrary")),
        name="dilated_attention",
    )(qkv, qkv, qkv)


def _conv_kernel(val_ref, gate_ref, w_ref, b_ref, g_ref, beta_ref, o_ref, abuf, shifted):
    n = pl.program_id(1)
    tc = CONV_TILE
    sub = 8

    @pl.when(n == 0)
    def _():
        abuf[0:CONV_HALO, :] = jnp.zeros((CONV_HALO, B_WIDTH), F32)

    @pl.when(n > 0)
    def _():
        abuf[0:CONV_HALO, :] = abuf[tc:tc + CONV_HALO, :]

    val = val_ref[...].astype(F32)
    gate = gate_ref[...].astype(F32)
    abuf[CONV_HALO:CONV_HALO + tc, :] = val * jax.nn.sigmoid(gate)
    off = CONV_HALO - (CONV_WIDTH - 1)
    span = shifted.shape[1]
    for s in range(1, sub):
        shifted[s - 1] = abuf[s:s + span, :]
    for c in range(tc // CONV_CHUNK):
        acc = jnp.broadcast_to(b_ref[...], (CONV_CHUNK, B_WIDTH))
        for j in range(CONV_WIDTH):
            s = (off + j) % sub
            r0 = c * CONV_CHUNK + off + j - s
            src = abuf[r0:r0 + CONV_CHUNK, :] if s == 0 else shifted[s - 1, r0:r0 + CONV_CHUNK, :]
            acc = acc + w_ref[j:j + 1, :] * src
        mu = jnp.mean(acc, axis=-1, keepdims=True)
        cen = acc - mu
        var = jnp.mean(cen * cen, axis=-1, keepdims=True)
        yn = cen * lax.rsqrt(var + NORM_EPS) * g_ref[...] + beta_ref[...]
        o_ref[c * CONV_CHUNK:(c + 1) * CONV_CHUNK, :] = (yn * jax.nn.sigmoid(yn)).astype(BF16)


def _conv(conv_in, w, b, g, beta, batch, seq):
    t = conv_in.shape[0]
    tc = CONV_TILE
    nt = seq // tc
    w_pad = jnp.concatenate([w, jnp.zeros((CONV_HALO - CONV_WIDTH, B_WIDTH), F32)], axis=0)
    vec = pl.BlockSpec((1, B_WIDTH), lambda bb, n: (0, 0))
    return pl.pallas_call(
        _conv_kernel,
        grid=(batch, nt),
        in_specs=[
            pl.BlockSpec((tc, B_WIDTH), lambda bb, n: (bb * nt + n, 0)),
            pl.BlockSpec((tc, B_WIDTH), lambda bb, n: (bb * nt + n, 1)),
            pl.BlockSpec((CONV_HALO, B_WIDTH), lambda bb, n: (0, 0)),
            vec, vec, vec,
        ],
        out_specs=pl.BlockSpec((tc, B_WIDTH), lambda bb, n: (bb * nt + n, 0)),
        out_shape=jax.ShapeDtypeStruct((t, B_WIDTH), BF16),
        scratch_shapes=[pltpu.VMEM((tc + CONV_HALO, B_WIDTH), F32),
                        pltpu.VMEM((7, tc + CONV_HALO - 8, B_WIDTH), F32)],
        compiler_params=_cparams(("parallel", "arbitrary")),
        name="conformer_conv",
    )(conv_in, conv_in, w_pad, b.reshape(1, -1), g.reshape(1, -1), beta.reshape(1, -1))


def _even_out_kernel(h_ref, a_ref, bc_ref, w_ref, out_ref, wb_ref):
    _cast_once(w_ref, wb_ref, pl.program_id(0) == 0)
    a = jnp.concatenate([a_ref[g] for g in range(ATTN_PAIRS)], axis=1)
    acc = jnp.dot(a, wb_ref[0:A_WIDTH, :], preferred_element_type=F32)
    acc = acc + jnp.dot(bc_ref[...], wb_ref[A_WIDTH:, :], preferred_element_type=F32)
    out_ref[...] = h_ref[...] + acc


def _even_out(h, attn, bconv, w):
    t = h.shape[0]
    tm = ROW_TILE
    row = lambda width: pl.BlockSpec((tm, width), lambda i: (i, 0))
    return pl.pallas_call(
        _even_out_kernel,
        grid=(t // tm,),
        in_specs=[row(D_MODEL), pl.BlockSpec((ATTN_PAIRS, tm, LANES), lambda i: (0, i, 0)), row(B_WIDTH),
                  pl.BlockSpec((None, D_MODEL, D_MODEL), lambda i: (0, 0, 0))],
        out_specs=row(D_MODEL),
        out_shape=jax.ShapeDtypeStruct((t, D_MODEL), F32),
        scratch_shapes=[pltpu.VMEM((D_MODEL, D_MODEL), BF16)],
        compiler_params=_cparams(("arbitrary",)),
        name="even_out_proj",
    )(h, attn, bconv, w)


def _odd_in_kernel(h_ref, g_ref, w_ref, cos_ref, sin_ref, o_ref, wb_ref):
    j = pl.program_id(0)
    _cast_once(w_ref, wb_ref, pl.program_id(1) == 0)
    u = _rms(h_ref[...], g_ref[...]).astype(BF16)
    acc = jnp.dot(u, wb_ref[...], preferred_element_type=F32)

    @pl.when(j == 0)
    def _():
        cos, sin = cos_ref[...], sin_ref[...]
        half = C_QK_DIM // 2
        for hd in range(2 * C_HEADS):
            x1 = acc[:, hd * C_QK_DIM:hd * C_QK_DIM + half]
            x2 = acc[:, hd * C_QK_DIM + half:(hd + 1) * C_QK_DIM]
            r1 = x1 * cos - x2 * sin
            r2 = x2 * cos + x1 * sin
            if hd >= C_HEADS:
                r1 = r1 * (C_QK_DIM ** -0.5)
                r2 = r2 * (C_QK_DIM ** -0.5)
            o_ref[:, hd * C_QK_DIM:hd * C_QK_DIM + half] = r1.astype(BF16)
            o_ref[:, hd * C_QK_DIM + half:(hd + 1) * C_QK_DIM] = r2.astype(BF16)

    @pl.when(j > 0)
    def _():
        o_ref[...] = acc.astype(BF16)


def _odd_in(h, gain, w, cos, sin, seq):
    t = h.shape[0]
    tm = WIDE_ROW_TILE
    tn = 2 * C_QK_WIDTH
    nseq = seq // tm
    tab = pl.BlockSpec((tm, C_QK_DIM // 2), lambda j, i: (i % nseq, 0))
    return pl.pallas_call(
        _odd_in_kernel,
        grid=(ODD_IN_WIDTH // tn, t // tm),
        in_specs=[
            pl.BlockSpec((tm, D_MODEL), lambda j, i: (i, 0)),
            pl.BlockSpec((1, D_MODEL), lambda j, i: (0, 0)),
            pl.BlockSpec((None, D_MODEL, tn), lambda j, i: (0, 0, j), pipeline_mode=pl.Buffered(1)),
            tab, tab,
        ],
        out_specs=pl.BlockSpec((tm, tn), lambda j, i: (i, j)),
        out_shape=jax.ShapeDtypeStruct((t, ODD_IN_WIDTH), BF16),
        scratch_shapes=[pltpu.VMEM((D_MODEL, tn), BF16)],
        compiler_params=_cparams(("arbitrary", "arbitrary")),
        name="odd_in_proj",
    )(h, gain.reshape(1, D_MODEL), w, cos, sin)


def _ret_kernel(ld_ref, q_ref, k_ref, v_ref, g_ref, o_ref, state):
    c = pl.program_id(0)
    ch = RET_CHUNK
    batch = q_ref.shape[0]

    @pl.when(c == 0)
    def _():
        state[...] = jnp.zeros(state.shape, F32)

    ii = lax.broadcasted_iota(I32, (ch, ch), 0)
    jj = lax.broadcasted_iota(I32, (ch, ch), 1)
    diff = (ii - jj).astype(F32)
    pos = lax.broadcasted_iota(I32, (ch, 1), 0).astype(F32)
    for hd in range(C_HEADS):
        ld = ld_ref[hd]
        intra = jnp.where(diff >= 0, jnp.exp(ld * jnp.maximum(diff, 0.0)), 0.0)
        q_decay = jnp.exp(ld * (pos + 1.0))
        k_decay = jnp.exp(ld * (ch - 1.0 - pos))
        chunk_decay = jnp.exp(ld * jnp.full((1, 1), float(ch), F32))
        for bb in range(batch):
            q = q_ref[bb, :, hd * C_QK_DIM:(hd + 1) * C_QK_DIM]
            k = k_ref[bb, :, hd * C_QK_DIM:(hd + 1) * C_QK_DIM]
            v = v_ref[bb, :, hd * C_V_DIM:(hd + 1) * C_V_DIM]
            s = lax.dot_general(q, k, (((1,), (1,)), ((), ())), preferred_element_type=F32) * intra
            inner = jnp.dot(s.astype(BF16), v, preferred_element_type=F32)
            st = state[bb, hd]
            cross = jnp.dot(q, st.astype(BF16), preferred_element_type=F32) * q_decay
            kd_t = jnp.transpose(k.astype(F32) * k_decay).astype(BF16)
            state[bb, hd] = st * chunk_decay + jnp.dot(kd_t, v, preferred_element_type=F32)
            out = inner + cross
            mu = jnp.mean(out, axis=-1, keepdims=True)
            cen = out - mu
            var = jnp.mean(cen * cen, axis=-1, keepdims=True)
            o = cen * lax.rsqrt(var + NORM_EPS)
            gf = g_ref[bb, :, hd * C_V_DIM:(hd + 1) * C_V_DIM].astype(F32)
            o_ref[bb, :, hd * C_V_DIM:(hd + 1) * C_V_DIM] = (gf * jax.nn.sigmoid(gf) * o).astype(BF16)


def _retention(proj, log_decay, batch, seq):
    ch = RET_CHUNK
    nc = seq // ch
    proj3 = proj.reshape(batch, seq, ODD_IN_WIDTH)
    v0 = 2 * C_QK_WIDTH // C_V_WIDTH
    grid_spec = pltpu.PrefetchScalarGridSpec(
        num_scalar_prefetch=1,
        grid=(nc,),
        in_specs=[
            pl.BlockSpec((batch, ch, C_QK_WIDTH), lambda c, ld: (0, c, 0)),
            pl.BlockSpec((batch, ch, C_QK_WIDTH), lambda c, ld: (0, c, 1)),
            pl.BlockSpec((batch, ch, C_V_WIDTH), lambda c, ld: (0, c, v0)),
            pl.BlockSpec((batch, ch, C_V_WIDTH), lambda c, ld: (0, c, v0 + 1)),
        ],
        out_specs=pl.BlockSpec((batch, ch, C_V_WIDTH), lambda c, ld: (0, c, 0)),
        scratch_shapes=[pltpu.VMEM((batch, C_HEADS, C_QK_DIM, C_V_DIM), F32)],
    )
    y = pl.pallas_call(
        _ret_kernel,
        grid_spec=grid_spec,
        out_shape=jax.ShapeDtypeStruct((batch, seq, C_V_WIDTH), BF16),
        compiler_params=_cparams(("arbitrary",)),
        name="retention",
    )(log_decay, proj3, proj3, proj3, proj3)
    return y.reshape(batch * seq, C_V_WIDTH)


def _odd_out_kernel(h_ref, y_ref, w_ref, out_ref, wb_ref):
    _cast_once(w_ref, wb_ref, pl.program_id(0) == 0)
    out_ref[...] = h_ref[...] + jnp.dot(y_ref[...], wb_ref[...], preferred_element_type=F32)


def _odd_out(h, y, w):
    t = h.shape[0]
    tm = WIDE_ROW_TILE
    return pl.pallas_call(
        _odd_out_kernel,
        grid=(t // tm,),
        in_specs=[pl.BlockSpec((tm, D_MODEL), lambda i: (i, 0)),
                  pl.BlockSpec((tm, C_V_WIDTH), lambda i: (i, 0)),
                  pl.BlockSpec((None, C_V_WIDTH, D_MODEL), lambda i: (0, 0, 0), pipeline_mode=pl.Buffered(1))],
        out_specs=pl.BlockSpec((tm, D_MODEL), lambda i: (i, 0)),
        out_shape=jax.ShapeDtypeStruct((t, D_MODEL), F32),
        scratch_shapes=[pltpu.VMEM((C_V_WIDTH, D_MODEL), BF16)],
        compiler_params=_cparams(("arbitrary",)),
        name="odd_out_proj",
    )(h, y, w)


def _router_kernel(h_ref, g_ref, wr_ref, br_ref, x2s_ref, codes_ref, gates_ref, cnt_ref, base_ref, *,
                   tiles_per_batch):
    i = pl.program_id(0)
    tm = h_ref.shape[0]
    batch = i // tiles_per_batch

    @pl.when(i % tiles_per_batch == 0)
    def _():
        base_ref[...] = jnp.zeros(base_ref.shape, F32)

    x2 = _rms(h_ref[...], g_ref[...])
    for j in range(SLABS):
        x2s_ref[pl.ds(j, tm, stride=SLABS), :] = x2[:, j * LANES:(j + 1) * LANES]
    wr = wr_ref[...]
    x_hi, w_hi = x2.astype(BF16), wr.astype(BF16)
    x_lo = (x2 - x_hi.astype(F32)).astype(BF16)
    w_lo = (wr - w_hi.astype(F32)).astype(BF16)
    both = jnp.dot(x_hi, jnp.concatenate([w_hi, w_lo], axis=1), preferred_element_type=F32)
    logits = both[:, :LANES] + both[:, LANES:] + jnp.dot(x_lo, w_hi, preferred_element_type=F32) + br_ref[...]
    lane = lax.broadcasted_iota(I32, (tm, LANES), 1)
    is_grp = lane < MOE_GROUPS
    lg = jnp.where(is_grp, logits, -jnp.inf)
    gmax = jnp.max(lg, axis=-1, keepdims=True)
    gsum = jnp.sum(jnp.where(is_grp, jnp.exp(logits - gmax), 0.0), axis=-1, keepdims=True)
    gp = 1.0 / gsum
    gi = jnp.min(jnp.where(lg == gmax, lane, LANES), axis=-1, keepdims=True)
    ex = lane - ROUTER_LANE0
    in_grp = (ex >= 0) & (ex < MOE_EXPERTS) & ((ex // MOE_EXPERTS_PER_GROUP) == gi)
    sel = jnp.where(in_grp, logits, -jnp.inf)
    v1 = jnp.max(sel, axis=-1, keepdims=True)
    i1 = jnp.min(jnp.where(sel == v1, lane, LANES), axis=-1, keepdims=True)
    sel2 = jnp.where(lane == i1, -jnp.inf, sel)
    v2 = jnp.max(sel2, axis=-1, keepdims=True)
    i2 = jnp.min(jnp.where(sel2 == v2, lane, LANES), axis=-1, keepdims=True)
    tt = jnp.exp(v2 - v1)
    g0 = gp / (1.0 + tt)
    g1 = gp * tt / (1.0 + tt)
    oh0 = (lane == i1).astype(F32)
    oh1 = (lane == i2).astype(F32)
    oh = oh0 + oh1
    rr = lax.broadcasted_iota(I32, (tm, tm), 0)
    cc = lax.broadcasted_iota(I32, (tm, tm), 1)
    tri = (cc < rr).astype(BF16)
    prefix = jnp.dot(tri, oh.astype(BF16), preferred_element_type=F32)
    tot = base_ref[...] + prefix
    rank0 = jnp.sum(oh0 * tot, axis=-1, keepdims=True).astype(I32)
    rank1 = jnp.sum(oh1 * tot, axis=-1, keepdims=True).astype(I32)
    base_ref[...] = base_ref[...] + jnp.sum(oh, axis=0, keepdims=True)
    eoff = batch * MOE_EXPERTS - ROUTER_LANE0
    code0 = (i1 + eoff) * (1 << RANK_BITS) + rank0
    code1 = (i2 + eoff) * (1 << RANK_BITS) + rank1
    meta = jnp.where(lane == 0, code0, jnp.where(lane == 1, code1, 0))
    codes_ref[...] = jnp.transpose(meta)[0:8, :]
    gates_ref[...] = jnp.where(lane == 0, g0, jnp.where(lane == 1, g1, 0.0))
    cnt_ref[...] = jnp.broadcast_to(base_ref[...], cnt_ref.shape)


def _router(h, gain, wr, br, batch):
    t = h.shape[0]
    tm = ROW_TILE
    tpb = t // batch // tm
    row = lambda width: pl.BlockSpec((tm, width), lambda i: (i, 0))
    fixed = lambda shape: pl.BlockSpec(shape, lambda i: (0, 0))
    return pl.pallas_call(
        functools.partial(_router_kernel, tiles_per_batch=tpb),
        grid=(t // tm,),
        in_specs=[row(D_MODEL), fixed((1, D_MODEL)), fixed((D_MODEL, LANES)), fixed((1, LANES))],
        out_specs=[pl.BlockSpec((tm * SLABS, LANES), lambda i: (i, 0)),
                   pl.BlockSpec((8, tm), lambda i: (0, i)), row(LANES),
                   pl.BlockSpec((8, LANES), lambda i: (i // tpb, 0))],
        out_shape=[
            jax.ShapeDtypeStruct((t * SLABS, LANES), F32),
            jax.ShapeDtypeStruct((8, t), I32),
            jax.ShapeDtypeStruct((t, LANES), F32),
            jax.ShapeDtypeStruct((batch * 8, LANES), F32),
        ],
        scratch_shapes=[pltpu.VMEM((1, LANES), F32)],
        compiler_params=_cparams(("arbitrary",)),
        name="moe_router",
    )(h, gain.reshape(1, D_MODEL), wr, br)


INVERT_UNROLL = 16


def _dest_kernel(pstart_ref, codes_ref, dest_ref):
    codes = codes_ref[...]
    seg = lax.shift_right_logical(codes, RANK_BITS)
    dest = codes & ((1 << RANK_BITS) - 1)
    for i in range(pstart_ref.shape[0]):
        dest = dest + jnp.where(seg == i, pstart_ref[i], 0)
    dest_ref[...] = dest


def _dest_rows(pad_start, codes):
    grid_spec = pltpu.PrefetchScalarGridSpec(
        num_scalar_prefetch=1,
        grid=(1,),
        in_specs=[pl.BlockSpec(codes.shape, lambda i, ps: (0, 0))],
        out_specs=pl.BlockSpec(codes.shape, lambda i, ps: (0, 0)),
    )
    return pl.pallas_call(
        _dest_kernel,
        grid_spec=grid_spec,
        out_shape=jax.ShapeDtypeStruct(codes.shape, I32),
        compiler_params=_cparams(("arbitrary",)),
        name="moe_dest_rows",
    )(pad_start, codes)


def _invert_kernel(trips_ref, dest_ref, dummy_ref, slot_ref, *, batch):
    per_batch = dest_ref.shape[0] // batch
    pltpu.sync_copy(dummy_ref, slot_ref)
    for b in range(batch):
        def place(i, carry, b=b):
            for v in range(INVERT_UNROLL):
                a = i * INVERT_UNROLL + v
                slot_ref[dest_ref[b * per_batch + a]] = a
            return carry

        lax.fori_loop(0, trips_ref[0], place, 0)


def _dummy_slots(n_rows, per_batch):
    assert MOE_BLOCK & (MOE_BLOCK - 1) == 0 and 3 * MOE_BLOCK <= DUMMY_SLOTS
    r = jnp.arange(n_rows + MOE_BLOCK, dtype=I32)
    pattern = per_batch + (r // MOE_BLOCK) % 2 * MOE_BLOCK + r % MOE_BLOCK
    return jnp.where(r < n_rows, pattern, per_batch + 2 * MOE_BLOCK + r % MOE_BLOCK)


def _invert(dest_flat, n_rows, batch):
    smem = pl.BlockSpec(memory_space=pltpu.SMEM)
    per_batch = dest_flat.shape[0] // batch
    assert per_batch % INVERT_UNROLL == 0
    trips = jnp.full((1,), per_batch // INVERT_UNROLL, I32)
    return pl.pallas_call(
        functools.partial(_invert_kernel, batch=batch),
        in_specs=[smem, smem, pl.BlockSpec(memory_space=pl.ANY)],
        out_specs=smem,
        out_shape=jax.ShapeDtypeStruct((n_rows + MOE_BLOCK,), I32),
        name="moe_invert_rows",
    )(trips, dest_flat, _dummy_slots(n_rows, per_batch))


def _expert_kernel(be_ref, run_ref, nxt_ref, nreal_ref, slot_ref, x2s_ref, wg_hbm, wu_hbm, wd_hbm, ys_ref,
                   xres, wbuf_g, wbuf_u, wbuf_d, wgb_ref, wub_ref, wdb_ref, tile, xb, ybuf, sems, wsems, *,
                   batch, nb, layer):
    b = pl.program_id(0)
    n = pl.program_id(1)
    g = b * nb + n
    last = batch * nb - 1
    blk = MOE_BLOCK
    seq = xres.shape[0] // SLABS
    bstride = 2 * seq + DUMMY_SLOTS
    tcur = g % 2
    ycur = g % 3
    yprev = (g + 2) % 3

    def weight_copies(e, ws):
        pairs = ((wg_hbm, wbuf_g), (wu_hbm, wbuf_u), (wd_hbm, wbuf_d))
        return [pltpu.make_async_copy(src.at[layer, e], dst.at[ws], wsems.at[ws, k])
                for k, (src, dst) in enumerate(pairs)]

    def wait_block(q):
        pltpu.make_async_copy(ybuf.at[q], ys_ref.at[pl.ds(0, blk * SLABS), :], sems.at[q]).wait()

    def gather_row(block, tslot, mi):
        s = slot_ref[block * blk + mi]
        if seq & (seq - 1) == 0:
            tok = s & (seq - 1)
        else:
            tok = jnp.minimum(jnp.where(s >= seq, s - seq, s), seq - 1)
        slab = xres[pl.ds(pl.multiple_of(tok * SLABS, SLABS), SLABS), :]
        tile[tslot, pl.ds(mi, SLABS, stride=GATHER_PITCH), :] = slab

    def to_bf16(tslot):
        for j in range(SLABS):
            xb[tslot, :, j * LANES:(j + 1) * LANES] = tile[tslot, j * GATHER_PITCH:j * GATHER_PITCH + blk, :].astype(BF16)

    def scatter_row(yslot, slot, row0, mi):
        dst = pl.multiple_of((row0 + slot) * SLABS, SLABS)
        pltpu.make_async_copy(ybuf.at[yslot, pl.ds(mi * SLABS, SLABS), :], ys_ref.at[pl.ds(dst, SLABS), :],
                              sems.at[yslot]).start()

    @pl.when(g == 0)
    def _():
        for c in weight_copies(be_ref[0], 0):
            c.start()
        ybuf[...] = jnp.zeros(ybuf.shape, F32)
        for bb in range(batch):
            for c in range(DUMMY_SLOTS // blk):
                start = (bb * bstride + 2 * seq + c * blk) * SLABS
                zero = pltpu.make_async_copy(ybuf.at[0], ys_ref.at[pl.ds(start, blk * SLABS), :], sems.at[0])
                zero.start()
                zero.wait()

    @pl.when(n == 0)
    def _():
        pltpu.sync_copy(x2s_ref.at[pl.ds(pl.multiple_of(b * (seq * SLABS), SLABS), seq * SLABS), :], xres)
        for mi in range(blk):
            gather_row(g, tcur, mi)
        to_bf16(tcur)

    run = run_ref[g]
    ws = run % 2

    @pl.when((g == 0) | (run != run_ref[jnp.maximum(g - 1, 0)]))
    def _():
        for c in weight_copies(be_ref[g], ws):
            c.wait()
        wgb_ref[...] = wbuf_g[ws].astype(BF16)
        wub_ref[...] = wbuf_u[ws].astype(BF16)
        wdb_ref[...] = wbuf_d[ws].astype(BF16)

        @pl.when(nxt_ref[g] >= 0)
        def _():
            for c in weight_copies(nxt_ref[g], 1 - ws):
                c.start()

    @pl.when(g >= 2)
    def _():
        wait_block(ycur)

    nxt_block = jnp.minimum(g + 1, last)
    prev_block = jnp.where(g == 0, batch * nb, g - 1)
    prev_row0 = jnp.where(n == 0, jnp.maximum(b - 1, 0), b) * bstride
    n_chunks = 8
    per = blk // n_chunks
    half = EXPERT_HIDDEN // 2
    quarter = D_MODEL // 4

    @pl.when(n < nreal_ref[b])
    def _():
        x = xb[tcur]
        acts = []
        hid = None
        for c in range(n_chunks):
            for mi in range(c * per, (c + 1) * per):
                gather_row(nxt_block, 1 - tcur, mi)
            if c < 4:
                for mi in range(2 * c * per, 2 * (c + 1) * per):
                    scatter_row(yprev, slot_ref[prev_block * blk + mi], prev_row0, mi)
                w_ref = wgb_ref if c < 2 else wub_ref
                acts.append(jnp.dot(x, w_ref[:, (c % 2) * half:(c % 2 + 1) * half], preferred_element_type=F32))
            if c == 3:
                a = jnp.concatenate(acts[0:2], axis=1)
                u = jnp.concatenate(acts[2:4], axis=1)
                hid = (a * jax.nn.sigmoid(a) * u).astype(BF16)
            if c >= 4:
                q = c - 4
                yq = jnp.dot(hid, wdb_ref[:, q * quarter:(q + 1) * quarter], preferred_element_type=F32)
                for jj in range(quarter // LANES):
                    j = q * (quarter // LANES) + jj
                    ybuf[ycur, pl.ds(j, blk, stride=SLABS), :] = yq[:, jj * LANES:(jj + 1) * LANES]
        to_bf16(1 - tcur)

    @pl.when(n >= nreal_ref[b])
    def _():
        for mi in range(blk):
            scatter_row(yprev, slot_ref[prev_block * blk + mi], prev_row0, mi)

    @pl.when(g == last)
    def _():
        wait_block((last - 2) % 3)
        for mi in range(blk):
            scatter_row(last % 3, slot_ref[last * blk + mi], (batch - 1) * bstride, mi)
        wait_block((last - 1) % 3)
        wait_block(last % 3)


def _experts(block_expert, n_real, row_slot, x2s, wg, wu, wd, layer, batch, seq):
    blk = MOE_BLOCK
    nb = (row_slot.shape[0] // blk - 1) // batch
    steps = batch * nb
    assert steps >= 3
    idx = jnp.arange(steps, dtype=I32)
    change = jnp.concatenate([jnp.zeros((1,), I32), (block_expert[1:] != block_expert[:-1]).astype(I32)])
    run = jnp.cumsum(change).astype(I32)
    later_change = (idx[None, :] > idx[:, None]) & (change[None, :] > 0)
    nxt_idx = jnp.min(jnp.where(later_change, idx[None, :], steps), axis=1)
    nxt = jnp.where(nxt_idx < steps, block_expert[jnp.minimum(nxt_idx, steps - 1)], -1).astype(I32)
    hbm = pl.BlockSpec(memory_space=pl.ANY)
    grid_spec = pltpu.PrefetchScalarGridSpec(
        num_scalar_prefetch=5,
        grid=(batch, nb),
        in_specs=[hbm, hbm, hbm, hbm],
        out_specs=hbm,
        scratch_shapes=[
            pltpu.VMEM((seq * SLABS, LANES), F32),
            pltpu.VMEM((2, D_MODEL, EXPERT_HIDDEN), F32), pltpu.VMEM((2, D_MODEL, EXPERT_HIDDEN), F32),
            pltpu.VMEM((2, EXPERT_HIDDEN, D_MODEL), F32),
            pltpu.VMEM((D_MODEL, EXPERT_HIDDEN), BF16), pltpu.VMEM((D_MODEL, EXPERT_HIDDEN), BF16),
            pltpu.VMEM((EXPERT_HIDDEN, D_MODEL), BF16),
            pltpu.VMEM((2, SLABS * GATHER_PITCH, LANES), F32),
            pltpu.VMEM((2, blk, D_MODEL), BF16),
            pltpu.VMEM((3, blk * SLABS, LANES), F32),
            pltpu.SemaphoreType.DMA((3,)),
            pltpu.SemaphoreType.DMA((2, 3)),
        ],
    )
    return pl.pallas_call(
        functools.partial(_expert_kernel, batch=batch, nb=nb, layer=layer),
        grid_spec=grid_spec,
        out_shape=jax.ShapeDtypeStruct((batch * (2 * seq + DUMMY_SLOTS) * SLABS, LANES), F32),
        compiler_params=_cparams(("arbitrary", "arbitrary"), VMEM_LIMIT_EXPERTS),
        name="moe_experts",
    )(block_expert, run, nxt, n_real, row_slot, x2s, wg, wu, wd)


def _combine_kernel(h_ref, gates_ref, y0_ref, y1_ref, *rest, final):
    if final:
        fg_ref, out_ref = rest
    else:
        (out_ref,) = rest
    tm = h_ref.shape[0]
    gates = gates_ref[...]
    g0, g1 = gates[:, 0:1], gates[:, 1:2]
    parts = []
    for j in range(SLABS):
        y0 = y0_ref[pl.ds(j, tm, stride=SLABS), :]
        y1 = y1_ref[pl.ds(j, tm, stride=SLABS), :]
        parts.append(g0 * y0 + g1 * y1)
    out = h_ref[...] + jnp.concatenate(parts, axis=1)
    if final:
        out = _rms(out, fg_ref[...])
    out_ref[...] = out


def _combine(h, gates, ys, batch, seq, final_gain=None):
    t = h.shape[0]
    tm = ROW_TILE
    tpb = seq // tm
    bstride = 2 * tpb + DUMMY_SLOTS // tm
    final = final_gain is not None
    in_specs = [pl.BlockSpec((tm, D_MODEL), lambda i: (i, 0)),
                pl.BlockSpec((tm, LANES), lambda i: (i, 0)),
                pl.BlockSpec((tm * SLABS, LANES), lambda i: (i // tpb * bstride + i % tpb, 0)),
                pl.BlockSpec((tm * SLABS, LANES), lambda i: (i // tpb * bstride + tpb + i % tpb, 0))]
    args = [h, gates, ys, ys]
    if final:
        in_specs.append(pl.BlockSpec((1, D_MODEL), lambda i: (0, 0)))
        args.append(final_gain.reshape(1, D_MODEL))
    return pl.pallas_call(
        functools.partial(_combine_kernel, final=final),
        grid=(t // tm,),
        in_specs=in_specs,
        out_specs=pl.BlockSpec((tm, D_MODEL), lambda i: (i, 0)),
        out_shape=jax.ShapeDtypeStruct((t, D_MODEL), F32),
        compiler_params=_cparams(("parallel",)),
        name="moe_combine_final" if final else "moe_combine",
    )(*args)


def _moe(h, gain, w_r1, b_r1, w_r2, b_r2, wg, wu, wd, layer, batch, final_gain=None):
    t = h.shape[0]
    seq = t // batch
    assert seq <= (1 << RANK_BITS) and seq % ROW_TILE == 0 and DUMMY_SLOTS % ROW_TILE == 0
    pad_w = jnp.zeros((D_MODEL, LANES - MOE_GROUPS - MOE_EXPERTS), F32)
    wr = jnp.concatenate([w_r1, w_r2, pad_w], axis=1)
    br = jnp.concatenate([b_r1, b_r2, jnp.zeros((LANES - MOE_GROUPS - MOE_EXPERTS,), F32)]).reshape(1, LANES)
    x2s, codes, gates, counts = _router(h, gain, wr, br, batch)
    cnt = counts.reshape(batch, 8, LANES)[:, 0, ROUTER_LANE0:ROUTER_LANE0 + MOE_EXPERTS].astype(I32)
    padded = (cnt + MOE_BLOCK - 1) // MOE_BLOCK * MOE_BLOCK
    pad_end = jnp.cumsum(padded, axis=1)
    rows_pb = seq * 2 + MOE_EXPERTS * MOE_BLOCK
    nb = rows_pb // MOE_BLOCK
    pad_start = pad_end - padded + (jnp.arange(batch, dtype=I32) * rows_pb)[:, None]
    block_start = jnp.arange(nb, dtype=I32) * MOE_BLOCK
    block_expert = jnp.minimum(
        jnp.sum((pad_end[:, None, :] <= block_start[None, :, None]).astype(I32), axis=2), MOE_EXPERTS - 1)
    dest = _dest_rows(pad_start.reshape(-1), codes)
    dest_flat = dest[0:2].reshape(2, batch, seq).transpose(1, 0, 2).reshape(-1)
    row_slot = _invert(dest_flat, batch * rows_pb, batch)
    n_real = (pad_end[:, -1] // MOE_BLOCK).astype(I32)
    ys = _experts(block_expert.reshape(-1), n_real, row_slot, x2s, wg, wu, wd, layer, batch, seq)
    return _combine(h, gates, ys, batch, seq, final_gain)


def kernel(x, mix_norm, ffn_norm, final_norm, even_w_in, even_w_out, conv_w, conv_b, conv_norm_g, conv_norm_b,
           odd_w_in, odd_w_out, router_w1, router_b1, router_w2, router_b2, expert_w_gate, expert_w_up,
           expert_w_down):
    batch, seq, d = x.shape
    assert d == D_MODEL and seq % (ATTN_BLOCK * max(dl for _, dl in A_BRANCHES)) == 0
    assert all(w // dl == ATTN_BLOCK for w, dl in A_BRANCHES)
    t = batch * seq
    h = x.reshape(t, d)

    qkv, conv_in = _even_in(h, mix_norm[0], even_w_in, _attn_rope_tables(seq), seq)
    attn = _attention(qkv, batch, seq)
    bconv = _conv(conv_in, conv_w[0], conv_b[0], conv_norm_g[0], conv_norm_b[0], batch, seq)
    h = _even_out(h, attn, bconv, even_w_out)
    h = _moe(h, ffn_norm[0], router_w1[0], router_b1[0], router_w2[0], router_b2[0],
             expert_w_gate, expert_w_up, expert_w_down, 0, batch)

    inv_freq = RET_ROT_THETA ** (-jnp.linspace(0.0, 1.0, C_QK_DIM // 2, dtype=F32))
    ang = jnp.arange(seq, dtype=F32)[:, None] * inv_freq[None, :]
    proj = _odd_in(h, mix_norm[1], odd_w_in, jnp.cos(ang), jnp.sin(ang), seq)
    log_decay = jnp.log(1.0 - jnp.exp2(-5.0 - jnp.arange(C_HEADS, dtype=F32)))
    y = _retention(proj, log_decay, batch, seq)
    h = _odd_out(h, y, odd_w_out)
    out = _moe(h, ffn_norm[1], router_w1[1], router_b1[1], router_w2[1], router_b2[1],
               expert_w_gate, expert_w_up, expert_w_down, 1, batch, final_gain=final_norm)
    return out.reshape(batch, seq, d)
```

```python
import functools

import jax
import jax.numpy as jnp
from jax import lax
from jax.experimental import pallas as pl
from jax.experimental.pallas import tpu as pltpu

F32 = jnp.float32
BF16 = jnp.bfloat16
I32 = jnp.int32

NORM_EPS = 1e-6
NEG_INF = -1e30

D_MODEL = 1024
A_HEADS = 8
A_HEAD_DIM = 64
A_WIDTH = A_HEADS * A_HEAD_DIM
A_BRANCHES = ((128, 1), (512, 4), (2048, 16))
ATTN_BLOCK = 128
ROPE_THETA = 500000.0
ROPE_DIM = A_HEAD_DIM // 4
B_WIDTH = D_MODEL - A_WIDTH
CONV_WIDTH = 31
C_HEADS = 4
C_QK_DIM = 256
C_V_DIM = 512
C_QK_WIDTH = C_HEADS * C_QK_DIM
C_V_WIDTH = C_HEADS * C_V_DIM
RET_CHUNK = 128
RET_ROT_THETA = 10000.0
MOE_GROUPS = 4
MOE_EXPERTS_PER_GROUP = 8
MOE_EXPERTS = MOE_GROUPS * MOE_EXPERTS_PER_GROUP
EXPERT_HIDDEN = 512
MOE_BLOCK = 128
EVEN_IN_WIDTH = 3 * A_WIDTH + 2 * B_WIDTH
ODD_IN_WIDTH = 2 * C_QK_WIDTH + 2 * C_V_WIDTH

LANES = 128
ROW_TILE = 512
WIDE_ROW_TILE = 1024
CONV_TILE = 512
CONV_HALO = 32
CONV_CHUNK = 64
ROUTER_LANE0 = MOE_GROUPS
VMEM_LIMIT = 56 * 1024 * 1024
SLABS = D_MODEL // LANES
RANK_BITS = 16
DUMMY_SLOTS = ROW_TILE


def _cparams(sem, vmem=VMEM_LIMIT):
    return pltpu.CompilerParams(dimension_semantics=sem, vmem_limit_bytes=vmem)


def _cast_once(w_ref, wb_ref, first):
    @pl.when(first)
    def _():
        wb_ref[...] = w_ref[...].astype(BF16)


def _rms(x, gain):
    ms = jnp.mean(x * x, axis=-1, keepdims=True)
    return x * lax.rsqrt(ms + NORM_EPS) * gain


def _even_in_kernel(h_ref, g_ref, w_ref, c_ref, s1_ref, s2_ref, qkv_ref, conv_ref, wb_ref):
    _cast_once(w_ref, wb_ref, pl.program_id(0) == 0)
    u = _rms(h_ref[...], g_ref[...]).astype(BF16)
    acc = jnp.dot(u, wb_ref[...], preferred_element_type=F32)
    c, s1, s2 = c_ref[...], s1_ref[...], s2_ref[...]
    for j in range(2 * A_WIDTH // LANES):
        xg = acc[:, j * LANES:(j + 1) * LANES]
        if j < A_WIDTH // LANES:
            xg = xg * (A_HEAD_DIM ** -0.5)
        qkv_ref[j] = xg * c + pltpu.roll(xg, LANES - ROPE_DIM // 2, 1) * s1 + pltpu.roll(xg, ROPE_DIM // 2, 1) * s2
    for j in range(2 * A_WIDTH // LANES, 3 * A_WIDTH // LANES):
        qkv_ref[j] = acc[:, j * LANES:(j + 1) * LANES]
    conv_ref[...] = acc[:, 3 * A_WIDTH:].astype(BF16)


def _even_in(h, gain, w, tabs, seq):
    t = h.shape[0]
    tm = ROW_TILE
    nseq = seq // tm
    tab_spec = pl.BlockSpec((tm, LANES), lambda i: (i % nseq, 0))
    return pl.pallas_call(
        _even_in_kernel,
        grid=(t // tm,),
        in_specs=[
            pl.BlockSpec((tm, D_MODEL), lambda i: (i, 0)),
            pl.BlockSpec((1, D_MODEL), lambda i: (0, 0)),
            pl.BlockSpec((None, D_MODEL, EVEN_IN_WIDTH), lambda i: (0, 0, 0)),
            tab_spec, tab_spec, tab_spec,
        ],
        out_specs=[
            pl.BlockSpec((3 * A_WIDTH // LANES, tm, LANES), lambda i: (0, i, 0)),
            pl.BlockSpec((tm, 2 * B_WIDTH), lambda i: (i, 0)),
        ],
        out_shape=[
            jax.ShapeDtypeStruct((3 * A_WIDTH // LANES, t, LANES), F32),
            jax.ShapeDtypeStruct((t, 2 * B_WIDTH), BF16),
        ],
        scratch_shapes=[pltpu.VMEM((D_MODEL, EVEN_IN_WIDTH), BF16)],
        compiler_params=_cparams(("arbitrary",)),
        name="even_in_proj",
    )(h, gain.reshape(1, D_MODEL), w, *tabs)


def _attn_rope_tables(seq):
    half = ROPE_DIM // 2
    inv_freq = ROPE_THETA ** (-jnp.arange(0, ROPE_DIM, 2, dtype=F32) / ROPE_DIM)
    ang = jnp.arange(seq, dtype=F32)[:, None] * inv_freq[None, :]
    cos, sin = jnp.cos(ang), jnp.sin(ang)
    rest = A_HEAD_DIM - ROPE_DIM
    ones = jnp.ones((seq, rest), F32)
    z_rest = jnp.zeros((seq, rest), F32)
    z_half = jnp.zeros((seq, half), F32)
    c = jnp.concatenate([cos, cos, ones], axis=1)
    s1 = jnp.concatenate([-sin, z_half, z_rest], axis=1)
    s2 = jnp.concatenate([z_half, sin, z_rest], axis=1)
    rep = LANES // A_HEAD_DIM
    return tuple(jnp.tile(a, (1, rep)) for a in (c, s1, s2))


ATTN_STEP = ATTN_BLOCK * max(d for _, d in A_BRANCHES)
ATTN_PAIRS = A_WIDTH // LANES
ATTN_MERGE_ROWS = 256


def _attn_kernel(q_ref, k_ref, v_ref, o_ref, kprev, vprev, oml):
    n = pl.program_id(1)
    blk = ATTN_BLOCK
    n_blocks = ATTN_STEP // blk

    @pl.when(n == 0)
    def _():
        kprev[...] = jnp.zeros(kprev.shape, BF16)
        vprev[...] = jnp.zeros(vprev.shape, BF16)

    qi = lax.broadcasted_iota(I32, (blk, 2 * blk), 0)
    kj = lax.broadcasted_iota(I32, (blk, 2 * blk), 1)
    dist = blk + qi - kj
    band = (dist >= 0) & (dist <= blk)
    in_cur = kj >= blk
    lane = lax.broadcasted_iota(I32, (blk, LANES), 1)
    lo = lane < A_HEAD_DIM

    off = 0
    for bi, (_, dil) in enumerate(A_BRANCHES):
        shift = dil.bit_length() - 1
        assert dil == 1 << shift

        def one_block(idx, carried, bi=bi, dil=dil, shift=shift, off=off):
            u = lax.shift_right_logical(idx, shift)
            r = idx & (dil - 1)
            start = u * (blk * dil) + r
            rows = pl.ds(pl.multiple_of(start, blk), blk) if dil == 1 else pl.ds(start, blk, stride=dil)
            has_prev = (n > 0) | (u > 0)
            valid = band & (in_cur | has_prev)
            valid2 = jnp.concatenate([valid, valid], axis=0)
            current = []
            for g in range(ATTN_PAIRS):
                q = q_ref[g, rows, :]
                kc = k_ref[g, rows, :].astype(BF16)
                vc = v_ref[g, rows, :].astype(BF16)
                current.append((kc, vc))
                kp, vp = (kprev[g, off + r], vprev[g, off + r]) if carried is None else carried[g]
                q2 = jnp.concatenate([jnp.where(lo, q, 0.0), jnp.where(lo, 0.0, q)], axis=0).astype(BF16)
                kk = jnp.concatenate([kp, kc], axis=0)
                vv = jnp.concatenate([vp, vc], axis=0)
                s = lax.dot_general(q2, kk, (((1,), (1,)), ((), ())), preferred_element_type=F32)
                s = jnp.where(valid2, s, NEG_INF)
                m = jnp.max(s, axis=-1, keepdims=True)
                e = jnp.exp(s - m)
                den = jnp.sum(e, axis=-1, keepdims=True)
                pv = jnp.dot(e.astype(BF16), vv, preferred_element_type=F32)
                o_new = jnp.where(lo, pv[:blk], pv[blk:])
                m_new = jnp.where(lo, m[:blk], m[blk:])
                l_new = jnp.where(lo, den[:blk], den[blk:])
                if bi > 0:
                    o_run, m_run, l_run = oml[0, g, rows, :], oml[1, g, rows, :], oml[2, g, rows, :]
                    m_both = jnp.maximum(m_run, m_new)
                    w_run, w_new = jnp.exp(m_run - m_both), jnp.exp(m_new - m_both)
                    o_new = w_run * o_run + w_new * o_new
                    l_new = w_run * l_run + w_new * l_new
                    m_new = m_both
                oml[0, g, rows, :] = o_new
                oml[1, g, rows, :] = m_new
                oml[2, g, rows, :] = l_new
            return r, current

        def save_prev(r, current, off=off):
            for g, (kc, vc) in enumerate(current):
                kprev[g, off + r] = kc
                vprev[g, off + r] = vc

        def block_body(i, c, dil=dil):
            r0, cur0 = one_block(2 * i, None)
            if dil == 1:
                r1, cur1 = one_block(2 * i + 1, cur0)
            else:
                save_prev(r0, cur0)
                r1, cur1 = one_block(2 * i + 1, None)
            save_prev(r1, cur1)
            return c

        lax.fori_loop(0, n_blocks // 2, block_body, 0)
        off += dil
    for g in range(ATTN_PAIRS):
        for c in range(ATTN_STEP // ATTN_MERGE_ROWS):
            sl = slice(c * ATTN_MERGE_ROWS, (c + 1) * ATTN_MERGE_ROWS)
            o_ref[g, sl, :] = (oml[0, g, sl, :] / oml[2, g, sl, :]).astype(BF16)


def _attention(qkv, batch, seq):
    t = qkv.shape[1]
    steps = seq // ATTN_STEP
    n_res = sum(d for _, d in A_BRANCHES)

    def slabs(which):
        return pl.BlockSpec((ATTN_PAIRS, ATTN_STEP, LANES), lambda b, n: (which, b * steps + n, 0))

    return pl.pallas_call(
        _attn_kernel,
        grid=(batch, steps),
        in_specs=[slabs(0), slabs(1), slabs(2)],
        out_specs=pl.BlockSpec((ATTN_PAIRS, ATTN_STEP, LANES), lambda b, n: (0, b * steps + n, 0)),
        out_shape=jax.ShapeDtypeStruct((ATTN_PAIRS, t, LANES), BF16),
        scratch_shapes=[
            pltpu.VMEM((ATTN_PAIRS, n_res, ATTN_BLOCK, LANES), BF16),
            pltpu.VMEM((ATTN_PAIRS, n_res, ATTN_BLOCK, LANES), BF16),
            pltpu.VMEM((3, ATTN_PAIRS, ATTN_STEP, LANES), F32),
        ],
        compiler_params=_cparams(("parallel", "arbitrary")),
        name="dilated_attention",
    )(qkv, qkv, qkv)


def _conv_kernel(val_ref, gate_ref, w_ref, b_ref, g_ref, beta_ref, o_ref, abuf, shifted):
    n = pl.program_id(1)
    tc = CONV_TILE
    sub = 8

    @pl.when(n == 0)
    def _():
        abuf[0:CONV_HALO, :] = jnp.zeros((CONV_HALO, B_WIDTH), F32)

    @pl.when(n > 0)
    def _():
        abuf[0:CONV_HALO, :] = abuf[tc:tc + CONV_HALO, :]

    val = val_ref[...].astype(F32)
    gate = gate_ref[...].astype(F32)
    abuf[CONV_HALO:CONV_HALO + tc, :] = val * jax.nn.sigmoid(gate)
    off = CONV_HALO - (CONV_WIDTH - 1)
    span = shifted.shape[1]
    for s in range(1, sub):
        shifted[s - 1] = abuf[s:s + span, :]
    for c in range(tc // CONV_CHUNK):
        acc = jnp.broadcast_to(b_ref[...], (CONV_CHUNK, B_WIDTH))
        for j in range(CONV_WIDTH):
            s = (off + j) % sub
            r0 = c * CONV_CHUNK + off + j - s
            src = abuf[r0:r0 + CONV_CHUNK, :] if s == 0 else shifted[s - 1, r0:r0 + CONV_CHUNK, :]
            acc = acc + w_ref[j:j + 1, :] * src
        mu = jnp.mean(acc, axis=-1, keepdims=True)
        cen = acc - mu
        var = jnp.mean(cen * cen, axis=-1, keepdims=True)
        yn = cen * lax.rsqrt(var + NORM_EPS) * g_ref[...] + beta_ref[...]
        o_ref[c * CONV_CHUNK:(c + 1) * CONV_CHUNK, :] = (yn * jax.nn.sigmoid(yn)).astype(BF16)


def _conv(conv_in, w, b, g, beta, batch, seq):
    t = conv_in.shape[0]
    tc = CONV_TILE
    nt = seq // tc
    w_pad = jnp.concatenate([w, jnp.zeros((CONV_HALO - CONV_WIDTH, B_WIDTH), F32)], axis=0)
    vec = pl.BlockSpec((1, B_WIDTH), lambda bb, n: (0, 0))
    return pl.pallas_call(
        _conv_kernel,
        grid=(batch, nt),
        in_specs=[
            pl.BlockSpec((tc, B_WIDTH), lambda bb, n: (bb * nt + n, 0)),
            pl.BlockSpec((tc, B_WIDTH), lambda bb, n: (bb * nt + n, 1)),
            pl.BlockSpec((CONV_HALO, B_WIDTH), lambda bb, n: (0, 0)),
            vec, vec, vec,
        ],
        out_specs=pl.BlockSpec((tc, B_WIDTH), lambda bb, n: (bb * nt + n, 0)),
        out_shape=jax.ShapeDtypeStruct((t, B_WIDTH), BF16),
        scratch_shapes=[pltpu.VMEM((tc + CONV_HALO, B_WIDTH), F32),
                        pltpu.VMEM((7, tc + CONV_HALO - 8, B_WIDTH), F32)],
        compiler_params=_cparams(("parallel", "arbitrary")),
        name="conformer_conv",
    )(conv_in, conv_in, w_pad, b.reshape(1, -1), g.reshape(1, -1), beta.reshape(1, -1))


def _even_out_kernel(h_ref, a_ref, bc_ref, w_ref, out_ref, wb_ref):
    _cast_once(w_ref, wb_ref, pl.program_id(0) == 0)
    a = jnp.concatenate([a_ref[g] for g in range(ATTN_PAIRS)], axis=1)
    acc = jnp.dot(a, wb_ref[0:A_WIDTH, :], preferred_element_type=F32)
    acc = acc + jnp.dot(bc_ref[...], wb_ref[A_WIDTH:, :], preferred_element_type=F32)
    out_ref[...] = h_ref[...] + acc


def _even_out(h, attn, bconv, w):
    t = h.shape[0]
    tm = ROW_TILE
    row = lambda width: pl.BlockSpec((tm, width), lambda i: (i, 0))
    return pl.pallas_call(
        _even_out_kernel,
        grid=(t // tm,),
        in_specs=[row(D_MODEL), pl.BlockSpec((ATTN_PAIRS, tm, LANES), lambda i: (0, i, 0)), row(B_WIDTH),
                  pl.BlockSpec((None, D_MODEL, D_MODEL), lambda i: (0, 0, 0))],
        out_specs=row(D_MODEL),
        out_shape=jax.ShapeDtypeStruct((t, D_MODEL), F32),
        scratch_shapes=[pltpu.VMEM((D_MODEL, D_MODEL), BF16)],
        compiler_params=_cparams(("arbitrary",)),
        name="even_out_proj",
    )(h, attn, bconv, w)


def _odd_in_kernel(h_ref, g_ref, w_ref, cos_ref, sin_ref, o_ref, wb_ref):
    j = pl.program_id(0)
    _cast_once(w_ref, wb_ref, pl.program_id(1) == 0)
    u = _rms(h_ref[...], g_ref[...]).astype(BF16)
    acc = jnp.dot(u, wb_ref[...], preferred_element_type=F32)

    @pl.when(j == 0)
    def _():
        cos, sin = cos_ref[...], sin_ref[...]
        half = C_QK_DIM // 2
        for hd in range(2 * C_HEADS):
            x1 = acc[:, hd * C_QK_DIM:hd * C_QK_DIM + half]
            x2 = acc[:, hd * C_QK_DIM + half:(hd + 1) * C_QK_DIM]
            r1 = x1 * cos - x2 * sin
            r2 = x2 * cos + x1 * sin
            if hd >= C_HEADS:
                r1 = r1 * (C_QK_DIM ** -0.5)
                r2 = r2 * (C_QK_DIM ** -0.5)
            o_ref[:, hd * C_QK_DIM:hd * C_QK_DIM + half] = r1.astype(BF16)
            o_ref[:, hd * C_QK_DIM + half:(hd + 1) * C_QK_DIM] = r2.astype(BF16)

    @pl.when(j > 0)
    def _():
        o_ref[...] = acc.astype(BF16)


def _odd_in(h, gain, w, cos, sin, seq):
    t = h.shape[0]
    tm = WIDE_ROW_TILE
    tn = 2 * C_QK_WIDTH
    nseq = seq // tm
    tab = pl.BlockSpec((tm, C_QK_DIM // 2), lambda j, i: (i % nseq, 0))
    return pl.pallas_call(
        _odd_in_kernel,
        grid=(ODD_IN_WIDTH // tn, t // tm),
        in_specs=[
            pl.BlockSpec((tm, D_MODEL), lambda j, i: (i, 0)),
            pl.BlockSpec((1, D_MODEL), lambda j, i: (0, 0)),
            pl.BlockSpec((None, D_MODEL, tn), lambda j, i: (0, 0, j), pipeline_mode=pl.Buffered(1)),
            tab, tab,
        ],
        out_specs=pl.BlockSpec((tm, tn), lambda j, i: (i, j)),
        out_shape=jax.ShapeDtypeStruct((t, ODD_IN_WIDTH), BF16),
        scratch_shapes=[pltpu.VMEM((D_MODEL, tn), BF16)],
        compiler_params=_cparams(("arbitrary", "arbitrary")),
        name="odd_in_proj",
    )(h, gain.reshape(1, D_MODEL), w, cos, sin)


def _ret_kernel(ld_ref, q_ref, k_ref, v_ref, g_ref, o_ref, state):
    c = pl.program_id(0)
    ch = RET_CHUNK
    batch = q_ref.shape[0]

    @pl.when(c == 0)
    def _():
        state[...] = jnp.zeros(state.shape, F32)

    ii = lax.broadcasted_iota(I32, (ch, ch), 0)
    jj = lax.broadcasted_iota(I32, (ch, ch), 1)
    diff = (ii - jj).astype(F32)
    pos = lax.broadcasted_iota(I32, (ch, 1), 0).astype(F32)
    for hd in range(C_HEADS):
        ld = ld_ref[hd]
        intra = jnp.where(diff >= 0, jnp.exp(ld * jnp.maximum(diff, 0.0)), 0.0)
        q_decay = jnp.exp(ld * (pos + 1.0))
        k_decay = jnp.exp(ld * (ch - 1.0 - pos))
        chunk_decay = jnp.exp(ld * jnp.full((1, 1), float(ch), F32))
        for bb in range(batch):
            q = q_ref[bb, :, hd * C_QK_DIM:(hd + 1) * C_QK_DIM]
            k = k_ref[bb, :, hd * C_QK_DIM:(hd + 1) * C_QK_DIM]
            v = v_ref[bb, :, hd * C_V_DIM:(hd + 1) * C_V_DIM]
            s = lax.dot_general(q, k, (((1,), (1,)), ((), ())), preferred_element_type=F32) * intra
            inner = jnp.dot(s.astype(BF16), v, preferred_element_type=F32)
            st = state[bb, hd]
            cross = jnp.dot(q, st.astype(BF16), preferred_element_type=F32) * q_decay
            kd_t = jnp.transpose(k.astype(F32) * k_decay).astype(BF16)
            state[bb, hd] = st * chunk_decay + jnp.dot(kd_t, v, preferred_element_type=F32)
            out = inner + cross
            mu = jnp.mean(out, axis=-1, keepdims=True)
            cen = out - mu
            var = jnp.mean(cen * cen, axis=-1, keepdims=True)
            o = cen * lax.rsqrt(var + NORM_EPS)
            gf = g_ref[bb, :, hd * C_V_DIM:(hd + 1) * C_V_DIM].astype(F32)
            o_ref[bb, :, hd * C_V_DIM:(hd + 1) * C_V_DIM] = (gf * jax.nn.sigmoid(gf) * o).astype(BF16)


def _retention(proj, log_decay, batch, seq):
    ch = RET_CHUNK
    nc = seq // ch
    proj3 = proj.reshape(batch, seq, ODD_IN_WIDTH)
    v0 = 2 * C_QK_WIDTH // C_V_WIDTH
    grid_spec = pltpu.PrefetchScalarGridSpec(
        num_scalar_prefetch=1,
        grid=(nc,),
        in_specs=[
            pl.BlockSpec((batch, ch, C_QK_WIDTH), lambda c, ld: (0, c, 0)),
            pl.BlockSpec((batch, ch, C_QK_WIDTH), lambda c, ld: (0, c, 1)),
            pl.BlockSpec((batch, ch, C_V_WIDTH), lambda c, ld: (0, c, v0)),
            pl.BlockSpec((batch, ch, C_V_WIDTH), lambda c, ld: (0, c, v0 + 1)),
        ],
        out_specs=pl.BlockSpec((batch, ch, C_V_WIDTH), lambda c, ld: (0, c, 0)),
        scratch_shapes=[pltpu.VMEM((batch, C_HEADS, C_QK_DIM, C_V_DIM), F32)],
    )
    y = pl.pallas_call(
        _ret_kernel,
        grid_spec=grid_spec,
        out_shape=jax.ShapeDtypeStruct((batch, seq, C_V_WIDTH), BF16),
        compiler_params=_cparams(("arbitrary",)),
        name="retention",
    )(log_decay, proj3, proj3, proj3, proj3)
    return y.reshape(batch * seq, C_V_WIDTH)


def _odd_out_kernel(h_ref, y_ref, w_ref, out_ref, wb_ref):
    _cast_once(w_ref, wb_ref, pl.program_id(0) == 0)
    out_ref[...] = h_ref[...] + jnp.dot(y_ref[...], wb_ref[...], preferred_element_type=F32)


def _odd_out(h, y, w):
    t = h.shape[0]
    tm = WIDE_ROW_TILE
    return pl.pallas_call(
        _odd_out_kernel,
        grid=(t // tm,),
        in_specs=[pl.BlockSpec((tm, D_MODEL), lambda i: (i, 0)),
                  pl.BlockSpec((tm, C_V_WIDTH), lambda i: (i, 0)),
                  pl.BlockSpec((None, C_V_WIDTH, D_MODEL), lambda i: (0, 0, 0), pipeline_mode=pl.Buffered(1))],
        out_specs=pl.BlockSpec((tm, D_MODEL), lambda i: (i, 0)),
        out_shape=jax.ShapeDtypeStruct((t, D_MODEL), F32),
        scratch_shapes=[pltpu.VMEM((C_V_WIDTH, D_MODEL), BF16)],
        compiler_params=_cparams(("arbitrary",)),
        name="odd_out_proj",
    )(h, y, w)


def _router_kernel(h_ref, g_ref, wr_ref, br_ref, x2s_ref, codes_ref, gates_ref, cnt_ref, base_ref, *,
                   tiles_per_batch):
    i = pl.program_id(0)
    tm = h_ref.shape[0]
    batch = i // tiles_per_batch

    @pl.when(i % tiles_per_batch == 0)
    def _():
        base_ref[...] = jnp.zeros(base_ref.shape, F32)

    x2 = _rms(h_ref[...], g_ref[...])
    for j in range(SLABS):
        x2s_ref[pl.ds(j, tm, stride=SLABS), :] = x2[:, j * LANES:(j + 1) * LANES]
    wr = wr_ref[...]
    x_hi, w_hi = x2.astype(BF16), wr.astype(BF16)
    x_lo = (x2 - x_hi.astype(F32)).astype(BF16)
    w_lo = (wr - w_hi.astype(F32)).astype(BF16)
    both = jnp.dot(x_hi, jnp.concatenate([w_hi, w_lo], axis=1), preferred_element_type=F32)
    logits = both[:, :LANES] + both[:, LANES:] + jnp.dot(x_lo, w_hi, preferred_element_type=F32) + br_ref[...]
    lane = lax.broadcasted_iota(I32, (tm, LANES), 1)
    is_grp = lane < MOE_GROUPS
    lg = jnp.where(is_grp, logits, -jnp.inf)
    gmax = jnp.max(lg, axis=-1, keepdims=True)
    gsum = jnp.sum(jnp.where(is_grp, jnp.exp(logits - gmax), 0.0), axis=-1, keepdims=True)
    gp = 1.0 / gsum
    gi = jnp.min(jnp.where(lg == gmax, lane, LANES), axis=-1, keepdims=True)
    ex = lane - ROUTER_LANE0
    in_grp = (ex >= 0) & (ex < MOE_EXPERTS) & ((ex // MOE_EXPERTS_PER_GROUP) == gi)
    sel = jnp.where(in_grp, logits, -jnp.inf)
    v1 = jnp.max(sel, axis=-1, keepdims=True)
    i1 = jnp.min(jnp.where(sel == v1, lane, LANES), axis=-1, keepdims=True)
    sel2 = jnp.where(lane == i1, -jnp.inf, sel)
    v2 = jnp.max(sel2, axis=-1, keepdims=True)
    i2 = jnp.min(jnp.where(sel2 == v2, lane, LANES), axis=-1, keepdims=True)
    tt = jnp.exp(v2 - v1)
    g0 = gp / (1.0 + tt)
    g1 = gp * tt / (1.0 + tt)
    oh0 = (lane == i1).astype(F32)
    oh1 = (lane == i2).astype(F32)
    oh = oh0 + oh1
    rr = lax.broadcasted_iota(I32, (tm, tm), 0)
    cc = lax.broadcasted_iota(I32, (tm, tm), 1)
    tri = (cc < rr).astype(BF16)
    prefix = jnp.dot(tri, oh.astype(BF16), preferred_element_type=F32)
    tot = base_ref[...] + prefix
    rank0 = jnp.sum(oh0 * tot, axis=-1, keepdims=True).astype(I32)
    rank1 = jnp.sum(oh1 * tot, axis=-1, keepdims=True).astype(I32)
    base_ref[...] = base_ref[...] + jnp.sum(oh, axis=0, keepdims=True)
    eoff = batch * MOE_EXPERTS - ROUTER_LANE0
    code0 = (i1 + eoff) * (1 << RANK_BITS) + rank0
    code1 = (i2 + eoff) * (1 << RANK_BITS) + rank1
    meta = jnp.where(lane == 0, code0, jnp.where(lane == 1, code1, 0))
    codes_ref[...] = jnp.transpose(meta)[0:8, :]
    gates_ref[...] = jnp.where(lane == 0, g0, jnp.where(lane == 1, g1, 0.0))
    cnt_ref[...] = jnp.broadcast_to(base_ref[...], cnt_ref.shape)


def _router(h, gain, wr, br, batch):
    t = h.shape[0]
    tm = ROW_TILE
    tpb = t // batch // tm
    row = lambda width: pl.BlockSpec((tm, width), lambda i: (i, 0))
    fixed = lambda shape: pl.BlockSpec(shape, lambda i: (0, 0))
    return pl.pallas_call(
        functools.partial(_router_kernel, tiles_per_batch=tpb),
        grid=(t // tm,),
        in_specs=[row(D_MODEL), fixed((1, D_MODEL)), fixed((D_MODEL, LANES)), fixed((1, LANES))],
        out_specs=[pl.BlockSpec((tm * SLABS, LANES), lambda i: (i, 0)),
                   pl.BlockSpec((8, tm), lambda i: (0, i)), row(LANES),
                   pl.BlockSpec((8, LANES), lambda i: (i // tpb, 0))],
        out_shape=[
            jax.ShapeDtypeStruct((t * SLABS, LANES), F32),
            jax.ShapeDtypeStruct((8, t), I32),
            jax.ShapeDtypeStruct((t, LANES), F32),
            jax.ShapeDtypeStruct((batch * 8, LANES), F32),
        ],
        scratch_shapes=[pltpu.VMEM((1, LANES), F32)],
        compiler_params=_cparams(("arbitrary",)),
        name="moe_router",
    )(h, gain.reshape(1, D_MODEL), wr, br)


INVERT_UNROLL = 16


def _dest_kernel(pstart_ref, codes_ref, dest_ref):
    codes = codes_ref[...]
    seg = lax.shift_right_logical(codes, RANK_BITS)
    dest = codes & ((1 << RANK_BITS) - 1)
    for i in range(pstart_ref.shape[0]):
        dest = dest + jnp.where(seg == i, pstart_ref[i], 0)
    dest_ref[...] = dest


def _dest_rows(pad_start, codes):
    grid_spec = pltpu.PrefetchScalarGridSpec(
        num_scalar_prefetch=1,
        grid=(1,),
        in_specs=[pl.BlockSpec(codes.shape, lambda i, ps: (0, 0))],
        out_specs=pl.BlockSpec(codes.shape, lambda i, ps: (0, 0)),
    )
    return pl.pallas_call(
        _dest_kernel,
        grid_spec=grid_spec,
        out_shape=jax.ShapeDtypeStruct(codes.shape, I32),
        compiler_params=_cparams(("arbitrary",)),
        name="moe_dest_rows",
    )(pad_start, codes)


def _invert_kernel(trips_ref, dest_ref, dummy_ref, slot_ref, *, batch):
    per_batch = dest_ref.shape[0] // batch
    pltpu.sync_copy(dummy_ref, slot_ref)
    for b in range(batch):
        def place(i, carry, b=b):
            for v in range(INVERT_UNROLL):
                a = i * INVERT_UNROLL + v
                slot_ref[dest_ref[b * per_batch + a]] = a
            return carry

        lax.fori_loop(0, trips_ref[0], place, 0)


def _dummy_slots(n_rows, per_batch):
    assert MOE_BLOCK & (MOE_BLOCK - 1) == 0 and 3 * MOE_BLOCK <= DUMMY_SLOTS
    r = jnp.arange(n_rows + MOE_BLOCK, dtype=I32)
    pattern = per_batch + (r // MOE_BLOCK) % 2 * MOE_BLOCK + r % MOE_BLOCK
    return jnp.where(r < n_rows, pattern, per_batch + 2 * MOE_BLOCK + r % MOE_BLOCK)


def _invert(dest_flat, n_rows, batch):
    smem = pl.BlockSpec(memory_space=pltpu.SMEM)
    per_batch = dest_flat.shape[0] // batch
    assert per_batch % INVERT_UNROLL == 0
    trips = jnp.full((1,), per_batch // INVERT_UNROLL, I32)
    return pl.pallas_call(
        functools.partial(_invert_kernel, batch=batch),
        in_specs=[smem, smem, pl.BlockSpec(memory_space=pl.ANY)],
        out_specs=smem,
        out_shape=jax.ShapeDtypeStruct((n_rows + MOE_BLOCK,), I32),
        name="moe_invert_rows",
    )(trips, dest_flat, _dummy_slots(n_rows, per_batch))


def _expert_kernel(be_ref, run_ref, nxt_ref, nreal_ref, slot_ref, x2s_ref, wg_hbm, wu_hbm, wd_hbm, ys_ref,
                   wbuf_g, wbuf_u, wbuf_d, wgb_ref, wub_ref, wdb_ref, tile, xb, ybuf, sems, wsems, gsems, *,
                   batch, nb, seq, layer):
    b = pl.program_id(0)
    n = pl.program_id(1)
    g = b * nb + n
    last = batch * nb - 1
    blk = MOE_BLOCK
    bstride = 2 * seq + DUMMY_SLOTS
    tcur = g % 2
    ycur = g % 3
    yprev = (g + 2) % 3
    n_real = nreal_ref[0]

    def tile_slot(block):
        return jnp.where(block < n_real, block % 3, 3 + (block & 1))

    def weight_copies(e, ws):
        pairs = ((wg_hbm, wbuf_g), (wu_hbm, wbuf_u), (wd_hbm, wbuf_d))
        return [pltpu.make_async_copy(src.at[layer, e], dst.at[ws], wsems.at[ws, k])
                for k, (src, dst) in enumerate(pairs)]

    def wait_block(q):
        pltpu.make_async_copy(ybuf.at[q], ys_ref.at[pl.ds(0, blk * SLABS), :], sems.at[q]).wait()

    def gather_row(block, tslot, mi):
        s = slot_ref[jnp.minimum(block, last) * blk + mi]
        if seq & (seq - 1) == 0:
            tok = s & (seq - 1)
        else:
            tok = jnp.minimum(jnp.where(s >= seq, s - seq, s), seq - 1)
        src = pl.multiple_of(tok * SLABS, SLABS)
        pltpu.make_async_copy(x2s_ref.at[pl.ds(src, SLABS), :], tile.at[tslot, pl.ds(mi * SLABS, SLABS), :],
                              gsems.at[tslot]).start()

    def wait_gather(tslot):
        pltpu.make_async_copy(x2s_ref.at[pl.ds(0, blk * SLABS), :], tile.at[tslot], gsems.at[tslot]).wait()

    def to_bf16(tslot, xslot):
        for j in range(SLABS):
            xb[xslot, :, j * LANES:(j + 1) * LANES] = tile[tslot, pl.ds(j, blk, stride=SLABS), :].astype(BF16)

    def scatter_row(yslot, slot, row0, mi):
        dst = pl.multiple_of((row0 + slot) * SLABS, SLABS)
        pltpu.make_async_copy(ybuf.at[yslot, pl.ds(mi * SLABS, SLABS), :], ys_ref.at[pl.ds(dst, SLABS), :],
                              sems.at[yslot]).start()

    @pl.when(g == 0)
    def _():
        for c in weight_copies(be_ref[0], 0):
            c.start()
        ybuf[...] = jnp.zeros(ybuf.shape, F32)
        for bb in range(batch):
            for c in range(DUMMY_SLOTS // blk):
                start = (bb * bstride + 2 * seq + c * blk) * SLABS
                zero = pltpu.make_async_copy(ybuf.at[0], ys_ref.at[pl.ds(start, blk * SLABS), :], sems.at[0])
                zero.start()
                zero.wait()
        for mi in range(blk):
            gather_row(0, 0, mi)
        wait_gather(0)
        to_bf16(0, 0)
        for mi in range(blk):
            gather_row(1, 1, mi)

    run = run_ref[g]
    ws = run % 2

    @pl.when((g == 0) | (run != run_ref[jnp.maximum(g - 1, 0)]))
    def _():
        for c in weight_copies(be_ref[g], ws):
            c.wait()
        wgb_ref[...] = wbuf_g[ws].astype(BF16)
        wub_ref[...] = wbuf_u[ws].astype(BF16)
        wdb_ref[...] = wbuf_d[ws].astype(BF16)

        @pl.when(nxt_ref[g] >= 0)
        def _():
            for c in weight_copies(nxt_ref[g], 1 - ws):
                c.start()

    @pl.when(g >= 2)
    def _():
        wait_block(ycur)

    next_slot = tile_slot(g + 1)
    fetch_slot = tile_slot(g + 2)
    prev_block = jnp.where(g == 0, batch * nb, g - 1)
    prev_row0 = jnp.where(n == 0, jnp.maximum(b - 1, 0), b) * bstride
    n_chunks = 8
    per = blk // n_chunks
    half = EXPERT_HIDDEN // 2
    quarter = D_MODEL // 4

    @pl.when(g < n_real)
    def _():
        x = xb[tcur]
        acts = []
        hid = None
        for c in range(n_chunks):
            for mi in range(c * per, (c + 1) * per):
                gather_row(g + 2, fetch_slot, mi)
            if c < 4:
                for mi in range(2 * c * per, 2 * (c + 1) * per):
                    scatter_row(yprev, slot_ref[prev_block * blk + mi], prev_row0, mi)
                w_ref = wgb_ref if c < 2 else wub_ref
                acts.append(jnp.dot(x, w_ref[:, (c % 2) * half:(c % 2 + 1) * half], preferred_element_type=F32))
            if c == 3:
                a = jnp.concatenate(acts[0:2], axis=1)
                u = jnp.concatenate(acts[2:4], axis=1)
                hid = (a * jax.nn.sigmoid(a) * u).astype(BF16)
            if c >= 4:
                q = c - 4
                yq = jnp.dot(hid, wdb_ref[:, q * quarter:(q + 1) * quarter], preferred_element_type=F32)
                for jj in range(quarter // LANES):
                    j = q * (quarter // LANES) + jj
                    ybuf[ycur, pl.ds(j, blk, stride=SLABS), :] = yq[:, jj * LANES:(jj + 1) * LANES]
        wait_gather(next_slot)
        to_bf16(next_slot, 1 - tcur)

    @pl.when(g >= n_real)
    def _():
        for mi in range(blk):
            scatter_row(yprev, slot_ref[prev_block * blk + mi], prev_row0, mi)

    @pl.when(g == last)
    def _():
        wait_block((last - 2) % 3)
        for mi in range(blk):
            scatter_row(last % 3, slot_ref[last * blk + mi], (batch - 1) * bstride, mi)
        wait_block((last - 1) % 3)
        wait_block(last % 3)
        wait_gather(tile_slot(n_real + 1))


def _experts(block_expert, n_real, row_slot, x2s, wg, wu, wd, layer, batch, seq):
    blk = MOE_BLOCK
    nb = (row_slot.shape[0] // blk - 1) // batch
    steps = batch * nb
    assert steps >= 3 and batch == 1
    idx = jnp.arange(steps, dtype=I32)
    change = jnp.concatenate([jnp.zeros((1,), I32), (block_expert[1:] != block_expert[:-1]).astype(I32)])
    run = jnp.cumsum(change).astype(I32)
    later_change = (idx[None, :] > idx[:, None]) & (change[None, :] > 0)
    nxt_idx = jnp.min(jnp.where(later_change, idx[None, :], steps), axis=1)
    nxt = jnp.where(nxt_idx < steps, block_expert[jnp.minimum(nxt_idx, steps - 1)], -1).astype(I32)
    hbm = pl.BlockSpec(memory_space=pl.ANY)
    grid_spec = pltpu.PrefetchScalarGridSpec(
        num_scalar_prefetch=5,
        grid=(batch, nb),
        in_specs=[hbm, hbm, hbm, hbm],
        out_specs=hbm,
        scratch_shapes=[
            pltpu.VMEM((2, D_MODEL, EXPERT_HIDDEN), F32), pltpu.VMEM((2, D_MODEL, EXPERT_HIDDEN), F32),
            pltpu.VMEM((2, EXPERT_HIDDEN, D_MODEL), F32),
            pltpu.VMEM((D_MODEL, EXPERT_HIDDEN), BF16), pltpu.VMEM((D_MODEL, EXPERT_HIDDEN), BF16),
            pltpu.VMEM((EXPERT_HIDDEN, D_MODEL), BF16),
            pltpu.VMEM((5, blk * SLABS, LANES), F32),
            pltpu.VMEM((2, blk, D_MODEL), BF16),
            pltpu.VMEM((3, blk * SLABS, LANES), F32),
            pltpu.SemaphoreType.DMA((3,)),
            pltpu.SemaphoreType.DMA((2, 3)),
            pltpu.SemaphoreType.DMA((5,)),
        ],
    )
    return pl.pallas_call(
        functools.partial(_expert_kernel, batch=batch, nb=nb, seq=seq, layer=layer),
        grid_spec=grid_spec,
        out_shape=jax.ShapeDtypeStruct((batch * (2 * seq + DUMMY_SLOTS) * SLABS, LANES), F32),
        compiler_params=_cparams(("arbitrary", "arbitrary")),
        name="moe_experts",
    )(block_expert, run, nxt, n_real, row_slot, x2s, wg, wu, wd)


def _combine_kernel(h_ref, gates_ref, y0_ref, y1_ref, *rest, final):
    if final:
        fg_ref, out_ref = rest
    else:
        (out_ref,) = rest
    tm = h_ref.shape[0]
    gates = gates_ref[...]
    g0, g1 = gates[:, 0:1], gates[:, 1:2]
    parts = []
    for j in range(SLABS):
        y0 = y0_ref[pl.ds(j, tm, stride=SLABS), :]
        y1 = y1_ref[pl.ds(j, tm, stride=SLABS), :]
        parts.append(g0 * y0 + g1 * y1)
    out = h_ref[...] + jnp.concatenate(parts, axis=1)
    if final:
        out = _rms(out, fg_ref[...])
    out_ref[...] = out


def _combine(h, gates, ys, batch, seq, final_gain=None):
    t = h.shape[0]
    tm = ROW_TILE
    tpb = seq // tm
    bstride = 2 * tpb + DUMMY_SLOTS // tm
    final = final_gain is not None
    in_specs = [pl.BlockSpec((tm, D_MODEL), lambda i: (i, 0)),
                pl.BlockSpec((tm, LANES), lambda i: (i, 0)),
                pl.BlockSpec((tm * SLABS, LANES), lambda i: (i // tpb * bstride + i % tpb, 0)),
                pl.BlockSpec((tm * SLABS, LANES), lambda i: (i // tpb * bstride + tpb + i % tpb, 0))]
    args = [h, gates, ys, ys]
    if final:
        in_specs.append(pl.BlockSpec((1, D_MODEL), lambda i: (0, 0)))
        args.append(final_gain.reshape(1, D_MODEL))
    return pl.pallas_call(
        functools.partial(_combine_kernel, final=final),
        grid=(t // tm,),
        in_specs=in_specs,
        out_specs=pl.BlockSpec((tm, D_MODEL), lambda i: (i, 0)),
        out_shape=jax.ShapeDtypeStruct((t, D_MODEL), F32),
        compiler_params=_cparams(("parallel",)),
        name="moe_combine_final" if final else "moe_combine",
    )(*args)


def _moe(h, gain, w_r1, b_r1, w_r2, b_r2, wg, wu, wd, layer, batch, final_gain=None):
    t = h.shape[0]
    seq = t // batch
    assert seq <= (1 << RANK_BITS) and seq % ROW_TILE == 0 and DUMMY_SLOTS % ROW_TILE == 0
    pad_w = jnp.zeros((D_MODEL, LANES - MOE_GROUPS - MOE_EXPERTS), F32)
    wr = jnp.concatenate([w_r1, w_r2, pad_w], axis=1)
    br = jnp.concatenate([b_r1, b_r2, jnp.zeros((LANES - MOE_GROUPS - MOE_EXPERTS,), F32)]).reshape(1, LANES)
    x2s, codes, gates, counts = _router(h, gain, wr, br, batch)
    cnt = counts.reshape(batch, 8, LANES)[:, 0, ROUTER_LANE0:ROUTER_LANE0 + MOE_EXPERTS].astype(I32)
    padded = (cnt + MOE_BLOCK - 1) // MOE_BLOCK * MOE_BLOCK
    pad_end = jnp.cumsum(padded, axis=1)
    rows_pb = seq * 2 + MOE_EXPERTS * MOE_BLOCK
    nb = rows_pb // MOE_BLOCK
    pad_start = pad_end - padded + (jnp.arange(batch, dtype=I32) * rows_pb)[:, None]
    block_start = jnp.arange(nb, dtype=I32) * MOE_BLOCK
    block_expert = jnp.minimum(
        jnp.sum((pad_end[:, None, :] <= block_start[None, :, None]).astype(I32), axis=2), MOE_EXPERTS - 1)
    dest = _dest_rows(pad_start.reshape(-1), codes)
    dest_flat = dest[0:2].reshape(2, batch, seq).transpose(1, 0, 2).reshape(-1)
    row_slot = _invert(dest_flat, batch * rows_pb, batch)
    n_real = (pad_end[:, -1] // MOE_BLOCK).astype(I32)
    ys = _experts(block_expert.reshape(-1), n_real, row_slot, x2s, wg, wu, wd, layer, batch, seq)
    return _combine(h, gates, ys, batch, seq, final_gain)


def kernel(x, mix_norm, ffn_norm, final_norm, even_w_in, even_w_out, conv_w, conv_b, conv_norm_g, conv_norm_b,
           odd_w_in, odd_w_out, router_w1, router_b1, router_w2, router_b2, expert_w_gate, expert_w_up,
           expert_w_down):
    batch, seq, d = x.shape
    assert d == D_MODEL and seq % (ATTN_BLOCK * max(dl for _, dl in A_BRANCHES)) == 0
    assert all(w // dl == ATTN_BLOCK for w, dl in A_BRANCHES)
    t = batch * seq
    h = x.reshape(t, d)

    qkv, conv_in = _even_in(h, mix_norm[0], even_w_in, _attn_rope_tables(seq), seq)
    attn = _attention(qkv, batch, seq)
    bconv = _conv(conv_in, conv_w[0], conv_b[0], conv_norm_g[0], conv_norm_b[0], batch, seq)
    h = _even_out(h, attn, bconv, even_w_out)
    h = _moe(h, ffn_norm[0], router_w1[0], router_b1[0], router_w2[0], router_b2[0],
             expert_w_gate, expert_w_up, expert_w_down, 0, 1)

    inv_freq = RET_ROT_THETA ** (-jnp.linspace(0.0, 1.0, C_QK_DIM // 2, dtype=F32))
    ang = jnp.arange(seq, dtype=F32)[:, None] * inv_freq[None, :]
    proj = _odd_in(h, mix_norm[1], odd_w_in, jnp.cos(ang), jnp.sin(ang), seq)
    log_decay = jnp.log(1.0 - jnp.exp2(-5.0 - jnp.arange(C_HEADS, dtype=F32)))
    y = _retention(proj, log_decay, batch, seq)
    h = _odd_out(h, y, odd_w_out)
    out = _moe(h, ffn_norm[1], router_w1[1], router_b1[1], router_w2[1], router_b2[1],
               expert_w_gate, expert_w_up, expert_w_down, 1, 1, final_gain=final_norm)
    return out.reshape(batch, seq, d)
```

```python
import functools

import jax
import jax.numpy as jnp
from jax import lax
from jax.experimental import pallas as pl
from jax.experimental.pallas import tpu as pltpu

F32 = jnp.float32
BF16 = jnp.bfloat16
I32 = jnp.int32

NORM_EPS = 1e-6
NEG_INF = -1e30

D_MODEL = 1024
A_HEADS = 8
A_HEAD_DIM = 64
A_WIDTH = A_HEADS * A_HEAD_DIM
A_BRANCHES = ((128, 1), (512, 4), (2048, 16))
ATTN_BLOCK = 128
ROPE_THETA = 500000.0
ROPE_DIM = A_HEAD_DIM // 4
B_WIDTH = D_MODEL - A_WIDTH
CONV_WIDTH = 31
C_HEADS = 4
C_QK_DIM = 256
C_V_DIM = 512
C_QK_WIDTH = C_HEADS * C_QK_DIM
C_V_WIDTH = C_HEADS * C_V_DIM
RET_CHUNK = 128
RET_ROT_THETA = 10000.0
MOE_GROUPS = 4
MOE_EXPERTS_PER_GROUP = 8
MOE_EXPERTS = MOE_GROUPS * MOE_EXPERTS_PER_GROUP
EXPERT_HIDDEN = 512
MOE_BLOCK = 128
EVEN_IN_WIDTH = 3 * A_WIDTH + 2 * B_WIDTH
ODD_IN_WIDTH = 2 * C_QK_WIDTH + 2 * C_V_WIDTH

LANES = 128
ROW_TILE = 512
WIDE_ROW_TILE = 1024
CONV_TILE = 512
CONV_HALO = 32
CONV_CHUNK = 64
ROUTER_LANE0 = MOE_GROUPS
VMEM_LIMIT = 56 * 1024 * 1024
SLABS = D_MODEL // LANES
RANK_BITS = 16
DUMMY_SLOTS = ROW_TILE


def _cparams(sem, vmem=VMEM_LIMIT):
    return pltpu.CompilerParams(dimension_semantics=sem, vmem_limit_bytes=vmem)


def _cast_once(w_ref, wb_ref, first):
    @pl.when(first)
    def _():
        wb_ref[...] = w_ref[...].astype(BF16)


def _rms(x, gain):
    ms = jnp.mean(x * x, axis=-1, keepdims=True)
    return x * lax.rsqrt(ms + NORM_EPS) * gain


def _even_in_kernel(h_ref, g_ref, w_ref, c_ref, s1_ref, s2_ref, qkv_ref, conv_ref, wb_ref):
    _cast_once(w_ref, wb_ref, pl.program_id(0) == 0)
    u = _rms(h_ref[...], g_ref[...]).astype(BF16)
    acc = jnp.dot(u, wb_ref[...], preferred_element_type=F32)
    c, s1, s2 = c_ref[...], s1_ref[...], s2_ref[...]
    for j in range(2 * A_WIDTH // LANES):
        xg = acc[:, j * LANES:(j + 1) * LANES]
        if j < A_WIDTH // LANES:
            xg = xg * (A_HEAD_DIM ** -0.5)
        qkv_ref[j] = xg * c + pltpu.roll(xg, LANES - ROPE_DIM // 2, 1) * s1 + pltpu.roll(xg, ROPE_DIM // 2, 1) * s2
    for j in range(2 * A_WIDTH // LANES, 3 * A_WIDTH // LANES):
        qkv_ref[j] = acc[:, j * LANES:(j + 1) * LANES]
    conv_ref[...] = acc[:, 3 * A_WIDTH:].astype(BF16)


def _even_in(h, gain, w, tabs, seq):
    t = h.shape[0]
    tm = ROW_TILE
    nseq = seq // tm
    tab_spec = pl.BlockSpec((tm, LANES), lambda i: (i % nseq, 0))
    return pl.pallas_call(
        _even_in_kernel,
        grid=(t // tm,),
        in_specs=[
            pl.BlockSpec((tm, D_MODEL), lambda i: (i, 0)),
            pl.BlockSpec((1, D_MODEL), lambda i: (0, 0)),
            pl.BlockSpec((None, D_MODEL, EVEN_IN_WIDTH), lambda i: (0, 0, 0)),
            tab_spec, tab_spec, tab_spec,
        ],
        out_specs=[
            pl.BlockSpec((3 * A_WIDTH // LANES, tm, LANES), lambda i: (0, i, 0)),
            pl.BlockSpec((tm, 2 * B_WIDTH), lambda i: (i, 0)),
        ],
        out_shape=[
            jax.ShapeDtypeStruct((3 * A_WIDTH // LANES, t, LANES), F32),
            jax.ShapeDtypeStruct((t, 2 * B_WIDTH), BF16),
        ],
        scratch_shapes=[pltpu.VMEM((D_MODEL, EVEN_IN_WIDTH), BF16)],
        compiler_params=_cparams(("arbitrary",)),
        name="even_in_proj",
    )(h, gain.reshape(1, D_MODEL), w, *tabs)


def _attn_rope_tables(seq):
    half = ROPE_DIM // 2
    inv_freq = ROPE_THETA ** (-jnp.arange(0, ROPE_DIM, 2, dtype=F32) / ROPE_DIM)
    ang = jnp.arange(seq, dtype=F32)[:, None] * inv_freq[None, :]
    cos, sin = jnp.cos(ang), jnp.sin(ang)
    rest = A_HEAD_DIM - ROPE_DIM
    ones = jnp.ones((seq, rest), F32)
    z_rest = jnp.zeros((seq, rest), F32)
    z_half = jnp.zeros((seq, half), F32)
    c = jnp.concatenate([cos, cos, ones], axis=1)
    s1 = jnp.concatenate([-sin, z_half, z_rest], axis=1)
    s2 = jnp.concatenate([z_half, sin, z_rest], axis=1)
    rep = LANES // A_HEAD_DIM
    return tuple(jnp.tile(a, (1, rep)) for a in (c, s1, s2))


ATTN_STEP = ATTN_BLOCK * max(d for _, d in A_BRANCHES)
ATTN_PAIRS = A_WIDTH // LANES
ATTN_MERGE_ROWS = 256


def _attn_kernel(q_ref, k_ref, v_ref, o_ref, kprev, vprev, oml):
    n = pl.program_id(1)
    blk = ATTN_BLOCK
    n_blocks = ATTN_STEP // blk

    @pl.when(n == 0)
    def _():
        kprev[...] = jnp.zeros(kprev.shape, BF16)
        vprev[...] = jnp.zeros(vprev.shape, BF16)

    qi = lax.broadcasted_iota(I32, (blk, 2 * blk), 0)
    kj = lax.broadcasted_iota(I32, (blk, 2 * blk), 1)
    dist = blk + qi - kj
    band = (dist >= 0) & (dist <= blk)
    in_cur = kj >= blk
    lane = lax.broadcasted_iota(I32, (blk, LANES), 1)
    lo = lane < A_HEAD_DIM

    off = 0
    for bi, (_, dil) in enumerate(A_BRANCHES):
        shift = dil.bit_length() - 1
        assert dil == 1 << shift

        def one_block(idx, carried, bi=bi, dil=dil, shift=shift, off=off):
            u = lax.shift_right_logical(idx, shift)
            r = idx & (dil - 1)
            start = u * (blk * dil) + r
            rows = pl.ds(pl.multiple_of(start, blk), blk) if dil == 1 else pl.ds(start, blk, stride=dil)
            has_prev = (n > 0) | (u > 0)
            valid = band & (in_cur | has_prev)
            valid2 = jnp.concatenate([valid, valid], axis=0)
            current = []
            for g in range(ATTN_PAIRS):
                q = q_ref[g, rows, :]
                kc = k_ref[g, rows, :].astype(BF16)
                vc = v_ref[g, rows, :].astype(BF16)
                current.append((kc, vc))
                kp, vp = (kprev[g, off + r], vprev[g, off + r]) if carried is None else carried[g]
                q2 = jnp.concatenate([jnp.where(lo, q, 0.0), jnp.where(lo, 0.0, q)], axis=0).astype(BF16)
                kk = jnp.concatenate([kp, kc], axis=0)
                vv = jnp.concatenate([vp, vc], axis=0)
                s = lax.dot_general(q2, kk, (((1,), (1,)), ((), ())), preferred_element_type=F32)
                s = jnp.where(valid2, s, NEG_INF)
                m = jnp.max(s, axis=-1, keepdims=True)
                e = jnp.exp(s - m)
                den = jnp.sum(e, axis=-1, keepdims=True)
                pv = jnp.dot(e.astype(BF16), vv, preferred_element_type=F32)
                o_new = jnp.where(lo, pv[:blk], pv[blk:])
                m_new = jnp.where(lo, m[:blk], m[blk:])
                l_new = jnp.where(lo, den[:blk], den[blk:])
                if bi > 0:
                    o_run, m_run, l_run = oml[0, g, rows, :], oml[1, g, rows, :], oml[2, g, rows, :]
                    m_both = jnp.maximum(m_run, m_new)
                    w_run, w_new = jnp.exp(m_run - m_both), jnp.exp(m_new - m_both)
                    o_new = w_run * o_run + w_new * o_new
                    l_new = w_run * l_run + w_new * l_new
                    m_new = m_both
                oml[0, g, rows, :] = o_new
                oml[1, g, rows, :] = m_new
                oml[2, g, rows, :] = l_new
            return r, current

        def save_prev(r, current, off=off):
            for g, (kc, vc) in enumerate(current):
                kprev[g, off + r] = kc
                vprev[g, off + r] = vc

        def block_body(i, c, dil=dil):
            r0, cur0 = one_block(2 * i, None)
            if dil == 1:
                r1, cur1 = one_block(2 * i + 1, cur0)
            else:
                save_prev(r0, cur0)
                r1, cur1 = one_block(2 * i + 1, None)
            save_prev(r1, cur1)
            return c

        lax.fori_loop(0, n_blocks // 2, block_body, 0)
        off += dil
    for g in range(ATTN_PAIRS):
        for c in range(ATTN_STEP // ATTN_MERGE_ROWS):
            sl = slice(c * ATTN_MERGE_ROWS, (c + 1) * ATTN_MERGE_ROWS)
            o_ref[g, sl, :] = (oml[0, g, sl, :] / oml[2, g, sl, :]).astype(BF16)


def _attention(qkv, batch, seq):
    t = qkv.shape[1]
    steps = seq // ATTN_STEP
    n_res = sum(d for _, d in A_BRANCHES)

    def slabs(which):
        return pl.BlockSpec((ATTN_PAIRS, ATTN_STEP, LANES), lambda b, n: (which, b * steps + n, 0))

    return pl.pallas_call(
        _attn_kernel,
        grid=(batch, steps),
        in_specs=[slabs(0), slabs(1), slabs(2)],
        out_specs=pl.BlockSpec((ATTN_PAIRS, ATTN_STEP, LANES), lambda b, n: (0, b * steps + n, 0)),
        out_shape=jax.ShapeDtypeStruct((ATTN_PAIRS, t, LANES), BF16),
        scratch_shapes=[
            pltpu.VMEM((ATTN_PAIRS, n_res, ATTN_BLOCK, LANES), BF16),
            pltpu.VMEM((ATTN_PAIRS, n_res, ATTN_BLOCK, LANES), BF16),
            pltpu.VMEM((3, ATTN_PAIRS, ATTN_STEP, LANES), F32),
        ],
        compiler_params=_cparams(("parallel", "arbitrary")),
        name="dilated_attention",
    )(qkv, qkv, qkv)


def _conv_kernel(val_ref, gate_ref, w_ref, b_ref, g_ref, beta_ref, o_ref, abuf, shifted):
    n = pl.program_id(1)
    tc = CONV_TILE
    sub = 8

    @pl.when(n == 0)
    def _():
        abuf[0:CONV_HALO, :] = jnp.zeros((CONV_HALO, B_WIDTH), F32)

    @pl.when(n > 0)
    def _():
        abuf[0:CONV_HALO, :] = abuf[tc:tc + CONV_HALO, :]

    val = val_ref[...].astype(F32)
    gate = gate_ref[...].astype(F32)
    abuf[CONV_HALO:CONV_HALO + tc, :] = val * jax.nn.sigmoid(gate)
    off = CONV_HALO - (CONV_WIDTH - 1)
    span = shifted.shape[1]
    for s in range(1, sub):
        shifted[s - 1] = abuf[s:s + span, :]
    for c in range(tc // CONV_CHUNK):
        acc = jnp.broadcast_to(b_ref[...], (CONV_CHUNK, B_WIDTH))
        for j in range(CONV_WIDTH):
            s = (off + j) % sub
            r0 = c * CONV_CHUNK + off + j - s
            src = abuf[r0:r0 + CONV_CHUNK, :] if s == 0 else shifted[s - 1, r0:r0 + CONV_CHUNK, :]
            acc = acc + w_ref[j:j + 1, :] * src
        mu = jnp.mean(acc, axis=-1, keepdims=True)
        cen = acc - mu
        var = jnp.mean(cen * cen, axis=-1, keepdims=True)
        yn = cen * lax.rsqrt(var + NORM_EPS) * g_ref[...] + beta_ref[...]
        o_ref[c * CONV_CHUNK:(c + 1) * CONV_CHUNK, :] = (yn * jax.nn.sigmoid(yn)).astype(BF16)


def _conv(conv_in, w, b, g, beta, batch, seq):
    t = conv_in.shape[0]
    tc = CONV_TILE
    nt = seq // tc
    w_pad = jnp.concatenate([w, jnp.zeros((CONV_HALO - CONV_WIDTH, B_WIDTH), F32)], axis=0)
    vec = pl.BlockSpec((1, B_WIDTH), lambda bb, n: (0, 0))
    return pl.pallas_call(
        _conv_kernel,
        grid=(batch, nt),
        in_specs=[
            pl.BlockSpec((tc, B_WIDTH), lambda bb, n: (bb * nt + n, 0)),
            pl.BlockSpec((tc, B_WIDTH), lambda bb, n: (bb * nt + n, 1)),
            pl.BlockSpec((CONV_HALO, B_WIDTH), lambda bb, n: (0, 0)),
            vec, vec, vec,
        ],
        out_specs=pl.BlockSpec((tc, B_WIDTH), lambda bb, n: (bb * nt + n, 0)),
        out_shape=jax.ShapeDtypeStruct((t, B_WIDTH), BF16),
        scratch_shapes=[pltpu.VMEM((tc + CONV_HALO, B_WIDTH), F32),
                        pltpu.VMEM((7, tc + CONV_HALO - 8, B_WIDTH), F32)],
        compiler_params=_cparams(("parallel", "arbitrary")),
        name="conformer_conv",
    )(conv_in, conv_in, w_pad, b.reshape(1, -1), g.reshape(1, -1), beta.reshape(1, -1))


def _even_out_kernel(h_ref, a_ref, bc_ref, w_ref, out_ref, wb_ref):
    _cast_once(w_ref, wb_ref, pl.program_id(0) == 0)
    a = jnp.concatenate([a_ref[g] for g in range(ATTN_PAIRS)], axis=1)
    acc = jnp.dot(a, wb_ref[0:A_WIDTH, :], preferred_element_type=F32)
    acc = acc + jnp.dot(bc_ref[...], wb_ref[A_WIDTH:, :], preferred_element_type=F32)
    out_ref[...] = h_ref[...] + acc


def _even_out(h, attn, bconv, w):
    t = h.shape[0]
    tm = WIDE_ROW_TILE
    row = lambda width: pl.BlockSpec((tm, width), lambda i: (i, 0))
    return pl.pallas_call(
        _even_out_kernel,
        grid=(t // tm,),
        in_specs=[row(D_MODEL), pl.BlockSpec((ATTN_PAIRS, tm, LANES), lambda i: (0, i, 0)), row(B_WIDTH),
                  pl.BlockSpec((None, D_MODEL, D_MODEL), lambda i: (0, 0, 0), pipeline_mode=pl.Buffered(1))],
        out_specs=row(D_MODEL),
        out_shape=jax.ShapeDtypeStruct((t, D_MODEL), F32),
        scratch_shapes=[pltpu.VMEM((D_MODEL, D_MODEL), BF16)],
        compiler_params=_cparams(("arbitrary",)),
        name="even_out_proj",
    )(h, attn, bconv, w)


def _odd_in_kernel(h_ref, g_ref, w_ref, cos_ref, sin_ref, o_ref, wb_ref):
    j = pl.program_id(0)
    _cast_once(w_ref, wb_ref, pl.program_id(1) == 0)
    u = _rms(h_ref[...], g_ref[...]).astype(BF16)
    acc = jnp.dot(u, wb_ref[...], preferred_element_type=F32)

    @pl.when(j == 0)
    def _():
        cos, sin = cos_ref[...], sin_ref[...]
        half = C_QK_DIM // 2
        for hd in range(2 * C_HEADS):
            x1 = acc[:, hd * C_QK_DIM:hd * C_QK_DIM + half]
            x2 = acc[:, hd * C_QK_DIM + half:(hd + 1) * C_QK_DIM]
            r1 = x1 * cos - x2 * sin
            r2 = x2 * cos + x1 * sin
            if hd >= C_HEADS:
                r1 = r1 * (C_QK_DIM ** -0.5)
                r2 = r2 * (C_QK_DIM ** -0.5)
            o_ref[:, hd * C_QK_DIM:hd * C_QK_DIM + half] = r1.astype(BF16)
            o_ref[:, hd * C_QK_DIM + half:(hd + 1) * C_QK_DIM] = r2.astype(BF16)

    @pl.when(j > 0)
    def _():
        o_ref[...] = acc.astype(BF16)


def _odd_in(h, gain, w, cos, sin, seq):
    t = h.shape[0]
    tm = WIDE_ROW_TILE
    tn = 2 * C_QK_WIDTH
    nseq = seq // tm
    tab = pl.BlockSpec((tm, C_QK_DIM // 2), lambda j, i: (i % nseq, 0))
    return pl.pallas_call(
        _odd_in_kernel,
        grid=(ODD_IN_WIDTH // tn, t // tm),
        in_specs=[
            pl.BlockSpec((tm, D_MODEL), lambda j, i: (i, 0)),
            pl.BlockSpec((1, D_MODEL), lambda j, i: (0, 0)),
            pl.BlockSpec((None, D_MODEL, tn), lambda j, i: (0, 0, j), pipeline_mode=pl.Buffered(1)),
            tab, tab,
        ],
        out_specs=pl.BlockSpec((tm, tn), lambda j, i: (i, j)),
        out_shape=jax.ShapeDtypeStruct((t, ODD_IN_WIDTH), BF16),
        scratch_shapes=[pltpu.VMEM((D_MODEL, tn), BF16)],
        compiler_params=_cparams(("arbitrary", "arbitrary")),
        name="odd_in_proj",
    )(h, gain.reshape(1, D_MODEL), w, cos, sin)


def _ret_kernel(ld_ref, q_ref, k_ref, v_ref, g_ref, o_ref, state):
    c = pl.program_id(0)
    ch = RET_CHUNK
    batch = q_ref.shape[0]

    @pl.when(c == 0)
    def _():
        state[...] = jnp.zeros(state.shape, F32)

    ii = lax.broadcasted_iota(I32, (ch, ch), 0)
    jj = lax.broadcasted_iota(I32, (ch, ch), 1)
    diff = (ii - jj).astype(F32)
    pos = lax.broadcasted_iota(I32, (ch, 1), 0).astype(F32)
    for hd in range(C_HEADS):
        ld = ld_ref[hd]
        intra = jnp.where(diff >= 0, jnp.exp(ld * jnp.maximum(diff, 0.0)), 0.0)
        q_decay = jnp.exp(ld * (pos + 1.0))
        k_decay = jnp.exp(ld * (ch - 1.0 - pos))
        chunk_decay = jnp.exp(ld * jnp.full((1, 1), float(ch), F32))
        for bb in range(batch):
            q = q_ref[bb, :, hd * C_QK_DIM:(hd + 1) * C_QK_DIM]
            k = k_ref[bb, :, hd * C_QK_DIM:(hd + 1) * C_QK_DIM]
            v = v_ref[bb, :, hd * C_V_DIM:(hd + 1) * C_V_DIM]
            s = lax.dot_general(q, k, (((1,), (1,)), ((), ())), preferred_element_type=F32) * intra
            inner = jnp.dot(s.astype(BF16), v, preferred_element_type=F32)
            st = state[bb, hd]
            cross = jnp.dot(q, st.astype(BF16), preferred_element_type=F32) * q_decay
            kd_t = jnp.transpose(k.astype(F32) * k_decay).astype(BF16)
            state[bb, hd] = st * chunk_decay + jnp.dot(kd_t, v, preferred_element_type=F32)
            out = inner + cross
            mu = jnp.mean(out, axis=-1, keepdims=True)
            cen = out - mu
            var = jnp.mean(cen * cen, axis=-1, keepdims=True)
            o = cen * lax.rsqrt(var + NORM_EPS)
            gf = g_ref[bb, :, hd * C_V_DIM:(hd + 1) * C_V_DIM].astype(F32)
            o_ref[bb, :, hd * C_V_DIM:(hd + 1) * C_V_DIM] = (gf * jax.nn.sigmoid(gf) * o).astype(BF16)


def _retention(proj, log_decay, batch, seq):
    ch = RET_CHUNK
    nc = seq // ch
    proj3 = proj.reshape(batch, seq, ODD_IN_WIDTH)
    v0 = 2 * C_QK_WIDTH // C_V_WIDTH
    grid_spec = pltpu.PrefetchScalarGridSpec(
        num_scalar_prefetch=1,
        grid=(nc,),
        in_specs=[
            pl.BlockSpec((batch, ch, C_QK_WIDTH), lambda c, ld: (0, c, 0)),
            pl.BlockSpec((batch, ch, C_QK_WIDTH), lambda c, ld: (0, c, 1)),
            pl.BlockSpec((batch, ch, C_V_WIDTH), lambda c, ld: (0, c, v0)),
            pl.BlockSpec((batch, ch, C_V_WIDTH), lambda c, ld: (0, c, v0 + 1)),
        ],
        out_specs=pl.BlockSpec((batch, ch, C_V_WIDTH), lambda c, ld: (0, c, 0)),
        scratch_shapes=[pltpu.VMEM((batch, C_HEADS, C_QK_DIM, C_V_DIM), F32)],
    )
    y = pl.pallas_call(
        _ret_kernel,
        grid_spec=grid_spec,
        out_shape=jax.ShapeDtypeStruct((batch, seq, C_V_WIDTH), BF16),
        compiler_params=_cparams(("arbitrary",)),
        name="retention",
    )(log_decay, proj3, proj3, proj3, proj3)
    return y.reshape(batch * seq, C_V_WIDTH)


def _odd_out_kernel(h_ref, y_ref, w_ref, out_ref, wb_ref):
    _cast_once(w_ref, wb_ref, pl.program_id(0) == 0)
    out_ref[...] = h_ref[...] + jnp.dot(y_ref[...], wb_ref[...], preferred_element_type=F32)


def _odd_out(h, y, w):
    t = h.shape[0]
    tm = WIDE_ROW_TILE
    return pl.pallas_call(
        _odd_out_kernel,
        grid=(t // tm,),
        in_specs=[pl.BlockSpec((tm, D_MODEL), lambda i: (i, 0)),
                  pl.BlockSpec((tm, C_V_WIDTH), lambda i: (i, 0)),
                  pl.BlockSpec((None, C_V_WIDTH, D_MODEL), lambda i: (0, 0, 0), pipeline_mode=pl.Buffered(1))],
        out_specs=pl.BlockSpec((tm, D_MODEL), lambda i: (i, 0)),
        out_shape=jax.ShapeDtypeStruct((t, D_MODEL), F32),
        scratch_shapes=[pltpu.VMEM((C_V_WIDTH, D_MODEL), BF16)],
        compiler_params=_cparams(("arbitrary",)),
        name="odd_out_proj",
    )(h, y, w)


def _router_kernel(h_ref, g_ref, wr_ref, br_ref, x2s_ref, codes_ref, gates_ref, cnt_ref, base_ref, *,
                   tiles_per_batch):
    i = pl.program_id(0)
    tm = h_ref.shape[0]
    batch = i // tiles_per_batch

    @pl.when(i % tiles_per_batch == 0)
    def _():
        base_ref[...] = jnp.zeros(base_ref.shape, F32)

    x2 = _rms(h_ref[...], g_ref[...])
    for j in range(SLABS):
        x2s_ref[pl.ds(j, tm, stride=SLABS), :] = x2[:, j * LANES:(j + 1) * LANES]
    wr = wr_ref[...]
    x_hi, w_hi = x2.astype(BF16), wr.astype(BF16)
    x_lo = (x2 - x_hi.astype(F32)).astype(BF16)
    w_lo = (wr - w_hi.astype(F32)).astype(BF16)
    both = jnp.dot(x_hi, jnp.concatenate([w_hi, w_lo], axis=1), preferred_element_type=F32)
    logits = both[:, :LANES] + both[:, LANES:] + jnp.dot(x_lo, w_hi, preferred_element_type=F32) + br_ref[...]
    lane = lax.broadcasted_iota(I32, (tm, LANES), 1)
    is_grp = lane < MOE_GROUPS
    lg = jnp.where(is_grp, logits, -jnp.inf)
    gmax = jnp.max(lg, axis=-1, keepdims=True)
    gsum = jnp.sum(jnp.where(is_grp, jnp.exp(logits - gmax), 0.0), axis=-1, keepdims=True)
    gp = 1.0 / gsum
    gi = jnp.min(jnp.where(lg == gmax, lane, LANES), axis=-1, keepdims=True)
    ex = lane - ROUTER_LANE0
    in_grp = (ex >= 0) & (ex < MOE_EXPERTS) & ((ex // MOE_EXPERTS_PER_GROUP) == gi)
    sel = jnp.where(in_grp, logits, -jnp.inf)
    v1 = jnp.max(sel, axis=-1, keepdims=True)
    i1 = jnp.min(jnp.where(sel == v1, lane, LANES), axis=-1, keepdims=True)
    sel2 = jnp.where(lane == i1, -jnp.inf, sel)
    v2 = jnp.max(sel2, axis=-1, keepdims=True)
    i2 = jnp.min(jnp.where(sel2 == v2, lane, LANES), axis=-1, keepdims=True)
    tt = jnp.exp(v2 - v1)
    g0 = gp / (1.0 + tt)
    g1 = gp * tt / (1.0 + tt)
    oh0 = (lane == i1).astype(F32)
    oh1 = (lane == i2).astype(F32)
    oh = oh0 + oh1
    rr = lax.broadcasted_iota(I32, (tm, tm), 0)
    cc = lax.broadcasted_iota(I32, (tm, tm), 1)
    tri = (cc < rr).astype(BF16)
    prefix = jnp.dot(tri, oh.astype(BF16), preferred_element_type=F32)
    tot = base_ref[...] + prefix
    rank0 = jnp.sum(oh0 * tot, axis=-1, keepdims=True).astype(I32)
    rank1 = jnp.sum(oh1 * tot, axis=-1, keepdims=True).astype(I32)
    base_ref[...] = base_ref[...] + jnp.sum(oh, axis=0, keepdims=True)
    eoff = batch * MOE_EXPERTS - ROUTER_LANE0
    code0 = (i1 + eoff) * (1 << RANK_BITS) + rank0
    code1 = (i2 + eoff) * (1 << RANK_BITS) + rank1
    meta = jnp.where(lane == 0, code0, jnp.where(lane == 1, code1, 0))
    codes_ref[...] = jnp.transpose(meta)[0:8, :]
    gates_ref[...] = jnp.where(lane == 0, g0, jnp.where(lane == 1, g1, 0.0))
    cnt_ref[...] = jnp.broadcast_to(base_ref[...], cnt_ref.shape)


def _router(h, gain, wr, br, batch):
    t = h.shape[0]
    tm = ROW_TILE
    tpb = t // batch // tm
    row = lambda width: pl.BlockSpec((tm, width), lambda i: (i, 0))
    fixed = lambda shape: pl.BlockSpec(shape, lambda i: (0, 0))
    return pl.pallas_call(
        functools.partial(_router_kernel, tiles_per_batch=tpb),
        grid=(t // tm,),
        in_specs=[row(D_MODEL), fixed((1, D_MODEL)), fixed((D_MODEL, LANES)), fixed((1, LANES))],
        out_specs=[pl.BlockSpec((tm * SLABS, LANES), lambda i: (i, 0)),
                   pl.BlockSpec((8, tm), lambda i: (0, i)), row(LANES),
                   pl.BlockSpec((8, LANES), lambda i: (i // tpb, 0))],
        out_shape=[
            jax.ShapeDtypeStruct((t * SLABS, LANES), F32),
            jax.ShapeDtypeStruct((8, t), I32),
            jax.ShapeDtypeStruct((t, LANES), F32),
            jax.ShapeDtypeStruct((batch * 8, LANES), F32),
        ],
        scratch_shapes=[pltpu.VMEM((1, LANES), F32)],
        compiler_params=_cparams(("arbitrary",)),
        name="moe_router",
    )(h, gain.reshape(1, D_MODEL), wr, br)


INVERT_UNROLL = 16


def _dest_kernel(pstart_ref, codes_ref, dest_ref):
    codes = codes_ref[...]
    seg = lax.shift_right_logical(codes, RANK_BITS)
    dest = codes & ((1 << RANK_BITS) - 1)
    for i in range(pstart_ref.shape[0]):
        dest = dest + jnp.where(seg == i, pstart_ref[i], 0)
    dest_ref[...] = dest


def _dest_rows(pad_start, codes):
    grid_spec = pltpu.PrefetchScalarGridSpec(
        num_scalar_prefetch=1,
        grid=(1,),
        in_specs=[pl.BlockSpec(codes.shape, lambda i, ps: (0, 0))],
        out_specs=pl.BlockSpec(codes.shape, lambda i, ps: (0, 0)),
    )
    return pl.pallas_call(
        _dest_kernel,
        grid_spec=grid_spec,
        out_shape=jax.ShapeDtypeStruct(codes.shape, I32),
        compiler_params=_cparams(("arbitrary",)),
        name="moe_dest_rows",
    )(pad_start, codes)


def _invert_kernel(trips_ref, dest_ref, dummy_ref, slot_ref, *, batch):
    per_batch = dest_ref.shape[0] // batch
    pltpu.sync_copy(dummy_ref, slot_ref)
    for b in range(batch):
        def place(i, carry, b=b):
            for v in range(INVERT_UNROLL):
                a = i * INVERT_UNROLL + v
                slot_ref[dest_ref[b * per_batch + a]] = a * SLABS
            return carry

        lax.fori_loop(0, trips_ref[0], place, 0)


def _dummy_slots(n_rows, per_batch):
    assert MOE_BLOCK & (MOE_BLOCK - 1) == 0 and 3 * MOE_BLOCK <= DUMMY_SLOTS
    r = jnp.arange(n_rows + MOE_BLOCK, dtype=I32)
    pattern = per_batch + (r // MOE_BLOCK) % 2 * MOE_BLOCK + r % MOE_BLOCK
    return jnp.where(r < n_rows, pattern, per_batch + 2 * MOE_BLOCK + r % MOE_BLOCK) * SLABS


def _invert(dest_flat, n_rows, batch):
    smem = pl.BlockSpec(memory_space=pltpu.SMEM)
    per_batch = dest_flat.shape[0] // batch
    assert per_batch % INVERT_UNROLL == 0
    trips = jnp.full((1,), per_batch // INVERT_UNROLL, I32)
    return pl.pallas_call(
        functools.partial(_invert_kernel, batch=batch),
        in_specs=[smem, smem, pl.BlockSpec(memory_space=pl.ANY)],
        out_specs=smem,
        out_shape=jax.ShapeDtypeStruct((n_rows + MOE_BLOCK,), I32),
        name="moe_invert_rows",
    )(trips, dest_flat, _dummy_slots(n_rows, per_batch))


def _expert_kernel(be_ref, run_ref, nxt_ref, nreal_ref, slot_ref, x2s_ref, wg_hbm, wu_hbm, wd_hbm, ys_ref,
                   wbuf_g, wbuf_u, wbuf_d, wgb_ref, wub_ref, wdb_ref, tile, xb, ybuf, sems, wsems, gsems, *,
                   batch, nb, seq, layer):
    b = pl.program_id(0)
    n = pl.program_id(1)
    g = b * nb + n
    last = batch * nb - 1
    blk = MOE_BLOCK
    bstride = 2 * seq + DUMMY_SLOTS
    ring = 4
    ycur = g % ring
    yprev = (g + ring - 1) % ring
    n_real = nreal_ref[0]

    def weight_copies(e, ws):
        pairs = ((wg_hbm, wbuf_g), (wu_hbm, wbuf_u), (wd_hbm, wbuf_d))
        return [pltpu.make_async_copy(src.at[layer, e], dst.at[ws], wsems.at[ws, k])
                for k, (src, dst) in enumerate(pairs)]

    def wait_block(q):
        pltpu.make_async_copy(ybuf.at[q], ys_ref.at[pl.ds(0, blk * SLABS), :], sems.at[q]).wait()

    def gather_row(block, tslot, mi):
        s = slot_ref[jnp.minimum(block, last) * blk + mi]
        span = seq * SLABS
        if span & (span - 1) == 0:
            src = pl.multiple_of(s & (span - 1), SLABS)
        else:
            src = pl.multiple_of(jnp.minimum(jnp.where(s >= span, s - span, s), span - SLABS), SLABS)
        pltpu.make_async_copy(x2s_ref.at[pl.ds(src, SLABS), :], tile.at[tslot, pl.ds(mi * SLABS, SLABS), :],
                              gsems.at[tslot]).start()

    def wait_gather(tslot):
        pltpu.make_async_copy(x2s_ref.at[pl.ds(0, blk * SLABS), :], tile.at[tslot], gsems.at[tslot]).wait()

    def to_bf16(tslot, xslot):
        for j in range(SLABS):
            xb[xslot, :, j * LANES:(j + 1) * LANES] = tile[tslot, pl.ds(j, blk, stride=SLABS), :].astype(BF16)

    def scatter_row(yslot, slot, row0, mi):
        dst = pl.multiple_of(row0 * SLABS + slot, SLABS)
        pltpu.make_async_copy(ybuf.at[yslot, pl.ds(mi * SLABS, SLABS), :], ys_ref.at[pl.ds(dst, SLABS), :],
                              sems.at[yslot]).start()

    @pl.when(g == 0)
    def _():
        for c in weight_copies(be_ref[0], 0):
            c.start()
        ybuf[...] = jnp.zeros(ybuf.shape, F32)
        for bb in range(batch):
            for c in range(DUMMY_SLOTS // blk):
                start = (bb * bstride + 2 * seq + c * blk) * SLABS
                zero = pltpu.make_async_copy(ybuf.at[0], ys_ref.at[pl.ds(start, blk * SLABS), :], sems.at[0])
                zero.start()
                zero.wait()
        for mi in range(blk):
            gather_row(0, 0, mi)
        wait_gather(0)
        to_bf16(0, 0)
        for first in (1, 2):
            for mi in range(blk):
                gather_row(first, first, mi)

    run = run_ref[g]
    ws = run % 2

    @pl.when((g == 0) | (run != run_ref[jnp.maximum(g - 1, 0)]))
    def _():
        for c in weight_copies(be_ref[g], ws):
            c.wait()
        wgb_ref[...] = wbuf_g[ws].astype(BF16)
        wub_ref[...] = wbuf_u[ws].astype(BF16)
        wdb_ref[...] = wbuf_d[ws].astype(BF16)

        @pl.when(nxt_ref[g] >= 0)
        def _():
            for c in weight_copies(nxt_ref[g], 1 - ws):
                c.start()

    @pl.when(g >= ring - 1)
    def _():
        wait_block(ycur)

    prev_block = jnp.where(g == 0, batch * nb, g - 1)
    prev_row0 = jnp.where(n == 0, jnp.maximum(b - 1, 0), b) * bstride
    n_chunks = 8
    per = blk // n_chunks
    half = EXPERT_HIDDEN // 2
    quarter = D_MODEL // 4

    @pl.when(g < n_real)
    def _():
        x_cur, y_cur, y_prev = g % 2, ycur, yprev
        next_slot, fetch_slot = (g + 1) % ring, (g + 3) % ring
        wait_gather(next_slot)
        for mi in range(blk):
            scatter_row(y_prev, slot_ref[prev_block * blk + mi], prev_row0, mi)
        to_bf16(next_slot, 1 - x_cur)
        x = xb[x_cur]
        acts = []
        hid = None
        for c in range(n_chunks):
            for mi in range(c * per, (c + 1) * per):
                gather_row(g + 3, fetch_slot, mi)
            if c < 4:
                w_ref = wgb_ref if c < 2 else wub_ref
                acts.append(jnp.dot(x, w_ref[:, (c % 2) * half:(c % 2 + 1) * half], preferred_element_type=F32))
            if c == 3:
                a = jnp.concatenate(acts[0:2], axis=1)
                u = jnp.concatenate(acts[2:4], axis=1)
                hid = (a * jax.nn.sigmoid(a) * u).astype(BF16)
            if c >= 4:
                q = c - 4
                yq = jnp.dot(hid, wdb_ref[:, q * quarter:(q + 1) * quarter], preferred_element_type=F32)
                for jj in range(quarter // LANES):
                    j = q * (quarter // LANES) + jj
                    ybuf[y_cur, pl.ds(j, blk, stride=SLABS), :] = yq[:, jj * LANES:(jj + 1) * LANES]

    @pl.when(g >= n_real)
    def _():
        for mi in range(blk):
            scatter_row(yprev, slot_ref[prev_block * blk + mi], prev_row0, mi)

    @pl.when(g == last)
    def _():
        wait_block((last - 3) % ring)
        wait_block((last - 2) % ring)
        for mi in range(blk):
            scatter_row(last % ring, slot_ref[last * blk + mi], (batch - 1) * bstride, mi)
        wait_block((last - 1) % ring)
        wait_block(last % ring)
        wait_gather((n_real + 1) % ring)
        wait_gather((n_real + 2) % ring)


def _experts(block_expert, n_real, row_slot, x2s, wg, wu, wd, layer, batch, seq):
    blk = MOE_BLOCK
    nb = (row_slot.shape[0] // blk - 1) // batch
    steps = batch * nb
    assert steps >= 4 and batch == 1
    idx = jnp.arange(steps, dtype=I32)
    change = jnp.concatenate([jnp.zeros((1,), I32), (block_expert[1:] != block_expert[:-1]).astype(I32)])
    run = jnp.cumsum(change).astype(I32)
    later_change = (idx[None, :] > idx[:, None]) & (change[None, :] > 0)
    nxt_idx = jnp.min(jnp.where(later_change, idx[None, :], steps), axis=1)
    nxt = jnp.where(nxt_idx < steps, block_expert[jnp.minimum(nxt_idx, steps - 1)], -1).astype(I32)
    hbm = pl.BlockSpec(memory_space=pl.ANY)
    grid_spec = pltpu.PrefetchScalarGridSpec(
        num_scalar_prefetch=5,
        grid=(batch, nb),
        in_specs=[hbm, hbm, hbm, hbm],
        out_specs=hbm,
        scratch_shapes=[
            pltpu.VMEM((2, D_MODEL, EXPERT_HIDDEN), F32), pltpu.VMEM((2, D_MODEL, EXPERT_HIDDEN), F32),
            pltpu.VMEM((2, EXPERT_HIDDEN, D_MODEL), F32),
            pltpu.VMEM((D_MODEL, EXPERT_HIDDEN), BF16), pltpu.VMEM((D_MODEL, EXPERT_HIDDEN), BF16),
            pltpu.VMEM((EXPERT_HIDDEN, D_MODEL), BF16),
            pltpu.VMEM((4, blk * SLABS, LANES), F32),
            pltpu.VMEM((2, blk, D_MODEL), BF16),
            pltpu.VMEM((4, blk * SLABS, LANES), F32),
            pltpu.SemaphoreType.DMA((4,)),
            pltpu.SemaphoreType.DMA((2, 3)),
            pltpu.SemaphoreType.DMA((4,)),
        ],
    )
    return pl.pallas_call(
        functools.partial(_expert_kernel, batch=batch, nb=nb, seq=seq, layer=layer),
        grid_spec=grid_spec,
        out_shape=jax.ShapeDtypeStruct((batch * (2 * seq + DUMMY_SLOTS) * SLABS, LANES), F32),
        compiler_params=_cparams(("arbitrary", "arbitrary")),
        name="moe_experts",
    )(block_expert, run, nxt, n_real, row_slot, x2s, wg, wu, wd)


def _combine_kernel(h_ref, gates_ref, y0_ref, y1_ref, *rest, final):
    if final:
        fg_ref, out_ref = rest
    else:
        (out_ref,) = rest
    tm = h_ref.shape[0]
    gates = gates_ref[...]
    g0, g1 = gates[:, 0:1], gates[:, 1:2]
    parts = []
    for j in range(SLABS):
        y0 = y0_ref[pl.ds(j, tm, stride=SLABS), :]
        y1 = y1_ref[pl.ds(j, tm, stride=SLABS), :]
        parts.append(g0 * y0 + g1 * y1)
    out = h_ref[...] + jnp.concatenate(parts, axis=1)
    if final:
        out = _rms(out, fg_ref[...])
    out_ref[...] = out


def _combine(h, gates, ys, batch, seq, final_gain=None):
    t = h.shape[0]
    tm = ROW_TILE
    tpb = seq // tm
    bstride = 2 * tpb + DUMMY_SLOTS // tm
    final = final_gain is not None
    in_specs = [pl.BlockSpec((tm, D_MODEL), lambda i: (i, 0)),
                pl.BlockSpec((tm, LANES), lambda i: (i, 0)),
                pl.BlockSpec((tm * SLABS, LANES), lambda i: (i // tpb * bstride + i % tpb, 0)),
                pl.BlockSpec((tm * SLABS, LANES), lambda i: (i // tpb * bstride + tpb + i % tpb, 0))]
    args = [h, gates, ys, ys]
    if final:
        in_specs.append(pl.BlockSpec((1, D_MODEL), lambda i: (0, 0)))
        args.append(final_gain.reshape(1, D_MODEL))
    return pl.pallas_call(
        functools.partial(_combine_kernel, final=final),
        grid=(t // tm,),
        in_specs=in_specs,
        out_specs=pl.BlockSpec((tm, D_MODEL), lambda i: (i, 0)),
        out_shape=jax.ShapeDtypeStruct((t, D_MODEL), F32),
        compiler_params=_cparams(("parallel",)),
        name="moe_combine_final" if final else "moe_combine",
    )(*args)


def _moe(h, gain, w_r1, b_r1, w_r2, b_r2, wg, wu, wd, layer, batch, final_gain=None):
    t = h.shape[0]
    seq = t // batch
    assert seq <= (1 << RANK_BITS) and seq % ROW_TILE == 0 and DUMMY_SLOTS % ROW_TILE == 0
    pad_w = jnp.zeros((D_MODEL, LANES - MOE_GROUPS - MOE_EXPERTS), F32)
    wr = jnp.concatenate([w_r1, w_r2, pad_w], axis=1)
    br = jnp.concatenate([b_r1, b_r2, jnp.zeros((LANES - MOE_GROUPS - MOE_EXPERTS,), F32)]).reshape(1, LANES)
    x2s, codes, gates, counts = _router(h, gain, wr, br, batch)
    cnt = counts.reshape(batch, 8, LANES)[:, 0, ROUTER_LANE0:ROUTER_LANE0 + MOE_EXPERTS].astype(I32)
    padded = (cnt + MOE_BLOCK - 1) // MOE_BLOCK * MOE_BLOCK
    pad_end = jnp.cumsum(padded, axis=1)
    rows_pb = seq * 2 + MOE_EXPERTS * MOE_BLOCK
    nb = rows_pb // MOE_BLOCK
    pad_start = pad_end - padded + (jnp.arange(batch, dtype=I32) * rows_pb)[:, None]
    block_start = jnp.arange(nb, dtype=I32) * MOE_BLOCK
    block_expert = jnp.minimum(
        jnp.sum((pad_end[:, None, :] <= block_start[None, :, None]).astype(I32), axis=2), MOE_EXPERTS - 1)
    dest = _dest_rows(pad_start.reshape(-1), codes)
    dest_flat = dest[0:2].reshape(2, batch, seq).transpose(1, 0, 2).reshape(-1)
    row_slot = _invert(dest_flat, batch * rows_pb, batch)
    n_real = (pad_end[:, -1] // MOE_BLOCK).astype(I32)
    ys = _experts(block_expert.reshape(-1), n_real, row_slot, x2s, wg, wu, wd, layer, batch, seq)
    return _combine(h, gates, ys, batch, seq, final_gain)


def kernel(x, mix_norm, ffn_norm, final_norm, even_w_in, even_w_out, conv_w, conv_b, conv_norm_g, conv_norm_b,
           odd_w_in, odd_w_out, router_w1, router_b1, router_w2, router_b2, expert_w_gate, expert_w_up,
           expert_w_down):
    batch, seq, d = x.shape
    assert d == D_MODEL and seq % (ATTN_BLOCK * max(dl for _, dl in A_BRANCHES)) == 0
    assert all(w // dl == ATTN_BLOCK for w, dl in A_BRANCHES)
    t = batch * seq
    h = x.reshape(t, d)

    qkv, conv_in = _even_in(h, mix_norm[0], even_w_in, _attn_rope_tables(seq), seq)
    attn = _attention(qkv, batch, seq)
    bconv = _conv(conv_in, conv_w[0], conv_b[0], conv_norm_g[0], conv_norm_b[0], batch, seq)
    h = _even_out(h, attn, bconv, even_w_out)
    h = _moe(h, ffn_norm[0], router_w1[0], router_b1[0], router_w2[0], router_b2[0],
             expert_w_gate, expert_w_up, expert_w_down, 0, 1)

    inv_freq = RET_ROT_THETA ** (-jnp.linspace(0.0, 1.0, C_QK_DIM // 2, dtype=F32))
    ang = jnp.arange(seq, dtype=F32)[:, None] * inv_freq[None, :]
    proj = _odd_in(h, mix_norm[1], odd_w_in, jnp.cos(ang), jnp.sin(ang), seq)
    log_decay = jnp.log(1.0 - jnp.exp2(-5.0 - jnp.arange(C_HEADS, dtype=F32)))
    y = _retention(proj, log_decay, batch, seq)
    h = _odd_out(h, y, odd_w_out)
    out = _moe(h, ffn_norm[1], router_w1[1], router_b1[1], router_w2[1], router_b2[1],
               expert_w_gate, expert_w_up, expert_w_down, 1, 1, final_gain=final_norm)
    return out.reshape(batch, seq, d)
```

```python
import functools

import jax
import jax.numpy as jnp
from jax import lax
from jax.experimental import pallas as pl
from jax.experimental.pallas import tpu as pltpu

F32 = jnp.float32
BF16 = jnp.bfloat16
I32 = jnp.int32

NORM_EPS = 1e-6
NEG_INF = -1e30

D_MODEL = 1024
A_HEADS = 8
A_HEAD_DIM = 64
A_WIDTH = A_HEADS * A_HEAD_DIM
A_BRANCHES = ((128, 1), (512, 4), (2048, 16))
ATTN_BLOCK = 128
ROPE_THETA = 500000.0
ROPE_DIM = A_HEAD_DIM // 4
B_WIDTH = D_MODEL - A_WIDTH
CONV_WIDTH = 31
C_HEADS = 4
C_QK_DIM = 256
C_V_DIM = 512
C_QK_WIDTH = C_HEADS * C_QK_DIM
C_V_WIDTH = C_HEADS * C_V_DIM
RET_CHUNK = 128
RET_ROT_THETA = 10000.0
MOE_GROUPS = 4
MOE_EXPERTS_PER_GROUP = 8
MOE_EXPERTS = MOE_GROUPS * MOE_EXPERTS_PER_GROUP
EXPERT_HIDDEN = 512
MOE_BLOCK = 128
EVEN_IN_WIDTH = 3 * A_WIDTH + 2 * B_WIDTH
ODD_IN_WIDTH = 2 * C_QK_WIDTH + 2 * C_V_WIDTH

LANES = 128
ROW_TILE = 512
WIDE_ROW_TILE = 1024
CONV_TILE = 512
CONV_HALO = 32
CONV_CHUNK = 64
ROUTER_LANE0 = MOE_GROUPS
VMEM_LIMIT = 56 * 1024 * 1024
SLABS = D_MODEL // LANES
RANK_BITS = 16
EXPERT_RING = 4
DUMMY_SLOTS = 2 * ROW_TILE


def _cparams(sem, vmem=VMEM_LIMIT):
    return pltpu.CompilerParams(dimension_semantics=sem, vmem_limit_bytes=vmem)


def _cast_once(w_ref, wb_ref, first):
    @pl.when(first)
    def _():
        wb_ref[...] = w_ref[...].astype(BF16)


def _rms(x, gain):
    ms = jnp.mean(x * x, axis=-1, keepdims=True)
    return x * lax.rsqrt(ms + NORM_EPS) * gain


def _even_in_kernel(h_ref, g_ref, w_ref, c_ref, s1_ref, s2_ref, qkv_ref, conv_ref, wb_ref):
    _cast_once(w_ref, wb_ref, pl.program_id(0) == 0)
    u = _rms(h_ref[...], g_ref[...]).astype(BF16)
    acc = jnp.dot(u, wb_ref[...], preferred_element_type=F32)
    c, s1, s2 = c_ref[...], s1_ref[...], s2_ref[...]
    for j in range(2 * A_WIDTH // LANES):
        xg = acc[:, j * LANES:(j + 1) * LANES]
        if j < A_WIDTH // LANES:
            xg = xg * (A_HEAD_DIM ** -0.5)
        qkv_ref[j] = xg * c + pltpu.roll(xg, LANES - ROPE_DIM // 2, 1) * s1 + pltpu.roll(xg, ROPE_DIM // 2, 1) * s2
    for j in range(2 * A_WIDTH // LANES, 3 * A_WIDTH // LANES):
        qkv_ref[j] = acc[:, j * LANES:(j + 1) * LANES]
    conv_ref[...] = acc[:, 3 * A_WIDTH:].astype(BF16)


def _even_in(h, gain, w, tabs, seq):
    t = h.shape[0]
    tm = ROW_TILE
    nseq = seq // tm
    tab_spec = pl.BlockSpec((tm, LANES), lambda i: (i % nseq, 0))
    return pl.pallas_call(
        _even_in_kernel,
        grid=(t // tm,),
        in_specs=[
            pl.BlockSpec((tm, D_MODEL), lambda i: (i, 0)),
            pl.BlockSpec((1, D_MODEL), lambda i: (0, 0)),
            pl.BlockSpec((None, D_MODEL, EVEN_IN_WIDTH), lambda i: (0, 0, 0)),
            tab_spec, tab_spec, tab_spec,
        ],
        out_specs=[
            pl.BlockSpec((3 * A_WIDTH // LANES, tm, LANES), lambda i: (0, i, 0)),
            pl.BlockSpec((tm, 2 * B_WIDTH), lambda i: (i, 0)),
        ],
        out_shape=[
            jax.ShapeDtypeStruct((3 * A_WIDTH // LANES, t, LANES), F32),
            jax.ShapeDtypeStruct((t, 2 * B_WIDTH), BF16),
        ],
        scratch_shapes=[pltpu.VMEM((D_MODEL, EVEN_IN_WIDTH), BF16)],
        compiler_params=_cparams(("arbitrary",)),
        name="even_in_proj",
    )(h, gain.reshape(1, D_MODEL), w, *tabs)


def _attn_rope_tables(seq):
    half = ROPE_DIM // 2
    inv_freq = ROPE_THETA ** (-jnp.arange(0, ROPE_DIM, 2, dtype=F32) / ROPE_DIM)
    ang = jnp.arange(seq, dtype=F32)[:, None] * inv_freq[None, :]
    cos, sin = jnp.cos(ang), jnp.sin(ang)
    rest = A_HEAD_DIM - ROPE_DIM
    ones = jnp.ones((seq, rest), F32)
    z_rest = jnp.zeros((seq, rest), F32)
    z_half = jnp.zeros((seq, half), F32)
    c = jnp.concatenate([cos, cos, ones], axis=1)
    s1 = jnp.concatenate([-sin, z_half, z_rest], axis=1)
    s2 = jnp.concatenate([z_half, sin, z_rest], axis=1)
    rep = LANES // A_HEAD_DIM
    return tuple(jnp.tile(a, (1, rep)) for a in (c, s1, s2))


ATTN_STEP = ATTN_BLOCK * max(d for _, d in A_BRANCHES)
ATTN_PAIRS = A_WIDTH // LANES
ATTN_MERGE_ROWS = 256


def _attn_kernel(q_ref, k_ref, v_ref, o_ref, kprev, vprev, oml):
    n = pl.program_id(1)
    blk = ATTN_BLOCK
    n_blocks = ATTN_STEP // blk

    @pl.when(n == 0)
    def _():
        kprev[...] = jnp.zeros(kprev.shape, BF16)
        vprev[...] = jnp.zeros(vprev.shape, BF16)

    qi = lax.broadcasted_iota(I32, (blk, 2 * blk), 0)
    kj = lax.broadcasted_iota(I32, (blk, 2 * blk), 1)
    dist = blk + qi - kj
    band = (dist >= 0) & (dist <= blk)
    in_cur = kj >= blk
    lane = lax.broadcasted_iota(I32, (blk, LANES), 1)
    lo = lane < A_HEAD_DIM

    off = 0
    for bi, (_, dil) in enumerate(A_BRANCHES):
        shift = dil.bit_length() - 1
        assert dil == 1 << shift

        def one_block(idx, carried, bi=bi, dil=dil, shift=shift, off=off):
            u = lax.shift_right_logical(idx, shift)
            r = idx & (dil - 1)
            start = u * (blk * dil) + r
            rows = pl.ds(pl.multiple_of(start, blk), blk) if dil == 1 else pl.ds(start, blk, stride=dil)
            has_prev = (n > 0) | (u > 0)
            valid = band & (in_cur | has_prev)
            valid2 = jnp.concatenate([valid, valid], axis=0)
            current = []
            for g in range(ATTN_PAIRS):
                q = q_ref[g, rows, :]
                kc = k_ref[g, rows, :].astype(BF16)
                vc = v_ref[g, rows, :].astype(BF16)
                current.append((kc, vc))
                kp, vp = (kprev[g, off + r], vprev[g, off + r]) if carried is None else carried[g]
                q2 = jnp.concatenate([jnp.where(lo, q, 0.0), jnp.where(lo, 0.0, q)], axis=0).astype(BF16)
                kk = jnp.concatenate([kp, kc], axis=0)
                vv = jnp.concatenate([vp, vc], axis=0)
                s = lax.dot_general(q2, kk, (((1,), (1,)), ((), ())), preferred_element_type=F32)
                s = jnp.where(valid2, s, NEG_INF)
                m = jnp.max(s, axis=-1, keepdims=True)
                e = jnp.exp(s - m)
                den = jnp.sum(e, axis=-1, keepdims=True)
                pv = jnp.dot(e.astype(BF16), vv, preferred_element_type=F32)
                o_new = jnp.where(lo, pv[:blk], pv[blk:])
                m_new = jnp.where(lo, m[:blk], m[blk:])
                l_new = jnp.where(lo, den[:blk], den[blk:])
                if bi > 0:
                    o_run, m_run, l_run = oml[0, g, rows, :], oml[1, g, rows, :], oml[2, g, rows, :]
                    m_both = jnp.maximum(m_run, m_new)
                    w_run, w_new = jnp.exp(m_run - m_both), jnp.exp(m_new - m_both)
                    o_new = w_run * o_run + w_new * o_new
                    l_new = w_run * l_run + w_new * l_new
                    m_new = m_both
                oml[0, g, rows, :] = o_new
                oml[1, g, rows, :] = m_new
                oml[2, g, rows, :] = l_new
            return r, current

        def save_prev(r, current, off=off):
            for g, (kc, vc) in enumerate(current):
                kprev[g, off + r] = kc
                vprev[g, off + r] = vc

        def block_body(i, c, dil=dil):
            r0, cur0 = one_block(2 * i, None)
            if dil == 1:
                r1, cur1 = one_block(2 * i + 1, cur0)
            else:
                save_prev(r0, cur0)
                r1, cur1 = one_block(2 * i + 1, None)
            save_prev(r1, cur1)
            return c

        lax.fori_loop(0, n_blocks // 2, block_body, 0)
        off += dil
    for g in range(ATTN_PAIRS):
        for c in range(ATTN_STEP // ATTN_MERGE_ROWS):
            sl = slice(c * ATTN_MERGE_ROWS, (c + 1) * ATTN_MERGE_ROWS)
            o_ref[g, sl, :] = (oml[0, g, sl, :] / oml[2, g, sl, :]).astype(BF16)


def _attention(qkv, batch, seq):
    t = qkv.shape[1]
    steps = seq // ATTN_STEP
    n_res = sum(d for _, d in A_BRANCHES)

    def slabs(which):
        return pl.BlockSpec((ATTN_PAIRS, ATTN_STEP, LANES), lambda b, n: (which, b * steps + n, 0))

    return pl.pallas_call(
        _attn_kernel,
        grid=(batch, steps),
        in_specs=[slabs(0), slabs(1), slabs(2)],
        out_specs=pl.BlockSpec((ATTN_PAIRS, ATTN_STEP, LANES), lambda b, n: (0, b * steps + n, 0)),
        out_shape=jax.ShapeDtypeStruct((ATTN_PAIRS, t, LANES), BF16),
        scratch_shapes=[
            pltpu.VMEM((ATTN_PAIRS, n_res, ATTN_BLOCK, LANES), BF16),
            pltpu.VMEM((ATTN_PAIRS, n_res, ATTN_BLOCK, LANES), BF16),
            pltpu.VMEM((3, ATTN_PAIRS, ATTN_STEP, LANES), F32),
        ],
        compiler_params=_cparams(("parallel", "arbitrary")),
        name="dilated_attention",
    )(qkv, qkv, qkv)


def _conv_kernel(val_ref, gate_ref, w_ref, b_ref, g_ref, beta_ref, o_ref, abuf, shifted):
    n = pl.program_id(1)
    tc = CONV_TILE
    sub = 8

    @pl.when(n == 0)
    def _():
        abuf[0:CONV_HALO, :] = jnp.zeros((CONV_HALO, B_WIDTH), F32)

    @pl.when(n > 0)
    def _():
        abuf[0:CONV_HALO, :] = abuf[tc:tc + CONV_HALO, :]

    val = val_ref[...].astype(F32)
    gate = gate_ref[...].astype(F32)
    abuf[CONV_HALO:CONV_HALO + tc, :] = val * jax.nn.sigmoid(gate)
    off = CONV_HALO - (CONV_WIDTH - 1)
    span = shifted.shape[1]
    for s in range(1, sub):
        shifted[s - 1] = abuf[s:s + span, :]
    for c in range(tc // CONV_CHUNK):
        acc = jnp.broadcast_to(b_ref[...], (CONV_CHUNK, B_WIDTH))
        for j in range(CONV_WIDTH):
            s = (off + j) % sub
            r0 = c * CONV_CHUNK + off + j - s
            src = abuf[r0:r0 + CONV_CHUNK, :] if s == 0 else shifted[s - 1, r0:r0 + CONV_CHUNK, :]
            acc = acc + w_ref[j:j + 1, :] * src
        mu = jnp.mean(acc, axis=-1, keepdims=True)
        cen = acc - mu
        var = jnp.mean(cen * cen, axis=-1, keepdims=True)
        yn = cen * lax.rsqrt(var + NORM_EPS) * g_ref[...] + beta_ref[...]
        o_ref[c * CONV_CHUNK:(c + 1) * CONV_CHUNK, :] = (yn * jax.nn.sigmoid(yn)).astype(BF16)


def _conv(conv_in, w, b, g, beta, batch, seq):
    t = conv_in.shape[0]
    tc = CONV_TILE
    nt = seq // tc
    w_pad = jnp.concatenate([w, jnp.zeros((CONV_HALO - CONV_WIDTH, B_WIDTH), F32)], axis=0)
    vec = pl.BlockSpec((1, B_WIDTH), lambda bb, n: (0, 0))
    return pl.pallas_call(
        _conv_kernel,
        grid=(batch, nt),
        in_specs=[
            pl.BlockSpec((tc, B_WIDTH), lambda bb, n: (bb * nt + n, 0)),
            pl.BlockSpec((tc, B_WIDTH), lambda bb, n: (bb * nt + n, 1)),
            pl.BlockSpec((CONV_HALO, B_WIDTH), lambda bb, n: (0, 0)),
            vec, vec, vec,
        ],
        out_specs=pl.BlockSpec((tc, B_WIDTH), lambda bb, n: (bb * nt + n, 0)),
        out_shape=jax.ShapeDtypeStruct((t, B_WIDTH), BF16),
        scratch_shapes=[pltpu.VMEM((tc + CONV_HALO, B_WIDTH), F32),
                        pltpu.VMEM((7, tc + CONV_HALO - 8, B_WIDTH), F32)],
        compiler_params=_cparams(("parallel", "arbitrary")),
        name="conformer_conv",
    )(conv_in, conv_in, w_pad, b.reshape(1, -1), g.reshape(1, -1), beta.reshape(1, -1))


def _even_out_kernel(h_ref, a_ref, bc_ref, w_ref, out_ref, wb_ref):
    _cast_once(w_ref, wb_ref, pl.program_id(0) == 0)
    a = jnp.concatenate([a_ref[g] for g in range(ATTN_PAIRS)], axis=1)
    acc = jnp.dot(a, wb_ref[0:A_WIDTH, :], preferred_element_type=F32)
    acc = acc + jnp.dot(bc_ref[...], wb_ref[A_WIDTH:, :], preferred_element_type=F32)
    out_ref[...] = h_ref[...] + acc


def _even_out(h, attn, bconv, w):
    t = h.shape[0]
    tm = WIDE_ROW_TILE
    row = lambda width: pl.BlockSpec((tm, width), lambda i: (i, 0))
    return pl.pallas_call(
        _even_out_kernel,
        grid=(t // tm,),
        in_specs=[row(D_MODEL), pl.BlockSpec((ATTN_PAIRS, tm, LANES), lambda i: (0, i, 0)), row(B_WIDTH),
                  pl.BlockSpec((None, D_MODEL, D_MODEL), lambda i: (0, 0, 0), pipeline_mode=pl.Buffered(1))],
        out_specs=row(D_MODEL),
        out_shape=jax.ShapeDtypeStruct((t, D_MODEL), F32),
        scratch_shapes=[pltpu.VMEM((D_MODEL, D_MODEL), BF16)],
        compiler_params=_cparams(("arbitrary",)),
        name="even_out_proj",
    )(h, attn, bconv, w)


def _odd_in_kernel(h_ref, g_ref, w_ref, cos_ref, sin_ref, o_ref, wb_ref):
    j = pl.program_id(0)
    _cast_once(w_ref, wb_ref, pl.program_id(1) == 0)
    u = _rms(h_ref[...], g_ref[...]).astype(BF16)
    acc = jnp.dot(u, wb_ref[...], preferred_element_type=F32)

    @pl.when(j == 0)
    def _():
        cos, sin = cos_ref[...], sin_ref[...]
        half = C_QK_DIM // 2
        for hd in range(2 * C_HEADS):
            x1 = acc[:, hd * C_QK_DIM:hd * C_QK_DIM + half]
            x2 = acc[:, hd * C_QK_DIM + half:(hd + 1) * C_QK_DIM]
            r1 = x1 * cos - x2 * sin
            r2 = x2 * cos + x1 * sin
            if hd >= C_HEADS:
                r1 = r1 * (C_QK_DIM ** -0.5)
                r2 = r2 * (C_QK_DIM ** -0.5)
            o_ref[:, hd * C_QK_DIM:hd * C_QK_DIM + half] = r1.astype(BF16)
            o_ref[:, hd * C_QK_DIM + half:(hd + 1) * C_QK_DIM] = r2.astype(BF16)

    @pl.when(j > 0)
    def _():
        o_ref[...] = acc.astype(BF16)


def _odd_in(h, gain, w, cos, sin, seq):
    t = h.shape[0]
    tm = WIDE_ROW_TILE
    tn = 2 * C_QK_WIDTH
    nseq = seq // tm
    tab = pl.BlockSpec((tm, C_QK_DIM // 2), lambda j, i: (i % nseq, 0))
    return pl.pallas_call(
        _odd_in_kernel,
        grid=(ODD_IN_WIDTH // tn, t // tm),
        in_specs=[
            pl.BlockSpec((tm, D_MODEL), lambda j, i: (i, 0)),
            pl.BlockSpec((1, D_MODEL), lambda j, i: (0, 0)),
            pl.BlockSpec((None, D_MODEL, tn), lambda j, i: (0, 0, j), pipeline_mode=pl.Buffered(1)),
            tab, tab,
        ],
        out_specs=pl.BlockSpec((tm, tn), lambda j, i: (i, j)),
        out_shape=jax.ShapeDtypeStruct((t, ODD_IN_WIDTH), BF16),
        scratch_shapes=[pltpu.VMEM((D_MODEL, tn), BF16)],
        compiler_params=_cparams(("arbitrary", "arbitrary")),
        name="odd_in_proj",
    )(h, gain.reshape(1, D_MODEL), w, cos, sin)


def _ret_kernel(ld_ref, q_ref, k_ref, v_ref, g_ref, o_ref, state):
    c = pl.program_id(0)
    ch = RET_CHUNK
    batch = q_ref.shape[0]

    @pl.when(c == 0)
    def _():
        state[...] = jnp.zeros(state.shape, F32)

    ii = lax.broadcasted_iota(I32, (ch, ch), 0)
    jj = lax.broadcasted_iota(I32, (ch, ch), 1)
    diff = (ii - jj).astype(F32)
    pos = lax.broadcasted_iota(I32, (ch, 1), 0).astype(F32)
    for hd in range(C_HEADS):
        ld = ld_ref[hd]
        intra = jnp.where(diff >= 0, jnp.exp(ld * jnp.maximum(diff, 0.0)), 0.0)
        q_decay = jnp.exp(ld * (pos + 1.0))
        k_decay = jnp.exp(ld * (ch - 1.0 - pos))
        chunk_decay = jnp.exp(ld * jnp.full((1, 1), float(ch), F32))
        for bb in range(batch):
            q = q_ref[bb, :, hd * C_QK_DIM:(hd + 1) * C_QK_DIM]
            k = k_ref[bb, :, hd * C_QK_DIM:(hd + 1) * C_QK_DIM]
            v = v_ref[bb, :, hd * C_V_DIM:(hd + 1) * C_V_DIM]
            s = lax.dot_general(q, k, (((1,), (1,)), ((), ())), preferred_element_type=F32) * intra
            inner = jnp.dot(s.astype(BF16), v, preferred_element_type=F32)
            st = state[bb, hd]
            cross = jnp.dot(q, st.astype(BF16), preferred_element_type=F32) * q_decay
            kd_t = jnp.transpose(k.astype(F32) * k_decay).astype(BF16)
            state[bb, hd] = st * chunk_decay + jnp.dot(kd_t, v, preferred_element_type=F32)
            out = inner + cross
            mu = jnp.mean(out, axis=-1, keepdims=True)
            cen = out - mu
            var = jnp.mean(cen * cen, axis=-1, keepdims=True)
            o = cen * lax.rsqrt(var + NORM_EPS)
            gf = g_ref[bb, :, hd * C_V_DIM:(hd + 1) * C_V_DIM].astype(F32)
            o_ref[bb, :, hd * C_V_DIM:(hd + 1) * C_V_DIM] = (gf * jax.nn.sigmoid(gf) * o).astype(BF16)


def _retention(proj, log_decay, batch, seq):
    ch = RET_CHUNK
    nc = seq // ch
    proj3 = proj.reshape(batch, seq, ODD_IN_WIDTH)
    v0 = 2 * C_QK_WIDTH // C_V_WIDTH
    grid_spec = pltpu.PrefetchScalarGridSpec(
        num_scalar_prefetch=1,
        grid=(nc,),
        in_specs=[
            pl.BlockSpec((batch, ch, C_QK_WIDTH), lambda c, ld: (0, c, 0)),
            pl.BlockSpec((batch, ch, C_QK_WIDTH), lambda c, ld: (0, c, 1)),
            pl.BlockSpec((batch, ch, C_V_WIDTH), lambda c, ld: (0, c, v0)),
            pl.BlockSpec((batch, ch, C_V_WIDTH), lambda c, ld: (0, c, v0 + 1)),
        ],
        out_specs=pl.BlockSpec((batch, ch, C_V_WIDTH), lambda c, ld: (0, c, 0)),
        scratch_shapes=[pltpu.VMEM((batch, C_HEADS, C_QK_DIM, C_V_DIM), F32)],
    )
    y = pl.pallas_call(
        _ret_kernel,
        grid_spec=grid_spec,
        out_shape=jax.ShapeDtypeStruct((batch, seq, C_V_WIDTH), BF16),
        compiler_params=_cparams(("arbitrary",)),
        name="retention",
    )(log_decay, proj3, proj3, proj3, proj3)
    return y.reshape(batch * seq, C_V_WIDTH)


def _odd_out_kernel(h_ref, y_ref, w_ref, out_ref, wb_ref):
    _cast_once(w_ref, wb_ref, pl.program_id(0) == 0)
    out_ref[...] = h_ref[...] + jnp.dot(y_ref[...], wb_ref[...], preferred_element_type=F32)


def _odd_out(h, y, w):
    t = h.shape[0]
    tm = WIDE_ROW_TILE
    return pl.pallas_call(
        _odd_out_kernel,
        grid=(t // tm,),
        in_specs=[pl.BlockSpec((tm, D_MODEL), lambda i: (i, 0)),
                  pl.BlockSpec((tm, C_V_WIDTH), lambda i: (i, 0)),
                  pl.BlockSpec((None, C_V_WIDTH, D_MODEL), lambda i: (0, 0, 0), pipeline_mode=pl.Buffered(1))],
        out_specs=pl.BlockSpec((tm, D_MODEL), lambda i: (i, 0)),
        out_shape=jax.ShapeDtypeStruct((t, D_MODEL), F32),
        scratch_shapes=[pltpu.VMEM((C_V_WIDTH, D_MODEL), BF16)],
        compiler_params=_cparams(("arbitrary",)),
        name="odd_out_proj",
    )(h, y, w)


def _router_kernel(h_ref, g_ref, wr_ref, br_ref, x2s_ref, codes_ref, gates_ref, cnt_ref, base_ref, *,
                   tiles_per_batch):
    i = pl.program_id(0)
    tm = h_ref.shape[0]
    batch = i // tiles_per_batch

    @pl.when(i % tiles_per_batch == 0)
    def _():
        base_ref[...] = jnp.zeros(base_ref.shape, F32)

    x2 = _rms(h_ref[...], g_ref[...])
    for j in range(SLABS):
        x2s_ref[pl.ds(j, tm, stride=SLABS), :] = x2[:, j * LANES:(j + 1) * LANES]
    wr = wr_ref[...]
    x_hi, w_hi = x2.astype(BF16), wr.astype(BF16)
    x_lo = (x2 - x_hi.astype(F32)).astype(BF16)
    w_lo = (wr - w_hi.astype(F32)).astype(BF16)
    both = jnp.dot(x_hi, jnp.concatenate([w_hi, w_lo], axis=1), preferred_element_type=F32)
    logits = both[:, :LANES] + both[:, LANES:] + jnp.dot(x_lo, w_hi, preferred_element_type=F32) + br_ref[...]
    lane = lax.broadcasted_iota(I32, (tm, LANES), 1)
    is_grp = lane < MOE_GROUPS
    lg = jnp.where(is_grp, logits, -jnp.inf)
    gmax = jnp.max(lg, axis=-1, keepdims=True)
    gsum = jnp.sum(jnp.where(is_grp, jnp.exp(logits - gmax), 0.0), axis=-1, keepdims=True)
    gp = 1.0 / gsum
    gi = jnp.min(jnp.where(lg == gmax, lane, LANES), axis=-1, keepdims=True)
    ex = lane - ROUTER_LANE0
    in_grp = (ex >= 0) & (ex < MOE_EXPERTS) & ((ex // MOE_EXPERTS_PER_GROUP) == gi)
    sel = jnp.where(in_grp, logits, -jnp.inf)
    v1 = jnp.max(sel, axis=-1, keepdims=True)
    i1 = jnp.min(jnp.where(sel == v1, lane, LANES), axis=-1, keepdims=True)
    sel2 = jnp.where(lane == i1, -jnp.inf, sel)
    v2 = jnp.max(sel2, axis=-1, keepdims=True)
    i2 = jnp.min(jnp.where(sel2 == v2, lane, LANES), axis=-1, keepdims=True)
    tt = jnp.exp(v2 - v1)
    g0 = gp / (1.0 + tt)
    g1 = gp * tt / (1.0 + tt)
    oh0 = (lane == i1).astype(F32)
    oh1 = (lane == i2).astype(F32)
    oh = oh0 + oh1
    rr = lax.broadcasted_iota(I32, (tm, tm), 0)
    cc = lax.broadcasted_iota(I32, (tm, tm), 1)
    tri = (cc < rr).astype(BF16)
    prefix = jnp.dot(tri, oh.astype(BF16), preferred_element_type=F32)
    tot = base_ref[...] + prefix
    rank0 = jnp.sum(oh0 * tot, axis=-1, keepdims=True).astype(I32)
    rank1 = jnp.sum(oh1 * tot, axis=-1, keepdims=True).astype(I32)
    base_ref[...] = base_ref[...] + jnp.sum(oh, axis=0, keepdims=True)
    eoff = batch * MOE_EXPERTS - ROUTER_LANE0
    code0 = (i1 + eoff) * (1 << RANK_BITS) + rank0
    code1 = (i2 + eoff) * (1 << RANK_BITS) + rank1
    meta = jnp.where(lane == 0, code0, jnp.where(lane == 1, code1, 0))
    codes_ref[...] = jnp.transpose(meta)[0:8, :]
    gates_ref[...] = jnp.where(lane == 0, g0, jnp.where(lane == 1, g1, 0.0))
    cnt_ref[...] = jnp.broadcast_to(base_ref[...], cnt_ref.shape)


def _router(h, gain, wr, br, batch):
    t = h.shape[0]
    tm = ROW_TILE
    tpb = t // batch // tm
    row = lambda width: pl.BlockSpec((tm, width), lambda i: (i, 0))
    fixed = lambda shape: pl.BlockSpec(shape, lambda i: (0, 0))
    return pl.pallas_call(
        functools.partial(_router_kernel, tiles_per_batch=tpb),
        grid=(t // tm,),
        in_specs=[row(D_MODEL), fixed((1, D_MODEL)), fixed((D_MODEL, LANES)), fixed((1, LANES))],
        out_specs=[pl.BlockSpec((tm * SLABS, LANES), lambda i: (i, 0)),
                   pl.BlockSpec((8, tm), lambda i: (0, i)), row(LANES),
                   pl.BlockSpec((8, LANES), lambda i: (i // tpb, 0))],
        out_shape=[
            jax.ShapeDtypeStruct((t * SLABS, LANES), F32),
            jax.ShapeDtypeStruct((8, t), I32),
            jax.ShapeDtypeStruct((t, LANES), F32),
            jax.ShapeDtypeStruct((batch * 8, LANES), F32),
        ],
        scratch_shapes=[pltpu.VMEM((1, LANES), F32)],
        compiler_params=_cparams(("arbitrary",)),
        name="moe_router",
    )(h, gain.reshape(1, D_MODEL), wr, br)


INVERT_UNROLL = 16


def _dest_kernel(pstart_ref, codes_ref, dest_ref):
    codes = codes_ref[...]
    seg = lax.shift_right_logical(codes, RANK_BITS)
    dest = codes & ((1 << RANK_BITS) - 1)
    for i in range(pstart_ref.shape[0]):
        dest = dest + jnp.where(seg == i, pstart_ref[i], 0)
    dest_ref[...] = dest


def _dest_rows(pad_start, codes):
    grid_spec = pltpu.PrefetchScalarGridSpec(
        num_scalar_prefetch=1,
        grid=(1,),
        in_specs=[pl.BlockSpec(codes.shape, lambda i, ps: (0, 0))],
        out_specs=pl.BlockSpec(codes.shape, lambda i, ps: (0, 0)),
    )
    return pl.pallas_call(
        _dest_kernel,
        grid_spec=grid_spec,
        out_shape=jax.ShapeDtypeStruct(codes.shape, I32),
        compiler_params=_cparams(("arbitrary",)),
        name="moe_dest_rows",
    )(pad_start, codes)


def _invert_kernel(trips_ref, dest_ref, dummy_ref, slot_ref, *, batch):
    per_batch = dest_ref.shape[0] // batch
    pltpu.sync_copy(dummy_ref, slot_ref)
    for b in range(batch):
        def place(i, carry, b=b):
            for v in range(INVERT_UNROLL):
                a = i * INVERT_UNROLL + v
                slot_ref[dest_ref[b * per_batch + a]] = a * SLABS
            return carry

        lax.fori_loop(0, trips_ref[0], place, 0)


def _dummy_slots(n_rows, per_batch):
    assert MOE_BLOCK & (MOE_BLOCK - 1) == 0 and (EXPERT_RING + 1) * MOE_BLOCK <= DUMMY_SLOTS
    r = jnp.arange(n_rows + MOE_BLOCK, dtype=I32)
    pattern = per_batch + (r // MOE_BLOCK) % EXPERT_RING * MOE_BLOCK + r % MOE_BLOCK
    return jnp.where(r < n_rows, pattern, per_batch + EXPERT_RING * MOE_BLOCK + r % MOE_BLOCK) * SLABS


def _invert(dest_flat, n_rows, batch):
    smem = pl.BlockSpec(memory_space=pltpu.SMEM)
    per_batch = dest_flat.shape[0] // batch
    assert per_batch % INVERT_UNROLL == 0
    trips = jnp.full((1,), per_batch // INVERT_UNROLL, I32)
    return pl.pallas_call(
        functools.partial(_invert_kernel, batch=batch),
        in_specs=[smem, smem, pl.BlockSpec(memory_space=pl.ANY)],
        out_specs=smem,
        out_shape=jax.ShapeDtypeStruct((n_rows + MOE_BLOCK,), I32),
        name="moe_invert_rows",
    )(trips, dest_flat, _dummy_slots(n_rows, per_batch))


def _expert_kernel(be_ref, run_ref, nxt_ref, nreal_ref, slot_ref, x2s_ref, wg_hbm, wu_hbm, wd_hbm, ys_ref,
                   wbuf_g, wbuf_u, wbuf_d, wgb_ref, wub_ref, wdb_ref, tile, xb, ybuf, sems, wsems, gsems, *,
                   batch, nb, seq, layer):
    b = pl.program_id(0)
    n = pl.program_id(1)
    g = b * nb + n
    last = batch * nb - 1
    blk = MOE_BLOCK
    bstride = 2 * seq + DUMMY_SLOTS
    ring = EXPERT_RING
    ycur = g % ring
    yprev = (g + ring - 1) % ring
    n_real = nreal_ref[0]

    def weight_copies(e, ws):
        pairs = ((wg_hbm, wbuf_g), (wu_hbm, wbuf_u), (wd_hbm, wbuf_d))
        return [pltpu.make_async_copy(src.at[layer, e], dst.at[ws], wsems.at[ws, k])
                for k, (src, dst) in enumerate(pairs)]

    def wait_block(q):
        pltpu.make_async_copy(ybuf.at[q], ys_ref.at[pl.ds(0, blk * SLABS), :], sems.at[q]).wait()

    def gather_row(block, tslot, mi):
        s = slot_ref[jnp.minimum(block, last) * blk + mi]
        span = seq * SLABS
        if span & (span - 1) == 0:
            src = pl.multiple_of(s & (span - 1), SLABS)
        else:
            src = pl.multiple_of(jnp.minimum(jnp.where(s >= span, s - span, s), span - SLABS), SLABS)
        pltpu.make_async_copy(x2s_ref.at[pl.ds(src, SLABS), :], tile.at[tslot, pl.ds(mi * SLABS, SLABS), :],
                              gsems.at[tslot]).start()

    def wait_gather(tslot):
        pltpu.make_async_copy(x2s_ref.at[pl.ds(0, blk * SLABS), :], tile.at[tslot], gsems.at[tslot]).wait()

    def to_bf16(tslot, xslot):
        for j in range(SLABS):
            xb[xslot, :, j * LANES:(j + 1) * LANES] = tile[tslot, pl.ds(j, blk, stride=SLABS), :].astype(BF16)

    def scatter_row(yslot, slot, row0, mi):
        dst = pl.multiple_of(row0 * SLABS + slot, SLABS)
        pltpu.make_async_copy(ybuf.at[yslot, pl.ds(mi * SLABS, SLABS), :], ys_ref.at[pl.ds(dst, SLABS), :],
                              sems.at[yslot]).start()

    @pl.when(g == 0)
    def _():
        for c in weight_copies(be_ref[0], 0):
            c.start()
        ybuf[...] = jnp.zeros(ybuf.shape, F32)
        for bb in range(batch):
            for c in range(DUMMY_SLOTS // blk):
                start = (bb * bstride + 2 * seq + c * blk) * SLABS
                zero = pltpu.make_async_copy(ybuf.at[0], ys_ref.at[pl.ds(start, blk * SLABS), :], sems.at[0])
                zero.start()
                zero.wait()
        for mi in range(blk):
            gather_row(0, 0, mi)
        wait_gather(0)
        to_bf16(0, 0)
        for first in range(1, ring - 1):
            for mi in range(blk):
                gather_row(first, first, mi)

    run = run_ref[g]
    ws = run % 2

    @pl.when((g == 0) | (run != run_ref[jnp.maximum(g - 1, 0)]))
    def _():
        for c in weight_copies(be_ref[g], ws):
            c.wait()
        wgb_ref[...] = wbuf_g[ws].astype(BF16)
        wub_ref[...] = wbuf_u[ws].astype(BF16)
        wdb_ref[...] = wbuf_d[ws].astype(BF16)

        @pl.when(nxt_ref[g] >= 0)
        def _():
            for c in weight_copies(nxt_ref[g], 1 - ws):
                c.start()

    @pl.when(g >= ring - 1)
    def _():
        wait_block(ycur)

    prev_block = jnp.where(g == 0, batch * nb, g - 1)
    prev_row0 = jnp.where(n == 0, jnp.maximum(b - 1, 0), b) * bstride
    n_chunks = 8
    per = blk // n_chunks
    half = EXPERT_HIDDEN // 2
    quarter = D_MODEL // 4

    @pl.when(g < n_real)
    def _():
        x_cur, y_cur, y_prev = g % 2, ycur, yprev
        next_slot, fetch_slot = (g + 1) % ring, (g + ring - 1) % ring
        wait_gather(next_slot)
        for mi in range(blk):
            scatter_row(y_prev, slot_ref[prev_block * blk + mi], prev_row0, mi)
        to_bf16(next_slot, 1 - x_cur)
        x = xb[x_cur]
        acts = []
        hid = None
        for c in range(n_chunks):
            for mi in range(c * per, (c + 1) * per):
                gather_row(g + ring - 1, fetch_slot, mi)
            if c < 4:
                w_ref = wgb_ref if c < 2 else wub_ref
                acts.append(jnp.dot(x, w_ref[:, (c % 2) * half:(c % 2 + 1) * half], preferred_element_type=F32))
            if c == 3:
                a = jnp.concatenate(acts[0:2], axis=1)
                u = jnp.concatenate(acts[2:4], axis=1)
                hid = (a * jax.nn.sigmoid(a) * u).astype(BF16)
            if c >= 4:
                q = c - 4
                yq = jnp.dot(hid, wdb_ref[:, q * quarter:(q + 1) * quarter], preferred_element_type=F32)
                for jj in range(quarter // LANES):
                    j = q * (quarter // LANES) + jj
                    ybuf[y_cur, pl.ds(j, blk, stride=SLABS), :] = yq[:, jj * LANES:(jj + 1) * LANES]

    @pl.when(g >= n_real)
    def _():
        for mi in range(blk):
            scatter_row(yprev, slot_ref[prev_block * blk + mi], prev_row0, mi)

    @pl.when(g == last)
    def _():
        wait_block((last - 3) % ring)
        wait_block((last - 2) % ring)
        for mi in range(blk):
            scatter_row(last % ring, slot_ref[last * blk + mi], (batch - 1) * bstride, mi)
        wait_block((last - 1) % ring)
        wait_block(last % ring)
        for extra in range(1, ring - 1):
            wait_gather((n_real + extra) % ring)


def _experts(block_expert, n_real, row_slot, x2s, wg, wu, wd, layer, batch, seq):
    blk = MOE_BLOCK
    nb = (row_slot.shape[0] // blk - 1) // batch
    steps = batch * nb
    assert steps >= 4 and batch == 1
    idx = jnp.arange(steps, dtype=I32)
    change = jnp.concatenate([jnp.zeros((1,), I32), (block_expert[1:] != block_expert[:-1]).astype(I32)])
    run = jnp.cumsum(change).astype(I32)
    later_change = (idx[None, :] > idx[:, None]) & (change[None, :] > 0)
    nxt_idx = jnp.min(jnp.where(later_change, idx[None, :], steps), axis=1)
    nxt = jnp.where(nxt_idx < steps, block_expert[jnp.minimum(nxt_idx, steps - 1)], -1).astype(I32)
    hbm = pl.BlockSpec(memory_space=pl.ANY)
    grid_spec = pltpu.PrefetchScalarGridSpec(
        num_scalar_prefetch=5,
        grid=(batch, nb),
        in_specs=[hbm, hbm, hbm, hbm],
        out_specs=hbm,
        scratch_shapes=[
            pltpu.VMEM((2, D_MODEL, EXPERT_HIDDEN), F32), pltpu.VMEM((2, D_MODEL, EXPERT_HIDDEN), F32),
            pltpu.VMEM((2, EXPERT_HIDDEN, D_MODEL), F32),
            pltpu.VMEM((D_MODEL, EXPERT_HIDDEN), BF16), pltpu.VMEM((D_MODEL, EXPERT_HIDDEN), BF16),
            pltpu.VMEM((EXPERT_HIDDEN, D_MODEL), BF16),
            pltpu.VMEM((EXPERT_RING, blk * SLABS, LANES), F32),
            pltpu.VMEM((2, blk, D_MODEL), BF16),
            pltpu.VMEM((EXPERT_RING, blk * SLABS, LANES), F32),
            pltpu.SemaphoreType.DMA((EXPERT_RING,)),
            pltpu.SemaphoreType.DMA((2, 3)),
            pltpu.SemaphoreType.DMA((EXPERT_RING,)),
        ],
    )
    return pl.pallas_call(
        functools.partial(_expert_kernel, batch=batch, nb=nb, seq=seq, layer=layer),
        grid_spec=grid_spec,
        out_shape=jax.ShapeDtypeStruct((batch * (2 * seq + DUMMY_SLOTS) * SLABS, LANES), F32),
        compiler_params=_cparams(("arbitrary", "arbitrary")),
        name="moe_experts",
    )(block_expert, run, nxt, n_real, row_slot, x2s, wg, wu, wd)


def _combine_kernel(h_ref, gates_ref, y0_ref, y1_ref, *rest, final):
    if final:
        fg_ref, out_ref = rest
    else:
        (out_ref,) = rest
    tm = h_ref.shape[0]
    gates = gates_ref[...]
    g0, g1 = gates[:, 0:1], gates[:, 1:2]
    parts = []
    for j in range(SLABS):
        y0 = y0_ref[pl.ds(j, tm, stride=SLABS), :]
        y1 = y1_ref[pl.ds(j, tm, stride=SLABS), :]
        parts.append(g0 * y0 + g1 * y1)
    out = h_ref[...] + jnp.concatenate(parts, axis=1)
    if final:
        out = _rms(out, fg_ref[...])
    out_ref[...] = out


def _combine(h, gates, ys, batch, seq, final_gain=None):
    t = h.shape[0]
    tm = ROW_TILE
    tpb = seq // tm
    bstride = 2 * tpb + DUMMY_SLOTS // tm
    final = final_gain is not None
    in_specs = [pl.BlockSpec((tm, D_MODEL), lambda i: (i, 0)),
                pl.BlockSpec((tm, LANES), lambda i: (i, 0)),
                pl.BlockSpec((tm * SLABS, LANES), lambda i: (i // tpb * bstride + i % tpb, 0)),
                pl.BlockSpec((tm * SLABS, LANES), lambda i: (i // tpb * bstride + tpb + i % tpb, 0))]
    args = [h, gates, ys, ys]
    if final:
        in_specs.append(pl.BlockSpec((1, D_MODEL), lambda i: (0, 0)))
        args.append(final_gain.reshape(1, D_MODEL))
    return pl.pallas_call(
        functools.partial(_combine_kernel, final=final),
        grid=(t // tm,),
        in_specs=in_specs,
        out_specs=pl.BlockSpec((tm, D_MODEL), lambda i: (i, 0)),
        out_shape=jax.ShapeDtypeStruct((t, D_MODEL), F32),
        compiler_params=_cparams(("parallel",)),
        name="moe_combine_final" if final else "moe_combine",
    )(*args)


def _moe(h, gain, w_r1, b_r1, w_r2, b_r2, wg, wu, wd, layer, batch, final_gain=None):
    t = h.shape[0]
    seq = t // batch
    assert seq <= (1 << RANK_BITS) and seq % ROW_TILE == 0 and DUMMY_SLOTS % ROW_TILE == 0
    pad_w = jnp.zeros((D_MODEL, LANES - MOE_GROUPS - MOE_EXPERTS), F32)
    wr = jnp.concatenate([w_r1, w_r2, pad_w], axis=1)
    br = jnp.concatenate([b_r1, b_r2, jnp.zeros((LANES - MOE_GROUPS - MOE_EXPERTS,), F32)]).reshape(1, LANES)
    x2s, codes, gates, counts = _router(h, gain, wr, br, batch)
    cnt = counts.reshape(batch, 8, LANES)[:, 0, ROUTER_LANE0:ROUTER_LANE0 + MOE_EXPERTS].astype(I32)
    padded = (cnt + MOE_BLOCK - 1) // MOE_BLOCK * MOE_BLOCK
    pad_end = jnp.cumsum(padded, axis=1)
    rows_pb = seq * 2 + MOE_EXPERTS * MOE_BLOCK
    nb = rows_pb // MOE_BLOCK
    pad_start = pad_end - padded + (jnp.arange(batch, dtype=I32) * rows_pb)[:, None]
    block_start = jnp.arange(nb, dtype=I32) * MOE_BLOCK
    block_expert = jnp.minimum(
        jnp.sum((pad_end[:, None, :] <= block_start[None, :, None]).astype(I32), axis=2), MOE_EXPERTS - 1)
    dest = _dest_rows(pad_start.reshape(-1), codes)
    dest_flat = dest[0:2].reshape(2, batch, seq).transpose(1, 0, 2).reshape(-1)
    row_slot = _invert(dest_flat, batch * rows_pb, batch)
    n_real = (pad_end[:, -1] // MOE_BLOCK).astype(I32)
    ys = _experts(block_expert.reshape(-1), n_real, row_slot, x2s, wg, wu, wd, layer, batch, seq)
    return _combine(h, gates, ys, batch, seq, final_gain)


def kernel(x, mix_norm, ffn_norm, final_norm, even_w_in, even_w_out, conv_w, conv_b, conv_norm_g, conv_norm_b,
           odd_w_in, odd_w_out, router_w1, router_b1, router_w2, router_b2, expert_w_gate, expert_w_up,
           expert_w_down):
    batch, seq, d = x.shape
    assert d == D_MODEL and seq % (ATTN_BLOCK * max(dl for _, dl in A_BRANCHES)) == 0
    assert all(w // dl == ATTN_BLOCK for w, dl in A_BRANCHES)
    t = batch * seq
    h = x.reshape(t, d)

    qkv, conv_in = _even_in(h, mix_norm[0], even_w_in, _attn_rope_tables(seq), seq)
    attn = _attention(qkv, batch, seq)
    bconv = _conv(conv_in, conv_w[0], conv_b[0], conv_norm_g[0], conv_norm_b[0], batch, seq)
    h = _even_out(h, attn, bconv, even_w_out)
    h = _moe(h, ffn_norm[0], router_w1[0], router_b1[0], router_w2[0], router_b2[0],
             expert_w_gate, expert_w_up, expert_w_down, 0, 1)

    inv_freq = RET_ROT_THETA ** (-jnp.linspace(0.0, 1.0, C_QK_DIM // 2, dtype=F32))
    ang = jnp.arange(seq, dtype=F32)[:, None] * inv_freq[None, :]
    proj = _odd_in(h, mix_norm[1], odd_w_in, jnp.cos(ang), jnp.sin(ang), seq)
    log_decay = jnp.log(1.0 - jnp.exp2(-5.0 - jnp.arange(C_HEADS, dtype=F32)))
    y = _retention(proj, log_decay, batch, seq)
    h = _odd_out(h, y, odd_w_out)
    out = _moe(h, ffn_norm[1], router_w1[1], router_b1[1], router_w2[1], router_b2[1],
               expert_w_gate, expert_w_up, expert_w_down, 1, 1, final_gain=final_norm)
    return out.reshape(batch, seq, d)
```

```python
import functools

import jax
import jax.numpy as jnp
from jax import lax
from jax.experimental import pallas as pl
from jax.experimental.pallas import tpu as pltpu

F32 = jnp.float32
BF16 = jnp.bfloat16
I32 = jnp.int32

NORM_EPS = 1e-6
NEG_INF = -1e30

D_MODEL = 1024
A_HEADS = 8
A_HEAD_DIM = 64
A_WIDTH = A_HEADS * A_HEAD_DIM
A_BRANCHES = ((128, 1), (512, 4), (2048, 16))
ATTN_BLOCK = 128
ROPE_THETA = 500000.0
ROPE_DIM = A_HEAD_DIM // 4
B_WIDTH = D_MODEL - A_WIDTH
CONV_WIDTH = 31
C_HEADS = 4
C_QK_DIM = 256
C_V_DIM = 512
C_QK_WIDTH = C_HEADS * C_QK_DIM
C_V_WIDTH = C_HEADS * C_V_DIM
RET_CHUNK = 128
RET_ROT_THETA = 10000.0
MOE_GROUPS = 4
MOE_EXPERTS_PER_GROUP = 8
MOE_EXPERTS = MOE_GROUPS * MOE_EXPERTS_PER_GROUP
EXPERT_HIDDEN = 512
MOE_BLOCK = 128
EVEN_IN_WIDTH = 3 * A_WIDTH + 2 * B_WIDTH
ODD_IN_WIDTH = 2 * C_QK_WIDTH + 2 * C_V_WIDTH

LANES = 128
ROW_TILE = 512
WIDE_ROW_TILE = 1024
CONV_TILE = 512
CONV_HALO = 32
CONV_CHUNK = 64
ROUTER_LANE0 = MOE_GROUPS
VMEM_LIMIT = 56 * 1024 * 1024
SLABS = D_MODEL // LANES
RANK_BITS = 16
EXPERT_RING = 4
WEIGHT_DMA_PRIORITY = 1
DUMMY_SLOTS = 2 * ROW_TILE


def _cparams(sem, vmem=VMEM_LIMIT):
    return pltpu.CompilerParams(dimension_semantics=sem, vmem_limit_bytes=vmem)


def _cast_once(w_ref, wb_ref, first):
    @pl.when(first)
    def _():
        wb_ref[...] = w_ref[...].astype(BF16)


def _rms(x, gain):
    ms = jnp.mean(x * x, axis=-1, keepdims=True)
    return x * lax.rsqrt(ms + NORM_EPS) * gain


def _even_in_kernel(h_ref, g_ref, w_ref, c_ref, s1_ref, s2_ref, qkv_ref, conv_ref, wb_ref):
    _cast_once(w_ref, wb_ref, pl.program_id(0) == 0)
    u = _rms(h_ref[...], g_ref[...]).astype(BF16)
    acc = jnp.dot(u, wb_ref[...], preferred_element_type=F32)
    c, s1, s2 = c_ref[...], s1_ref[...], s2_ref[...]
    for j in range(2 * A_WIDTH // LANES):
        xg = acc[:, j * LANES:(j + 1) * LANES]
        if j < A_WIDTH // LANES:
            xg = xg * (A_HEAD_DIM ** -0.5)
        qkv_ref[j] = xg * c + pltpu.roll(xg, LANES - ROPE_DIM // 2, 1) * s1 + pltpu.roll(xg, ROPE_DIM // 2, 1) * s2
    for j in range(2 * A_WIDTH // LANES, 3 * A_WIDTH // LANES):
        qkv_ref[j] = acc[:, j * LANES:(j + 1) * LANES]
    conv_ref[...] = acc[:, 3 * A_WIDTH:].astype(BF16)


def _even_in(h, gain, w, tabs, seq):
    t = h.shape[0]
    tm = ROW_TILE
    nseq = seq // tm
    tab_spec = pl.BlockSpec((tm, LANES), lambda i: (i % nseq, 0))
    return pl.pallas_call(
        _even_in_kernel,
        grid=(t // tm,),
        in_specs=[
            pl.BlockSpec((tm, D_MODEL), lambda i: (i, 0)),
            pl.BlockSpec((1, D_MODEL), lambda i: (0, 0)),
            pl.BlockSpec((None, D_MODEL, EVEN_IN_WIDTH), lambda i: (0, 0, 0)),
            tab_spec, tab_spec, tab_spec,
        ],
        out_specs=[
            pl.BlockSpec((3 * A_WIDTH // LANES, tm, LANES), lambda i: (0, i, 0)),
            pl.BlockSpec((tm, 2 * B_WIDTH), lambda i: (i, 0)),
        ],
        out_shape=[
            jax.ShapeDtypeStruct((3 * A_WIDTH // LANES, t, LANES), F32),
            jax.ShapeDtypeStruct((t, 2 * B_WIDTH), BF16),
        ],
        scratch_shapes=[pltpu.VMEM((D_MODEL, EVEN_IN_WIDTH), BF16)],
        compiler_params=_cparams(("arbitrary",)),
        name="even_in_proj",
    )(h, gain.reshape(1, D_MODEL), w, *tabs)


def _attn_rope_tables(seq):
    half = ROPE_DIM // 2
    inv_freq = ROPE_THETA ** (-jnp.arange(0, ROPE_DIM, 2, dtype=F32) / ROPE_DIM)
    ang = jnp.arange(seq, dtype=F32)[:, None] * inv_freq[None, :]
    cos, sin = jnp.cos(ang), jnp.sin(ang)
    rest = A_HEAD_DIM - ROPE_DIM
    ones = jnp.ones((seq, rest), F32)
    z_rest = jnp.zeros((seq, rest), F32)
    z_half = jnp.zeros((seq, half), F32)
    c = jnp.concatenate([cos, cos, ones], axis=1)
    s1 = jnp.concatenate([-sin, z_half, z_rest], axis=1)
    s2 = jnp.concatenate([z_half, sin, z_rest], axis=1)
    rep = LANES // A_HEAD_DIM
    return tuple(jnp.tile(a, (1, rep)) for a in (c, s1, s2))


ATTN_STEP = ATTN_BLOCK * max(d for _, d in A_BRANCHES)
ATTN_PAIRS = A_WIDTH // LANES
ATTN_MERGE_ROWS = 256


def _attn_kernel(q_ref, k_ref, v_ref, o_ref, kprev, vprev, oml):
    n = pl.program_id(1)
    blk = ATTN_BLOCK
    n_blocks = ATTN_STEP // blk

    @pl.when(n == 0)
    def _():
        kprev[...] = jnp.zeros(kprev.shape, BF16)
        vprev[...] = jnp.zeros(vprev.shape, BF16)

    qi = lax.broadcasted_iota(I32, (blk, 2 * blk), 0)
    kj = lax.broadcasted_iota(I32, (blk, 2 * blk), 1)
    dist = blk + qi - kj
    band = (dist >= 0) & (dist <= blk)
    in_cur = kj >= blk
    lane = lax.broadcasted_iota(I32, (blk, LANES), 1)
    lo = lane < A_HEAD_DIM

    off = 0
    for bi, (_, dil) in enumerate(A_BRANCHES):
        shift = dil.bit_length() - 1
        assert dil == 1 << shift

        def one_block(idx, carried, bi=bi, dil=dil, shift=shift, off=off):
            u = lax.shift_right_logical(idx, shift)
            r = idx & (dil - 1)
            start = u * (blk * dil) + r
            rows = pl.ds(pl.multiple_of(start, blk), blk) if dil == 1 else pl.ds(start, blk, stride=dil)
            has_prev = (n > 0) | (u > 0)
            valid = band & (in_cur | has_prev)
            valid2 = jnp.concatenate([valid, valid], axis=0)
            current = []
            for g in range(ATTN_PAIRS):
                q = q_ref[g, rows, :]
                kc = k_ref[g, rows, :].astype(BF16)
                vc = v_ref[g, rows, :].astype(BF16)
                current.append((kc, vc))
                kp, vp = (kprev[g, off + r], vprev[g, off + r]) if carried is None else carried[g]
                q2 = jnp.concatenate([jnp.where(lo, q, 0.0), jnp.where(lo, 0.0, q)], axis=0).astype(BF16)
                kk = jnp.concatenate([kp, kc], axis=0)
                vv = jnp.concatenate([vp, vc], axis=0)
                s = lax.dot_general(q2, kk, (((1,), (1,)), ((), ())), preferred_element_type=F32)
                s = jnp.where(valid2, s, NEG_INF)
                m = jnp.max(s, axis=-1, keepdims=True)
                e = jnp.exp(s - m)
                den = jnp.sum(e, axis=-1, keepdims=True)
                pv = jnp.dot(e.astype(BF16), vv, preferred_element_type=F32)
                o_new = jnp.where(lo, pv[:blk], pv[blk:])
                m_new = jnp.where(lo, m[:blk], m[blk:])
                l_new = jnp.where(lo, den[:blk], den[blk:])
                if bi > 0:
                    o_run, m_run, l_run = oml[0, g, rows, :], oml[1, g, rows, :], oml[2, g, rows, :]
                    m_both = jnp.maximum(m_run, m_new)
                    w_run, w_new = jnp.exp(m_run - m_both), jnp.exp(m_new - m_both)
                    o_new = w_run * o_run + w_new * o_new
                    l_new = w_run * l_run + w_new * l_new
                    m_new = m_both
                oml[0, g, rows, :] = o_new
                oml[1, g, rows, :] = m_new
                oml[2, g, rows, :] = l_new
            return r, current

        def save_prev(r, current, off=off):
            for g, (kc, vc) in enumerate(current):
                kprev[g, off + r] = kc
                vprev[g, off + r] = vc

        def block_body(i, c, dil=dil):
            r0, cur0 = one_block(2 * i, None)
            if dil == 1:
                r1, cur1 = one_block(2 * i + 1, cur0)
            else:
                save_prev(r0, cur0)
                r1, cur1 = one_block(2 * i + 1, None)
            save_prev(r1, cur1)
            return c

        lax.fori_loop(0, n_blocks // 2, block_body, 0)
        off += dil
    for g in range(ATTN_PAIRS):
        for c in range(ATTN_STEP // ATTN_MERGE_ROWS):
            sl = slice(c * ATTN_MERGE_ROWS, (c + 1) * ATTN_MERGE_ROWS)
            o_ref[g, sl, :] = (oml[0, g, sl, :] / oml[2, g, sl, :]).astype(BF16)


def _attention(qkv, batch, seq):
    t = qkv.shape[1]
    steps = seq // ATTN_STEP
    n_res = sum(d for _, d in A_BRANCHES)

    def slabs(which):
        return pl.BlockSpec((ATTN_PAIRS, ATTN_STEP, LANES), lambda b, n: (which, b * steps + n, 0))

    return pl.pallas_call(
        _attn_kernel,
        grid=(batch, steps),
        in_specs=[slabs(0), slabs(1), slabs(2)],
        out_specs=pl.BlockSpec((ATTN_PAIRS, ATTN_STEP, LANES), lambda b, n: (0, b * steps + n, 0)),
        out_shape=jax.ShapeDtypeStruct((ATTN_PAIRS, t, LANES), BF16),
        scratch_shapes=[
            pltpu.VMEM((ATTN_PAIRS, n_res, ATTN_BLOCK, LANES), BF16),
            pltpu.VMEM((ATTN_PAIRS, n_res, ATTN_BLOCK, LANES), BF16),
            pltpu.VMEM((3, ATTN_PAIRS, ATTN_STEP, LANES), F32),
        ],
        compiler_params=_cparams(("parallel", "arbitrary")),
        name="dilated_attention",
    )(qkv, qkv, qkv)


def _conv_kernel(val_ref, gate_ref, w_ref, b_ref, g_ref, beta_ref, o_ref, abuf, shifted):
    n = pl.program_id(1)
    tc = CONV_TILE
    sub = 8

    @pl.when(n == 0)
    def _():
        abuf[0:CONV_HALO, :] = jnp.zeros((CONV_HALO, B_WIDTH), F32)

    @pl.when(n > 0)
    def _():
        abuf[0:CONV_HALO, :] = abuf[tc:tc + CONV_HALO, :]

    val = val_ref[...].astype(F32)
    gate = gate_ref[...].astype(F32)
    abuf[CONV_HALO:CONV_HALO + tc, :] = val * jax.nn.sigmoid(gate)
    off = CONV_HALO - (CONV_WIDTH - 1)
    span = shifted.shape[1]
    for s in range(1, sub):
        shifted[s - 1] = abuf[s:s + span, :]
    for c in range(tc // CONV_CHUNK):
        acc = jnp.broadcast_to(b_ref[...], (CONV_CHUNK, B_WIDTH))
        for j in range(CONV_WIDTH):
            s = (off + j) % sub
            r0 = c * CONV_CHUNK + off + j - s
            src = abuf[r0:r0 + CONV_CHUNK, :] if s == 0 else shifted[s - 1, r0:r0 + CONV_CHUNK, :]
            acc = acc + w_ref[j:j + 1, :] * src
        mu = jnp.mean(acc, axis=-1, keepdims=True)
        cen = acc - mu
        var = jnp.mean(cen * cen, axis=-1, keepdims=True)
        yn = cen * lax.rsqrt(var + NORM_EPS) * g_ref[...] + beta_ref[...]
        o_ref[c * CONV_CHUNK:(c + 1) * CONV_CHUNK, :] = (yn * jax.nn.sigmoid(yn)).astype(BF16)


def _conv(conv_in, w, b, g, beta, batch, seq):
    t = conv_in.shape[0]
    tc = CONV_TILE
    nt = seq // tc
    w_pad = jnp.concatenate([w, jnp.zeros((CONV_HALO - CONV_WIDTH, B_WIDTH), F32)], axis=0)
    vec = pl.BlockSpec((1, B_WIDTH), lambda bb, n: (0, 0))
    return pl.pallas_call(
        _conv_kernel,
        grid=(batch, nt),
        in_specs=[
            pl.BlockSpec((tc, B_WIDTH), lambda bb, n: (bb * nt + n, 0)),
            pl.BlockSpec((tc, B_WIDTH), lambda bb, n: (bb * nt + n, 1)),
            pl.BlockSpec((CONV_HALO, B_WIDTH), lambda bb, n: (0, 0)),
            vec, vec, vec,
        ],
        out_specs=pl.BlockSpec((tc, B_WIDTH), lambda bb, n: (bb * nt + n, 0)),
        out_shape=jax.ShapeDtypeStruct((t, B_WIDTH), BF16),
        scratch_shapes=[pltpu.VMEM((tc + CONV_HALO, B_WIDTH), F32),
                        pltpu.VMEM((7, tc + CONV_HALO - 8, B_WIDTH), F32)],
        compiler_params=_cparams(("parallel", "arbitrary")),
        name="conformer_conv",
    )(conv_in, conv_in, w_pad, b.reshape(1, -1), g.reshape(1, -1), beta.reshape(1, -1))


def _even_out_kernel(h_ref, a_ref, bc_ref, w_ref, out_ref, wb_ref):
    _cast_once(w_ref, wb_ref, pl.program_id(0) == 0)
    a = jnp.concatenate([a_ref[g] for g in range(ATTN_PAIRS)], axis=1)
    acc = jnp.dot(a, wb_ref[0:A_WIDTH, :], preferred_element_type=F32)
    acc = acc + jnp.dot(bc_ref[...], wb_ref[A_WIDTH:, :], preferred_element_type=F32)
    out_ref[...] = h_ref[...] + acc


def _even_out(h, attn, bconv, w):
    t = h.shape[0]
    tm = WIDE_ROW_TILE
    row = lambda width: pl.BlockSpec((tm, width), lambda i: (i, 0))
    return pl.pallas_call(
        _even_out_kernel,
        grid=(t // tm,),
        in_specs=[row(D_MODEL), pl.BlockSpec((ATTN_PAIRS, tm, LANES), lambda i: (0, i, 0)), row(B_WIDTH),
                  pl.BlockSpec((None, D_MODEL, D_MODEL), lambda i: (0, 0, 0), pipeline_mode=pl.Buffered(1))],
        out_specs=row(D_MODEL),
        out_shape=jax.ShapeDtypeStruct((t, D_MODEL), F32),
        scratch_shapes=[pltpu.VMEM((D_MODEL, D_MODEL), BF16)],
        compiler_params=_cparams(("arbitrary",)),
        name="even_out_proj",
    )(h, attn, bconv, w)


def _odd_in_kernel(h_ref, g_ref, w_ref, cos_ref, sin_ref, o_ref, wb_ref):
    j = pl.program_id(0)
    _cast_once(w_ref, wb_ref, pl.program_id(1) == 0)
    u = _rms(h_ref[...], g_ref[...]).astype(BF16)
    acc = jnp.dot(u, wb_ref[...], preferred_element_type=F32)

    @pl.when(j == 0)
    def _():
        cos, sin = cos_ref[...], sin_ref[...]
        half = C_QK_DIM // 2
        for hd in range(2 * C_HEADS):
            x1 = acc[:, hd * C_QK_DIM:hd * C_QK_DIM + half]
            x2 = acc[:, hd * C_QK_DIM + half:(hd + 1) * C_QK_DIM]
            r1 = x1 * cos - x2 * sin
            r2 = x2 * cos + x1 * sin
            if hd >= C_HEADS:
                r1 = r1 * (C_QK_DIM ** -0.5)
                r2 = r2 * (C_QK_DIM ** -0.5)
            o_ref[:, hd * C_QK_DIM:hd * C_QK_DIM + half] = r1.astype(BF16)
            o_ref[:, hd * C_QK_DIM + half:(hd + 1) * C_QK_DIM] = r2.astype(BF16)

    @pl.when(j > 0)
    def _():
        o_ref[...] = acc.astype(BF16)


def _odd_in(h, gain, w, cos, sin, seq):
    t = h.shape[0]
    tm = WIDE_ROW_TILE
    tn = 2 * C_QK_WIDTH
    nseq = seq // tm
    tab = pl.BlockSpec((tm, C_QK_DIM // 2), lambda j, i: (i % nseq, 0))
    return pl.pallas_call(
        _odd_in_kernel,
        grid=(ODD_IN_WIDTH // tn, t // tm),
        in_specs=[
            pl.BlockSpec((tm, D_MODEL), lambda j, i: (i, 0)),
            pl.BlockSpec((1, D_MODEL), lambda j, i: (0, 0)),
            pl.BlockSpec((None, D_MODEL, tn), lambda j, i: (0, 0, j), pipeline_mode=pl.Buffered(1)),
            tab, tab,
        ],
        out_specs=pl.BlockSpec((tm, tn), lambda j, i: (i, j)),
        out_shape=jax.ShapeDtypeStruct((t, ODD_IN_WIDTH), BF16),
        scratch_shapes=[pltpu.VMEM((D_MODEL, tn), BF16)],
        compiler_params=_cparams(("arbitrary", "arbitrary")),
        name="odd_in_proj",
    )(h, gain.reshape(1, D_MODEL), w, cos, sin)


def _ret_kernel(ld_ref, q_ref, k_ref, v_ref, g_ref, o_ref, state):
    c = pl.program_id(0)
    ch = RET_CHUNK
    batch = q_ref.shape[0]

    @pl.when(c == 0)
    def _():
        state[...] = jnp.zeros(state.shape, F32)

    ii = lax.broadcasted_iota(I32, (ch, ch), 0)
    jj = lax.broadcasted_iota(I32, (ch, ch), 1)
    diff = (ii - jj).astype(F32)
    pos = lax.broadcasted_iota(I32, (ch, 1), 0).astype(F32)
    for hd in range(C_HEADS):
        ld = ld_ref[hd]
        intra = jnp.where(diff >= 0, jnp.exp(ld * jnp.maximum(diff, 0.0)), 0.0)
        q_decay = jnp.exp(ld * (pos + 1.0))
        k_decay = jnp.exp(ld * (ch - 1.0 - pos))
        chunk_decay = jnp.exp(ld * jnp.full((1, 1), float(ch), F32))
        for bb in range(batch):
            q = q_ref[bb, :, hd * C_QK_DIM:(hd + 1) * C_QK_DIM]
            k = k_ref[bb, :, hd * C_QK_DIM:(hd + 1) * C_QK_DIM]
            v = v_ref[bb, :, hd * C_V_DIM:(hd + 1) * C_V_DIM]
            s = lax.dot_general(q, k, (((1,), (1,)), ((), ())), preferred_element_type=F32) * intra
            inner = jnp.dot(s.astype(BF16), v, preferred_element_type=F32)
            st = state[bb, hd]
            cross = jnp.dot(q, st.astype(BF16), preferred_element_type=F32) * q_decay
            kd_t = jnp.transpose(k.astype(F32) * k_decay).astype(BF16)
            state[bb, hd] = st * chunk_decay + jnp.dot(kd_t, v, preferred_element_type=F32)
            out = inner + cross
            mu = jnp.mean(out, axis=-1, keepdims=True)
            cen = out - mu
            var = jnp.mean(cen * cen, axis=-1, keepdims=True)
            o = cen * lax.rsqrt(var + NORM_EPS)
            gf = g_ref[bb, :, hd * C_V_DIM:(hd + 1) * C_V_DIM].astype(F32)
            o_ref[bb, :, hd * C_V_DIM:(hd + 1) * C_V_DIM] = (gf * jax.nn.sigmoid(gf) * o).astype(BF16)


def _retention(proj, log_decay, batch, seq):
    ch = RET_CHUNK
    nc = seq // ch
    proj3 = proj.reshape(batch, seq, ODD_IN_WIDTH)
    v0 = 2 * C_QK_WIDTH // C_V_WIDTH
    grid_spec = pltpu.PrefetchScalarGridSpec(
        num_scalar_prefetch=1,
        grid=(nc,),
        in_specs=[
            pl.BlockSpec((batch, ch, C_QK_WIDTH), lambda c, ld: (0, c, 0)),
            pl.BlockSpec((batch, ch, C_QK_WIDTH), lambda c, ld: (0, c, 1)),
            pl.BlockSpec((batch, ch, C_V_WIDTH), lambda c, ld: (0, c, v0)),
            pl.BlockSpec((batch, ch, C_V_WIDTH), lambda c, ld: (0, c, v0 + 1)),
        ],
        out_specs=pl.BlockSpec((batch, ch, C_V_WIDTH), lambda c, ld: (0, c, 0)),
        scratch_shapes=[pltpu.VMEM((batch, C_HEADS, C_QK_DIM, C_V_DIM), F32)],
    )
    y = pl.pallas_call(
        _ret_kernel,
        grid_spec=grid_spec,
        out_shape=jax.ShapeDtypeStruct((batch, seq, C_V_WIDTH), BF16),
        compiler_params=_cparams(("arbitrary",)),
        name="retention",
    )(log_decay, proj3, proj3, proj3, proj3)
    return y.reshape(batch * seq, C_V_WIDTH)


def _odd_out_kernel(h_ref, y_ref, w_ref, out_ref, wb_ref):
    _cast_once(w_ref, wb_ref, pl.program_id(0) == 0)
    out_ref[...] = h_ref[...] + jnp.dot(y_ref[...], wb_ref[...], preferred_element_type=F32)


def _odd_out(h, y, w):
    t = h.shape[0]
    tm = WIDE_ROW_TILE
    return pl.pallas_call(
        _odd_out_kernel,
        grid=(t // tm,),
        in_specs=[pl.BlockSpec((tm, D_MODEL), lambda i: (i, 0)),
                  pl.BlockSpec((tm, C_V_WIDTH), lambda i: (i, 0)),
                  pl.BlockSpec((None, C_V_WIDTH, D_MODEL), lambda i: (0, 0, 0), pipeline_mode=pl.Buffered(1))],
        out_specs=pl.BlockSpec((tm, D_MODEL), lambda i: (i, 0)),
        out_shape=jax.ShapeDtypeStruct((t, D_MODEL), F32),
        scratch_shapes=[pltpu.VMEM((C_V_WIDTH, D_MODEL), BF16)],
        compiler_params=_cparams(("arbitrary",)),
        name="odd_out_proj",
    )(h, y, w)


def _router_kernel(h_ref, g_ref, wr_ref, br_ref, x2s_ref, codes_ref, gates_ref, cnt_ref, base_ref, *,
                   tiles_per_batch):
    i = pl.program_id(0)
    tm = h_ref.shape[0]
    batch = i // tiles_per_batch

    @pl.when(i % tiles_per_batch == 0)
    def _():
        base_ref[...] = jnp.zeros(base_ref.shape, F32)

    x2 = _rms(h_ref[...], g_ref[...])
    for j in range(SLABS):
        x2s_ref[pl.ds(j, tm, stride=SLABS), :] = x2[:, j * LANES:(j + 1) * LANES]
    wr = wr_ref[...]
    x_hi, w_hi = x2.astype(BF16), wr.astype(BF16)
    x_lo = (x2 - x_hi.astype(F32)).astype(BF16)
    w_lo = (wr - w_hi.astype(F32)).astype(BF16)
    both = jnp.dot(x_hi, jnp.concatenate([w_hi, w_lo], axis=1), preferred_element_type=F32)
    logits = both[:, :LANES] + both[:, LANES:] + jnp.dot(x_lo, w_hi, preferred_element_type=F32) + br_ref[...]
    lane = lax.broadcasted_iota(I32, (tm, LANES), 1)
    is_grp = lane < MOE_GROUPS
    lg = jnp.where(is_grp, logits, -jnp.inf)
    gmax = jnp.max(lg, axis=-1, keepdims=True)
    gsum = jnp.sum(jnp.where(is_grp, jnp.exp(logits - gmax), 0.0), axis=-1, keepdims=True)
    gp = 1.0 / gsum
    gi = jnp.min(jnp.where(lg == gmax, lane, LANES), axis=-1, keepdims=True)
    ex = lane - ROUTER_LANE0
    in_grp = (ex >= 0) & (ex < MOE_EXPERTS) & ((ex // MOE_EXPERTS_PER_GROUP) == gi)
    sel = jnp.where(in_grp, logits, -jnp.inf)
    v1 = jnp.max(sel, axis=-1, keepdims=True)
    i1 = jnp.min(jnp.where(sel == v1, lane, LANES), axis=-1, keepdims=True)
    sel2 = jnp.where(lane == i1, -jnp.inf, sel)
    v2 = jnp.max(sel2, axis=-1, keepdims=True)
    i2 = jnp.min(jnp.where(sel2 == v2, lane, LANES), axis=-1, keepdims=True)
    tt = jnp.exp(v2 - v1)
    g0 = gp / (1.0 + tt)
    g1 = gp * tt / (1.0 + tt)
    oh0 = (lane == i1).astype(F32)
    oh1 = (lane == i2).astype(F32)
    oh = oh0 + oh1
    rr = lax.broadcasted_iota(I32, (tm, tm), 0)
    cc = lax.broadcasted_iota(I32, (tm, tm), 1)
    tri = (cc < rr).astype(BF16)
    prefix = jnp.dot(tri, oh.astype(BF16), preferred_element_type=F32)
    tot = base_ref[...] + prefix
    rank0 = jnp.sum(oh0 * tot, axis=-1, keepdims=True).astype(I32)
    rank1 = jnp.sum(oh1 * tot, axis=-1, keepdims=True).astype(I32)
    base_ref[...] = base_ref[...] + jnp.sum(oh, axis=0, keepdims=True)
    eoff = batch * MOE_EXPERTS - ROUTER_LANE0
    code0 = (i1 + eoff) * (1 << RANK_BITS) + rank0
    code1 = (i2 + eoff) * (1 << RANK_BITS) + rank1
    meta = jnp.where(lane == 0, code0, jnp.where(lane == 1, code1, 0))
    codes_ref[...] = jnp.transpose(meta)[0:8, :]
    gates_ref[...] = jnp.where(lane == 0, g0, jnp.where(lane == 1, g1, 0.0))
    cnt_ref[...] = jnp.broadcast_to(base_ref[...], cnt_ref.shape)


def _router(h, gain, wr, br, batch):
    t = h.shape[0]
    tm = ROW_TILE
    tpb = t // batch // tm
    row = lambda width: pl.BlockSpec((tm, width), lambda i: (i, 0))
    fixed = lambda shape: pl.BlockSpec(shape, lambda i: (0, 0))
    return pl.pallas_call(
        functools.partial(_router_kernel, tiles_per_batch=tpb),
        grid=(t // tm,),
        in_specs=[row(D_MODEL), fixed((1, D_MODEL)), fixed((D_MODEL, LANES)), fixed((1, LANES))],
        out_specs=[pl.BlockSpec((tm * SLABS, LANES), lambda i: (i, 0)),
                   pl.BlockSpec((8, tm), lambda i: (0, i)), row(LANES),
                   pl.BlockSpec((8, LANES), lambda i: (i // tpb, 0))],
        out_shape=[
            jax.ShapeDtypeStruct((t * SLABS, LANES), F32),
            jax.ShapeDtypeStruct((8, t), I32),
            jax.ShapeDtypeStruct((t, LANES), F32),
            jax.ShapeDtypeStruct((batch * 8, LANES), F32),
        ],
        scratch_shapes=[pltpu.VMEM((1, LANES), F32)],
        compiler_params=_cparams(("arbitrary",)),
        name="moe_router",
    )(h, gain.reshape(1, D_MODEL), wr, br)


INVERT_UNROLL = 16


def _dest_kernel(pstart_ref, codes_ref, dest_ref):
    codes = codes_ref[...]
    seg = lax.shift_right_logical(codes, RANK_BITS)
    dest = codes & ((1 << RANK_BITS) - 1)
    for i in range(pstart_ref.shape[0]):
        dest = dest + jnp.where(seg == i, pstart_ref[i], 0)
    dest_ref[...] = dest


def _dest_rows(pad_start, codes):
    grid_spec = pltpu.PrefetchScalarGridSpec(
        num_scalar_prefetch=1,
        grid=(1,),
        in_specs=[pl.BlockSpec(codes.shape, lambda i, ps: (0, 0))],
        out_specs=pl.BlockSpec(codes.shape, lambda i, ps: (0, 0)),
    )
    return pl.pallas_call(
        _dest_kernel,
        grid_spec=grid_spec,
        out_shape=jax.ShapeDtypeStruct(codes.shape, I32),
        compiler_params=_cparams(("arbitrary",)),
        name="moe_dest_rows",
    )(pad_start, codes)


def _invert_kernel(trips_ref, dest_ref, dummy_ref, slot_ref, *, batch):
    per_batch = dest_ref.shape[0] // batch
    pltpu.sync_copy(dummy_ref, slot_ref)
    for b in range(batch):
        def place(i, carry, b=b):
            for v in range(INVERT_UNROLL):
                a = i * INVERT_UNROLL + v
                slot_ref[dest_ref[b * per_batch + a]] = a * SLABS
            return carry

        lax.fori_loop(0, trips_ref[0], place, 0)


def _dummy_slots(n_rows, per_batch):
    assert MOE_BLOCK & (MOE_BLOCK - 1) == 0 and (EXPERT_RING + 1) * MOE_BLOCK <= DUMMY_SLOTS
    r = jnp.arange(n_rows + MOE_BLOCK, dtype=I32)
    pattern = per_batch + (r // MOE_BLOCK) % EXPERT_RING * MOE_BLOCK + r % MOE_BLOCK
    return jnp.where(r < n_rows, pattern, per_batch + EXPERT_RING * MOE_BLOCK + r % MOE_BLOCK) * SLABS


def _invert(dest_flat, n_rows, batch):
    smem = pl.BlockSpec(memory_space=pltpu.SMEM)
    per_batch = dest_flat.shape[0] // batch
    assert per_batch % INVERT_UNROLL == 0
    trips = jnp.full((1,), per_batch // INVERT_UNROLL, I32)
    return pl.pallas_call(
        functools.partial(_invert_kernel, batch=batch),
        in_specs=[smem, smem, pl.BlockSpec(memory_space=pl.ANY)],
        out_specs=smem,
        out_shape=jax.ShapeDtypeStruct((n_rows + MOE_BLOCK,), I32),
        name="moe_invert_rows",
    )(trips, dest_flat, _dummy_slots(n_rows, per_batch))


def _expert_kernel(be_ref, run_ref, nxt_ref, nreal_ref, slot_ref, x2s_ref, wg_hbm, wu_hbm, wd_hbm, ys_ref,
                   wbuf_g, wbuf_u, wbuf_d, wgb_ref, wub_ref, wdb_ref, tile, xb, ybuf, sems, wsems, gsems, *,
                   batch, nb, seq, layer):
    b = pl.program_id(0)
    n = pl.program_id(1)
    g = b * nb + n
    last = batch * nb - 1
    blk = MOE_BLOCK
    bstride = 2 * seq + DUMMY_SLOTS
    ring = EXPERT_RING
    ycur = g % ring
    yprev = (g + ring - 1) % ring
    n_real = nreal_ref[0]

    def weight_copies(e, ws):
        pairs = ((wg_hbm, wbuf_g), (wu_hbm, wbuf_u), (wd_hbm, wbuf_d))
        return [pltpu.make_async_copy(src.at[layer, e], dst.at[ws], wsems.at[ws, k])
                for k, (src, dst) in enumerate(pairs)]

    def wait_block(q):
        pltpu.make_async_copy(ybuf.at[q], ys_ref.at[pl.ds(0, blk * SLABS), :], sems.at[q]).wait()

    def gather_row(block, tslot, mi):
        s = slot_ref[jnp.minimum(block, last) * blk + mi]
        span = seq * SLABS
        if span & (span - 1) == 0:
            src = pl.multiple_of(s & (span - 1), SLABS)
        else:
            src = pl.multiple_of(jnp.minimum(jnp.where(s >= span, s - span, s), span - SLABS), SLABS)
        pltpu.make_async_copy(x2s_ref.at[pl.ds(src, SLABS), :], tile.at[tslot, pl.ds(mi * SLABS, SLABS), :],
                              gsems.at[tslot]).start()

    def wait_gather(tslot):
        pltpu.make_async_copy(x2s_ref.at[pl.ds(0, blk * SLABS), :], tile.at[tslot], gsems.at[tslot]).wait()

    def to_bf16(tslot, xslot):
        for j in range(SLABS):
            xb[xslot, :, j * LANES:(j + 1) * LANES] = tile[tslot, pl.ds(j, blk, stride=SLABS), :].astype(BF16)

    def scatter_row(yslot, slot, row0, mi):
        dst = pl.multiple_of(row0 * SLABS + slot, SLABS)
        pltpu.make_async_copy(ybuf.at[yslot, pl.ds(mi * SLABS, SLABS), :], ys_ref.at[pl.ds(dst, SLABS), :],
                              sems.at[yslot]).start()

    @pl.when(g == 0)
    def _():
        for c in weight_copies(be_ref[0], 0):
            c.start(priority=WEIGHT_DMA_PRIORITY)
        ybuf[...] = jnp.zeros(ybuf.shape, F32)
        for bb in range(batch):
            for c in range(DUMMY_SLOTS // blk):
                start = (bb * bstride + 2 * seq + c * blk) * SLABS
                zero = pltpu.make_async_copy(ybuf.at[0], ys_ref.at[pl.ds(start, blk * SLABS), :], sems.at[0])
                zero.start()
                zero.wait()
        for mi in range(blk):
            gather_row(0, 0, mi)
        wait_gather(0)
        to_bf16(0, 0)
        for first in range(1, ring - 1):
            for mi in range(blk):
                gather_row(first, first, mi)

    run = run_ref[g]
    ws = run % 2

    @pl.when((g == 0) | (run != run_ref[jnp.maximum(g - 1, 0)]))
    def _():
        for c in weight_copies(be_ref[g], ws):
            c.wait()
        wgb_ref[...] = wbuf_g[ws].astype(BF16)
        wub_ref[...] = wbuf_u[ws].astype(BF16)
        wdb_ref[...] = wbuf_d[ws].astype(BF16)

        @pl.when(nxt_ref[g] >= 0)
        def _():
            for c in weight_copies(nxt_ref[g], 1 - ws):
                c.start(priority=WEIGHT_DMA_PRIORITY)

    @pl.when(g >= ring - 1)
    def _():
        wait_block(ycur)

    prev_block = jnp.where(g == 0, batch * nb, g - 1)
    prev_row0 = jnp.where(n == 0, jnp.maximum(b - 1, 0), b) * bstride
    n_chunks = 8
    per = blk // n_chunks
    half = EXPERT_HIDDEN // 2
    quarter = D_MODEL // 4

    @pl.when(g < n_real)
    def _():
        x_cur, y_cur, y_prev = g % 2, ycur, yprev
        next_slot, fetch_slot = (g + 1) % ring, (g + ring - 1) % ring
        wait_gather(next_slot)
        for mi in range(blk):
            scatter_row(y_prev, slot_ref[prev_block * blk + mi], prev_row0, mi)
        to_bf16(next_slot, 1 - x_cur)
        x = xb[x_cur]
        acts = []
        hid = None
        for c in range(n_chunks):
            for mi in range(c * per, (c + 1) * per):
                gather_row(g + ring - 1, fetch_slot, mi)
            if c < 4:
                w_ref = wgb_ref if c < 2 else wub_ref
                acts.append(jnp.dot(x, w_ref[:, (c % 2) * half:(c % 2 + 1) * half], preferred_element_type=F32))
            if c == 3:
                a = jnp.concatenate(acts[0:2], axis=1)
                u = jnp.concatenate(acts[2:4], axis=1)
                hid = (a * jax.nn.sigmoid(a) * u).astype(BF16)
            if c >= 4:
                q = c - 4
                yq = jnp.dot(hid, wdb_ref[:, q * quarter:(q + 1) * quarter], preferred_element_type=F32)
                for jj in range(quarter // LANES):
                    j = q * (quarter // LANES) + jj
                    ybuf[y_cur, pl.ds(j, blk, stride=SLABS), :] = yq[:, jj * LANES:(jj + 1) * LANES]

    @pl.when(g >= n_real)
    def _():
        for mi in range(blk):
            scatter_row(yprev, slot_ref[prev_block * blk + mi], prev_row0, mi)

    @pl.when(g == last)
    def _():
        wait_block((last - 3) % ring)
        wait_block((last - 2) % ring)
        for mi in range(blk):
            scatter_row(last % ring, slot_ref[last * blk + mi], (batch - 1) * bstride, mi)
        wait_block((last - 1) % ring)
        wait_block(last % ring)
        for extra in range(1, ring - 1):
            wait_gather((n_real + extra) % ring)


def _experts(block_expert, n_real, row_slot, x2s, wg, wu, wd, layer, batch, seq):
    blk = MOE_BLOCK
    nb = (row_slot.shape[0] // blk - 1) // batch
    steps = batch * nb
    assert steps >= 4 and batch == 1
    idx = jnp.arange(steps, dtype=I32)
    change = jnp.concatenate([jnp.zeros((1,), I32), (block_expert[1:] != block_expert[:-1]).astype(I32)])
    run = jnp.cumsum(change).astype(I32)
    later_change = (idx[None, :] > idx[:, None]) & (change[None, :] > 0)
    nxt_idx = jnp.min(jnp.where(later_change, idx[None, :], steps), axis=1)
    nxt = jnp.where(nxt_idx < steps, block_expert[jnp.minimum(nxt_idx, steps - 1)], -1).astype(I32)
    hbm = pl.BlockSpec(memory_space=pl.ANY)
    grid_spec = pltpu.PrefetchScalarGridSpec(
        num_scalar_prefetch=5,
        grid=(batch, nb),
        in_specs=[hbm, hbm, hbm, hbm],
        out_specs=hbm,
        scratch_shapes=[
            pltpu.VMEM((2, D_MODEL, EXPERT_HIDDEN), F32), pltpu.VMEM((2, D_MODEL, EXPERT_HIDDEN), F32),
            pltpu.VMEM((2, EXPERT_HIDDEN, D_MODEL), F32),
            pltpu.VMEM((D_MODEL, EXPERT_HIDDEN), BF16), pltpu.VMEM((D_MODEL, EXPERT_HIDDEN), BF16),
            pltpu.VMEM((EXPERT_HIDDEN, D_MODEL), BF16),
            pltpu.VMEM((EXPERT_RING, blk * SLABS, LANES), F32),
            pltpu.VMEM((2, blk, D_MODEL), BF16),
            pltpu.VMEM((EXPERT_RING, blk * SLABS, LANES), F32),
            pltpu.SemaphoreType.DMA((EXPERT_RING,)),
            pltpu.SemaphoreType.DMA((2, 3)),
            pltpu.SemaphoreType.DMA((EXPERT_RING,)),
        ],
    )
    return pl.pallas_call(
        functools.partial(_expert_kernel, batch=batch, nb=nb, seq=seq, layer=layer),
        grid_spec=grid_spec,
        out_shape=jax.ShapeDtypeStruct((batch * (2 * seq + DUMMY_SLOTS) * SLABS, LANES), F32),
        compiler_params=_cparams(("arbitrary", "arbitrary")),
        name="moe_experts",
    )(block_expert, run, nxt, n_real, row_slot, x2s, wg, wu, wd)


def _combine_kernel(h_ref, gates_ref, y0_ref, y1_ref, *rest, final):
    if final:
        fg_ref, out_ref = rest
    else:
        (out_ref,) = rest
    tm = h_ref.shape[0]
    gates = gates_ref[...]
    g0, g1 = gates[:, 0:1], gates[:, 1:2]
    parts = []
    for j in range(SLABS):
        y0 = y0_ref[pl.ds(j, tm, stride=SLABS), :]
        y1 = y1_ref[pl.ds(j, tm, stride=SLABS), :]
        parts.append(g0 * y0 + g1 * y1)
    out = h_ref[...] + jnp.concatenate(parts, axis=1)
    if final:
        out = _rms(out, fg_ref[...])
    out_ref[...] = out


def _combine(h, gates, ys, batch, seq, final_gain=None):
    t = h.shape[0]
    tm = ROW_TILE
    tpb = seq // tm
    bstride = 2 * tpb + DUMMY_SLOTS // tm
    final = final_gain is not None
    in_specs = [pl.BlockSpec((tm, D_MODEL), lambda i: (i, 0)),
                pl.BlockSpec((tm, LANES), lambda i: (i, 0)),
                pl.BlockSpec((tm * SLABS, LANES), lambda i: (i // tpb * bstride + i % tpb, 0)),
                pl.BlockSpec((tm * SLABS, LANES), lambda i: (i // tpb * bstride + tpb + i % tpb, 0))]
    args = [h, gates, ys, ys]
    if final:
        in_specs.append(pl.BlockSpec((1, D_MODEL), lambda i: (0, 0)))
        args.append(final_gain.reshape(1, D_MODEL))
    return pl.pallas_call(
        functools.partial(_combine_kernel, final=final),
        grid=(t // tm,),
        in_specs=in_specs,
        out_specs=pl.BlockSpec((tm, D_MODEL), lambda i: (i, 0)),
        out_shape=jax.ShapeDtypeStruct((t, D_MODEL), F32),
        compiler_params=_cparams(("parallel",)),
        name="moe_combine_final" if final else "moe_combine",
    )(*args)


def _moe(h, gain, w_r1, b_r1, w_r2, b_r2, wg, wu, wd, layer, batch, final_gain=None):
    t = h.shape[0]
    seq = t // batch
    assert seq <= (1 << RANK_BITS) and seq % ROW_TILE == 0 and DUMMY_SLOTS % ROW_TILE == 0
    pad_w = jnp.zeros((D_MODEL, LANES - MOE_GROUPS - MOE_EXPERTS), F32)
    wr = jnp.concatenate([w_r1, w_r2, pad_w], axis=1)
    br = jnp.concatenate([b_r1, b_r2, jnp.zeros((LANES - MOE_GROUPS - MOE_EXPERTS,), F32)]).reshape(1, LANES)
    x2s, codes, gates, counts = _router(h, gain, wr, br, batch)
    cnt = counts.reshape(batch, 8, LANES)[:, 0, ROUTER_LANE0:ROUTER_LANE0 + MOE_EXPERTS].astype(I32)
    padded = (cnt + MOE_BLOCK - 1) // MOE_BLOCK * MOE_BLOCK
    pad_end = jnp.cumsum(padded, axis=1)
    rows_pb = seq * 2 + MOE_EXPERTS * MOE_BLOCK
    nb = rows_pb // MOE_BLOCK
    pad_start = pad_end - padded + (jnp.arange(batch, dtype=I32) * rows_pb)[:, None]
    block_start = jnp.arange(nb, dtype=I32) * MOE_BLOCK
    block_expert = jnp.minimum(
        jnp.sum((pad_end[:, None, :] <= block_start[None, :, None]).astype(I32), axis=2), MOE_EXPERTS - 1)
    dest = _dest_rows(pad_start.reshape(-1), codes)
    dest_flat = dest[0:2].reshape(2, batch, seq).transpose(1, 0, 2).reshape(-1)
    row_slot = _invert(dest_flat, batch * rows_pb, batch)
    n_real = (pad_end[:, -1] // MOE_BLOCK).astype(I32)
    ys = _experts(block_expert.reshape(-1), n_real, row_slot, x2s, wg, wu, wd, layer, batch, seq)
    return _combine(h, gates, ys, batch, seq, final_gain)


def kernel(x, mix_norm, ffn_norm, final_norm, even_w_in, even_w_out, conv_w, conv_b, conv_norm_g, conv_norm_b,
           odd_w_in, odd_w_out, router_w1, router_b1, router_w2, router_b2, expert_w_gate, expert_w_up,
           expert_w_down):
    batch, seq, d = x.shape
    assert d == D_MODEL and seq % (ATTN_BLOCK * max(dl for _, dl in A_BRANCHES)) == 0
    assert all(w // dl == ATTN_BLOCK for w, dl in A_BRANCHES)
    t = batch * seq
    h = x.reshape(t, d)

    qkv, conv_in = _even_in(h, mix_norm[0], even_w_in, _attn_rope_tables(seq), seq)
    attn = _attention(qkv, batch, seq)
    bconv = _conv(conv_in, conv_w[0], conv_b[0], conv_norm_g[0], conv_norm_b[0], batch, seq)
    h = _even_out(h, attn, bconv, even_w_out)
    h = _moe(h, ffn_norm[0], router_w1[0], router_b1[0], router_w2[0], router_b2[0],
             expert_w_gate, expert_w_up, expert_w_down, 0, 1)

    inv_freq = RET_ROT_THETA ** (-jnp.linspace(0.0, 1.0, C_QK_DIM // 2, dtype=F32))
    ang = jnp.arange(seq, dtype=F32)[:, None] * inv_freq[None, :]
    proj = _odd_in(h, mix_norm[1], odd_w_in, jnp.cos(ang), jnp.sin(ang), seq)
    log_decay = jnp.log(1.0 - jnp.exp2(-5.0 - jnp.arange(C_HEADS, dtype=F32)))
    y = _retention(proj, log_decay, batch, seq)
    h = _odd_out(h, y, odd_w_out)
    out = _moe(h, ffn_norm[1], router_w1[1], router_b1[1], router_w2[1], router_b2[1],
               expert_w_gate, expert_w_up, expert_w_down, 1, 1, final_gain=final_norm)
    return out.reshape(batch, seq, d)
```

```python
import functools

import jax
import jax.numpy as jnp
from jax import lax
from jax.experimental import pallas as pl
from jax.experimental.pallas import tpu as pltpu

F32 = jnp.float32
BF16 = jnp.bfloat16
I32 = jnp.int32

NORM_EPS = 1e-6
NEG_INF = -1e30

D_MODEL = 1024
A_HEADS = 8
A_HEAD_DIM = 64
A_WIDTH = A_HEADS * A_HEAD_DIM
A_BRANCHES = ((128, 1), (512, 4), (2048, 16))
ATTN_BLOCK = 128
ROPE_THETA = 500000.0
ROPE_DIM = A_HEAD_DIM // 4
B_WIDTH = D_MODEL - A_WIDTH
CONV_WIDTH = 31
C_HEADS = 4
C_QK_DIM = 256
C_V_DIM = 512
C_QK_WIDTH = C_HEADS * C_QK_DIM
C_V_WIDTH = C_HEADS * C_V_DIM
RET_CHUNK = 128
RET_ROT_THETA = 10000.0
MOE_GROUPS = 4
MOE_EXPERTS_PER_GROUP = 8
MOE_EXPERTS = MOE_GROUPS * MOE_EXPERTS_PER_GROUP
EXPERT_HIDDEN = 512
MOE_BLOCK = 128
EVEN_IN_WIDTH = 3 * A_WIDTH + 2 * B_WIDTH
ODD_IN_WIDTH = 2 * C_QK_WIDTH + 2 * C_V_WIDTH

LANES = 128
ROW_TILE = 512
WIDE_ROW_TILE = 1024
CONV_TILE = 512
CONV_HALO = 32
CONV_CHUNK = 64
ROUTER_LANE0 = MOE_GROUPS
VMEM_LIMIT = 56 * 1024 * 1024
SLABS = D_MODEL // LANES
RANK_BITS = 16
EXPERT_RING = 4
DUMMY_SLOTS = 2 * ROW_TILE


def _cparams(sem, vmem=VMEM_LIMIT):
    return pltpu.CompilerParams(dimension_semantics=sem, vmem_limit_bytes=vmem)


def _cast_once(w_ref, wb_ref, first):
    @pl.when(first)
    def _():
        wb_ref[...] = w_ref[...].astype(BF16)


def _rms(x, gain):
    ms = jnp.mean(x * x, axis=-1, keepdims=True)
    return x * lax.rsqrt(ms + NORM_EPS) * gain


def _even_in_kernel(h_ref, g_ref, w_ref, c_ref, s1_ref, s2_ref, qkv_ref, conv_ref, wb_ref):
    _cast_once(w_ref, wb_ref, pl.program_id(0) == 0)
    u = _rms(h_ref[...], g_ref[...]).astype(BF16)
    acc = jnp.dot(u, wb_ref[...], preferred_element_type=F32)
    c, s1, s2 = c_ref[...], s1_ref[...], s2_ref[...]
    for j in range(2 * A_WIDTH // LANES):
        xg = acc[:, j * LANES:(j + 1) * LANES]
        if j < A_WIDTH // LANES:
            xg = xg * (A_HEAD_DIM ** -0.5)
        qkv_ref[j] = xg * c + pltpu.roll(xg, LANES - ROPE_DIM // 2, 1) * s1 + pltpu.roll(xg, ROPE_DIM // 2, 1) * s2
    for j in range(2 * A_WIDTH // LANES, 3 * A_WIDTH // LANES):
        qkv_ref[j] = acc[:, j * LANES:(j + 1) * LANES]
    conv_ref[...] = acc[:, 3 * A_WIDTH:].astype(BF16)


def _even_in(h, gain, w, tabs, seq):
    t = h.shape[0]
    tm = ROW_TILE
    nseq = seq // tm
    tab_spec = pl.BlockSpec((tm, LANES), lambda i: (i % nseq, 0))
    return pl.pallas_call(
        _even_in_kernel,
        grid=(t // tm,),
        in_specs=[
            pl.BlockSpec((tm, D_MODEL), lambda i: (i, 0)),
            pl.BlockSpec((1, D_MODEL), lambda i: (0, 0)),
            pl.BlockSpec((None, D_MODEL, EVEN_IN_WIDTH), lambda i: (0, 0, 0)),
            tab_spec, tab_spec, tab_spec,
        ],
        out_specs=[
            pl.BlockSpec((3 * A_WIDTH // LANES, tm, LANES), lambda i: (0, i, 0)),
            pl.BlockSpec((tm, 2 * B_WIDTH), lambda i: (i, 0)),
        ],
        out_shape=[
            jax.ShapeDtypeStruct((3 * A_WIDTH // LANES, t, LANES), F32),
            jax.ShapeDtypeStruct((t, 2 * B_WIDTH), BF16),
        ],
        scratch_shapes=[pltpu.VMEM((D_MODEL, EVEN_IN_WIDTH), BF16)],
        compiler_params=_cparams(("arbitrary",)),
        name="even_in_proj",
    )(h, gain.reshape(1, D_MODEL), w, *tabs)


def _attn_rope_tables(seq):
    half = ROPE_DIM // 2
    inv_freq = ROPE_THETA ** (-jnp.arange(0, ROPE_DIM, 2, dtype=F32) / ROPE_DIM)
    ang = jnp.arange(seq, dtype=F32)[:, None] * inv_freq[None, :]
    cos, sin = jnp.cos(ang), jnp.sin(ang)
    rest = A_HEAD_DIM - ROPE_DIM
    ones = jnp.ones((seq, rest), F32)
    z_rest = jnp.zeros((seq, rest), F32)
    z_half = jnp.zeros((seq, half), F32)
    c = jnp.concatenate([cos, cos, ones], axis=1)
    s1 = jnp.concatenate([-sin, z_half, z_rest], axis=1)
    s2 = jnp.concatenate([z_half, sin, z_rest], axis=1)
    rep = LANES // A_HEAD_DIM
    return tuple(jnp.tile(a, (1, rep)) for a in (c, s1, s2))


ATTN_STEP = ATTN_BLOCK * max(d for _, d in A_BRANCHES)
ATTN_PAIRS = A_WIDTH // LANES
ATTN_MERGE_ROWS = 256


def _attn_kernel(q_ref, k_ref, v_ref, o_ref, kprev, vprev, oml):
    n = pl.program_id(1)
    blk = ATTN_BLOCK
    n_blocks = ATTN_STEP // blk

    @pl.when(n == 0)
    def _():
        kprev[...] = jnp.zeros(kprev.shape, BF16)
        vprev[...] = jnp.zeros(vprev.shape, BF16)

    qi = lax.broadcasted_iota(I32, (blk, 2 * blk), 0)
    kj = lax.broadcasted_iota(I32, (blk, 2 * blk), 1)
    dist = blk + qi - kj
    band = (dist >= 0) & (dist <= blk)
    in_cur = kj >= blk
    lane = lax.broadcasted_iota(I32, (blk, LANES), 1)
    lo = lane < A_HEAD_DIM

    off = 0
    for bi, (_, dil) in enumerate(A_BRANCHES):
        shift = dil.bit_length() - 1
        assert dil == 1 << shift

        def one_block(idx, carried, bi=bi, dil=dil, shift=shift, off=off):
            u = lax.shift_right_logical(idx, shift)
            r = idx & (dil - 1)
            start = u * (blk * dil) + r
            rows = pl.ds(pl.multiple_of(start, blk), blk) if dil == 1 else pl.ds(start, blk, stride=dil)
            has_prev = (n > 0) | (u > 0)
            valid = band & (in_cur | has_prev)
            valid2 = jnp.concatenate([valid, valid], axis=0)
            current = []
            for g in range(ATTN_PAIRS):
                q = q_ref[g, rows, :]
                kc = k_ref[g, rows, :].astype(BF16)
                vc = v_ref[g, rows, :].astype(BF16)
                current.append((kc, vc))
                kp, vp = (kprev[g, off + r], vprev[g, off + r]) if carried is None else carried[g]
                q2 = jnp.concatenate([jnp.where(lo, q, 0.0), jnp.where(lo, 0.0, q)], axis=0).astype(BF16)
                kk = jnp.concatenate([kp, kc], axis=0)
                vv = jnp.concatenate([vp, vc], axis=0)
                s = lax.dot_general(q2, kk, (((1,), (1,)), ((), ())), preferred_element_type=F32)
                s = jnp.where(valid2, s, NEG_INF)
                m = jnp.max(s, axis=-1, keepdims=True)
                e = jnp.exp(s - m)
                den = jnp.sum(e, axis=-1, keepdims=True)
                pv = jnp.dot(e.astype(BF16), vv, preferred_element_type=F32)
                o_new = jnp.where(lo, pv[:blk], pv[blk:])
                m_new = jnp.where(lo, m[:blk], m[blk:])
                l_new = jnp.where(lo, den[:blk], den[blk:])
                if bi > 0:
                    o_run, m_run, l_run = oml[0, g, rows, :], oml[1, g, rows, :], oml[2, g, rows, :]
                    m_both = jnp.maximum(m_run, m_new)
                    w_run, w_new = jnp.exp(m_run - m_both), jnp.exp(m_new - m_both)
                    o_new = w_run * o_run + w_new * o_new
                    l_new = w_run * l_run + w_new * l_new
                    m_new = m_both
                oml[0, g, rows, :] = o_new
                oml[1, g, rows, :] = m_new
                oml[2, g, rows, :] = l_new
            return r, current

        def save_prev(r, current, off=off):
            for g, (kc, vc) in enumerate(current):
                kprev[g, off + r] = kc
                vprev[g, off + r] = vc

        def block_body(i, c, dil=dil):
            r0, cur0 = one_block(2 * i, None)
            if dil == 1:
                r1, cur1 = one_block(2 * i + 1, cur0)
            else:
                save_prev(r0, cur0)
                r1, cur1 = one_block(2 * i + 1, None)
            save_prev(r1, cur1)
            return c

        lax.fori_loop(0, n_blocks // 2, block_body, 0)
        off += dil
    for g in range(ATTN_PAIRS):
        for c in range(ATTN_STEP // ATTN_MERGE_ROWS):
            sl = slice(c * ATTN_MERGE_ROWS, (c + 1) * ATTN_MERGE_ROWS)
            o_ref[g, sl, :] = (oml[0, g, sl, :] / oml[2, g, sl, :]).astype(BF16)


def _attention(qkv, batch, seq):
    t = qkv.shape[1]
    steps = seq // ATTN_STEP
    n_res = sum(d for _, d in A_BRANCHES)

    def slabs(which):
        return pl.BlockSpec((ATTN_PAIRS, ATTN_STEP, LANES), lambda b, n: (which, b * steps + n, 0))

    return pl.pallas_call(
        _attn_kernel,
        grid=(batch, steps),
        in_specs=[slabs(0), slabs(1), slabs(2)],
        out_specs=pl.BlockSpec((ATTN_PAIRS, ATTN_STEP, LANES), lambda b, n: (0, b * steps + n, 0)),
        out_shape=jax.ShapeDtypeStruct((ATTN_PAIRS, t, LANES), BF16),
        scratch_shapes=[
            pltpu.VMEM((ATTN_PAIRS, n_res, ATTN_BLOCK, LANES), BF16),
            pltpu.VMEM((ATTN_PAIRS, n_res, ATTN_BLOCK, LANES), BF16),
            pltpu.VMEM((3, ATTN_PAIRS, ATTN_STEP, LANES), F32),
        ],
        compiler_params=_cparams(("parallel", "arbitrary")),
        name="dilated_attention",
    )(qkv, qkv, qkv)


def _conv_kernel(val_ref, gate_ref, w_ref, b_ref, g_ref, beta_ref, o_ref, abuf, shifted):
    n = pl.program_id(1)
    tc = CONV_TILE
    sub = 8

    @pl.when(n == 0)
    def _():
        abuf[0:CONV_HALO, :] = jnp.zeros((CONV_HALO, B_WIDTH), F32)

    @pl.when(n > 0)
    def _():
        abuf[0:CONV_HALO, :] = abuf[tc:tc + CONV_HALO, :]

    val = val_ref[...].astype(F32)
    gate = gate_ref[...].astype(F32)
    abuf[CONV_HALO:CONV_HALO + tc, :] = val * jax.nn.sigmoid(gate)
    off = CONV_HALO - (CONV_WIDTH - 1)
    span = shifted.shape[1]
    for s in range(1, sub):
        shifted[s - 1] = abuf[s:s + span, :]
    for c in range(tc // CONV_CHUNK):
        acc = jnp.broadcast_to(b_ref[...], (CONV_CHUNK, B_WIDTH))
        for j in range(CONV_WIDTH):
            s = (off + j) % sub
            r0 = c * CONV_CHUNK + off + j - s
            src = abuf[r0:r0 + CONV_CHUNK, :] if s == 0 else shifted[s - 1, r0:r0 + CONV_CHUNK, :]
            acc = acc + w_ref[j:j + 1, :] * src
        mu = jnp.mean(acc, axis=-1, keepdims=True)
        cen = acc - mu
        var = jnp.mean(cen * cen, axis=-1, keepdims=True)
        yn = cen * lax.rsqrt(var + NORM_EPS) * g_ref[...] + beta_ref[...]
        o_ref[c * CONV_CHUNK:(c + 1) * CONV_CHUNK, :] = (yn * jax.nn.sigmoid(yn)).astype(BF16)


def _conv(conv_in, w, b, g, beta, batch, seq):
    t = conv_in.shape[0]
    tc = CONV_TILE
    nt = seq // tc
    w_pad = jnp.concatenate([w, jnp.zeros((CONV_HALO - CONV_WIDTH, B_WIDTH), F32)], axis=0)
    vec = pl.BlockSpec((1, B_WIDTH), lambda bb, n: (0, 0))
    return pl.pallas_call(
        _conv_kernel,
        grid=(batch, nt),
        in_specs=[
            pl.BlockSpec((tc, B_WIDTH), lambda bb, n: (bb * nt + n, 0)),
            pl.BlockSpec((tc, B_WIDTH), lambda bb, n: (bb * nt + n, 1)),
            pl.BlockSpec((CONV_HALO, B_WIDTH), lambda bb, n: (0, 0)),
            vec, vec, vec,
        ],
        out_specs=pl.BlockSpec((tc, B_WIDTH), lambda bb, n: (bb * nt + n, 0)),
        out_shape=jax.ShapeDtypeStruct((t, B_WIDTH), BF16),
        scratch_shapes=[pltpu.VMEM((tc + CONV_HALO, B_WIDTH), F32),
                        pltpu.VMEM((7, tc + CONV_HALO - 8, B_WIDTH), F32)],
        compiler_params=_cparams(("parallel", "arbitrary")),
        name="conformer_conv",
    )(conv_in, conv_in, w_pad, b.reshape(1, -1), g.reshape(1, -1), beta.reshape(1, -1))


def _even_out_kernel(h_ref, a_ref, bc_ref, w_ref, out_ref, wb_ref):
    _cast_once(w_ref, wb_ref, pl.program_id(0) == 0)
    a = jnp.concatenate([a_ref[g] for g in range(ATTN_PAIRS)], axis=1)
    acc = jnp.dot(a, wb_ref[0:A_WIDTH, :], preferred_element_type=F32)
    acc = acc + jnp.dot(bc_ref[...], wb_ref[A_WIDTH:, :], preferred_element_type=F32)
    out_ref[...] = h_ref[...] + acc


def _even_out(h, attn, bconv, w):
    t = h.shape[0]
    tm = WIDE_ROW_TILE
    row = lambda width: pl.BlockSpec((tm, width), lambda i: (i, 0))
    return pl.pallas_call(
        _even_out_kernel,
        grid=(t // tm,),
        in_specs=[row(D_MODEL), pl.BlockSpec((ATTN_PAIRS, tm, LANES), lambda i: (0, i, 0)), row(B_WIDTH),
                  pl.BlockSpec((None, D_MODEL, D_MODEL), lambda i: (0, 0, 0), pipeline_mode=pl.Buffered(1))],
        out_specs=row(D_MODEL),
        out_shape=jax.ShapeDtypeStruct((t, D_MODEL), F32),
        scratch_shapes=[pltpu.VMEM((D_MODEL, D_MODEL), BF16)],
        compiler_params=_cparams(("arbitrary",)),
        name="even_out_proj",
    )(h, attn, bconv, w)


def _odd_in_kernel(h_ref, g_ref, w_ref, cos_ref, sin_ref, o_ref, wb_ref):
    j = pl.program_id(0)
    _cast_once(w_ref, wb_ref, pl.program_id(1) == 0)
    u = _rms(h_ref[...], g_ref[...]).astype(BF16)
    acc = jnp.dot(u, wb_ref[...], preferred_element_type=F32)

    @pl.when(j == 0)
    def _():
        cos, sin = cos_ref[...], sin_ref[...]
        half = C_QK_DIM // 2
        for hd in range(2 * C_HEADS):
            x1 = acc[:, hd * C_QK_DIM:hd * C_QK_DIM + half]
            x2 = acc[:, hd * C_QK_DIM + half:(hd + 1) * C_QK_DIM]
            r1 = x1 * cos - x2 * sin
            r2 = x2 * cos + x1 * sin
            if hd >= C_HEADS:
                r1 = r1 * (C_QK_DIM ** -0.5)
                r2 = r2 * (C_QK_DIM ** -0.5)
            o_ref[:, hd * C_QK_DIM:hd * C_QK_DIM + half] = r1.astype(BF16)
            o_ref[:, hd * C_QK_DIM + half:(hd + 1) * C_QK_DIM] = r2.astype(BF16)

    @pl.when(j > 0)
    def _():
        o_ref[...] = acc.astype(BF16)


def _odd_in(h, gain, w, cos, sin, seq):
    t = h.shape[0]
    tm = WIDE_ROW_TILE
    tn = 2 * C_QK_WIDTH
    nseq = seq // tm
    tab = pl.BlockSpec((tm, C_QK_DIM // 2), lambda j, i: (i % nseq, 0))
    return pl.pallas_call(
        _odd_in_kernel,
        grid=(ODD_IN_WIDTH // tn, t // tm),
        in_specs=[
            pl.BlockSpec((tm, D_MODEL), lambda j, i: (i, 0)),
            pl.BlockSpec((1, D_MODEL), lambda j, i: (0, 0)),
            pl.BlockSpec((None, D_MODEL, tn), lambda j, i: (0, 0, j), pipeline_mode=pl.Buffered(1)),
            tab, tab,
        ],
        out_specs=pl.BlockSpec((tm, tn), lambda j, i: (i, j)),
        out_shape=jax.ShapeDtypeStruct((t, ODD_IN_WIDTH), BF16),
        scratch_shapes=[pltpu.VMEM((D_MODEL, tn), BF16)],
        compiler_params=_cparams(("arbitrary", "arbitrary")),
        name="odd_in_proj",
    )(h, gain.reshape(1, D_MODEL), w, cos, sin)


def _ret_kernel(ld_ref, q_ref, k_ref, v_ref, g_ref, o_ref, state):
    c = pl.program_id(0)
    ch = RET_CHUNK
    batch = q_ref.shape[0]

    @pl.when(c == 0)
    def _():
        state[...] = jnp.zeros(state.shape, F32)

    ii = lax.broadcasted_iota(I32, (ch, ch), 0)
    jj = lax.broadcasted_iota(I32, (ch, ch), 1)
    diff = (ii - jj).astype(F32)
    pos = lax.broadcasted_iota(I32, (ch, 1), 0).astype(F32)
    for hd in range(C_HEADS):
        ld = ld_ref[hd]
        intra = jnp.where(diff >= 0, jnp.exp(ld * jnp.maximum(diff, 0.0)), 0.0)
        q_decay = jnp.exp(ld * (pos + 1.0))
        k_decay = jnp.exp(ld * (ch - 1.0 - pos))
        chunk_decay = jnp.exp(ld * jnp.full((1, 1), float(ch), F32))
        for bb in range(batch):
            q = q_ref[bb, :, hd * C_QK_DIM:(hd + 1) * C_QK_DIM]
            k = k_ref[bb, :, hd * C_QK_DIM:(hd + 1) * C_QK_DIM]
            v = v_ref[bb, :, hd * C_V_DIM:(hd + 1) * C_V_DIM]
            s = lax.dot_general(q, k, (((1,), (1,)), ((), ())), preferred_element_type=F32) * intra
            inner = jnp.dot(s.astype(BF16), v, preferred_element_type=F32)
            st = state[bb, hd]
            cross = jnp.dot(q, st.astype(BF16), preferred_element_type=F32) * q_decay
            kd_t = jnp.transpose(k.astype(F32) * k_decay).astype(BF16)
            state[bb, hd] = st * chunk_decay + jnp.dot(kd_t, v, preferred_element_type=F32)
            out = inner + cross
            mu = jnp.mean(out, axis=-1, keepdims=True)
            cen = out - mu
            var = jnp.mean(cen * cen, axis=-1, keepdims=True)
            o = cen * lax.rsqrt(var + NORM_EPS)
            gf = g_ref[bb, :, hd * C_V_DIM:(hd + 1) * C_V_DIM].astype(F32)
            o_ref[bb, :, hd * C_V_DIM:(hd + 1) * C_V_DIM] = (gf * jax.nn.sigmoid(gf) * o).astype(BF16)


def _retention(proj, log_decay, batch, seq):
    ch = RET_CHUNK
    nc = seq // ch
    proj3 = proj.reshape(batch, seq, ODD_IN_WIDTH)
    v0 = 2 * C_QK_WIDTH // C_V_WIDTH
    grid_spec = pltpu.PrefetchScalarGridSpec(
        num_scalar_prefetch=1,
        grid=(nc,),
        in_specs=[
            pl.BlockSpec((batch, ch, C_QK_WIDTH), lambda c, ld: (0, c, 0)),
            pl.BlockSpec((batch, ch, C_QK_WIDTH), lambda c, ld: (0, c, 1)),
            pl.BlockSpec((batch, ch, C_V_WIDTH), lambda c, ld: (0, c, v0)),
            pl.BlockSpec((batch, ch, C_V_WIDTH), lambda c, ld: (0, c, v0 + 1)),
        ],
        out_specs=pl.BlockSpec((batch, ch, C_V_WIDTH), lambda c, ld: (0, c, 0)),
        scratch_shapes=[pltpu.VMEM((batch, C_HEADS, C_QK_DIM, C_V_DIM), F32)],
    )
    y = pl.pallas_call(
        _ret_kernel,
        grid_spec=grid_spec,
        out_shape=jax.ShapeDtypeStruct((batch, seq, C_V_WIDTH), BF16),
        compiler_params=_cparams(("arbitrary",)),
        name="retention",
    )(log_decay, proj3, proj3, proj3, proj3)
    return y.reshape(batch * seq, C_V_WIDTH)


def _odd_out_kernel(h_ref, y_ref, w_ref, out_ref, wb_ref):
    _cast_once(w_ref, wb_ref, pl.program_id(0) == 0)
    out_ref[...] = h_ref[...] + jnp.dot(y_ref[...], wb_ref[...], preferred_element_type=F32)


def _odd_out(h, y, w):
    t = h.shape[0]
    tm = WIDE_ROW_TILE
    return pl.pallas_call(
        _odd_out_kernel,
        grid=(t // tm,),
        in_specs=[pl.BlockSpec((tm, D_MODEL), lambda i: (i, 0)),
                  pl.BlockSpec((tm, C_V_WIDTH), lambda i: (i, 0)),
                  pl.BlockSpec((None, C_V_WIDTH, D_MODEL), lambda i: (0, 0, 0), pipeline_mode=pl.Buffered(1))],
        out_specs=pl.BlockSpec((tm, D_MODEL), lambda i: (i, 0)),
        out_shape=jax.ShapeDtypeStruct((t, D_MODEL), F32),
        scratch_shapes=[pltpu.VMEM((C_V_WIDTH, D_MODEL), BF16)],
        compiler_params=_cparams(("arbitrary",)),
        name="odd_out_proj",
    )(h, y, w)


def _router_kernel(h_ref, g_ref, wr_ref, br_ref, x2s_ref, codes_ref, gates_ref, cnt_ref, base_ref, *,
                   tiles_per_batch):
    i = pl.program_id(0)
    tm = h_ref.shape[0]
    batch = i // tiles_per_batch

    @pl.when(i % tiles_per_batch == 0)
    def _():
        base_ref[...] = jnp.zeros(base_ref.shape, F32)

    x2 = _rms(h_ref[...], g_ref[...])
    for j in range(SLABS):
        x2s_ref[pl.ds(j, tm, stride=SLABS), :] = x2[:, j * LANES:(j + 1) * LANES]
    wr = wr_ref[...]
    x_hi, w_hi = x2.astype(BF16), wr.astype(BF16)
    x_lo = (x2 - x_hi.astype(F32)).astype(BF16)
    w_lo = (wr - w_hi.astype(F32)).astype(BF16)
    both = jnp.dot(x_hi, jnp.concatenate([w_hi, w_lo], axis=1), preferred_element_type=F32)
    logits = both[:, :LANES] + both[:, LANES:] + jnp.dot(x_lo, w_hi, preferred_element_type=F32) + br_ref[...]
    lane = lax.broadcasted_iota(I32, (tm, LANES), 1)
    is_grp = lane < MOE_GROUPS
    lg = jnp.where(is_grp, logits, -jnp.inf)
    gmax = jnp.max(lg, axis=-1, keepdims=True)
    gsum = jnp.sum(jnp.where(is_grp, jnp.exp(logits - gmax), 0.0), axis=-1, keepdims=True)
    gp = 1.0 / gsum
    gi = jnp.min(jnp.where(lg == gmax, lane, LANES), axis=-1, keepdims=True)
    ex = lane - ROUTER_LANE0
    in_grp = (ex >= 0) & (ex < MOE_EXPERTS) & ((ex // MOE_EXPERTS_PER_GROUP) == gi)
    sel = jnp.where(in_grp, logits, -jnp.inf)
    v1 = jnp.max(sel, axis=-1, keepdims=True)
    i1 = jnp.min(jnp.where(sel == v1, lane, LANES), axis=-1, keepdims=True)
    sel2 = jnp.where(lane == i1, -jnp.inf, sel)
    v2 = jnp.max(sel2, axis=-1, keepdims=True)
    i2 = jnp.min(jnp.where(sel2 == v2, lane, LANES), axis=-1, keepdims=True)
    tt = jnp.exp(v2 - v1)
    g0 = gp / (1.0 + tt)
    g1 = gp * tt / (1.0 + tt)
    oh0 = (lane == i1).astype(F32)
    oh1 = (lane == i2).astype(F32)
    oh = oh0 + oh1
    rr = lax.broadcasted_iota(I32, (tm, tm), 0)
    cc = lax.broadcasted_iota(I32, (tm, tm), 1)
    tri = (cc < rr).astype(BF16)
    prefix = jnp.dot(tri, oh.astype(BF16), preferred_element_type=F32)
    tot = base_ref[...] + prefix
    rank0 = jnp.sum(oh0 * tot, axis=-1, keepdims=True).astype(I32)
    rank1 = jnp.sum(oh1 * tot, axis=-1, keepdims=True).astype(I32)
    base_ref[...] = base_ref[...] + jnp.sum(oh, axis=0, keepdims=True)
    eoff = batch * MOE_EXPERTS - ROUTER_LANE0
    code0 = (i1 + eoff) * (1 << RANK_BITS) + rank0
    code1 = (i2 + eoff) * (1 << RANK_BITS) + rank1
    meta = jnp.where(lane == 0, code0, jnp.where(lane == 1, code1, 0))
    codes_ref[...] = jnp.transpose(meta)[0:8, :]
    gates_ref[...] = jnp.where(lane == 0, g0, jnp.where(lane == 1, g1, 0.0))
    cnt_ref[...] = jnp.broadcast_to(base_ref[...], cnt_ref.shape)


def _router(h, gain, wr, br, batch):
    t = h.shape[0]
    tm = ROW_TILE
    tpb = t // batch // tm
    row = lambda width: pl.BlockSpec((tm, width), lambda i: (i, 0))
    fixed = lambda shape: pl.BlockSpec(shape, lambda i: (0, 0))
    return pl.pallas_call(
        functools.partial(_router_kernel, tiles_per_batch=tpb),
        grid=(t // tm,),
        in_specs=[row(D_MODEL), fixed((1, D_MODEL)), fixed((D_MODEL, LANES)), fixed((1, LANES))],
        out_specs=[pl.BlockSpec((tm * SLABS, LANES), lambda i: (i, 0)),
                   pl.BlockSpec((8, tm), lambda i: (0, i)), row(LANES),
                   pl.BlockSpec((8, LANES), lambda i: (i // tpb, 0))],
        out_shape=[
            jax.ShapeDtypeStruct((t * SLABS, LANES), F32),
            jax.ShapeDtypeStruct((8, t), I32),
            jax.ShapeDtypeStruct((t, LANES), F32),
            jax.ShapeDtypeStruct((batch * 8, LANES), F32),
        ],
        scratch_shapes=[pltpu.VMEM((1, LANES), F32)],
        compiler_params=_cparams(("arbitrary",)),
        name="moe_router",
    )(h, gain.reshape(1, D_MODEL), wr, br)


INVERT_UNROLL = 32


def _dest_kernel(pstart_ref, codes_ref, dest_ref):
    codes = codes_ref[...]
    seg = lax.shift_right_logical(codes, RANK_BITS)
    dest = codes & ((1 << RANK_BITS) - 1)
    for i in range(pstart_ref.shape[0]):
        dest = dest + jnp.where(seg == i, pstart_ref[i], 0)
    dest_ref[...] = dest


def _dest_rows(pad_start, codes):
    grid_spec = pltpu.PrefetchScalarGridSpec(
        num_scalar_prefetch=1,
        grid=(1,),
        in_specs=[pl.BlockSpec(codes.shape, lambda i, ps: (0, 0))],
        out_specs=pl.BlockSpec(codes.shape, lambda i, ps: (0, 0)),
    )
    return pl.pallas_call(
        _dest_kernel,
        grid_spec=grid_spec,
        out_shape=jax.ShapeDtypeStruct(codes.shape, I32),
        compiler_params=_cparams(("arbitrary",)),
        name="moe_dest_rows",
    )(pad_start, codes)


def _invert_kernel(trips_ref, dest_ref, dummy_ref, slot_ref, *, batch):
    per_batch = dest_ref.shape[0] // batch
    pltpu.sync_copy(dummy_ref, slot_ref)
    for b in range(batch):
        def place(i, carry, b=b):
            for v in range(INVERT_UNROLL):
                a = i * INVERT_UNROLL + v
                slot_ref[dest_ref[b * per_batch + a]] = a * SLABS
            return carry

        lax.fori_loop(0, trips_ref[0], place, 0)


def _dummy_slots(n_rows, per_batch):
    assert MOE_BLOCK & (MOE_BLOCK - 1) == 0 and (EXPERT_RING + 1) * MOE_BLOCK <= DUMMY_SLOTS
    r = jnp.arange(n_rows + MOE_BLOCK, dtype=I32)
    pattern = per_batch + (r // MOE_BLOCK) % EXPERT_RING * MOE_BLOCK + r % MOE_BLOCK
    return jnp.where(r < n_rows, pattern, per_batch + EXPERT_RING * MOE_BLOCK + r % MOE_BLOCK) * SLABS


def _invert(dest_flat, n_rows, batch):
    smem = pl.BlockSpec(memory_space=pltpu.SMEM)
    per_batch = dest_flat.shape[0] // batch
    assert per_batch % INVERT_UNROLL == 0
    trips = jnp.full((1,), per_batch // INVERT_UNROLL, I32)
    return pl.pallas_call(
        functools.partial(_invert_kernel, batch=batch),
        in_specs=[smem, smem, pl.BlockSpec(memory_space=pl.ANY)],
        out_specs=smem,
        out_shape=jax.ShapeDtypeStruct((n_rows + MOE_BLOCK,), I32),
        name="moe_invert_rows",
    )(trips, dest_flat, _dummy_slots(n_rows, per_batch))


def _expert_kernel(be_ref, run_ref, nxt_ref, nreal_ref, slot_ref, x2s_ref, wg_hbm, wu_hbm, wd_hbm, ys_ref,
                   wbuf_g, wbuf_u, wbuf_d, wgb_ref, wub_ref, wdb_ref, tile, xb, ybuf, sems, wsems, gsems, *,
                   batch, nb, seq, layer):
    b = pl.program_id(0)
    n = pl.program_id(1)
    g = b * nb + n
    last = batch * nb - 1
    blk = MOE_BLOCK
    bstride = 2 * seq + DUMMY_SLOTS
    ring = EXPERT_RING
    ycur = g % ring
    yprev = (g + ring - 1) % ring
    n_real = nreal_ref[0]

    def weight_copies(e, ws):
        pairs = ((wg_hbm, wbuf_g), (wu_hbm, wbuf_u), (wd_hbm, wbuf_d))
        return [pltpu.make_async_copy(src.at[layer, e], dst.at[ws], wsems.at[ws, k])
                for k, (src, dst) in enumerate(pairs)]

    def wait_block(q):
        pltpu.make_async_copy(ybuf.at[q], ys_ref.at[pl.ds(0, blk * SLABS), :], sems.at[q]).wait()

    def gather_row(block, tslot, mi):
        s = slot_ref[jnp.minimum(block, last) * blk + mi]
        span = seq * SLABS
        if span & (span - 1) == 0:
            src = pl.multiple_of(s & (span - 1), SLABS)
        else:
            src = pl.multiple_of(jnp.minimum(jnp.where(s >= span, s - span, s), span - SLABS), SLABS)
        pltpu.make_async_copy(x2s_ref.at[pl.ds(src, SLABS), :], tile.at[tslot, pl.ds(mi * SLABS, SLABS), :],
                              gsems.at[tslot]).start()

    def wait_gather(tslot):
        pltpu.make_async_copy(x2s_ref.at[pl.ds(0, blk * SLABS), :], tile.at[tslot], gsems.at[tslot]).wait()

    def to_bf16(tslot, xslot):
        for j in range(SLABS):
            xb[xslot, :, j * LANES:(j + 1) * LANES] = tile[tslot, pl.ds(j, blk, stride=SLABS), :].astype(BF16)

    def scatter_row(yslot, slot, row0, mi):
        dst = pl.multiple_of(row0 * SLABS + slot, SLABS)
        pltpu.make_async_copy(ybuf.at[yslot, pl.ds(mi * SLABS, SLABS), :], ys_ref.at[pl.ds(dst, SLABS), :],
                              sems.at[yslot]).start()

    @pl.when(g == 0)
    def _():
        for c in weight_copies(be_ref[0], 0):
            c.start()
        ybuf[...] = jnp.zeros(ybuf.shape, F32)
        for bb in range(batch):
            for c in range(DUMMY_SLOTS // blk):
                start = (bb * bstride + 2 * seq + c * blk) * SLABS
                zero = pltpu.make_async_copy(ybuf.at[0], ys_ref.at[pl.ds(start, blk * SLABS), :], sems.at[0])
                zero.start()
                zero.wait()
        for mi in range(blk):
            gather_row(0, 0, mi)
        wait_gather(0)
        to_bf16(0, 0)
        for first in range(1, ring - 1):
            for mi in range(blk):
                gather_row(first, first, mi)

    run = run_ref[g]
    ws = run % 2

    @pl.when((g == 0) | (run != run_ref[jnp.maximum(g - 1, 0)]))
    def _():
        for c in weight_copies(be_ref[g], ws):
            c.wait()
        wgb_ref[...] = wbuf_g[ws].astype(BF16)
        wub_ref[...] = wbuf_u[ws].astype(BF16)
        wdb_ref[...] = wbuf_d[ws].astype(BF16)

        @pl.when(nxt_ref[g] >= 0)
        def _():
            for c in weight_copies(nxt_ref[g], 1 - ws):
                c.start()

    @pl.when(g >= ring - 1)
    def _():
        wait_block(ycur)

    prev_block = jnp.where(g == 0, batch * nb, g - 1)
    prev_row0 = jnp.where(n == 0, jnp.maximum(b - 1, 0), b) * bstride
    n_chunks = 8
    per = blk // n_chunks
    half = EXPERT_HIDDEN // 2
    quarter = D_MODEL // 4

    @pl.when(g < n_real)
    def _():
        x_cur, y_cur, y_prev = g % 2, ycur, yprev
        next_slot, fetch_slot = (g + 1) % ring, (g + ring - 1) % ring
        wait_gather(next_slot)
        for mi in range(blk):
            scatter_row(y_prev, slot_ref[prev_block * blk + mi], prev_row0, mi)
        to_bf16(next_slot, 1 - x_cur)
        x = xb[x_cur]
        acts = []
        hid = None
        for c in range(n_chunks):
            for mi in range(c * per, (c + 1) * per):
                gather_row(g + ring - 1, fetch_slot, mi)
            if c < 4:
                w_ref = wgb_ref if c < 2 else wub_ref
                acts.append(jnp.dot(x, w_ref[:, (c % 2) * half:(c % 2 + 1) * half], preferred_element_type=F32))
            if c == 3:
                a = jnp.concatenate(acts[0:2], axis=1)
                u = jnp.concatenate(acts[2:4], axis=1)
                hid = (a * jax.nn.sigmoid(a) * u).astype(BF16)
            if c >= 4:
                q = c - 4
                yq = jnp.dot(hid, wdb_ref[:, q * quarter:(q + 1) * quarter], preferred_element_type=F32)
                for jj in range(quarter // LANES):
                    j = q * (quarter // LANES) + jj
                    ybuf[y_cur, pl.ds(j, blk, stride=SLABS), :] = yq[:, jj * LANES:(jj + 1) * LANES]

    @pl.when(g >= n_real)
    def _():
        for mi in range(blk):
            scatter_row(yprev, slot_ref[prev_block * blk + mi], prev_row0, mi)

    @pl.when(g == last)
    def _():
        wait_block((last - 3) % ring)
        wait_block((last - 2) % ring)
        for mi in range(blk):
            scatter_row(last % ring, slot_ref[last * blk + mi], (batch - 1) * bstride, mi)
        wait_block((last - 1) % ring)
        wait_block(last % ring)
        for extra in range(1, ring - 1):
            wait_gather((n_real + extra) % ring)


def _experts(block_expert, n_real, row_slot, x2s, wg, wu, wd, layer, batch, seq):
    blk = MOE_BLOCK
    nb = (row_slot.shape[0] // blk - 1) // batch
    steps = batch * nb
    assert steps >= 4 and batch == 1
    idx = jnp.arange(steps, dtype=I32)
    change = jnp.concatenate([jnp.zeros((1,), I32), (block_expert[1:] != block_expert[:-1]).astype(I32)])
    run = jnp.cumsum(change).astype(I32)
    later_change = (idx[None, :] > idx[:, None]) & (change[None, :] > 0)
    nxt_idx = jnp.min(jnp.where(later_change, idx[None, :], steps), axis=1)
    nxt = jnp.where(nxt_idx < steps, block_expert[jnp.minimum(nxt_idx, steps - 1)], -1).astype(I32)
    hbm = pl.BlockSpec(memory_space=pl.ANY)
    grid_spec = pltpu.PrefetchScalarGridSpec(
        num_scalar_prefetch=5,
        grid=(batch, nb),
        in_specs=[hbm, hbm, hbm, hbm],
        out_specs=hbm,
        scratch_shapes=[
            pltpu.VMEM((2, D_MODEL, EXPERT_HIDDEN), F32), pltpu.VMEM((2, D_MODEL, EXPERT_HIDDEN), F32),
            pltpu.VMEM((2, EXPERT_HIDDEN, D_MODEL), F32),
            pltpu.VMEM((D_MODEL, EXPERT_HIDDEN), BF16), pltpu.VMEM((D_MODEL, EXPERT_HIDDEN), BF16),
            pltpu.VMEM((EXPERT_HIDDEN, D_MODEL), BF16),
            pltpu.VMEM((EXPERT_RING, blk * SLABS, LANES), F32),
            pltpu.VMEM((2, blk, D_MODEL), BF16),
            pltpu.VMEM((EXPERT_RING, blk * SLABS, LANES), F32),
            pltpu.SemaphoreType.DMA((EXPERT_RING,)),
            pltpu.SemaphoreType.DMA((2, 3)),
            pltpu.SemaphoreType.DMA((EXPERT_RING,)),
        ],
    )
    return pl.pallas_call(
        functools.partial(_expert_kernel, batch=batch, nb=nb, seq=seq, layer=layer),
        grid_spec=grid_spec,
        out_shape=jax.ShapeDtypeStruct((batch * (2 * seq + DUMMY_SLOTS) * SLABS, LANES), F32),
        compiler_params=_cparams(("arbitrary", "arbitrary")),
        name="moe_experts",
    )(block_expert, run, nxt, n_real, row_slot, x2s, wg, wu, wd)


def _combine_kernel(h_ref, gates_ref, y0_ref, y1_ref, *rest, final):
    if final:
        fg_ref, out_ref = rest
    else:
        (out_ref,) = rest
    tm = h_ref.shape[0]
    gates = gates_ref[...]
    g0, g1 = gates[:, 0:1], gates[:, 1:2]
    parts = []
    for j in range(SLABS):
        y0 = y0_ref[pl.ds(j, tm, stride=SLABS), :]
        y1 = y1_ref[pl.ds(j, tm, stride=SLABS), :]
        parts.append(g0 * y0 + g1 * y1)
    out = h_ref[...] + jnp.concatenate(parts, axis=1)
    if final:
        out = _rms(out, fg_ref[...])
    out_ref[...] = out


def _combine(h, gates, ys, batch, seq, final_gain=None):
    t = h.shape[0]
    tm = ROW_TILE
    tpb = seq // tm
    bstride = 2 * tpb + DUMMY_SLOTS // tm
    final = final_gain is not None
    in_specs = [pl.BlockSpec((tm, D_MODEL), lambda i: (i, 0)),
                pl.BlockSpec((tm, LANES), lambda i: (i, 0)),
                pl.BlockSpec((tm * SLABS, LANES), lambda i: (i // tpb * bstride + i % tpb, 0)),
                pl.BlockSpec((tm * SLABS, LANES), lambda i: (i // tpb * bstride + tpb + i % tpb, 0))]
    args = [h, gates, ys, ys]
    if final:
        in_specs.append(pl.BlockSpec((1, D_MODEL), lambda i: (0, 0)))
        args.append(final_gain.reshape(1, D_MODEL))
    return pl.pallas_call(
        functools.partial(_combine_kernel, final=final),
        grid=(t // tm,),
        in_specs=in_specs,
        out_specs=pl.BlockSpec((tm, D_MODEL), lambda i: (i, 0)),
        out_shape=jax.ShapeDtypeStruct((t, D_MODEL), F32),
        compiler_params=_cparams(("parallel",)),
        name="moe_combine_final" if final else "moe_combine",
    )(*args)


def _moe(h, gain, w_r1, b_r1, w_r2, b_r2, wg, wu, wd, layer, batch, final_gain=None):
    t = h.shape[0]
    seq = t // batch
    assert seq <= (1 << RANK_BITS) and seq % ROW_TILE == 0 and DUMMY_SLOTS % ROW_TILE == 0
    pad_w = jnp.zeros((D_MODEL, LANES - MOE_GROUPS - MOE_EXPERTS), F32)
    wr = jnp.concatenate([w_r1, w_r2, pad_w], axis=1)
    br = jnp.concatenate([b_r1, b_r2, jnp.zeros((LANES - MOE_GROUPS - MOE_EXPERTS,), F32)]).reshape(1, LANES)
    x2s, codes, gates, counts = _router(h, gain, wr, br, batch)
    cnt = counts.reshape(batch, 8, LANES)[:, 0, ROUTER_LANE0:ROUTER_LANE0 + MOE_EXPERTS].astype(I32)
    padded = (cnt + MOE_BLOCK - 1) // MOE_BLOCK * MOE_BLOCK
    pad_end = jnp.cumsum(padded, axis=1)
    rows_pb = seq * 2 + MOE_EXPERTS * MOE_BLOCK
    nb = rows_pb // MOE_BLOCK
    pad_start = pad_end - padded + (jnp.arange(batch, dtype=I32) * rows_pb)[:, None]
    block_start = jnp.arange(nb, dtype=I32) * MOE_BLOCK
    block_expert = jnp.minimum(
        jnp.sum((pad_end[:, None, :] <= block_start[None, :, None]).astype(I32), axis=2), MOE_EXPERTS - 1)
    dest = _dest_rows(pad_start.reshape(-1), codes)
    dest_flat = dest[0:2].reshape(2, batch, seq).transpose(1, 0, 2).reshape(-1)
    row_slot = _invert(dest_flat, batch * rows_pb, batch)
    n_real = (pad_end[:, -1] // MOE_BLOCK).astype(I32)
    ys = _experts(block_expert.reshape(-1), n_real, row_slot, x2s, wg, wu, wd, layer, batch, seq)
    return _combine(h, gates, ys, batch, seq, final_gain)


def kernel(x, mix_norm, ffn_norm, final_norm, even_w_in, even_w_out, conv_w, conv_b, conv_norm_g, conv_norm_b,
           odd_w_in, odd_w_out, router_w1, router_b1, router_w2, router_b2, expert_w_gate, expert_w_up,
           expert_w_down):
    batch, seq, d = x.shape
    assert d == D_MODEL and seq % (ATTN_BLOCK * max(dl for _, dl in A_BRANCHES)) == 0
    assert all(w // dl == ATTN_BLOCK for w, dl in A_BRANCHES)
    t = batch * seq
    h = x.reshape(t, d)

    qkv, conv_in = _even_in(h, mix_norm[0], even_w_in, _attn_rope_tables(seq), seq)
    attn = _attention(qkv, batch, seq)
    bconv = _conv(conv_in, conv_w[0], conv_b[0], conv_norm_g[0], conv_norm_b[0], batch, seq)
    h = _even_out(h, attn, bconv, even_w_out)
    h = _moe(h, ffn_norm[0], router_w1[0], router_b1[0], router_w2[0], router_b2[0],
             expert_w_gate, expert_w_up, expert_w_down, 0, 1)

    inv_freq = RET_ROT_THETA ** (-jnp.linspace(0.0, 1.0, C_QK_DIM // 2, dtype=F32))
    ang = jnp.arange(seq, dtype=F32)[:, None] * inv_freq[None, :]
    proj = _odd_in(h, mix_norm[1], odd_w_in, jnp.cos(ang), jnp.sin(ang), seq)
    log_decay = jnp.log(1.0 - jnp.exp2(-5.0 - jnp.arange(C_HEADS, dtype=F32)))
    y = _retention(proj, log_decay, batch, seq)
    h = _odd_out(h, y, odd_w_out)
    out = _moe(h, ffn_norm[1], router_w1[1], router_b1[1], router_w2[1], router_b2[1],
               expert_w_gate, expert_w_up, expert_w_down, 1, 1, final_gain=final_norm)
    return out.reshape(batch, seq, d)
```

```python
import functools

import jax
import jax.numpy as jnp
from jax import lax
from jax.experimental import pallas as pl
from jax.experimental.pallas import tpu as pltpu

F32 = jnp.float32
BF16 = jnp.bfloat16
I32 = jnp.int32

NORM_EPS = 1e-6
NEG_INF = -1e30

D_MODEL = 1024
A_HEADS = 8
A_HEAD_DIM = 64
A_WIDTH = A_HEADS * A_HEAD_DIM
A_BRANCHES = ((128, 1), (512, 4), (2048, 16))
ATTN_BLOCK = 128
ROPE_THETA = 500000.0
ROPE_DIM = A_HEAD_DIM // 4
B_WIDTH = D_MODEL - A_WIDTH
CONV_WIDTH = 31
C_HEADS = 4
C_QK_DIM = 256
C_V_DIM = 512
C_QK_WIDTH = C_HEADS * C_QK_DIM
C_V_WIDTH = C_HEADS * C_V_DIM
RET_CHUNK = 128
RET_ROT_THETA = 10000.0
MOE_GROUPS = 4
MOE_EXPERTS_PER_GROUP = 8
MOE_EXPERTS = MOE_GROUPS * MOE_EXPERTS_PER_GROUP
EXPERT_HIDDEN = 512
MOE_BLOCK = 128
EVEN_IN_WIDTH = 3 * A_WIDTH + 2 * B_WIDTH
ODD_IN_WIDTH = 2 * C_QK_WIDTH + 2 * C_V_WIDTH

LANES = 128
ROW_TILE = 512
WIDE_ROW_TILE = 1024
CONV_TILE = 512
CONV_HALO = 32
CONV_CHUNK = 64
ROUTER_LANE0 = MOE_GROUPS
VMEM_LIMIT = 56 * 1024 * 1024
SLABS = D_MODEL // LANES
RANK_BITS = 16
EXPERT_RING = 4
DUMMY_SLOTS = 2 * ROW_TILE


def _cparams(sem, vmem=VMEM_LIMIT):
    return pltpu.CompilerParams(dimension_semantics=sem, vmem_limit_bytes=vmem)


def _cast_once(w_ref, wb_ref, first):
    @pl.when(first)
    def _():
        wb_ref[...] = w_ref[...].astype(BF16)


def _rms(x, gain):
    ms = jnp.mean(x * x, axis=-1, keepdims=True)
    return x * lax.rsqrt(ms + NORM_EPS) * gain


def _even_in_kernel(h_ref, g_ref, w_ref, c_ref, s1_ref, s2_ref, qkv_ref, conv_ref, wb_ref):
    _cast_once(w_ref, wb_ref, pl.program_id(0) == 0)
    u = _rms(h_ref[...], g_ref[...]).astype(BF16)
    acc = jnp.dot(u, wb_ref[...], preferred_element_type=F32)
    c, s1, s2 = c_ref[...], s1_ref[...], s2_ref[...]
    for j in range(2 * A_WIDTH // LANES):
        xg = acc[:, j * LANES:(j + 1) * LANES]
        if j < A_WIDTH // LANES:
            xg = xg * (A_HEAD_DIM ** -0.5)
        qkv_ref[j] = xg * c + pltpu.roll(xg, LANES - ROPE_DIM // 2, 1) * s1 + pltpu.roll(xg, ROPE_DIM // 2, 1) * s2
    for j in range(2 * A_WIDTH // LANES, 3 * A_WIDTH // LANES):
        qkv_ref[j] = acc[:, j * LANES:(j + 1) * LANES]
    conv_ref[...] = acc[:, 3 * A_WIDTH:].astype(BF16)


def _even_in(h, gain, w, tabs, seq):
    t = h.shape[0]
    tm = ROW_TILE
    nseq = seq // tm
    tab_spec = pl.BlockSpec((tm, LANES), lambda i: (i % nseq, 0))
    return pl.pallas_call(
        _even_in_kernel,
        grid=(t // tm,),
        in_specs=[
            pl.BlockSpec((tm, D_MODEL), lambda i: (i, 0)),
            pl.BlockSpec((1, D_MODEL), lambda i: (0, 0)),
            pl.BlockSpec((None, D_MODEL, EVEN_IN_WIDTH), lambda i: (0, 0, 0)),
            tab_spec, tab_spec, tab_spec,
        ],
        out_specs=[
            pl.BlockSpec((3 * A_WIDTH // LANES, tm, LANES), lambda i: (0, i, 0)),
            pl.BlockSpec((tm, 2 * B_WIDTH), lambda i: (i, 0)),
        ],
        out_shape=[
            jax.ShapeDtypeStruct((3 * A_WIDTH // LANES, t, LANES), F32),
            jax.ShapeDtypeStruct((t, 2 * B_WIDTH), BF16),
        ],
        scratch_shapes=[pltpu.VMEM((D_MODEL, EVEN_IN_WIDTH), BF16)],
        compiler_params=_cparams(("arbitrary",)),
        name="even_in_proj",
    )(h, gain.reshape(1, D_MODEL), w, *tabs)


def _attn_rope_tables(seq):
    half = ROPE_DIM // 2
    inv_freq = ROPE_THETA ** (-jnp.arange(0, ROPE_DIM, 2, dtype=F32) / ROPE_DIM)
    ang = jnp.arange(seq, dtype=F32)[:, None] * inv_freq[None, :]
    cos, sin = jnp.cos(ang), jnp.sin(ang)
    rest = A_HEAD_DIM - ROPE_DIM
    ones = jnp.ones((seq, rest), F32)
    z_rest = jnp.zeros((seq, rest), F32)
    z_half = jnp.zeros((seq, half), F32)
    c = jnp.concatenate([cos, cos, ones], axis=1)
    s1 = jnp.concatenate([-sin, z_half, z_rest], axis=1)
    s2 = jnp.concatenate([z_half, sin, z_rest], axis=1)
    rep = LANES // A_HEAD_DIM
    return tuple(jnp.tile(a, (1, rep)) for a in (c, s1, s2))


ATTN_STEP = ATTN_BLOCK * max(d for _, d in A_BRANCHES)
ATTN_PAIRS = A_WIDTH // LANES
ATTN_MERGE_ROWS = 256


def _attn_kernel(q_ref, k_ref, v_ref, o_ref, kprev, vprev, oml):
    n = pl.program_id(1)
    blk = ATTN_BLOCK
    n_blocks = ATTN_STEP // blk

    @pl.when(n == 0)
    def _():
        kprev[...] = jnp.zeros(kprev.shape, BF16)
        vprev[...] = jnp.zeros(vprev.shape, BF16)

    qi = lax.broadcasted_iota(I32, (blk, 2 * blk), 0)
    kj = lax.broadcasted_iota(I32, (blk, 2 * blk), 1)
    dist = blk + qi - kj
    band = (dist >= 0) & (dist <= blk)
    in_cur = kj >= blk
    lane = lax.broadcasted_iota(I32, (blk, LANES), 1)
    lo = lane < A_HEAD_DIM

    off = 0
    for bi, (_, dil) in enumerate(A_BRANCHES):
        shift = dil.bit_length() - 1
        assert dil == 1 << shift

        def one_block(idx, carried, bi=bi, dil=dil, shift=shift, off=off):
            u = lax.shift_right_logical(idx, shift)
            r = idx & (dil - 1)
            start = u * (blk * dil) + r
            rows = pl.ds(pl.multiple_of(start, blk), blk) if dil == 1 else pl.ds(start, blk, stride=dil)
            has_prev = (n > 0) | (u > 0)
            valid = band & (in_cur | has_prev)
            valid2 = jnp.concatenate([valid, valid], axis=0)
            current = []
            for g in range(ATTN_PAIRS):
                q = q_ref[g, rows, :]
                kc = k_ref[g, rows, :].astype(BF16)
                vc = v_ref[g, rows, :].astype(BF16)
                current.append((kc, vc))
                kp, vp = (kprev[g, off + r], vprev[g, off + r]) if carried is None else carried[g]
                q2 = jnp.concatenate([jnp.where(lo, q, 0.0), jnp.where(lo, 0.0, q)], axis=0).astype(BF16)
                kk = jnp.concatenate([kp, kc], axis=0)
                vv = jnp.concatenate([vp, vc], axis=0)
                s = lax.dot_general(q2, kk, (((1,), (1,)), ((), ())), preferred_element_type=F32)
                s = jnp.where(valid2, s, NEG_INF)
                m = jnp.max(s, axis=-1, keepdims=True)
                e = jnp.exp(s - m)
                den = jnp.sum(e, axis=-1, keepdims=True)
                pv = jnp.dot(e.astype(BF16), vv, preferred_element_type=F32)
                o_new = jnp.where(lo, pv[:blk], pv[blk:])
                m_new = jnp.where(lo, m[:blk], m[blk:])
                l_new = jnp.where(lo, den[:blk], den[blk:])
                if bi > 0:
                    o_run, m_run, l_run = oml[0, g, rows, :], oml[1, g, rows, :], oml[2, g, rows, :]
                    m_both = jnp.maximum(m_run, m_new)
                    w_run, w_new = jnp.exp(m_run - m_both), jnp.exp(m_new - m_both)
                    o_new = w_run * o_run + w_new * o_new
                    l_new = w_run * l_run + w_new * l_new
                    m_new = m_both
                oml[0, g, rows, :] = o_new
                oml[1, g, rows, :] = m_new
                oml[2, g, rows, :] = l_new
            return r, current

        def save_prev(r, current, off=off):
            for g, (kc, vc) in enumerate(current):
                kprev[g, off + r] = kc
                vprev[g, off + r] = vc

        def block_body(i, c, dil=dil):
            r0, cur0 = one_block(2 * i, None)
            if dil == 1:
                r1, cur1 = one_block(2 * i + 1, cur0)
            else:
                save_prev(r0, cur0)
                r1, cur1 = one_block(2 * i + 1, None)
            save_prev(r1, cur1)
            return c

        lax.fori_loop(0, n_blocks // 2, block_body, 0)
        off += dil
    for g in range(ATTN_PAIRS):
        for c in range(ATTN_STEP // ATTN_MERGE_ROWS):
            sl = slice(c * ATTN_MERGE_ROWS, (c + 1) * ATTN_MERGE_ROWS)
            o_ref[g, sl, :] = (oml[0, g, sl, :] / oml[2, g, sl, :]).astype(BF16)


def _attention(qkv, batch, seq):
    t = qkv.shape[1]
    steps = seq // ATTN_STEP
    n_res = sum(d for _, d in A_BRANCHES)

    def slabs(which):
        return pl.BlockSpec((ATTN_PAIRS, ATTN_STEP, LANES), lambda b, n: (which, b * steps + n, 0))

    return pl.pallas_call(
        _attn_kernel,
        grid=(batch, steps),
        in_specs=[slabs(0), slabs(1), slabs(2)],
        out_specs=pl.BlockSpec((ATTN_PAIRS, ATTN_STEP, LANES), lambda b, n: (0, b * steps + n, 0)),
        out_shape=jax.ShapeDtypeStruct((ATTN_PAIRS, t, LANES), BF16),
        scratch_shapes=[
            pltpu.VMEM((ATTN_PAIRS, n_res, ATTN_BLOCK, LANES), BF16),
            pltpu.VMEM((ATTN_PAIRS, n_res, ATTN_BLOCK, LANES), BF16),
            pltpu.VMEM((3, ATTN_PAIRS, ATTN_STEP, LANES), F32),
        ],
        compiler_params=_cparams(("parallel", "arbitrary")),
        name="dilated_attention",
    )(qkv, qkv, qkv)


def _conv_kernel(val_ref, gate_ref, w_ref, b_ref, g_ref, beta_ref, o_ref, abuf, shifted):
    n = pl.program_id(1)
    tc = CONV_TILE
    sub = 8

    @pl.when(n == 0)
    def _():
        abuf[0:CONV_HALO, :] = jnp.zeros((CONV_HALO, B_WIDTH), F32)

    @pl.when(n > 0)
    def _():
        abuf[0:CONV_HALO, :] = abuf[tc:tc + CONV_HALO, :]

    val = val_ref[...].astype(F32)
    gate = gate_ref[...].astype(F32)
    abuf[CONV_HALO:CONV_HALO + tc, :] = val * jax.nn.sigmoid(gate)
    off = CONV_HALO - (CONV_WIDTH - 1)
    span = shifted.shape[1]
    for s in range(1, sub):
        shifted[s - 1] = abuf[s:s + span, :]
    for c in range(tc // CONV_CHUNK):
        acc = jnp.broadcast_to(b_ref[...], (CONV_CHUNK, B_WIDTH))
        for j in range(CONV_WIDTH):
            s = (off + j) % sub
            r0 = c * CONV_CHUNK + off + j - s
            src = abuf[r0:r0 + CONV_CHUNK, :] if s == 0 else shifted[s - 1, r0:r0 + CONV_CHUNK, :]
            acc = acc + w_ref[j:j + 1, :] * src
        mu = jnp.mean(acc, axis=-1, keepdims=True)
        cen = acc - mu
        var = jnp.mean(cen * cen, axis=-1, keepdims=True)
        yn = cen * lax.rsqrt(var + NORM_EPS) * g_ref[...] + beta_ref[...]
        o_ref[c * CONV_CHUNK:(c + 1) * CONV_CHUNK, :] = (yn * jax.nn.sigmoid(yn)).astype(BF16)


def _conv(conv_in, w, b, g, beta, batch, seq):
    t = conv_in.shape[0]
    tc = CONV_TILE
    nt = seq // tc
    w_pad = jnp.concatenate([w, jnp.zeros((CONV_HALO - CONV_WIDTH, B_WIDTH), F32)], axis=0)
    vec = pl.BlockSpec((1, B_WIDTH), lambda bb, n: (0, 0))
    return pl.pallas_call(
        _conv_kernel,
        grid=(batch, nt),
        in_specs=[
            pl.BlockSpec((tc, B_WIDTH), lambda bb, n: (bb * nt + n, 0)),
            pl.BlockSpec((tc, B_WIDTH), lambda bb, n: (bb * nt + n, 1)),
            pl.BlockSpec((CONV_HALO, B_WIDTH), lambda bb, n: (0, 0)),
            vec, vec, vec,
        ],
        out_specs=pl.BlockSpec((tc, B_WIDTH), lambda bb, n: (bb * nt + n, 0)),
        out_shape=jax.ShapeDtypeStruct((t, B_WIDTH), BF16),
        scratch_shapes=[pltpu.VMEM((tc + CONV_HALO, B_WIDTH), F32),
                        pltpu.VMEM((7, tc + CONV_HALO - 8, B_WIDTH), F32)],
        compiler_params=_cparams(("parallel", "arbitrary")),
        name="conformer_conv",
    )(conv_in, conv_in, w_pad, b.reshape(1, -1), g.reshape(1, -1), beta.reshape(1, -1))


def _even_out_kernel(h_ref, a_ref, bc_ref, w_ref, out_ref, wb_ref):
    _cast_once(w_ref, wb_ref, pl.program_id(0) == 0)
    a = jnp.concatenate([a_ref[g] for g in range(ATTN_PAIRS)], axis=1)
    acc = jnp.dot(a, wb_ref[0:A_WIDTH, :], preferred_element_type=F32)
    acc = acc + jnp.dot(bc_ref[...], wb_ref[A_WIDTH:, :], preferred_element_type=F32)
    out_ref[...] = h_ref[...] + acc


def _even_out(h, attn, bconv, w):
    t = h.shape[0]
    tm = WIDE_ROW_TILE
    row = lambda width: pl.BlockSpec((tm, width), lambda i: (i, 0))
    return pl.pallas_call(
        _even_out_kernel,
        grid=(t // tm,),
        in_specs=[row(D_MODEL), pl.BlockSpec((ATTN_PAIRS, tm, LANES), lambda i: (0, i, 0)), row(B_WIDTH),
                  pl.BlockSpec((None, D_MODEL, D_MODEL), lambda i: (0, 0, 0), pipeline_mode=pl.Buffered(1))],
        out_specs=row(D_MODEL),
        out_shape=jax.ShapeDtypeStruct((t, D_MODEL), F32),
        scratch_shapes=[pltpu.VMEM((D_MODEL, D_MODEL), BF16)],
        compiler_params=_cparams(("arbitrary",)),
        name="even_out_proj",
    )(h, attn, bconv, w)


def _even_tail_kernel(h_ref, a_ref, val_ref, gate_ref, cw_ref, cb_ref, cg_ref, cbeta_ref, w_ref, out_ref,
                      wb_ref, abuf, shifted, bc):
    _cast_once(w_ref, wb_ref, (pl.program_id(0) == 0) & (pl.program_id(1) == 0))
    a = jnp.concatenate([a_ref[g] for g in range(ATTN_PAIRS)], axis=1)
    acc = jnp.dot(a, wb_ref[0:A_WIDTH, :], preferred_element_type=F32)
    _conv_kernel(val_ref, gate_ref, cw_ref, cb_ref, cg_ref, cbeta_ref, bc, abuf, shifted)
    acc = acc + jnp.dot(bc[...], wb_ref[A_WIDTH:, :], preferred_element_type=F32)
    out_ref[...] = h_ref[...] + acc


def _even_tail(h, attn, conv_in, cw, cb, cg, cbeta, w, batch, seq):
    t = h.shape[0]
    tc = CONV_TILE
    nt = seq // tc
    w_pad = jnp.concatenate([cw, jnp.zeros((CONV_HALO - CONV_WIDTH, B_WIDTH), F32)], axis=0)
    vec = pl.BlockSpec((1, B_WIDTH), lambda bb, n: (0, 0))
    row = lambda width: pl.BlockSpec((tc, width), lambda bb, n: (bb * nt + n, 0))
    return pl.pallas_call(
        _even_tail_kernel,
        grid=(batch, nt),
        in_specs=[
            row(D_MODEL),
            pl.BlockSpec((ATTN_PAIRS, tc, LANES), lambda bb, n: (0, bb * nt + n, 0)),
            pl.BlockSpec((tc, B_WIDTH), lambda bb, n: (bb * nt + n, 0)),
            pl.BlockSpec((tc, B_WIDTH), lambda bb, n: (bb * nt + n, 1)),
            pl.BlockSpec((CONV_HALO, B_WIDTH), lambda bb, n: (0, 0)),
            vec, vec, vec,
            pl.BlockSpec((None, D_MODEL, D_MODEL), lambda bb, n: (0, 0, 0), pipeline_mode=pl.Buffered(1)),
        ],
        out_specs=row(D_MODEL),
        out_shape=jax.ShapeDtypeStruct((t, D_MODEL), F32),
        scratch_shapes=[pltpu.VMEM((D_MODEL, D_MODEL), BF16),
                        pltpu.VMEM((tc + CONV_HALO, B_WIDTH), F32),
                        pltpu.VMEM((7, tc + CONV_HALO - 8, B_WIDTH), F32),
                        pltpu.VMEM((tc, B_WIDTH), BF16)],
        compiler_params=_cparams(("arbitrary", "arbitrary")),
        name="even_conv_out_proj",
    )(h, attn, conv_in, conv_in, w_pad, cb.reshape(1, -1), cg.reshape(1, -1), cbeta.reshape(1, -1), w)


def _odd_in_kernel(h_ref, g_ref, w_ref, cos_ref, sin_ref, o_ref, wb_ref):
    j = pl.program_id(0)
    _cast_once(w_ref, wb_ref, pl.program_id(1) == 0)
    u = _rms(h_ref[...], g_ref[...]).astype(BF16)
    acc = jnp.dot(u, wb_ref[...], preferred_element_type=F32)

    @pl.when(j == 0)
    def _():
        cos, sin = cos_ref[...], sin_ref[...]
        half = C_QK_DIM // 2
        for hd in range(2 * C_HEADS):
            x1 = acc[:, hd * C_QK_DIM:hd * C_QK_DIM + half]
            x2 = acc[:, hd * C_QK_DIM + half:(hd + 1) * C_QK_DIM]
            r1 = x1 * cos - x2 * sin
            r2 = x2 * cos + x1 * sin
            if hd >= C_HEADS:
                r1 = r1 * (C_QK_DIM ** -0.5)
                r2 = r2 * (C_QK_DIM ** -0.5)
            o_ref[:, hd * C_QK_DIM:hd * C_QK_DIM + half] = r1.astype(BF16)
            o_ref[:, hd * C_QK_DIM + half:(hd + 1) * C_QK_DIM] = r2.astype(BF16)

    @pl.when(j > 0)
    def _():
        o_ref[...] = acc.astype(BF16)


def _odd_in(h, gain, w, cos, sin, seq):
    t = h.shape[0]
    tm = WIDE_ROW_TILE
    tn = 2 * C_QK_WIDTH
    nseq = seq // tm
    tab = pl.BlockSpec((tm, C_QK_DIM // 2), lambda j, i: (i % nseq, 0))
    return pl.pallas_call(
        _odd_in_kernel,
        grid=(ODD_IN_WIDTH // tn, t // tm),
        in_specs=[
            pl.BlockSpec((tm, D_MODEL), lambda j, i: (i, 0)),
            pl.BlockSpec((1, D_MODEL), lambda j, i: (0, 0)),
            pl.BlockSpec((None, D_MODEL, tn), lambda j, i: (0, 0, j), pipeline_mode=pl.Buffered(1)),
            tab, tab,
        ],
        out_specs=pl.BlockSpec((tm, tn), lambda j, i: (i, j)),
        out_shape=jax.ShapeDtypeStruct((t, ODD_IN_WIDTH), BF16),
        scratch_shapes=[pltpu.VMEM((D_MODEL, tn), BF16)],
        compiler_params=_cparams(("arbitrary", "arbitrary")),
        name="odd_in_proj",
    )(h, gain.reshape(1, D_MODEL), w, cos, sin)


def _ret_kernel(ld_ref, q_ref, k_ref, v_ref, g_ref, o_ref, state):
    c = pl.program_id(0)
    ch = RET_CHUNK
    batch = q_ref.shape[0]

    @pl.when(c == 0)
    def _():
        state[...] = jnp.zeros(state.shape, F32)

    ii = lax.broadcasted_iota(I32, (ch, ch), 0)
    jj = lax.broadcasted_iota(I32, (ch, ch), 1)
    diff = (ii - jj).astype(F32)
    pos = lax.broadcasted_iota(I32, (ch, 1), 0).astype(F32)
    for hd in range(C_HEADS):
        ld = ld_ref[hd]
        intra = jnp.where(diff >= 0, jnp.exp(ld * jnp.maximum(diff, 0.0)), 0.0)
        q_decay = jnp.exp(ld * (pos + 1.0))
        k_decay = jnp.exp(ld * (ch - 1.0 - pos))
        chunk_decay = jnp.exp(ld * jnp.full((1, 1), float(ch), F32))
        for bb in range(batch):
            q = q_ref[bb, :, hd * C_QK_DIM:(hd + 1) * C_QK_DIM]
            k = k_ref[bb, :, hd * C_QK_DIM:(hd + 1) * C_QK_DIM]
            v = v_ref[bb, :, hd * C_V_DIM:(hd + 1) * C_V_DIM]
            s = lax.dot_general(q, k, (((1,), (1,)), ((), ())), preferred_element_type=F32) * intra
            inner = jnp.dot(s.astype(BF16), v, preferred_element_type=F32)
            st = state[bb, hd]
            cross = jnp.dot(q, st.astype(BF16), preferred_element_type=F32) * q_decay
            kd_t = jnp.transpose(k.astype(F32) * k_decay).astype(BF16)
            state[bb, hd] = st * chunk_decay + jnp.dot(kd_t, v, preferred_element_type=F32)
            out = inner + cross
            mu = jnp.mean(out, axis=-1, keepdims=True)
            cen = out - mu
            var = jnp.mean(cen * cen, axis=-1, keepdims=True)
            o = cen * lax.rsqrt(var + NORM_EPS)
            gf = g_ref[bb, :, hd * C_V_DIM:(hd + 1) * C_V_DIM].astype(F32)
            o_ref[bb, :, hd * C_V_DIM:(hd + 1) * C_V_DIM] = (gf * jax.nn.sigmoid(gf) * o).astype(BF16)


def _retention(proj, log_decay, batch, seq):
    ch = RET_CHUNK
    nc = seq // ch
    proj3 = proj.reshape(batch, seq, ODD_IN_WIDTH)
    v0 = 2 * C_QK_WIDTH // C_V_WIDTH
    grid_spec = pltpu.PrefetchScalarGridSpec(
        num_scalar_prefetch=1,
        grid=(nc,),
        in_specs=[
            pl.BlockSpec((batch, ch, C_QK_WIDTH), lambda c, ld: (0, c, 0)),
            pl.BlockSpec((batch, ch, C_QK_WIDTH), lambda c, ld: (0, c, 1)),
            pl.BlockSpec((batch, ch, C_V_WIDTH), lambda c, ld: (0, c, v0)),
            pl.BlockSpec((batch, ch, C_V_WIDTH), lambda c, ld: (0, c, v0 + 1)),
        ],
        out_specs=pl.BlockSpec((batch, ch, C_V_WIDTH), lambda c, ld: (0, c, 0)),
        scratch_shapes=[pltpu.VMEM((batch, C_HEADS, C_QK_DIM, C_V_DIM), F32)],
    )
    y = pl.pallas_call(
        _ret_kernel,
        grid_spec=grid_spec,
        out_shape=jax.ShapeDtypeStruct((batch, seq, C_V_WIDTH), BF16),
        compiler_params=_cparams(("arbitrary",)),
        name="retention",
    )(log_decay, proj3, proj3, proj3, proj3)
    return y.reshape(batch * seq, C_V_WIDTH)


def _odd_out_kernel(h_ref, y_ref, w_ref, out_ref, wb_ref):
    _cast_once(w_ref, wb_ref, pl.program_id(0) == 0)
    out_ref[...] = h_ref[...] + jnp.dot(y_ref[...], wb_ref[...], preferred_element_type=F32)


def _odd_out(h, y, w):
    t = h.shape[0]
    tm = WIDE_ROW_TILE
    return pl.pallas_call(
        _odd_out_kernel,
        grid=(t // tm,),
        in_specs=[pl.BlockSpec((tm, D_MODEL), lambda i: (i, 0)),
                  pl.BlockSpec((tm, C_V_WIDTH), lambda i: (i, 0)),
                  pl.BlockSpec((None, C_V_WIDTH, D_MODEL), lambda i: (0, 0, 0), pipeline_mode=pl.Buffered(1))],
        out_specs=pl.BlockSpec((tm, D_MODEL), lambda i: (i, 0)),
        out_shape=jax.ShapeDtypeStruct((t, D_MODEL), F32),
        scratch_shapes=[pltpu.VMEM((C_V_WIDTH, D_MODEL), BF16)],
        compiler_params=_cparams(("arbitrary",)),
        name="odd_out_proj",
    )(h, y, w)


def _router_kernel(h_ref, g_ref, wr_ref, br_ref, x2s_ref, codes_ref, gates_ref, cnt_ref, base_ref, *,
                   tiles_per_batch):
    i = pl.program_id(0)
    tm = h_ref.shape[0]
    batch = i // tiles_per_batch

    @pl.when(i % tiles_per_batch == 0)
    def _():
        base_ref[...] = jnp.zeros(base_ref.shape, F32)

    x2 = _rms(h_ref[...], g_ref[...])
    for j in range(SLABS):
        x2s_ref[pl.ds(j, tm, stride=SLABS), :] = x2[:, j * LANES:(j + 1) * LANES]
    wr = wr_ref[...]
    x_hi, w_hi = x2.astype(BF16), wr.astype(BF16)
    x_lo = (x2 - x_hi.astype(F32)).astype(BF16)
    w_lo = (wr - w_hi.astype(F32)).astype(BF16)
    both = jnp.dot(x_hi, jnp.concatenate([w_hi, w_lo], axis=1), preferred_element_type=F32)
    logits = both[:, :LANES] + both[:, LANES:] + jnp.dot(x_lo, w_hi, preferred_element_type=F32) + br_ref[...]
    lane = lax.broadcasted_iota(I32, (tm, LANES), 1)
    is_grp = lane < MOE_GROUPS
    lg = jnp.where(is_grp, logits, -jnp.inf)
    gmax = jnp.max(lg, axis=-1, keepdims=True)
    gsum = jnp.sum(jnp.where(is_grp, jnp.exp(logits - gmax), 0.0), axis=-1, keepdims=True)
    gp = 1.0 / gsum
    gi = jnp.min(jnp.where(lg == gmax, lane, LANES), axis=-1, keepdims=True)
    ex = lane - ROUTER_LANE0
    in_grp = (ex >= 0) & (ex < MOE_EXPERTS) & ((ex // MOE_EXPERTS_PER_GROUP) == gi)
    sel = jnp.where(in_grp, logits, -jnp.inf)
    v1 = jnp.max(sel, axis=-1, keepdims=True)
    i1 = jnp.min(jnp.where(sel == v1, lane, LANES), axis=-1, keepdims=True)
    sel2 = jnp.where(lane == i1, -jnp.inf, sel)
    v2 = jnp.max(sel2, axis=-1, keepdims=True)
    i2 = jnp.min(jnp.where(sel2 == v2, lane, LANES), axis=-1, keepdims=True)
    tt = jnp.exp(v2 - v1)
    g0 = gp / (1.0 + tt)
    g1 = gp * tt / (1.0 + tt)
    oh0 = (lane == i1).astype(F32)
    oh1 = (lane == i2).astype(F32)
    oh = oh0 + oh1
    rr = lax.broadcasted_iota(I32, (tm, tm), 0)
    cc = lax.broadcasted_iota(I32, (tm, tm), 1)
    tri = (cc < rr).astype(BF16)
    prefix = jnp.dot(tri, oh.astype(BF16), preferred_element_type=F32)
    tot = base_ref[...] + prefix
    rank0 = jnp.sum(oh0 * tot, axis=-1, keepdims=True).astype(I32)
    rank1 = jnp.sum(oh1 * tot, axis=-1, keepdims=True).astype(I32)
    base_ref[...] = base_ref[...] + jnp.sum(oh, axis=0, keepdims=True)
    eoff = batch * MOE_EXPERTS - ROUTER_LANE0
    code0 = (i1 + eoff) * (1 << RANK_BITS) + rank0
    code1 = (i2 + eoff) * (1 << RANK_BITS) + rank1
    meta = jnp.where(lane == 0, code0, jnp.where(lane == 1, code1, 0))
    codes_ref[...] = jnp.transpose(meta)[0:8, :]
    gates_ref[...] = jnp.where(lane == 0, g0, jnp.where(lane == 1, g1, 0.0))
    cnt_ref[...] = jnp.broadcast_to(base_ref[...], cnt_ref.shape)


def _router(h, gain, wr, br, batch):
    t = h.shape[0]
    tm = ROW_TILE
    tpb = t // batch // tm
    row = lambda width: pl.BlockSpec((tm, width), lambda i: (i, 0))
    fixed = lambda shape: pl.BlockSpec(shape, lambda i: (0, 0))
    return pl.pallas_call(
        functools.partial(_router_kernel, tiles_per_batch=tpb),
        grid=(t // tm,),
        in_specs=[row(D_MODEL), fixed((1, D_MODEL)), fixed((D_MODEL, LANES)), fixed((1, LANES))],
        out_specs=[pl.BlockSpec((tm * SLABS, LANES), lambda i: (i, 0)),
                   pl.BlockSpec((8, tm), lambda i: (0, i)), row(LANES),
                   pl.BlockSpec((8, LANES), lambda i: (i // tpb, 0))],
        out_shape=[
            jax.ShapeDtypeStruct((t * SLABS, LANES), F32),
            jax.ShapeDtypeStruct((8, t), I32),
            jax.ShapeDtypeStruct((t, LANES), F32),
            jax.ShapeDtypeStruct((batch * 8, LANES), F32),
        ],
        scratch_shapes=[pltpu.VMEM((1, LANES), F32)],
        compiler_params=_cparams(("arbitrary",)),
        name="moe_router",
    )(h, gain.reshape(1, D_MODEL), wr, br)


INVERT_UNROLL = 32


def _dest_kernel(pstart_ref, codes_ref, dest_ref):
    codes = codes_ref[...]
    seg = lax.shift_right_logical(codes, RANK_BITS)
    dest = codes & ((1 << RANK_BITS) - 1)
    for i in range(pstart_ref.shape[0]):
        dest = dest + jnp.where(seg == i, pstart_ref[i], 0)
    dest_ref[...] = dest


def _dest_rows(pad_start, codes):
    grid_spec = pltpu.PrefetchScalarGridSpec(
        num_scalar_prefetch=1,
        grid=(1,),
        in_specs=[pl.BlockSpec(codes.shape, lambda i, ps: (0, 0))],
        out_specs=pl.BlockSpec(codes.shape, lambda i, ps: (0, 0)),
    )
    return pl.pallas_call(
        _dest_kernel,
        grid_spec=grid_spec,
        out_shape=jax.ShapeDtypeStruct(codes.shape, I32),
        compiler_params=_cparams(("arbitrary",)),
        name="moe_dest_rows",
    )(pad_start, codes)


def _invert_kernel(trips_ref, dest_ref, dummy_ref, slot_ref, *, batch):
    per_batch = dest_ref.shape[0] // batch
    pltpu.sync_copy(dummy_ref, slot_ref)
    for b in range(batch):
        def place(i, carry, b=b):
            for v in range(INVERT_UNROLL):
                a = i * INVERT_UNROLL + v
                slot_ref[dest_ref[b * per_batch + a]] = a * SLABS
            return carry

        lax.fori_loop(0, trips_ref[0], place, 0)


def _dummy_slots(n_rows, per_batch):
    assert MOE_BLOCK & (MOE_BLOCK - 1) == 0 and (EXPERT_RING + 1) * MOE_BLOCK <= DUMMY_SLOTS
    r = jnp.arange(n_rows + MOE_BLOCK, dtype=I32)
    pattern = per_batch + (r // MOE_BLOCK) % EXPERT_RING * MOE_BLOCK + r % MOE_BLOCK
    return jnp.where(r < n_rows, pattern, per_batch + EXPERT_RING * MOE_BLOCK + r % MOE_BLOCK) * SLABS


def _invert(dest_flat, n_rows, batch):
    smem = pl.BlockSpec(memory_space=pltpu.SMEM)
    per_batch = dest_flat.shape[0] // batch
    assert per_batch % INVERT_UNROLL == 0
    trips = jnp.full((1,), per_batch // INVERT_UNROLL, I32)
    return pl.pallas_call(
        functools.partial(_invert_kernel, batch=batch),
        in_specs=[smem, smem, pl.BlockSpec(memory_space=pl.ANY)],
        out_specs=smem,
        out_shape=jax.ShapeDtypeStruct((n_rows + MOE_BLOCK,), I32),
        name="moe_invert_rows",
    )(trips, dest_flat, _dummy_slots(n_rows, per_batch))


def _expert_kernel(be_ref, run_ref, nxt_ref, nreal_ref, slot_ref, x2s_ref, wg_hbm, wu_hbm, wd_hbm, ys_ref,
                   wbuf_g, wbuf_u, wbuf_d, wgb_ref, wub_ref, wdb_ref, tile, xb, ybuf, sems, wsems, gsems, *,
                   batch, nb, seq, layer):
    b = pl.program_id(0)
    n = pl.program_id(1)
    g = b * nb + n
    last = batch * nb - 1
    blk = MOE_BLOCK
    bstride = 2 * seq + DUMMY_SLOTS
    ring = EXPERT_RING
    ycur = g % ring
    yprev = (g + ring - 1) % ring
    n_real = nreal_ref[0]

    def weight_copies(e, ws):
        pairs = ((wg_hbm, wbuf_g), (wu_hbm, wbuf_u), (wd_hbm, wbuf_d))
        return [pltpu.make_async_copy(src.at[layer, e], dst.at[ws], wsems.at[ws, k])
                for k, (src, dst) in enumerate(pairs)]

    def wait_block(q):
        pltpu.make_async_copy(ybuf.at[q], ys_ref.at[pl.ds(0, blk * SLABS), :], sems.at[q]).wait()

    def gather_row(block, tslot, mi):
        s = slot_ref[jnp.minimum(block, last) * blk + mi]
        span = seq * SLABS
        if span & (span - 1) == 0:
            src = pl.multiple_of(s & (span - 1), SLABS)
        else:
            src = pl.multiple_of(jnp.minimum(jnp.where(s >= span, s - span, s), span - SLABS), SLABS)
        pltpu.make_async_copy(x2s_ref.at[pl.ds(src, SLABS), :], tile.at[tslot, pl.ds(mi * SLABS, SLABS), :],
                              gsems.at[tslot]).start()

    def wait_gather(tslot):
        pltpu.make_async_copy(x2s_ref.at[pl.ds(0, blk * SLABS), :], tile.at[tslot], gsems.at[tslot]).wait()

    def to_bf16(tslot, xslot):
        for j in range(SLABS):
            xb[xslot, :, j * LANES:(j + 1) * LANES] = tile[tslot, pl.ds(j, blk, stride=SLABS), :].astype(BF16)

    def scatter_row(yslot, slot, row0, mi):
        dst = pl.multiple_of(row0 * SLABS + slot, SLABS)
        pltpu.make_async_copy(ybuf.at[yslot, pl.ds(mi * SLABS, SLABS), :], ys_ref.at[pl.ds(dst, SLABS), :],
                              sems.at[yslot]).start()

    @pl.when(g == 0)
    def _():
        for c in weight_copies(be_ref[0], 0):
            c.start()
        ybuf[...] = jnp.zeros(ybuf.shape, F32)
        for bb in range(batch):
            for c in range(DUMMY_SLOTS // blk):
                start = (bb * bstride + 2 * seq + c * blk) * SLABS
                zero = pltpu.make_async_copy(ybuf.at[0], ys_ref.at[pl.ds(start, blk * SLABS), :], sems.at[0])
                zero.start()
                zero.wait()
        for mi in range(blk):
            gather_row(0, 0, mi)
        wait_gather(0)
        to_bf16(0, 0)
        for first in range(1, ring - 1):
            for mi in range(blk):
                gather_row(first, first, mi)

    run = run_ref[g]
    ws = run % 2

    @pl.when((g == 0) | (run != run_ref[jnp.maximum(g - 1, 0)]))
    def _():
        for c in weight_copies(be_ref[g], ws):
            c.wait()
        wgb_ref[...] = wbuf_g[ws].astype(BF16)
        wub_ref[...] = wbuf_u[ws].astype(BF16)
        wdb_ref[...] = wbuf_d[ws].astype(BF16)

        @pl.when(nxt_ref[g] >= 0)
        def _():
            for c in weight_copies(nxt_ref[g], 1 - ws):
                c.start()

    @pl.when(g >= ring - 1)
    def _():
        wait_block(ycur)

    prev_block = jnp.where(g == 0, batch * nb, g - 1)
    prev_row0 = jnp.where(n == 0, jnp.maximum(b - 1, 0), b) * bstride
    n_chunks = 8
    per = blk // n_chunks
    half = EXPERT_HIDDEN // 2
    quarter = D_MODEL // 4

    @pl.when(g < n_real)
    def _():
        x_cur, y_cur, y_prev = g % 2, ycur, yprev
        next_slot, fetch_slot = (g + 1) % ring, (g + ring - 1) % ring
        wait_gather(next_slot)
        for mi in range(blk):
            scatter_row(y_prev, slot_ref[prev_block * blk + mi], prev_row0, mi)
        to_bf16(next_slot, 1 - x_cur)
        x = xb[x_cur]
        acts = []
        hid = None
        for c in range(n_chunks):
            for mi in range(c * per, (c + 1) * per):
                gather_row(g + ring - 1, fetch_slot, mi)
            if c < 4:
                w_ref = wgb_ref if c < 2 else wub_ref
                acts.append(jnp.dot(x, w_ref[:, (c % 2) * half:(c % 2 + 1) * half], preferred_element_type=F32))
            if c == 3:
                a = jnp.concatenate(acts[0:2], axis=1)
                u = jnp.concatenate(acts[2:4], axis=1)
                hid = (a * jax.nn.sigmoid(a) * u).astype(BF16)
            if c >= 4:
                q = c - 4
                yq = jnp.dot(hid, wdb_ref[:, q * quarter:(q + 1) * quarter], preferred_element_type=F32)
                for jj in range(quarter // LANES):
                    j = q * (quarter // LANES) + jj
                    ybuf[y_cur, pl.ds(j, blk, stride=SLABS), :] = yq[:, jj * LANES:(jj + 1) * LANES]

    @pl.when(g >= n_real)
    def _():
        for mi in range(blk):
            scatter_row(yprev, slot_ref[prev_block * blk + mi], prev_row0, mi)

    @pl.when(g == last)
    def _():
        wait_block((last - 3) % ring)
        wait_block((last - 2) % ring)
        for mi in range(blk):
            scatter_row(last % ring, slot_ref[last * blk + mi], (batch - 1) * bstride, mi)
        wait_block((last - 1) % ring)
        wait_block(last % ring)
        for extra in range(1, ring - 1):
            wait_gather((n_real + extra) % ring)


def _experts(block_expert, n_real, row_slot, x2s, wg, wu, wd, layer, batch, seq):
    blk = MOE_BLOCK
    nb = (row_slot.shape[0] // blk - 1) // batch
    steps = batch * nb
    assert steps >= 4 and batch == 1
    idx = jnp.arange(steps, dtype=I32)
    change = jnp.concatenate([jnp.zeros((1,), I32), (block_expert[1:] != block_expert[:-1]).astype(I32)])
    run = jnp.cumsum(change).astype(I32)
    later_change = (idx[None, :] > idx[:, None]) & (change[None, :] > 0)
    nxt_idx = jnp.min(jnp.where(later_change, idx[None, :], steps), axis=1)
    nxt = jnp.where(nxt_idx < steps, block_expert[jnp.minimum(nxt_idx, steps - 1)], -1).astype(I32)
    hbm = pl.BlockSpec(memory_space=pl.ANY)
    grid_spec = pltpu.PrefetchScalarGridSpec(
        num_scalar_prefetch=5,
        grid=(batch, nb),
        in_specs=[hbm, hbm, hbm, hbm],
        out_specs=hbm,
        scratch_shapes=[
            pltpu.VMEM((2, D_MODEL, EXPERT_HIDDEN), F32), pltpu.VMEM((2, D_MODEL, EXPERT_HIDDEN), F32),
            pltpu.VMEM((2, EXPERT_HIDDEN, D_MODEL), F32),
            pltpu.VMEM((D_MODEL, EXPERT_HIDDEN), BF16), pltpu.VMEM((D_MODEL, EXPERT_HIDDEN), BF16),
            pltpu.VMEM((EXPERT_HIDDEN, D_MODEL), BF16),
            pltpu.VMEM((EXPERT_RING, blk * SLABS, LANES), F32),
            pltpu.VMEM((2, blk, D_MODEL), BF16),
            pltpu.VMEM((EXPERT_RING, blk * SLABS, LANES), F32),
            pltpu.SemaphoreType.DMA((EXPERT_RING,)),
            pltpu.SemaphoreType.DMA((2, 3)),
            pltpu.SemaphoreType.DMA((EXPERT_RING,)),
        ],
    )
    return pl.pallas_call(
        functools.partial(_expert_kernel, batch=batch, nb=nb, seq=seq, layer=layer),
        grid_spec=grid_spec,
        out_shape=jax.ShapeDtypeStruct((batch * (2 * seq + DUMMY_SLOTS) * SLABS, LANES), F32),
        compiler_params=_cparams(("arbitrary", "arbitrary")),
        name="moe_experts",
    )(block_expert, run, nxt, n_real, row_slot, x2s, wg, wu, wd)


def _combine_kernel(h_ref, gates_ref, y0_ref, y1_ref, *rest, final):
    if final:
        fg_ref, out_ref = rest
    else:
        (out_ref,) = rest
    tm = h_ref.shape[0]
    gates = gates_ref[...]
    g0, g1 = gates[:, 0:1], gates[:, 1:2]
    parts = []
    for j in range(SLABS):
        y0 = y0_ref[pl.ds(j, tm, stride=SLABS), :]
        y1 = y1_ref[pl.ds(j, tm, stride=SLABS), :]
        parts.append(g0 * y0 + g1 * y1)
    out = h_ref[...] + jnp.concatenate(parts, axis=1)
    if final:
        out = _rms(out, fg_ref[...])
    out_ref[...] = out


def _combine(h, gates, ys, batch, seq, final_gain=None):
    t = h.shape[0]
    tm = ROW_TILE
    tpb = seq // tm
    bstride = 2 * tpb + DUMMY_SLOTS // tm
    final = final_gain is not None
    in_specs = [pl.BlockSpec((tm, D_MODEL), lambda i: (i, 0)),
                pl.BlockSpec((tm, LANES), lambda i: (i, 0)),
                pl.BlockSpec((tm * SLABS, LANES), lambda i: (i // tpb * bstride + i % tpb, 0)),
                pl.BlockSpec((tm * SLABS, LANES), lambda i: (i // tpb * bstride + tpb + i % tpb, 0))]
    args = [h, gates, ys, ys]
    if final:
        in_specs.append(pl.BlockSpec((1, D_MODEL), lambda i: (0, 0)))
        args.append(final_gain.reshape(1, D_MODEL))
    return pl.pallas_call(
        functools.partial(_combine_kernel, final=final),
        grid=(t // tm,),
        in_specs=in_specs,
        out_specs=pl.BlockSpec((tm, D_MODEL), lambda i: (i, 0)),
        out_shape=jax.ShapeDtypeStruct((t, D_MODEL), F32),
        compiler_params=_cparams(("parallel",)),
        name="moe_combine_final" if final else "moe_combine",
    )(*args)


def _moe(h, gain, w_r1, b_r1, w_r2, b_r2, wg, wu, wd, layer, batch, final_gain=None):
    t = h.shape[0]
    seq = t // batch
    assert seq <= (1 << RANK_BITS) and seq % ROW_TILE == 0 and DUMMY_SLOTS % ROW_TILE == 0
    pad_w = jnp.zeros((D_MODEL, LANES - MOE_GROUPS - MOE_EXPERTS), F32)
    wr = jnp.concatenate([w_r1, w_r2, pad_w], axis=1)
    br = jnp.concatenate([b_r1, b_r2, jnp.zeros((LANES - MOE_GROUPS - MOE_EXPERTS,), F32)]).reshape(1, LANES)
    x2s, codes, gates, counts = _router(h, gain, wr, br, batch)
    cnt = counts.reshape(batch, 8, LANES)[:, 0, ROUTER_LANE0:ROUTER_LANE0 + MOE_EXPERTS].astype(I32)
    padded = (cnt + MOE_BLOCK - 1) // MOE_BLOCK * MOE_BLOCK
    pad_end = jnp.cumsum(padded, axis=1)
    rows_pb = seq * 2 + MOE_EXPERTS * MOE_BLOCK
    nb = rows_pb // MOE_BLOCK
    pad_start = pad_end - padded + (jnp.arange(batch, dtype=I32) * rows_pb)[:, None]
    block_start = jnp.arange(nb, dtype=I32) * MOE_BLOCK
    block_expert = jnp.minimum(
        jnp.sum((pad_end[:, None, :] <= block_start[None, :, None]).astype(I32), axis=2), MOE_EXPERTS - 1)
    dest = _dest_rows(pad_start.reshape(-1), codes)
    dest_flat = dest[0:2].reshape(2, batch, seq).transpose(1, 0, 2).reshape(-1)
    row_slot = _invert(dest_flat, batch * rows_pb, batch)
    n_real = (pad_end[:, -1] // MOE_BLOCK).astype(I32)
    ys = _experts(block_expert.reshape(-1), n_real, row_slot, x2s, wg, wu, wd, layer, batch, seq)
    return _combine(h, gates, ys, batch, seq, final_gain)


def kernel(x, mix_norm, ffn_norm, final_norm, even_w_in, even_w_out, conv_w, conv_b, conv_norm_g, conv_norm_b,
           odd_w_in, odd_w_out, router_w1, router_b1, router_w2, router_b2, expert_w_gate, expert_w_up,
           expert_w_down):
    batch, seq, d = x.shape
    assert d == D_MODEL and seq % (ATTN_BLOCK * max(dl for _, dl in A_BRANCHES)) == 0
    assert all(w // dl == ATTN_BLOCK for w, dl in A_BRANCHES)
    t = batch * seq
    h = x.reshape(t, d)

    qkv, conv_in = _even_in(h, mix_norm[0], even_w_in, _attn_rope_tables(seq), seq)
    attn = _attention(qkv, batch, seq)
    h = _even_tail(h, attn, conv_in, conv_w[0], conv_b[0], conv_norm_g[0], conv_norm_b[0], even_w_out, batch, seq)
    h = _moe(h, ffn_norm[0], router_w1[0], router_b1[0], router_w2[0], router_b2[0],
             expert_w_gate, expert_w_up, expert_w_down, 0, 1)

    inv_freq = RET_ROT_THETA ** (-jnp.linspace(0.0, 1.0, C_QK_DIM // 2, dtype=F32))
    ang = jnp.arange(seq, dtype=F32)[:, None] * inv_freq[None, :]
    proj = _odd_in(h, mix_norm[1], odd_w_in, jnp.cos(ang), jnp.sin(ang), seq)
    log_decay = jnp.log(1.0 - jnp.exp2(-5.0 - jnp.arange(C_HEADS, dtype=F32)))
    y = _retention(proj, log_decay, batch, seq)
    h = _odd_out(h, y, odd_w_out)
    out = _moe(h, ffn_norm[1], router_w1[1], router_b1[1], router_w2[1], router_b2[1],
               expert_w_gate, expert_w_up, expert_w_down, 1, 1, final_gain=final_norm)
    return out.reshape(batch, seq, d)
```
